```python
import math
import jax, jax.numpy as jnp
from jax import lax
import numpy as np

D_MODEL = 2048
BATCH = 16
SEQ = 2048
DEPTH = 2

GRID_W = 64
CTX_LEN = 256
N_BRANCH = 3
W_HY = 768
HY_ORDER = 2
HY_BANDS = 16
HY_FEAT = 1 + 2 * HY_BANDS
HY_HIDDEN = 64
HY_DECAY_MIN = -math.log(1e-2) / 1.5
HY_DECAY_MAX = -math.log(1e-2) / 0.3
W_ML = 768
ML_HEADS = 4
ML_DH = W_ML // ML_HEADS
ML_CHUNK = 64
W_S5 = 768
S5_GROUP = 16
S5_GROUPS = W_S5 // S5_GROUP
S5_STATE = 64
D_FF = 4 * D_MODEL
SPLIT_SIZES = ((HY_ORDER + 1) * W_HY, 2 * W_ML, W_ML, W_ML, 4 * ML_HEADS, W_S5, N_BRANCH * D_MODEL)
SPLIT_POINTS = tuple(int(v) for v in np.cumsum(SPLIT_SIZES)[:-1])
N_IN = int(sum(SPLIT_SIZES))
DEEPNORM_ALPHA = (2 * DEPTH) ** 0.25
DEEPNORM_BETA = (8 * DEPTH) ** -0.25
LN_EPS = 1e-5
F32 = jnp.float32

kernel_name = 'hybrid_hyena_mlstm_s5_diffusion_block'


def layer_norm(x, g, b):
    xf = x.astype(F32)
    mu = jnp.mean(xf, axis=-1, keepdims=True)
    var = jnp.mean(jnp.square(xf - mu), axis=-1, keepdims=True)
    y = (xf - mu) * lax.rsqrt(var + LN_EPS)
    return (y * g.astype(F32) + b.astype(F32)).astype(x.dtype)


def modulate(h, shift, scale):
    return h * (1 + scale) + shift


def short_conv(x, w, b, rows):
    B, L, C = x.shape
    xp = jnp.pad(x.reshape(B, rows, L // rows, C), ((0, 0), (0, 0), (1, 1), (0, 0)))
    y = xp[:, :, :-2] * w[0] + xp[:, :, 1:-1] * w[1] + xp[:, :, 2:] * w[2] + b
    return y.reshape(B, L, C)


def hyena_filters(L, w1, b1, w2, b2, w3, freq, decay):
    pos = jnp.arange(L, dtype=F32)
    t = pos / (L - 1)
    bands = jnp.linspace(1e-4, HY_BANDS - 1, HY_BANDS, dtype=F32)
    ang = (2.0 * math.pi / L) * pos[:, None] * bands[None, :]
    feats = jnp.concatenate([t[:, None], jnp.cos(ang), jnp.sin(ang)], axis=-1)
    freq = freq.astype(F32)
    h = jnp.sin(freq * (feats @ w1.astype(F32) + b1.astype(F32)))
    h = jnp.sin(freq * (h @ w2.astype(F32) + b2.astype(F32)))
    h = (h @ w3.astype(F32)).reshape(L, HY_ORDER, 2, W_HY)
    window = jnp.exp(-t[:, None, None] * jnp.abs(decay.astype(F32))[None])
    h = h * window[:, :, None, :]
    k_circ = jnp.concatenate([h[:, :, 0], jnp.zeros((1, HY_ORDER, W_HY), F32), h[:0:-1, :, 1]], axis=0)
    return jnp.fft.rfft(k_circ, axis=0)


def fft_long_conv(u, k_freq, bias):
    L = u.shape[1]
    y = jnp.fft.irfft(jnp.fft.rfft(u, n=2 * L, axis=1) * k_freq, n=2 * L, axis=1)[:, :L]
    return y + u * bias


def hyena_branch(z, rows, conv_w, conv_b, w1, b1, w2, b2, w3, freq, decay, bias):
    z = short_conv(z, conv_w, conv_b, rows)
    v, x1, x2 = jnp.split(z.astype(F32), HY_ORDER + 1, axis=-1)
    k_freq = hyena_filters(z.shape[1], w1, b1, w2, b2, w3, freq, decay)
    bias = bias.astype(F32)
    y = x1 * fft_long_conv(v, k_freq[:, 0], bias[0])
    y = x2 * fft_long_conv(y, k_freq[:, 1], bias[1])
    return y.astype(z.dtype)


def mlstm_zero_state(B):
    return (jnp.zeros((B, ML_HEADS, ML_DH, ML_DH), F32), jnp.zeros((B, ML_HEADS, ML_DH), F32),
            jnp.zeros((B, ML_HEADS), F32))


def mlstm_chunk_scan(q, k, v, li, lf, state):
    B, H, L, Dh = q.shape
    nc = L // ML_CHUNK

    def chunks(a):
        return jnp.moveaxis(a.reshape((B, H, nc, ML_CHUNK) + a.shape[3:]), 2, 0)

    causal = jnp.tril(jnp.ones((ML_CHUNK, ML_CHUNK), bool))

    def step(carry, inp):
        C, n, m = carry
        qc, kc, vc, lic, lfc = inp
        b = jnp.cumsum(lfc, axis=-1)
        d = jnp.where(causal, b[..., :, None] - b[..., None, :] + lic[..., None, :], -jnp.inf)
        g = b + m[..., None]
        m_t = jnp.maximum(g, d.max(-1))
        w_inter = jnp.exp(g - m_t)
        s = jnp.einsum('bhtd,bhsd->bhts', qc, kc) * jnp.exp(d - m_t[..., None])
        num = w_inter[..., None] * jnp.einsum('bhvd,bhtd->bhtv', C, qc) + jnp.einsum('bhts,bhsv->bhtv', s, vc)
        den = w_inter * jnp.einsum('bhd,bhtd->bht', n, qc) + s.sum(-1)
        h = num / jnp.maximum(jnp.abs(den), jnp.exp(-m_t))[..., None]
        a = b[..., -1:] - b + lic
        m_new = jnp.maximum(b[..., -1] + m, a.max(-1))
        wk = jnp.exp(a - m_new[..., None])
        dec = jnp.exp(b[..., -1] + m - m_new)
        C = dec[..., None, None] * C + jnp.einsum('bhs,bhsv,bhsd->bhvd', wk, vc, kc)
        n = dec[..., None] * n + jnp.einsum('bhs,bhsd->bhd', wk, kc)
        return (C, n, m_new), h

    state, h = lax.scan(step, state, (chunks(q), chunks(k), chunks(v), chunks(li), chunks(lf)))
    return jnp.moveaxis(h, 0, 2).reshape(B, H, L, Dh), state


def mlstm_prep(qk_pre, v_pre, gate_pre, rows, conv_w, conv_b, gate_b):
    B, L, _ = qk_pre.shape
    qk = jax.nn.silu(short_conv(qk_pre, conv_w, conv_b, rows)).astype(F32)

    def heads(a):
        return a.reshape(B, L, ML_HEADS, ML_DH).transpose(0, 2, 1, 3)

    q = heads(qk[..., :W_ML])
    k = heads(qk[..., W_ML:]) * (ML_DH ** -0.5)
    v = heads(v_pre.astype(F32))
    gates = (gate_pre.astype(F32).reshape(B, L, 4, ML_HEADS) + gate_b.astype(F32)).transpose(2, 0, 3, 1)
    return (q, k, v, gates[0], jax.nn.log_sigmoid(gates[1]), gates[2], jax.nn.log_sigmoid(gates[3]))


def mlstm_bidir(prep, state_f, state_b):
    q, k, v, li_f, lf_f, li_b, lf_b = prep
    h_f, st_f = mlstm_chunk_scan(q, k, v, li_f, lf_f, state_f)
    fl = lambda a: jnp.flip(a, axis=2)
    h_b, st_b = mlstm_chunk_scan(fl(q), fl(k), fl(v), fl(li_b), fl(lf_b), state_b)
    return h_f + fl(h_b), st_f, st_b


def mlstm_out(h, o_pre, norm_g):
    B, H, L, Dh = h.shape
    mu = jnp.mean(h, axis=-1, keepdims=True)
    var = jnp.mean(jnp.square(h - mu), axis=-1, keepdims=True)
    hn = ((h - mu) * lax.rsqrt(var + LN_EPS)).transpose(0, 2, 1, 3).reshape(B, L, W_ML)
    return (hn * norm_g.astype(F32) * jax.nn.sigmoid(o_pre.astype(F32))).astype(o_pre.dtype)


def s5_zero_state(B):
    return (jnp.zeros((B, S5_GROUPS, S5_STATE), F32), jnp.zeros((B, S5_GROUPS, S5_STATE), F32))


def s5_discretise(a_re, a_im, log_dt, b_re, b_im):
    a_re, a_im, b_re, b_im = (t.astype(F32) for t in (a_re, a_im, b_re, b_im))
    dt = jnp.exp(log_dt.astype(F32))[:, None]
    mag = jnp.exp(dt * a_re)
    ab_re, ab_im = mag * jnp.cos(dt * a_im), mag * jnp.sin(dt * a_im)
    den = jnp.square(a_re) + jnp.square(a_im)
    co_re = ((ab_re - 1.0) * a_re + ab_im * a_im) / den
    co_im = (ab_im * a_re - (ab_re - 1.0) * a_im) / den
    bb_re = co_re[..., None] * b_re - co_im[..., None] * b_im
    bb_im = co_re[..., None] * b_im + co_im[..., None] * b_re
    return ab_re, ab_im, bb_re, bb_im


def s5_combine(e1, e2):
    a1r, a1i, b1r, b1i = e1
    a2r, a2i, b2r, b2i = e2
    return (a1r * a2r - a1i * a2i, a1r * a2i + a1i * a2r,
            a2r * b1r - a2i * b1i + b2r, a2r * b1i + a2i * b1r + b2i)


def s5_direction(u, ab_re, ab_im, bb_re, bb_im, c_re, c_im, x0, readout):
    L = u.shape[1]
    bu_re = jnp.einsum('blgc,gpc->blgp', u, bb_re)
    bu_im = jnp.einsum('blgc,gpc->blgp', u, bb_im)
    x0_re, x0_im = x0
    bu_re = bu_re.at[:, 0].add(ab_re * x0_re - ab_im * x0_im)
    bu_im = bu_im.at[:, 0].add(ab_re * x0_im + ab_im * x0_re)
    a_re = jnp.broadcast_to(ab_re[None, None], (1, L) + ab_re.shape)
    a_im = jnp.broadcast_to(ab_im[None, None], (1, L) + ab_im.shape)
    _, _, s_re, s_im = lax.associative_scan(s5_combine, (a_re, a_im, bu_re, bu_im), axis=1)
    final = (s_re[:, -1], s_im[:, -1])
    if not readout:
        return None, final
    y = (jnp.einsum('blgp,gcp->blgc', s_re, c_re.astype(F32))
         - jnp.einsum('blgp,gcp->blgc', s_im, c_im.astype(F32)))
    return y, final


def s5_bidir(u, disc, c_re, c_im, x0_f, x0_b, readout):
    B, L, _ = u.shape
    ug = u.astype(F32).reshape(B, L, S5_GROUPS, S5_GROUP)
    y_f, st_f = s5_direction(ug, *disc[0], c_re[0], c_im[0], x0_f, readout)
    y_b, st_b = s5_direction(jnp.flip(ug, 1), *disc[1], c_re[1], c_im[1], x0_b, readout)
    y = y_f + jnp.flip(y_b, 1) if readout else None
    return y, st_f, st_b


def s5_out(y, u, d, glu_w, glu_b):
    B, L, _ = u.shape
    y = y.reshape(B, L, W_S5) + d.astype(F32) * u.astype(F32)
    z = jax.nn.gelu(y).astype(u.dtype)
    return z * jax.nn.sigmoid(z @ glu_w + glu_b)


def merge_branches(y_hy, y_ml, y_s5, gate_pre, w_hy_out, w_ml_out, w_s5_out, w_out):
    g_hy, g_ml, g_s5 = jnp.split(jax.nn.sigmoid(gate_pre), N_BRANCH, axis=-1)
    merged = g_hy * (y_hy @ w_hy_out) + g_ml * (y_ml @ w_ml_out) + g_s5 * (y_s5 @ w_s5_out)
    return merged @ w_out


def token_mixer(h_ctx, h_lat, rows, need_ctx, w_in, hy_p, ml_p, s5_p, out_p):
    hy_c, qk_c, v_c, o_c, gt_c, u_c, mg_c = jnp.split(h_ctx @ w_in, SPLIT_POINTS, axis=-1)
    hy_l, qk_l, v_l, o_l, gt_l, u_l, mg_l = jnp.split(h_lat @ w_in, SPLIT_POINTS, axis=-1)
    B = h_lat.shape[0]
    ml_conv_w, ml_conv_b, ml_gate_b, ml_norm_g = ml_p
    zm = mlstm_zero_state(B)
    hm_c, mst_f, mst_b = mlstm_bidir(mlstm_prep(qk_c, v_c, gt_c, 1, ml_conv_w, ml_conv_b, ml_gate_b), zm, zm)
    hm_l, _, _ = mlstm_bidir(mlstm_prep(qk_l, v_l, gt_l, rows, ml_conv_w, ml_conv_b, ml_gate_b), mst_f, mst_b)
    a_re, a_im, log_dt, b_re, b_im, c_re, c_im, s5_d, glu_w, glu_b = s5_p
    disc = [s5_discretise(a_re[i], a_im[i], log_dt[i], b_re[i], b_im[i]) for i in range(2)]
    zs = s5_zero_state(B)
    ys_c, sst_f, sst_b = s5_bidir(u_c, disc, c_re, c_im, zs, zs, need_ctx)
    ys_l, _, _ = s5_bidir(u_l, disc, c_re, c_im, sst_f, sst_b, True)
    y_lat = merge_branches(hyena_branch(hy_l, rows, *hy_p), mlstm_out(hm_l, o_l, ml_norm_g),
                           s5_out(ys_l, u_l, s5_d, glu_w, glu_b), mg_l, *out_p)
    if not need_ctx:
        return None, y_lat
    y_ctx = merge_branches(hyena_branch(hy_c, 1, *hy_p), mlstm_out(hm_c, o_c, ml_norm_g),
                           s5_out(ys_c, u_c, s5_d, glu_w, glu_b), mg_c, *out_p)
    return y_ctx, y_lat


def sq_relu_mlp(h, w1, w2):
    return jnp.square(jax.nn.relu(h @ w1)) @ w2


def setup_inputs(seed: int = 0) -> dict:
    key = jax.random.key(seed)
    ks = iter(jax.random.split(key, 64))

    def nrm(shape, scale=1.0):
        return scale * jax.random.normal(next(ks), shape, F32)

    H = ML_HEADS
    G, P, CG = S5_GROUPS, S5_STATE, S5_GROUP
    f_bias = jnp.linspace(3.0, 6.0, H, dtype=F32)
    gate_base = jnp.stack([jnp.zeros((H,), F32), f_bias, jnp.zeros((H,), F32), f_bias])
    n_idx = jnp.arange(P, dtype=F32)
    return {
        'x': nrm((BATCH, SEQ, D_MODEL)),
        'c': nrm((BATCH, D_MODEL)),
        'ctx': nrm((BATCH, CTX_LEN, D_MODEL)),
        'c_ctx': nrm((D_MODEL,)),
        'w_mod': nrm((DEPTH, D_MODEL, 6 * D_MODEL), 0.5 * D_MODEL ** -0.5),
        'b_mod': nrm((DEPTH, 6 * D_MODEL), 0.02),
        'w_in': nrm((DEPTH, D_MODEL, N_IN), D_MODEL ** -0.5),
        'hy_conv_w': nrm((DEPTH, 3, (HY_ORDER + 1) * W_HY), 0.5),
        'hy_conv_b': nrm((DEPTH, (HY_ORDER + 1) * W_HY), 0.02),
        'hy_ffn_w1': nrm((DEPTH, HY_FEAT, HY_HIDDEN), HY_FEAT ** -0.5),
        'hy_ffn_b1': nrm((DEPTH, HY_HIDDEN), 0.1),
        'hy_ffn_w2': nrm((DEPTH, HY_HIDDEN, HY_HIDDEN), HY_HIDDEN ** -0.5),
        'hy_ffn_b2': nrm((DEPTH, HY_HIDDEN), 0.1),
        'hy_ffn_w3': nrm((DEPTH, HY_HIDDEN, HY_ORDER * 2 * W_HY), 0.05 * HY_HIDDEN ** -0.5),
        'hy_sin_freq': 1.0 + nrm((DEPTH, HY_HIDDEN), 0.01),
        'hy_decay': jnp.linspace(HY_DECAY_MIN, HY_DECAY_MAX, W_HY, dtype=F32) + nrm((DEPTH, HY_ORDER, W_HY), 0.01),
        'hy_bias': nrm((DEPTH, HY_ORDER, W_HY)),
        'ml_conv_w': nrm((DEPTH, 3, 2 * W_ML), 0.5),
        'ml_conv_b': nrm((DEPTH, 2 * W_ML), 0.02),
        'ml_gate_b': gate_base + nrm((DEPTH, 4, H), 0.1),
        'ml_norm_g': 1.0 + nrm((DEPTH, W_ML), 0.02),
        's5_a_re': -0.5 + nrm((DEPTH, 2, G, P), 0.01),
        's5_a_im': math.pi * n_idx + nrm((DEPTH, 2, G, P), 0.01),
        's5_log_dt': jax.random.uniform(next(ks), (DEPTH, 2, G), F32, math.log(1e-3), math.log(1e-1)),
        's5_b_re': nrm((DEPTH, 2, G, P, CG), (2 * CG) ** -0.5),
        's5_b_im': nrm((DEPTH, 2, G, P, CG), (2 * CG) ** -0.5),
        's5_c_re': nrm((DEPTH, 2, G, CG, P), P ** -0.5),
        's5_c_im': nrm((DEPTH, 2, G, CG, P), P ** -0.5),
        's5_d': nrm((DEPTH, W_S5)),
        's5_glu_w': nrm((DEPTH, W_S5, W_S5), W_S5 ** -0.5),
        's5_glu_b': nrm((DEPTH, W_S5), 0.02),
        'w_hy_out': nrm((DEPTH, W_HY, D_MODEL), DEEPNORM_BETA * W_HY ** -0.5),
        'w_ml_out': nrm((DEPTH, W_ML, D_MODEL), DEEPNORM_BETA * W_ML ** -0.5),
        'w_s5_out': nrm((DEPTH, W_S5, D_MODEL), DEEPNORM_BETA * W_S5 ** -0.5),
        'w_out': nrm((DEPTH, D_MODEL, D_MODEL), DEEPNORM_BETA * D_MODEL ** -0.5),
        'ln1_g': 1.0 + nrm((DEPTH, D_MODEL), 0.02),
        'ln1_b': nrm((DEPTH, D_MODEL), 0.02),
        'ln2_g': 1.0 + nrm((DEPTH, D_MODEL), 0.02),
        'ln2_b': nrm((DEPTH, D_MODEL), 0.02),
        'w_ff1': nrm((DEPTH, D_MODEL, D_FF), DEEPNORM_BETA * D_MODEL ** -0.5),
        'w_ff2': nrm((DEPTH, D_FF, D_MODEL), DEEPNORM_BETA * D_FF ** -0.5),
    }


def reference(x, c, ctx, c_ctx, w_mod, b_mod, w_in, hy_conv_w, hy_conv_b, hy_ffn_w1, hy_ffn_b1,
              hy_ffn_w2, hy_ffn_b2, hy_ffn_w3, hy_sin_freq, hy_decay, hy_bias, ml_conv_w, ml_conv_b,
              ml_gate_b, ml_norm_g, s5_a_re, s5_a_im, s5_log_dt, s5_b_re, s5_b_im, s5_c_re, s5_c_im,
              s5_d, s5_glu_w, s5_glu_b, w_hy_out, w_ml_out, w_s5_out, w_out, ln1_g, ln1_b, ln2_g, ln2_b,
              w_ff1, w_ff2):
    rows = x.shape[1] // GRID_W
    silu_c = jax.nn.silu(c)
    silu_cc = jax.nn.silu(c_ctx)
    for l in range(DEPTH):
        need_ctx = l < DEPTH - 1
        mod_l = jnp.split((silu_c @ w_mod[l] + b_mod[l])[:, None, :], 6, axis=-1)
        mod_c = jnp.split(silu_cc @ w_mod[l] + b_mod[l], 6, axis=-1)
        hy_p = (hy_conv_w[l], hy_conv_b[l], hy_ffn_w1[l], hy_ffn_b1[l], hy_ffn_w2[l], hy_ffn_b2[l],
                hy_ffn_w3[l], hy_sin_freq[l], hy_decay[l], hy_bias[l])
        ml_p = (ml_conv_w[l], ml_conv_b[l], ml_gate_b[l], ml_norm_g[l])
        s5_p = (s5_a_re[l], s5_a_im[l], s5_log_dt[l], s5_b_re[l], s5_b_im[l], s5_c_re[l], s5_c_im[l],
                s5_d[l], s5_glu_w[l], s5_glu_b[l])
        out_p = (w_hy_out[l], w_ml_out[l], w_s5_out[l], w_out[l])
        y_ctx, y_lat = token_mixer(modulate(ctx, mod_c[0], mod_c[1]), modulate(x, mod_l[0], mod_l[1]),
                                   rows, need_ctx, w_in[l], hy_p, ml_p, s5_p, out_p)
        x = layer_norm(DEEPNORM_ALPHA * x + mod_l[2] * y_lat, ln1_g[l], ln1_b[l])
        x = layer_norm(DEEPNORM_ALPHA * x + mod_l[5] * sq_relu_mlp(modulate(x, mod_l[3], mod_l[4]), w_ff1[l], w_ff2[l]),
                       ln2_g[l], ln2_b[l])
        if need_ctx:
            ctx = layer_norm(DEEPNORM_ALPHA * ctx + mod_c[2] * y_ctx, ln1_g[l], ln1_b[l])
            ctx = layer_norm(DEEPNORM_ALPHA * ctx + mod_c[5] * sq_relu_mlp(modulate(ctx, mod_c[3], mod_c[4]), w_ff1[l], w_ff2[l]),
                             ln2_g[l], ln2_b[l])
    return x
```

```python
import functools
import math

import jax
import jax.numpy as jnp
from jax import lax
from jax.experimental import pallas as pl
from jax.experimental.pallas import tpu as pltpu

F32 = jnp.float32
BF16 = jnp.bfloat16

GRID_W = 64
HY_ORDER = 2
HY_BANDS = 16
HY_FEAT = 1 + 2 * HY_BANDS
HY_HIDDEN = 64
ML_HEADS = 4
ML_CHUNK = 256
S5_GROUP = 16
S5_STATE = 64
N_BRANCH = 3
LN_EPS = 1e-5

LANES = 128
MXU_DIM = 256
VMEM_LIMIT = 56 * 1024 * 1024

HEAD_PAD = MXU_DIM
S5_LANE_BLOCK = LANES
S5_TIME_BLOCK = 64


def _cparams(sem):
    return pltpu.CompilerParams(dimension_semantics=sem, vmem_limit_bytes=VMEM_LIMIT)


def _sigmoid(x):
    return 1.0 / (1.0 + jnp.exp(-x))


def _layer_norm(r, g, b):
    mu = jnp.mean(r, axis=-1, keepdims=True)
    d = r - mu
    var = jnp.mean(d * d, axis=-1, keepdims=True)
    return d * lax.rsqrt(var + LN_EPS) * g + b


def _mod_kernel(c_ref, w_ref, b_ref, o_ref):
    c = c_ref[...]
    a = (c * _sigmoid(c)).astype(BF16)
    o_ref[...] = jnp.dot(a, w_ref[...].astype(BF16), preferred_element_type=F32) + b_ref[...]


def mod_vectors(c_rows, w, b):
    m, d = c_rows.shape
    n = w.shape[1]
    tn = 1024
    return pl.pallas_call(
        _mod_kernel,
        out_shape=jax.ShapeDtypeStruct((m, n), F32),
        grid=(n // tn,),
        in_specs=[pl.BlockSpec((m, d), lambda j: (0, 0)),
                  pl.BlockSpec((d, tn), lambda j: (0, j)),
                  pl.BlockSpec((1, tn), lambda j: (0, j))],
        out_specs=pl.BlockSpec((m, tn), lambda j: (0, j)),
        compiler_params=_cparams(("parallel",)),
        name="mod_vectors",
    )(c_rows, w, b.reshape(1, n))


def _short_conv(acc, w, b, row_len):
    tm = acc.shape[0]
    t = lax.broadcasted_iota(jnp.int32, acc.shape, 0) & (row_len - 1)
    prev = jnp.where(t == 0, 0.0, pltpu.roll(acc, 1, 0))
    nxt = jnp.where(t == row_len - 1, 0.0, pltpu.roll(acc, tm - 1, 0))
    return prev * w[0:1] + acc * w[1:2] + nxt * w[2:3] + b


def _inproj_kernel(x_ref, sh_ref, sc_ref, w_ref, cw_ref, cb_ref, cs_ref, o_ref, xm_ref, *, kinds, row_len):
    j = pl.program_id(2)

    @pl.when(j == 0)
    def _():
        xm_ref[...] = (x_ref[0] * (1.0 + sc_ref[0]) + sh_ref[0]).astype(BF16)

    acc = jnp.dot(xm_ref[...], w_ref[...], preferred_element_type=F32)

    def emit(val):
        o_ref[...] = val.astype(o_ref.dtype).reshape(o_ref.shape)

    for kind in sorted(set(kinds)):
        idx = [i for i, k in enumerate(kinds) if k == kind]
        lo, hi = idx[0], idx[-1]
        assert idx == list(range(lo, hi + 1))

        @pl.when((j >= lo) & (j <= hi))
        def _(kind=kind):
            if kind == "conv":
                emit(_short_conv(acc, cw_ref[...], cb_ref[...], row_len))
            elif kind == "convsilu":
                y = _short_conv(acc, cw_ref[...], cb_ref[...], row_len)
                emit(y * _sigmoid(y) * cs_ref[...])
            elif kind == "sigmoid":
                emit(_sigmoid(acc))
            elif kind == "bias":
                emit(acc + cb_ref[...])
            else:
                emit(acc)


def in_projection(x, shift, scale, w, cw, cb, cs, *, kinds, tn, tm, row_len, out_dtype=BF16, time_major=False):
    bv, lv, d = x.shape
    n = w.shape[1]
    nt = n // tn
    assert len(kinds) == nt and lv % tm == 0 and tm % row_len == 0
    bm = shift.shape[0]
    mod_map = (lambda b, i, j: (b, 0, 0)) if bm == bv else (lambda b, i, j: (0, 0, 0))
    if time_major:
        assert nt == 1
        out_shape = jax.ShapeDtypeStruct((lv, bv * n), out_dtype)
        out_spec = pl.BlockSpec((tm, tn), lambda b, i, j: (i, b))
    else:
        out_shape = jax.ShapeDtypeStruct((bv, lv, n), out_dtype)
        out_spec = pl.BlockSpec((1, tm, tn), lambda b, i, j: (b, i, j))
    return pl.pallas_call(
        functools.partial(_inproj_kernel, kinds=tuple(kinds), row_len=row_len),
        out_shape=out_shape,
        grid=(bv, lv // tm, nt),
        in_specs=[pl.BlockSpec((1, tm, d), lambda b, i, j: (b, i, 0)),
                  pl.BlockSpec((1, 1, d), mod_map),
                  pl.BlockSpec((1, 1, d), mod_map),
                  pl.BlockSpec((d, tn), lambda b, i, j: (0, j)),
                  pl.BlockSpec((3, tn), lambda b, i, j: (0, j)),
                  pl.BlockSpec((1, tn), lambda b, i, j: (0, j)),
                  pl.BlockSpec((1, tn), lambda b, i, j: (0, j))],
        out_specs=out_spec,
        scratch_shapes=[pltpu.VMEM((tm, d), BF16)],
        compiler_params=_cparams(("parallel", "parallel", "arbitrary")),
        name="in_projection",
    )(x, shift, scale, w, cw, cb, cs)


def _hy_ffn_kernel(feat_ref, w1_ref, b1_ref, w2_ref, b2_ref, w3_ref, fr_ref, dec_ref, o_ref):
    hp = lax.Precision.HIGHEST
    feats = feat_ref[...]
    fr = fr_ref[...]
    h = jnp.sin(fr * (jnp.dot(feats, w1_ref[...], precision=hp, preferred_element_type=F32) + b1_ref[...]))
    h = jnp.sin(fr * (jnp.dot(h, w2_ref[...], precision=hp, preferred_element_type=F32) + b2_ref[...]))
    h = jnp.dot(h, w3_ref[...], precision=hp, preferred_element_type=F32)
    t = feats[:, 0:1]
    o_ref[...] = h * jnp.exp(-t * jnp.abs(dec_ref[...]))


def hyena_filter_taps(feats, w1p, b1, w2, b2, w3, freq, decay_cols):
    L = feats.shape[0]
    n = w3.shape[1]
    tl = min(L, 256)
    full = lambda a: pl.BlockSpec(a.shape, lambda i: (0, 0))
    return pl.pallas_call(
        _hy_ffn_kernel,
        out_shape=jax.ShapeDtypeStruct((L, n), F32),
        grid=(L // tl,),
        in_specs=[pl.BlockSpec((tl, feats.shape[1]), lambda i: (i, 0)),
                  full(w1p), full(b1), full(w2), full(b2), full(w3), full(freq), full(decay_cols)],
        out_specs=pl.BlockSpec((tl, n), lambda i: (i, 0)),
        compiler_params=_cparams(("parallel",)),
        name="hyena_filter_taps",
    )(feats, w1p, b1, w2, b2, w3, freq, decay_cols)


def _hy_spec_kernel(fp_ref, fq_ref, hf_ref, hb_ref, kp_ref, kq_ref, *, L):
    i = pl.program_id(1)
    hf = hf_ref[...]
    hb = hb_ref[...]
    n = lax.broadcasted_iota(jnp.int32, hb.shape, 0)
    hb = jnp.where(n == 0, 0.0, hb)
    s = (hf + hb).astype(BF16)
    dlt = (hf - hb).astype(BF16)
    kp = jnp.dot(fp_ref[...], s, preferred_element_type=F32)
    kq = jnp.dot(fq_ref[...], dlt, preferred_element_type=F32)
    sign = (1 - 2 * (n & 1)).astype(F32)
    corr = 2.0 * jnp.sum(hb * sign, axis=0, keepdims=True)
    f = lax.broadcasted_iota(jnp.int32, kq.shape, 0) + i * kq.shape[0]
    kq = kq + jnp.where(f == 0, corr, 0.0)
    kp_ref[0] = kp * (1.0 / L)
    kq_ref[0] = kq * (1.0 / L)


def hyena_filter_spectrum(fp, fq, taps, w):
    L = fp.shape[0]
    tf = min(L, 512)
    out = jax.ShapeDtypeStruct((HY_ORDER, L, w), F32)
    return pl.pallas_call(
        functools.partial(_hy_spec_kernel, L=L),
        out_shape=(out, out),
        grid=(HY_ORDER, L // tf),
        in_specs=[pl.BlockSpec((tf, L), lambda o, i: (i, 0)),
                  pl.BlockSpec((tf, L), lambda o, i: (i, 0)),
                  pl.BlockSpec((L, w), lambda o, i: (0, 2 * o)),
                  pl.BlockSpec((L, w), lambda o, i: (0, 2 * o + 1))],
        out_specs=(pl.BlockSpec((1, tf, w), lambda o, i: (o, i, 0)),
                   pl.BlockSpec((1, tf, w), lambda o, i: (o, i, 0))),
        compiler_params=_cparams(("parallel", "parallel")),
        name="hyena_filter_spectrum",
    )(fp, fq, taps, taps)


def _hy_fwd_kernel(fp_ref, fq_ref, u_ref, kp_ref, kq_ref, yp_ref, yq_ref):
    i = pl.program_id(1)
    u = u_ref[0]
    pu = jnp.dot(fp_ref[...], u, preferred_element_type=F32)
    qu = jnp.dot(fq_ref[...], u, preferred_element_type=F32)
    kp = kp_ref[0]
    kq = kq_ref[0]
    f = lax.broadcasted_iota(jnp.int32, pu.shape, 0) + i * pu.shape[0]
    dc = f == 0
    pp = pu * kp
    qq = qu * kq
    yp_ref[0] = jnp.where(dc, 0.5 * pp, pp - qq).astype(yp_ref.dtype)
    yq_ref[0] = jnp.where(dc, 0.5 * qq, pu * kq + qu * kp).astype(yq_ref.dtype)


def hyena_forward(fp, fq, u_arr, u_col, kp, kq, order):
    bv, L, _ = u_arr.shape
    w = kp.shape[2]
    tf = min(L, 512)
    out = jax.ShapeDtypeStruct((bv, L, w), BF16)
    return pl.pallas_call(
        _hy_fwd_kernel,
        out_shape=(out, out),
        grid=(bv, L // tf),
        in_specs=[pl.BlockSpec((tf, L), lambda b, i: (i, 0)),
                  pl.BlockSpec((tf, L), lambda b, i: (i, 0)),
                  pl.BlockSpec((1, L, w), lambda b, i: (b, 0, u_col)),
                  pl.BlockSpec((1, tf, w), lambda b, i: (order, i, 0)),
                  pl.BlockSpec((1, tf, w), lambda b, i: (order, i, 0))],
        out_specs=(pl.BlockSpec((1, tf, w), lambda b, i: (b, i, 0)),
                   pl.BlockSpec((1, tf, w), lambda b, i: (b, i, 0))),
        compiler_params=_cparams(("parallel", "parallel")),
        name="hyena_forward_dft",
    )(fp, fq, u_arr, kp, kq)


def _hy_inv_kernel(fpt_ref, fqt_ref, yp_ref, yq_ref, u_ref, g_ref, bias_ref, o_ref):
    y = jnp.dot(fpt_ref[...], yp_ref[0], preferred_element_type=F32)
    y = y + jnp.dot(fqt_ref[...], yq_ref[0], preferred_element_type=F32)
    u = u_ref[0].astype(F32)
    o_ref[0] = (g_ref[0].astype(F32) * (y + u * bias_ref[0])).astype(o_ref.dtype)


def hyena_inverse(fpt, fqt, yp, yq, u_arr, u_col, g_arr, g_col, bias, order):
    bv, L, w = yp.shape
    tt = min(L, 512)
    return pl.pallas_call(
        _hy_inv_kernel,
        out_shape=jax.ShapeDtypeStruct((bv, L, w), BF16),
        grid=(bv, L // tt),
        in_specs=[pl.BlockSpec((tt, L), lambda b, i: (i, 0)),
                  pl.BlockSpec((tt, L), lambda b, i: (i, 0)),
                  pl.BlockSpec((1, L, w), lambda b, i: (b, 0, 0)),
                  pl.BlockSpec((1, L, w), lambda b, i: (b, 0, 0)),
                  pl.BlockSpec((1, tt, w), lambda b, i: (b, i, u_col)),
                  pl.BlockSpec((1, tt, w), lambda b, i: (b, i, g_col)),
                  pl.BlockSpec((1, 1, w), lambda b, i: (order, 0, 0))],
        out_specs=pl.BlockSpec((1, tt, w), lambda b, i: (b, i, 0)),
        compiler_params=_cparams(("parallel", "parallel")),
        name="hyena_inverse_dft",
    )(fpt, fqt, yp, yq, u_arr, g_arr, bias)


def dft_tables(L):
    f = jnp.arange(L, dtype=jnp.int32)[:, None]
    n = jnp.arange(L, dtype=jnp.int32)[None, :]
    ang = ((f * n) % (2 * L)).astype(F32) * (math.pi / L)
    fp = jnp.cos(ang)
    nyq = (1 - 2 * (n % 2)).astype(F32)
    fq = jnp.where(f == 0, nyq, jnp.sin(ang))
    fp, fq = fp.astype(BF16), fq.astype(BF16)
    return fp, fq, fp.T, fq.T


def hyena_features(L):
    pos = jnp.arange(L, dtype=F32)
    t = pos / (L - 1)
    bands = jnp.linspace(1e-4, HY_BANDS - 1, HY_BANDS, dtype=F32)
    ang = (2.0 * math.pi / L) * pos[:, None] * bands[None, :]
    feats = jnp.concatenate([t[:, None], jnp.cos(ang), jnp.sin(ang)], axis=-1)
    return jnp.pad(feats, ((0, 0), (0, LANES - HY_FEAT)))


def hyena_branch(z_arr, cols, tables, feats, ffn, bias):
    fp, fq, fpt, fqt = tables
    w = bias.shape[-1]
    taps = hyena_filter_taps(feats, *ffn)
    kp, kq = hyena_filter_spectrum(fp, fq, taps, w)
    bias3 = bias.reshape(HY_ORDER, 1, w)
    yp, yq = hyena_forward(fp, fq, z_arr, cols[0], kp, kq, 0)
    y1 = hyena_inverse(fpt, fqt, yp, yq, z_arr, cols[0], z_arr, cols[1], bias3, 0)
    yp, yq = hyena_forward(fp, fq, y1, 0, kp, kq, 1)
    return hyena_inverse(fpt, fqt, yp, yq, y1, 0, z_arr, cols[2], bias3, 1)


def _log_sigmoid(x):
    return jnp.minimum(x, 0.0) - jnp.log(1.0 + jnp.exp(-jnp.abs(x)))


def _mlstm_chunk(q, k, v1, li_col, lf_col, li_row, lf_row, ct, m, reverse):
    T = q.shape[0]
    r = lax.broadcasted_iota(jnp.int32, (T, T), 0)
    s = lax.broadcasted_iota(jnp.int32, (T, T), 1)
    mask, mask_t = (s >= r, r >= s) if reverse else (s <= r, r <= s)
    b_col = jnp.sum(jnp.where(mask, lf_row, 0.0), axis=1, keepdims=True)
    b_row = jnp.sum(jnp.where(mask_t, lf_col, 0.0), axis=0, keepdims=True)
    d = jnp.where(mask, b_col - b_row + li_row, -1e30)
    g = b_col + m
    m_t = jnp.maximum(g, jnp.max(d, axis=-1, keepdims=True))
    w_inter = jnp.exp(g - m_t)
    qk = lax.dot_general(q, k, (((1,), (1,)), ((), ())), preferred_element_type=F32)
    smat = (qk * jnp.exp(d - m_t)).astype(BF16)
    num = w_inter * jnp.dot(q, ct.astype(BF16), preferred_element_type=F32)
    num = num + jnp.dot(smat, v1, preferred_element_type=F32)
    lane = lax.broadcasted_iota(jnp.int32, num.shape, 1)
    den = jnp.sum(jnp.where(lane == HEAD_PAD - 1, num, 0.0), axis=-1, keepdims=True)
    h = num / jnp.maximum(jnp.abs(den), jnp.exp(-m_t))
    btot = jnp.sum(lf_col, axis=0, keepdims=True)
    a = btot - b_col + li_col
    m_new = jnp.maximum(btot + m, jnp.max(a, axis=0, keepdims=True))
    wk = jnp.exp(a - m_new)
    dec = jnp.exp(btot + m - m_new)
    wkv = (wk * v1.astype(F32)).astype(BF16)
    ct_new = dec * ct + lax.dot_general(k, wkv, (((0,), (0,)), ((), ())), preferred_element_type=F32)
    return h, ct_new, m_new


def _mlstm_kernel(qc_ref, kc_ref, vc_ref, oc_ref, gcc_ref, grc_ref,
                  ql_ref, kl_ref, vl_ref, ol_ref, gcl_ref, grl_ref, ng_ref,
                  outc_ref, outl_ref, hf_ref, hb_ref, *, dh):
    T = ML_CHUNK
    L = ql_ref.shape[1]
    lane = lax.broadcasted_iota(jnp.int32, (T, HEAD_PAD), 1)
    ones_col = lane == HEAD_PAD - 1
    valid = (lane < dh).astype(F32)
    ng = ng_ref[...]

    def gates(gc, gr):
        return (gc[:, 0:1], _log_sigmoid(gc[:, 1:2]), gr[0:1, :], _log_sigmoid(gr[1:2, :]),
                gc[:, 2:3], _log_sigmoid(gc[:, 3:4]), gr[2:3, :], _log_sigmoid(gr[3:4, :]))

    def finish(h, o):
        h = h * valid
        mu = jnp.sum(h, axis=-1, keepdims=True) * (1.0 / dh)
        dlt = (h - mu) * valid
        var = jnp.sum(dlt * dlt, axis=-1, keepdims=True) * (1.0 / dh)
        return (dlt * lax.rsqrt(var + LN_EPS) * ng * o.astype(F32)).astype(BF16)

    def aug(v):
        return jnp.where(ones_col, jnp.ones_like(v), v)

    zero_c = jnp.zeros((HEAD_PAD, HEAD_PAD), F32)
    zero_m = jnp.zeros((1, 1), F32)

    q, k, v1 = qc_ref[0], kc_ref[0], aug(vc_ref[0])
    lif_c, lff_c, lif_r, lff_r, lib_c, lfb_c, lib_r, lfb_r = gates(gcc_ref[0, 0], grc_ref[0, 0])
    h_f, ct_f, m_f = _mlstm_chunk(q, k, v1, lif_c, lff_c, lif_r, lff_r, zero_c, zero_m, False)
    h_b, ct_b, m_b = _mlstm_chunk(q, k, v1, lib_c, lfb_c, lib_r, lfb_r, zero_c, zero_m, True)
    outc_ref[0] = finish(h_f + h_b, oc_ref[0])

    nc = L // T
    for c in range(nc):
        lo = c * T
        gc = gcl_ref[0, 0, lo:lo + T, :]
        gr = grl_ref[0, 0, :, lo:lo + T]
        lif_c, lff_c, lif_r, lff_r, _, _, _, _ = gates(gc, gr)
        h_f, ct_f, m_f = _mlstm_chunk(ql_ref[0, lo:lo + T, :], kl_ref[0, lo:lo + T, :],
                                      aug(vl_ref[0, lo:lo + T, :]),
                                      lif_c, lff_c, lif_r, lff_r, ct_f, m_f, False)
        hf_ref[lo:lo + T, :] = h_f
        lo = (nc - 1 - c) * T
        gc = gcl_ref[0, 0, lo:lo + T, :]
        gr = grl_ref[0, 0, :, lo:lo + T]
        _, _, _, _, lib_c, lfb_c, lib_r, lfb_r = gates(gc, gr)
        h_b, ct_b, m_b = _mlstm_chunk(ql_ref[0, lo:lo + T, :], kl_ref[0, lo:lo + T, :],
                                      aug(vl_ref[0, lo:lo + T, :]),
                                      lib_c, lfb_c, lib_r, lfb_r, ct_b, m_b, True)
        hb_ref[lo:lo + T, :] = h_b
    for c in range(nc):
        lo = c * T
        outl_ref[0, lo:lo + T, :] = finish(hf_ref[lo:lo + T, :] + hb_ref[lo:lo + T, :], ol_ref[0, lo:lo + T, :])


def mlstm_branch(qkvo_c, gate_c, qkvo_l, gate_l, norm_g_pad, dh):
    bv, lc, _ = qkvo_c.shape
    L = qkvo_l.shape[1]
    H = ML_HEADS
    assert lc == ML_CHUNK and L % ML_CHUNK == 0

    def per_head(g):
        g = g[:, :, :4 * H].reshape(bv, -1, 4, H)
        return g.transpose(0, 3, 1, 2), g.transpose(0, 3, 2, 1)

    gcc, grc = per_head(gate_c)
    gcl, grl = per_head(gate_l)

    def blk(length, part):
        return pl.BlockSpec((1, length, HEAD_PAD), lambda b, h: (b, 0, part * H + h))

    def gspecs(length):
        return [pl.BlockSpec((1, 1, length, 4), lambda b, h: (b, h, 0, 0)),
                pl.BlockSpec((1, 1, 4, length), lambda b, h: (b, h, 0, 0))]

    out_c = jax.ShapeDtypeStruct((bv, lc, H * HEAD_PAD), BF16)
    out_l = jax.ShapeDtypeStruct((bv, L, H * HEAD_PAD), BF16)
    return pl.pallas_call(
        functools.partial(_mlstm_kernel, dh=dh),
        out_shape=(out_c, out_l),
        grid=(bv, H),
        in_specs=([blk(lc, p) for p in range(4)] + gspecs(lc) + [blk(L, p) for p in range(4)] + gspecs(L)
                  + [pl.BlockSpec((1, HEAD_PAD), lambda b, h: (0, h))]),
        out_specs=(pl.BlockSpec((1, lc, HEAD_PAD), lambda b, h: (b, 0, h)),
                   pl.BlockSpec((1, L, HEAD_PAD), lambda b, h: (b, 0, h))),
        scratch_shapes=[pltpu.VMEM((L, HEAD_PAD), F32), pltpu.VMEM((L, HEAD_PAD), F32)],
        compiler_params=_cparams(("parallel", "parallel")),
        name="mlstm_scan",
    )(qkvo_c, qkvo_c, qkvo_c, qkvo_c, gcc, grc, qkvo_l, qkvo_l, qkvo_l, qkvo_l, gcl, grl, norm_g_pad)


def _s5_kernel(u_ref, are_ref, aim_ref, ldt_ref, bre_ref, bim_ref, cre_ref, cim_ref, y_ref,
               wb_ref, wc_ref, a_ref, st_ref, v_ref, *, nb, tt, ns):
    d = pl.program_id(0)
    tb = pl.program_id(2)

    @pl.when(tb == 0)
    def _():
        a_re, a_im = are_ref[0, 0], aim_ref[0, 0]
        dt = jnp.exp(ldt_ref[0, 0])
        mag = jnp.exp(dt * a_re)
        ab_re, ab_im = mag * jnp.cos(dt * a_im), mag * jnp.sin(dt * a_im)
        den = a_re * a_re + a_im * a_im
        co_re = ((ab_re - 1.0) * a_re + ab_im * a_im) / den
        co_im = (ab_im * a_re - (ab_re - 1.0) * a_im) / den
        b_re, b_im = bre_ref[0, 0], bim_ref[0, 0]
        wb_ref[:, :ns] = (co_re * b_re - co_im * b_im).astype(BF16)
        wb_ref[:, ns:] = (co_re * b_im + co_im * b_re).astype(BF16)
        wc_ref[:ns, :] = cre_ref[0, 0].astype(BF16)
        wc_ref[ns:, :] = (-cim_ref[0, 0]).astype(BF16)
        a_ref[0] = jnp.broadcast_to(ab_re, (nb, ns))
        a_ref[1] = jnp.broadcast_to(ab_im, (nb, ns))
        st_ref[...] = jnp.zeros_like(st_ref)

    u = u_ref[...].reshape(tt * nb, u_ref.shape[2])
    v_ref[...] = jnp.dot(u, wb_ref[...], preferred_element_type=F32)
    a_re, a_im = a_ref[0], a_ref[1]

    def step(t, carry):
        x_re, x_im = carry
        te = jnp.where(d == 0, t, tt - 1 - t)
        off = pl.multiple_of(te * nb, nb)
        n_re = a_re * x_re - a_im * x_im + v_ref[pl.ds(off, nb), :ns]
        n_im = a_re * x_im + a_im * x_re + v_ref[pl.ds(off, nb), ns:]
        v_ref[pl.ds(off, nb), :ns] = n_re
        v_ref[pl.ds(off, nb), ns:] = n_im
        return n_re, n_im

    x_re, x_im = lax.fori_loop(0, tt, step, (st_ref[0], st_ref[1]), unroll=8)
    st_ref[0] = x_re
    st_ref[1] = x_im
    y = jnp.dot(v_ref[...].astype(BF16), wc_ref[...], preferred_element_type=F32)
    y_ref[0] = y.astype(y_ref.dtype).reshape(y_ref.shape[1:])


def s5_scan(u_tm, a_re, a_im, log_dt, b_re, b_im, c_re, c_im, ctx_len):
    lt, nb, w = u_tm.shape
    G = a_re.shape[1]
    P, CG = S5_STATE, S5_GROUP
    gl = S5_LANE_BLOCK // CG
    nj = G // gl
    ns = gl * P
    tt = S5_TIME_BLOCK
    nctx, ntot = ctx_len // tt, lt // tt

    eye = jnp.eye(gl, dtype=F32)

    def rows(a):
        return a.astype(F32).reshape(2, nj, 1, ns)

    def bdiag_in(b):
        b = b.astype(F32).reshape(2, nj, gl, P, CG)
        return jnp.einsum("djgpc,gh->djgchp", b, eye).reshape(2, nj, gl * CG, ns)

    def bdiag_out(c):
        c = c.astype(F32).reshape(2, nj, gl, CG, P)
        return jnp.einsum("djgcp,gh->djgphc", c, eye).reshape(2, nj, ns, gl * CG)

    ldt = jnp.broadcast_to(log_dt.astype(F32)[:, :, None], (2, G, P))

    def tmap(d, j, tb):
        bwd = jnp.where(tb < nctx, nctx - 1 - tb, ntot + nctx - 1 - tb)
        return jnp.where(d == 0, tb, bwd)

    pspec = lambda r, c: pl.BlockSpec((1, 1, r, c), lambda d, j, tb: (d, j, 0, 0))
    return pl.pallas_call(
        functools.partial(_s5_kernel, nb=nb, tt=tt, ns=ns),
        out_shape=jax.ShapeDtypeStruct((2, lt, nb, w), BF16),
        grid=(2, nj, ntot),
        in_specs=[pl.BlockSpec((tt, nb, S5_LANE_BLOCK), lambda d, j, tb: (tmap(d, j, tb), 0, j)),
                  pspec(1, ns), pspec(1, ns), pspec(1, ns),
                  pspec(gl * CG, ns), pspec(gl * CG, ns), pspec(ns, gl * CG), pspec(ns, gl * CG)],
        out_specs=pl.BlockSpec((1, tt, nb, S5_LANE_BLOCK), lambda d, j, tb: (d, tmap(d, j, tb), 0, j)),
        scratch_shapes=[pltpu.VMEM((gl * CG, 2 * ns), BF16), pltpu.VMEM((2 * ns, gl * CG), BF16),
                        pltpu.VMEM((2, nb, ns), F32), pltpu.VMEM((2, nb, ns), F32),
                        pltpu.VMEM((tt * nb, 2 * ns), F32)],
        compiler_params=_cparams(("parallel", "parallel", "arbitrary")),
        name="s5_scan",
    )(u_tm, rows(a_re), rows(a_im), rows(ldt), bdiag_in(b_re), bdiag_in(b_im), bdiag_out(c_re), bdiag_out(c_im))


def _gelu_tanh(x):
    return 0.5 * x * (1.0 + jnp.tanh(math.sqrt(2.0 / math.pi) * (x + 0.044715 * (x * x * x))))


def _merge_kernel(yhy_ref, yml_ref, ysf_ref, ysb_ref, u_ref, ghy_ref, gml_ref, gs5_ref,
                  d_ref, gw_ref, gb_ref, why_ref, wml_ref, ws5_ref, o_ref):
    ys = ysf_ref[0].astype(F32) + ysb_ref[0].astype(F32) + d_ref[...] * u_ref[...].astype(F32)
    z = _gelu_tanh(ys).astype(BF16)
    glu = jnp.dot(z, gw_ref[...], preferred_element_type=F32) + gb_ref[...]
    y_s5 = (z.astype(F32) * _sigmoid(glu)).astype(BF16)
    acc = ghy_ref[0].astype(F32) * jnp.dot(yhy_ref[0], why_ref[...], preferred_element_type=F32)
    acc = acc + gml_ref[0].astype(F32) * jnp.dot(yml_ref[0], wml_ref[...], preferred_element_type=F32)
    acc = acc + gs5_ref[0].astype(F32) * jnp.dot(y_s5, ws5_ref[...], preferred_element_type=F32)
    o_ref[0] = acc.astype(o_ref.dtype)


def merge_branches(y_hy, y_ml, ys, u_tm2, t_off, gate_arr, s5_d, glu_w, glu_b, w_hy, w_ml, w_s5):
    bv, lv, w = y_hy.shape
    dm = w_hy.shape[1]
    tm = ML_CHUNK
    full = lambda a: pl.BlockSpec(a.shape, lambda b, i: (0,) * a.ndim)
    gate = lambda c: pl.BlockSpec((1, tm, dm), lambda b, i: (b, i, c))
    return pl.pallas_call(
        _merge_kernel,
        out_shape=jax.ShapeDtypeStruct((bv, lv, dm), BF16),
        grid=(bv, lv // tm),
        in_specs=[pl.BlockSpec((1, tm, w), lambda b, i: (b, i, 0)),
                  pl.BlockSpec((1, tm, y_ml.shape[2]), lambda b, i: (b, i, 0)),
                  pl.BlockSpec((1, tm, w), lambda b, i: (0, i + t_off, b)),
                  pl.BlockSpec((1, tm, w), lambda b, i: (1, i + t_off, b)),
                  pl.BlockSpec((tm, w), lambda b, i: (i + t_off, b)),
                  gate(0), gate(1), gate(2),
                  full(s5_d), full(glu_w), full(glu_b), full(w_hy), full(w_ml), full(w_s5)],
        out_specs=pl.BlockSpec((1, tm, dm), lambda b, i: (b, i, 0)),
        compiler_params=_cparams(("parallel", "parallel")),
        name="merge_branches",
    )(y_hy, y_ml, ys, ys, u_tm2, gate_arr, gate_arr, gate_arr, s5_d, glu_w, glu_b, w_hy, w_ml, w_s5)


def _outproj_kernel(m_ref, w_ref, x_ref, gate_ref, g_ref, b_ref, o_ref, *, alpha):
    y = jnp.dot(m_ref[0], w_ref[...], preferred_element_type=F32)
    r = alpha * x_ref[0] + gate_ref[0] * y
    o_ref[0] = _layer_norm(r, g_ref[...], b_ref[...])


def out_projection(merged, w_out, x, gate, ln_g, ln_b, alpha, tm):
    bv, lv, d = x.shape
    bm = gate.shape[0]
    mod_map = (lambda b, i: (b, 0, 0)) if bm == bv else (lambda b, i: (0, 0, 0))
    vec = lambda: pl.BlockSpec((1, d), lambda b, i: (0, 0))
    return pl.pallas_call(
        functools.partial(_outproj_kernel, alpha=alpha),
        out_shape=jax.ShapeDtypeStruct((bv, lv, d), F32),
        grid=(bv, lv // tm),
        in_specs=[pl.BlockSpec((1, tm, d), lambda b, i: (b, i, 0)),
                  pl.BlockSpec((d, d), lambda b, i: (0, 0)),
                  pl.BlockSpec((1, tm, d), lambda b, i: (b, i, 0)),
                  pl.BlockSpec((1, 1, d), mod_map), vec(), vec()],
        out_specs=pl.BlockSpec((1, tm, d), lambda b, i: (b, i, 0)),
        compiler_params=_cparams(("parallel", "parallel")),
        name="out_projection_ln",
    )(merged, w_out, x, gate, ln_g, ln_b)


def _mlp_kernel(x_ref, sh_ref, sc_ref, gate_ref, w1_ref, w2_ref, g_ref, b_ref, o_ref, xm_ref, acc_ref, *, alpha):
    k = pl.program_id(2)

    @pl.when(k == 0)
    def _():
        xm_ref[...] = (x_ref[0] * (1.0 + sc_ref[0]) + sh_ref[0]).astype(BF16)
        acc_ref[...] = jnp.zeros_like(acc_ref)

    h = jnp.maximum(jnp.dot(xm_ref[...], w1_ref[...], preferred_element_type=F32), 0.0)
    acc_ref[...] += jnp.dot((h * h).astype(BF16), w2_ref[...], preferred_element_type=F32)

    @pl.when(k == pl.num_programs(2) - 1)
    def _():
        r = alpha * x_ref[0] + gate_ref[0] * acc_ref[...]
        o_ref[0] = _layer_norm(r, g_ref[...], b_ref[...])


def mlp_block(x, shift, scale, gate, w1, w2, ln_g, ln_b, alpha, tm, kf):
    bv, lv, d = x.shape
    dff = w1.shape[1]
    bm = gate.shape[0]
    mod_map = (lambda b, i, k: (b, 0, 0)) if bm == bv else (lambda b, i, k: (0, 0, 0))
    vec = lambda: pl.BlockSpec((1, d), lambda b, i, k: (0, 0))
    return pl.pallas_call(
        functools.partial(_mlp_kernel, alpha=alpha),
        out_shape=jax.ShapeDtypeStruct((bv, lv, d), F32),
        grid=(bv, lv // tm, dff // kf),
        in_specs=[pl.BlockSpec((1, tm, d), lambda b, i, k: (b, i, 0)),
                  pl.BlockSpec((1, 1, d), mod_map), pl.BlockSpec((1, 1, d), mod_map),
                  pl.BlockSpec((1, 1, d), mod_map),
                  pl.BlockSpec((d, kf), lambda b, i, k: (0, k)),
                  pl.BlockSpec((kf, d), lambda b, i, k: (k, 0)),
                  vec(), vec()],
        out_specs=pl.BlockSpec((1, tm, d), lambda b, i, k: (b, i, 0)),
        scratch_shapes=[pltpu.VMEM((tm, d), BF16), pltpu.VMEM((tm, d), F32)],
        compiler_params=_cparams(("parallel", "parallel", "arbitrary")),
        name="mlp_ln",
    )(x, shift, scale, gate, w1, w2, ln_g, ln_b)


def _pad_heads(a, axis, dh):
    shp = a.shape
    a = a.reshape(shp[:axis] + (ML_HEADS, dh) + shp[axis + 1:])
    pad = [(0, 0)] * a.ndim
    pad[axis + 1] = (0, HEAD_PAD - dh)
    a = jnp.pad(a, pad)
    return a.reshape(shp[:axis] + (ML_HEADS * HEAD_PAD,) + shp[axis + 1:])


def _layer_params(l, p, w_hy, w_ml, w_s5, d_model):
    dh = w_ml // ML_HEADS
    sizes = ((HY_ORDER + 1) * w_hy, 2 * w_ml, w_ml, w_ml, 4 * ML_HEADS, w_s5, N_BRANCH * d_model)
    pts = [0]
    for s in sizes:
        pts.append(pts[-1] + s)
    w_in = p["w_in"][l]
    hy, qk, v, o, gt, u, mg = (w_in[:, pts[i]:pts[i + 1]] for i in range(7))
    out = {}
    out["wa"] = jnp.concatenate([mg, hy], axis=1).astype(BF16)
    nmg = mg.shape[1]
    out["cwa"] = jnp.pad(p["hy_conv_w"][l], ((0, 0), (nmg, 0)))
    out["cba"] = jnp.pad(p["hy_conv_b"][l], ((nmg, 0),)).reshape(1, -1)
    out["csa"] = jnp.ones_like(out["cba"])
    ph = lambda a: _pad_heads(a, a.ndim - 1, dh)
    out["wb"] = jnp.concatenate([ph(qk[:, :w_ml]), ph(qk[:, w_ml:]), ph(v), ph(o)], axis=1).astype(BF16)
    nqk = 2 * ML_HEADS * HEAD_PAD
    cw = p["ml_conv_w"][l]
    cb = p["ml_conv_b"][l]
    out["cwb"] = jnp.pad(jnp.concatenate([ph(cw[:, :w_ml]), ph(cw[:, w_ml:])], axis=1), ((0, 0), (0, nqk)))
    out["cbb"] = jnp.pad(jnp.concatenate([ph(cb[:w_ml]), ph(cb[w_ml:])]), ((0, nqk),)).reshape(1, -1)
    half = ML_HEADS * HEAD_PAD
    out["csb"] = jnp.concatenate([jnp.ones((half,), F32), jnp.full((half,), dh ** -0.5, F32),
                                  jnp.ones((nqk,), F32)]).reshape(1, -1)
    out["wc"] = u.astype(BF16)
    ngt = gt.shape[1]
    out["wd"] = jnp.pad(gt, ((0, 0), (0, LANES - ngt))).astype(BF16)
    out["cbd"] = jnp.pad(p["ml_gate_b"][l].reshape(-1), ((0, LANES - ngt),)).reshape(1, -1)
    out["norm_g"] = ph(p["ml_norm_g"][l]).reshape(1, -1)
    out["w_ml_out"] = _pad_heads(p["w_ml_out"][l], 0, dh).astype(BF16)
    w3 = p["hy_ffn_w3"][l]
    dec = p["hy_decay"][l]
    dec_cols = jnp.broadcast_to(dec[:, None, :], (HY_ORDER, 2, w_hy)).reshape(1, -1)
    hpad = LANES - HY_HIDDEN
    row = lambda a: jnp.pad(a.reshape(1, -1), ((0, 0), (0, hpad)))
    out["hy_ffn"] = (jnp.pad(p["hy_ffn_w1"][l], ((0, LANES - HY_FEAT), (0, hpad))), row(p["hy_ffn_b1"][l]),
                     jnp.pad(p["hy_ffn_w2"][l], ((0, hpad), (0, hpad))), row(p["hy_ffn_b2"][l]),
                     jnp.pad(w3, ((0, hpad), (0, 0))), row(p["hy_sin_freq"][l]), dec_cols)
    return out


def _zeros_like_cols(n):
    return jnp.zeros((3, n), F32), jnp.zeros((1, n), F32), jnp.ones((1, n), F32)


def kernel(x, c, ctx, c_ctx, w_mod, b_mod, w_in, hy_conv_w, hy_conv_b, hy_ffn_w1, hy_ffn_b1, hy_ffn_w2,
           hy_ffn_b2, hy_ffn_w3, hy_sin_freq, hy_decay, hy_bias, ml_conv_w, ml_conv_b, ml_gate_b, ml_norm_g,
           s5_a_re, s5_a_im, s5_log_dt, s5_b_re, s5_b_im, s5_c_re, s5_c_im, s5_d, s5_glu_w, s5_glu_b,
           w_hy_out, w_ml_out, w_s5_out, w_out, ln1_g, ln1_b, ln2_g, ln2_b, w_ff1, w_ff2):
    p = dict(w_in=w_in, hy_conv_w=hy_conv_w, hy_conv_b=hy_conv_b, hy_ffn_w1=hy_ffn_w1, hy_ffn_b1=hy_ffn_b1,
             hy_ffn_w2=hy_ffn_w2, hy_ffn_b2=hy_ffn_b2, hy_ffn_w3=hy_ffn_w3, hy_sin_freq=hy_sin_freq,
             hy_decay=hy_decay, ml_conv_w=ml_conv_w, ml_conv_b=ml_conv_b, ml_gate_b=ml_gate_b,
             ml_norm_g=ml_norm_g, w_ml_out=w_ml_out)
    B, L, D = x.shape
    LC = ctx.shape[1]
    depth = w_in.shape[0]
    w_hy = hy_bias.shape[-1]
    w_ml = ml_norm_g.shape[-1]
    w_s5 = s5_d.shape[-1]
    dh = w_ml // ML_HEADS
    alpha = (2 * depth) ** 0.25
    nmg = N_BRANCH * D
    hy_col0 = nmg // w_hy

    tab_l, tab_c = dft_tables(L), dft_tables(LC)
    feats_l, feats_c = hyena_features(L), hyena_features(LC)

    mrows = 8 * ((B + 1 + 7) // 8)
    c_rows = jnp.zeros((mrows, D), F32).at[:B].set(c).at[B].set(c_ctx)

    for l in range(depth):
        need_ctx = l < depth - 1
        lp = _layer_params(l, p, w_hy, w_ml, w_s5, D)
        mod = mod_vectors(c_rows, w_mod[l], b_mod[l])
        mod_l = [mod[:B, k * D:(k + 1) * D].reshape(B, 1, D) for k in range(6)]
        mod_c = [mod[B:B + 1, k * D:(k + 1) * D].reshape(1, 1, D) for k in range(6)]

        kinds_a = ["sigmoid"] * (nmg // w_hy) + ["conv"] * (HY_ORDER + 1)
        kinds_b = ["convsilu", "convsilu", "plain", "sigmoid"]
        tnb = ML_HEADS * HEAD_PAD
        zc, zb, zs = _zeros_like_cols(w_s5)
        zcd, _, zsd = _zeros_like_cols(LANES)

        def project(xv, sh, sc, row_len, tm):
            a = in_projection(xv, sh, sc, lp["wa"], lp["cwa"], lp["cba"], lp["csa"],
                              kinds=kinds_a, tn=w_hy, tm=tm, row_len=row_len)
            b = in_projection(xv, sh, sc, lp["wb"], lp["cwb"], lp["cbb"], lp["csb"],
                              kinds=kinds_b, tn=tnb, tm=tm, row_len=row_len)
            u = in_projection(xv, sh, sc, lp["wc"], zc, zb, zs,
                              kinds=["plain"], tn=w_s5, tm=tm, row_len=row_len, time_major=True)
            g = in_projection(xv, sh, sc, lp["wd"], zcd, lp["cbd"], zsd,
                              kinds=["bias"], tn=LANES, tm=tm, row_len=row_len, out_dtype=F32)
            return a, b, u, g

        a_l, b_l, u_l, g_l = project(x, mod_l[0], mod_l[1], GRID_W, 1024)
        a_c, b_c, u_c, g_c = project(ctx, mod_c[0], mod_c[1], LC, LC)

        hm_c, hm_l = mlstm_branch(b_c, g_c, b_l, g_l, lp["norm_g"], dh)

        u_tm2 = jnp.concatenate([u_c, u_l], axis=0)
        ys = s5_scan(u_tm2.reshape(LC + L, B, w_s5), s5_a_re[l], s5_a_im[l], s5_log_dt[l],
                     s5_b_re[l], s5_b_im[l], s5_c_re[l], s5_c_im[l], LC)
        ys = ys.reshape(2, LC + L, B * w_s5)

        s5_dv = s5_d[l].reshape(1, -1)
        glu_w = s5_glu_w[l].astype(BF16)
        glu_b = s5_glu_b[l].reshape(1, -1)
        w_hy_o = w_hy_out[l].astype(BF16)
        w_s5_o = w_s5_out[l].astype(BF16)
        w_o = w_out[l].astype(BF16)
        w1 = w_ff1[l].astype(BF16)
        w2 = w_ff2[l].astype(BF16)
        g1, b1 = ln1_g[l].reshape(1, -1), ln1_b[l].reshape(1, -1)
        g2, b2 = ln2_g[l].reshape(1, -1), ln2_b[l].reshape(1, -1)
        cols = (hy_col0, hy_col0 + 1, hy_col0 + 2)

        def finish(xv, a, hm, t_off, tables, feats, md, tm):
            y_hy = hyena_branch(a, cols, tables, feats, lp["hy_ffn"], hy_bias[l])
            merged = merge_branches(y_hy, hm, ys, u_tm2, t_off, a, s5_dv, glu_w, glu_b,
                                    w_hy_o, lp["w_ml_out"], w_s5_o)
            x1 = out_projection(merged, w_o, xv, md[2], g1, b1, alpha, tm)
            return mlp_block(x1, md[3], md[4], md[5], w1, w2, g2, b2, alpha, tm, 512)

        x = finish(x, a_l, hm_l, LC // ML_CHUNK, tab_l, feats_l, mod_l, 512)
        if need_ctx:
            ctx = finish(ctx, a_c, hm_c, 0, tab_c, feats_c, mod_c, LC)
    return x
```

```python
import functools
import math

import jax
import jax.numpy as jnp
from jax import lax
from jax.experimental import pallas as pl
from jax.experimental.pallas import tpu as pltpu

F32 = jnp.float32
BF16 = jnp.bfloat16

GRID_W = 64
HY_ORDER = 2
HY_BANDS = 16
HY_FEAT = 1 + 2 * HY_BANDS
HY_HIDDEN = 64
ML_HEADS = 4
ML_CHUNK = 256
S5_GROUP = 16
S5_STATE = 64
N_BRANCH = 3
LN_EPS = 1e-5

LANES = 128
MXU_DIM = 256
VMEM_LIMIT = 56 * 1024 * 1024

HEAD_PAD = MXU_DIM
S5_LANE_BLOCK = LANES
S5_TIME_BLOCK = 64


def _cparams(sem):
    return pltpu.CompilerParams(dimension_semantics=sem, vmem_limit_bytes=VMEM_LIMIT)


def _sigmoid(x):
    return 1.0 / (1.0 + jnp.exp(-x))


def _layer_norm(r, g, b):
    mu = jnp.mean(r, axis=-1, keepdims=True)
    d = r - mu
    var = jnp.mean(d * d, axis=-1, keepdims=True)
    return d * lax.rsqrt(var + LN_EPS) * g + b


def _mod_kernel(c_ref, w_ref, b_ref, o_ref):
    c = c_ref[...]
    a = (c * _sigmoid(c)).astype(BF16)
    o_ref[...] = jnp.dot(a, w_ref[...].astype(BF16), preferred_element_type=F32) + b_ref[...]


def mod_vectors(c_rows, w, b):
    m, d = c_rows.shape
    n = w.shape[1]
    tn = 1024
    return pl.pallas_call(
        _mod_kernel,
        out_shape=jax.ShapeDtypeStruct((m, n), F32),
        grid=(n // tn,),
        in_specs=[pl.BlockSpec((m, d), lambda j: (0, 0)),
                  pl.BlockSpec((d, tn), lambda j: (0, j)),
                  pl.BlockSpec((1, tn), lambda j: (0, j))],
        out_specs=pl.BlockSpec((m, tn), lambda j: (0, j)),
        compiler_params=_cparams(("parallel",)),
        name="mod_vectors",
    )(c_rows, w, b.reshape(1, n))


def _short_conv(acc, w, b, row_len):
    rows = acc.shape[0]
    t = lax.broadcasted_iota(jnp.int32, acc.shape, 0) & (row_len - 1)
    prev = jnp.where(t == 0, 0.0, pltpu.roll(acc, 1, 0))
    nxt = jnp.where(t == row_len - 1, 0.0, pltpu.roll(acc, rows - 1, 0))
    return prev * w[0:1] + acc * w[1:2] + nxt * w[2:3] + b


def _modulate_into(xm_ref, x_ref, sh_ref, sc_ref):
    xm_ref[...] = (x_ref[0] * (1.0 + sc_ref[0]) + sh_ref[0]).astype(BF16)


def _proj_kernel(x_ref, sh_ref, sc_ref, w_ref, cw_ref, cb_ref, cs_ref, o_ref, xm_ref, *, kind, row_len, rb,
                 transposed):
    @pl.when(pl.program_id(2) == 0)
    def _():
        _modulate_into(xm_ref, x_ref, sh_ref, sc_ref)

    def epilogue(acc):
        if kind == "conv":
            return _short_conv(acc, cw_ref[...], cb_ref[...], row_len)
        if kind == "convsilu":
            y = _short_conv(acc, cw_ref[...], cb_ref[...], row_len)
            return y * _sigmoid(y) * cs_ref[...]
        if kind == "sigmoid":
            return _sigmoid(acc)
        return acc

    for r in range(xm_ref.shape[0] // rb):
        xs = xm_ref[r * rb:(r + 1) * rb, :]
        if transposed:
            acc = lax.dot_general(w_ref[...], xs, (((1,), (1,)), ((), ())), preferred_element_type=F32)
            o_ref[0, :, r * rb:(r + 1) * rb] = epilogue(acc).astype(o_ref.dtype)
        else:
            acc = jnp.dot(xs, w_ref[...], preferred_element_type=F32)
            o_ref[0, r * rb:(r + 1) * rb, :] = epilogue(acc).astype(o_ref.dtype)


def projection(x, shift, scale, w, cw, cb, cs, *, kind, tn, tm, row_len, transposed=False):
    bv, lv, d = x.shape
    n = w.shape[0] if transposed else w.shape[1]
    rb = min(tm, ML_CHUNK)
    assert n % tn == 0 and lv % tm == 0 and rb % row_len == 0
    bm = shift.shape[0]
    mod_map = (lambda b, i, j: (b, 0, 0)) if bm == bv else (lambda b, i, j: (0, 0, 0))
    if transposed:
        assert kind in ("plain", "sigmoid")
        w_spec = pl.BlockSpec((tn, d), lambda b, i, j: (j, 0))
        out_shape = jax.ShapeDtypeStruct((bv, n, lv), BF16)
        out_spec = pl.BlockSpec((1, tn, tm), lambda b, i, j: (b, j, i))
    else:
        w_spec = pl.BlockSpec((d, tn), lambda b, i, j: (0, j))
        out_shape = jax.ShapeDtypeStruct((bv, lv, n), BF16)
        out_spec = pl.BlockSpec((1, tm, tn), lambda b, i, j: (b, i, j))
    col = lambda rows: pl.BlockSpec((rows, tn), lambda b, i, j: (0, j))
    return pl.pallas_call(
        functools.partial(_proj_kernel, kind=kind, row_len=row_len, rb=rb, transposed=transposed),
        out_shape=out_shape,
        grid=(bv, lv // tm, n // tn),
        in_specs=[pl.BlockSpec((1, tm, d), lambda b, i, j: (b, i, 0)),
                  pl.BlockSpec((1, 1, d), mod_map), pl.BlockSpec((1, 1, d), mod_map),
                  w_spec, col(3), col(1), col(1)],
        out_specs=out_spec,
        scratch_shapes=[pltpu.VMEM((tm, d), BF16)],
        compiler_params=_cparams(("parallel", "parallel", "arbitrary")),
        name="projection_" + kind + ("_t" if transposed else ""),
    )(x, shift, scale, w, cw, cb, cs)


def _log_sigmoid(x):
    return jnp.minimum(x, 0.0) - jnp.log(1.0 + jnp.exp(-jnp.abs(x)))


def _ugate_kernel(x_ref, sh_ref, sc_ref, wu_ref, wg_ref, gb_ref, u_ref, g_ref, xm_ref):
    _modulate_into(xm_ref, x_ref, sh_ref, sc_ref)
    T = ML_CHUNK
    hp = lax.Precision.HIGHEST
    r = lax.broadcasted_iota(jnp.int32, (T, T), 0)
    s = lax.broadcasted_iota(jnp.int32, (T, T), 1)
    tri_f = (s <= r).astype(F32)
    tri_b = (s >= r).astype(F32)
    kind = lax.broadcasted_iota(jnp.int32, (T, LANES), 1) // ML_HEADS
    for c in range(xm_ref.shape[0] // T):
        xs = xm_ref[c * T:(c + 1) * T, :]
        u_ref[c * T:(c + 1) * T, :] = jnp.dot(xs, wu_ref[...], preferred_element_type=F32).astype(u_ref.dtype)
        g = jnp.dot(xs, wg_ref[...], preferred_element_type=F32) + gb_ref[...]
        ls = _log_sigmoid(g)
        cum_f = jnp.dot(tri_f, ls, precision=hp, preferred_element_type=F32)
        cum_b = jnp.dot(tri_b, ls, precision=hp, preferred_element_type=F32)
        g_ref[0, c * T:(c + 1) * T, :] = jnp.where(kind == 1, cum_f, jnp.where(kind == 3, cum_b, g))


def u_and_gates(x, shift, scale, wu, wg, gb, *, tm):
    bv, lv, d = x.shape
    w = wu.shape[1]
    assert lv % tm == 0 and tm % ML_CHUNK == 0
    bm = shift.shape[0]
    mod_map = (lambda b, i: (b, 0, 0)) if bm == bv else (lambda b, i: (0, 0, 0))
    full = lambda a: pl.BlockSpec(a.shape, lambda b, i: (0, 0))
    return pl.pallas_call(
        _ugate_kernel,
        out_shape=(jax.ShapeDtypeStruct((lv, bv * w), BF16), jax.ShapeDtypeStruct((bv, lv, LANES), F32)),
        grid=(bv, lv // tm),
        in_specs=[pl.BlockSpec((1, tm, d), lambda b, i: (b, i, 0)),
                  pl.BlockSpec((1, 1, d), mod_map), pl.BlockSpec((1, 1, d), mod_map),
                  full(wu), full(wg), full(gb)],
        out_specs=(pl.BlockSpec((tm, w), lambda b, i: (i, b)),
                   pl.BlockSpec((1, tm, LANES), lambda b, i: (b, i, 0))),
        scratch_shapes=[pltpu.VMEM((tm, d), BF16)],
        compiler_params=_cparams(("parallel", "parallel")),
        name="u_and_gates",
    )(x, shift, scale, wu, wg, gb)


def _hy_ffn_kernel(feat_ref, w1_ref, b1_ref, w2_ref, b2_ref, w3_ref, fr_ref, dec_ref, o_ref):
    hp = lax.Precision.HIGHEST
    feats = feat_ref[...]
    fr = fr_ref[...]
    h = jnp.sin(fr * (jnp.dot(feats, w1_ref[...], precision=hp, preferred_element_type=F32) + b1_ref[...]))
    h = jnp.sin(fr * (jnp.dot(h, w2_ref[...], precision=hp, preferred_element_type=F32) + b2_ref[...]))
    h = jnp.dot(h, w3_ref[...], precision=hp, preferred_element_type=F32)
    t = feats[:, 0:1]
    o_ref[...] = h * jnp.exp(-t * jnp.abs(dec_ref[...]))


def hyena_filter_taps(feats, w1p, b1, w2, b2, w3, freq, decay_cols):
    L = feats.shape[0]
    n = w3.shape[1]
    tl = min(L, 256)
    full = lambda a: pl.BlockSpec(a.shape, lambda i: (0, 0))
    return pl.pallas_call(
        _hy_ffn_kernel,
        out_shape=jax.ShapeDtypeStruct((L, n), F32),
        grid=(L // tl,),
        in_specs=[pl.BlockSpec((tl, feats.shape[1]), lambda i: (i, 0)),
                  full(w1p), full(b1), full(w2), full(b2), full(w3), full(freq), full(decay_cols)],
        out_specs=pl.BlockSpec((tl, n), lambda i: (i, 0)),
        compiler_params=_cparams(("parallel",)),
        name="hyena_filter_taps",
    )(feats, w1p, b1, w2, b2, w3, freq, decay_cols)


def _hy_spec_kernel(fp_ref, fq_ref, hf_ref, hb_ref, kp_ref, kq_ref, *, L):
    i = pl.program_id(1)
    hf = hf_ref[...]
    hb = hb_ref[...]
    n = lax.broadcasted_iota(jnp.int32, hb.shape, 0)
    hb = jnp.where(n == 0, 0.0, hb)
    s = (hf + hb).astype(BF16)
    dlt = (hf - hb).astype(BF16)
    kp = jnp.dot(fp_ref[...], s, preferred_element_type=F32)
    kq = jnp.dot(fq_ref[...], dlt, preferred_element_type=F32)
    sign = (1 - 2 * (n & 1)).astype(F32)
    corr = 2.0 * jnp.sum(hb * sign, axis=0, keepdims=True)
    f = lax.broadcasted_iota(jnp.int32, kq.shape, 0) + i * kq.shape[0]
    kq = kq + jnp.where(f == 0, corr, 0.0)
    kp_ref[0] = kp * (1.0 / L)
    kq_ref[0] = kq * (1.0 / L)


def hyena_filter_spectrum(fp, fq, taps, w):
    L = fp.shape[0]
    tf = min(L, 512)
    out = jax.ShapeDtypeStruct((HY_ORDER, L, w), F32)
    return pl.pallas_call(
        functools.partial(_hy_spec_kernel, L=L),
        out_shape=(out, out),
        grid=(HY_ORDER, L // tf),
        in_specs=[pl.BlockSpec((tf, L), lambda o, i: (i, 0)),
                  pl.BlockSpec((tf, L), lambda o, i: (i, 0)),
                  pl.BlockSpec((L, w), lambda o, i: (0, 2 * o)),
                  pl.BlockSpec((L, w), lambda o, i: (0, 2 * o + 1))],
        out_specs=(pl.BlockSpec((1, tf, w), lambda o, i: (o, i, 0)),
                   pl.BlockSpec((1, tf, w), lambda o, i: (o, i, 0))),
        compiler_params=_cparams(("parallel", "parallel")),
        name="hyena_filter_spectrum",
    )(fp, fq, taps, taps)


def _hy_fwd_kernel(fp_ref, fq_ref, u_ref, kp_ref, kq_ref, yp_ref, yq_ref):
    i = pl.program_id(1)
    u = u_ref[0]
    pu = jnp.dot(fp_ref[...], u, preferred_element_type=F32)
    qu = jnp.dot(fq_ref[...], u, preferred_element_type=F32)
    kp = kp_ref[0]
    kq = kq_ref[0]
    f = lax.broadcasted_iota(jnp.int32, pu.shape, 0) + i * pu.shape[0]
    dc = f == 0
    pp = pu * kp
    qq = qu * kq
    yp_ref[0] = jnp.where(dc, 0.5 * pp, pp - qq).astype(yp_ref.dtype)
    yq_ref[0] = jnp.where(dc, 0.5 * qq, pu * kq + qu * kp).astype(yq_ref.dtype)


def hyena_forward(fp, fq, u_arr, u_col, kp, kq, order):
    bv, L, _ = u_arr.shape
    w = kp.shape[2]
    tf = min(L, 512)
    out = jax.ShapeDtypeStruct((bv, L, w), BF16)
    return pl.pallas_call(
        _hy_fwd_kernel,
        out_shape=(out, out),
        grid=(bv, L // tf),
        in_specs=[pl.BlockSpec((tf, L), lambda b, i: (i, 0)),
                  pl.BlockSpec((tf, L), lambda b, i: (i, 0)),
                  pl.BlockSpec((1, L, w), lambda b, i: (b, 0, u_col)),
                  pl.BlockSpec((1, tf, w), lambda b, i: (order, i, 0)),
                  pl.BlockSpec((1, tf, w), lambda b, i: (order, i, 0))],
        out_specs=(pl.BlockSpec((1, tf, w), lambda b, i: (b, i, 0)),
                   pl.BlockSpec((1, tf, w), lambda b, i: (b, i, 0))),
        compiler_params=_cparams(("parallel", "parallel")),
        name="hyena_forward_dft",
    )(fp, fq, u_arr, kp, kq)


def _hy_inv_kernel(fpt_ref, fqt_ref, yp_ref, yq_ref, u_ref, g_ref, bias_ref, o_ref):
    y = jnp.dot(fpt_ref[...], yp_ref[0], preferred_element_type=F32)
    y = y + jnp.dot(fqt_ref[...], yq_ref[0], preferred_element_type=F32)
    u = u_ref[0].astype(F32)
    o_ref[0] = (g_ref[0].astype(F32) * (y + u * bias_ref[0])).astype(o_ref.dtype)


def hyena_inverse(fpt, fqt, yp, yq, u_arr, u_col, g_arr, g_col, bias, order):
    bv, L, w = yp.shape
    tt = min(L, 512)
    return pl.pallas_call(
        _hy_inv_kernel,
        out_shape=jax.ShapeDtypeStruct((bv, L, w), BF16),
        grid=(bv, L // tt),
        in_specs=[pl.BlockSpec((tt, L), lambda b, i: (i, 0)),
                  pl.BlockSpec((tt, L), lambda b, i: (i, 0)),
                  pl.BlockSpec((1, L, w), lambda b, i: (b, 0, 0)),
                  pl.BlockSpec((1, L, w), lambda b, i: (b, 0, 0)),
                  pl.BlockSpec((1, tt, w), lambda b, i: (b, i, u_col)),
                  pl.BlockSpec((1, tt, w), lambda b, i: (b, i, g_col)),
                  pl.BlockSpec((1, 1, w), lambda b, i: (order, 0, 0))],
        out_specs=pl.BlockSpec((1, tt, w), lambda b, i: (b, i, 0)),
        compiler_params=_cparams(("parallel", "parallel")),
        name="hyena_inverse_dft",
    )(fpt, fqt, yp, yq, u_arr, g_arr, bias)


def dft_tables(L):
    f = jnp.arange(L, dtype=jnp.int32)[:, None]
    n = jnp.arange(L, dtype=jnp.int32)[None, :]
    ang = ((f * n) % (2 * L)).astype(F32) * (math.pi / L)
    fp = jnp.cos(ang)
    nyq = (1 - 2 * (n % 2)).astype(F32)
    fq = jnp.where(f == 0, nyq, jnp.sin(ang))
    fp, fq = fp.astype(BF16), fq.astype(BF16)
    return fp, fq, fp.T, fq.T


def hyena_features(L):
    pos = jnp.arange(L, dtype=F32)
    t = pos / (L - 1)
    bands = jnp.linspace(1e-4, HY_BANDS - 1, HY_BANDS, dtype=F32)
    ang = (2.0 * math.pi / L) * pos[:, None] * bands[None, :]
    feats = jnp.concatenate([t[:, None], jnp.cos(ang), jnp.sin(ang)], axis=-1)
    return jnp.pad(feats, ((0, 0), (0, LANES - HY_FEAT)))


def hyena_branch(z_arr, cols, tables, feats, ffn, bias):
    fp, fq, fpt, fqt = tables
    w = bias.shape[-1]
    taps = hyena_filter_taps(feats, *ffn)
    kp, kq = hyena_filter_spectrum(fp, fq, taps, w)
    bias3 = bias.reshape(HY_ORDER, 1, w)
    yp, yq = hyena_forward(fp, fq, z_arr, cols[0], kp, kq, 0)
    y1 = hyena_inverse(fpt, fqt, yp, yq, z_arr, cols[0], z_arr, cols[1], bias3, 0)
    yp, yq = hyena_forward(fp, fq, y1, 0, kp, kq, 1)
    return hyena_inverse(fpt, fqt, yp, yq, y1, 0, z_arr, cols[2], bias3, 1)


def _mlstm_chunk(q, k, v1t, li_row, b_row, r_col, c, m, reverse):
    T = q.shape[0]
    s_idx = lax.broadcasted_iota(jnp.int32, (T, T), 0)
    t_idx = lax.broadcasted_iota(jnp.int32, (T, T), 1)
    mask = (s_idx >= t_idx) if reverse else (s_idx <= t_idx)
    nt = (((1,), (1,)), ((), ()))
    kq = lax.dot_general(k, q, nt, preferred_element_type=F32)
    d = jnp.where(mask, r_col + b_row, -1e30)
    m_loc = jnp.max(d, axis=0, keepdims=True)
    st = (kq * jnp.exp(d - m_loc)).astype(BF16)
    intra = jnp.dot(v1t, st, preferred_element_type=F32)
    g = b_row + m
    m_t = jnp.maximum(g, m_loc)
    inter = lax.dot_general(c.astype(BF16), q, nt, preferred_element_type=F32)
    num = jnp.exp(g - m_t) * inter + jnp.exp(m_loc - m_t) * intra
    den = num[HEAD_PAD - 1:HEAD_PAD, :]
    h = num * (1.0 / jnp.maximum(jnp.abs(den), jnp.exp(-m_t)))
    btot = b_row[:, 0:1] if reverse else b_row[:, T - 1:T]
    a = btot - b_row + li_row
    a_max = jnp.max(a, axis=1, keepdims=True)
    wv = (v1t.astype(F32) * jnp.exp(a - a_max)).astype(BF16)
    delta = jnp.dot(wv, k, preferred_element_type=F32)
    m_new = jnp.maximum(btot + m, a_max)
    c_new = jnp.exp(btot + m - m_new) * c + jnp.exp(a_max - m_new) * delta
    return h, c_new, m_new


def _mlstm_kernel(qc_ref, kc_ref, vc_ref, oc_ref, gcc_ref, grc_ref,
                  ql_ref, kl_ref, vl_ref, ol_ref, gcl_ref, grl_ref, ng_ref,
                  outc_ref, outl_ref, hf_ref, hb_ref, *, dh):
    T = ML_CHUNK
    L = ql_ref.shape[1]
    row = lax.broadcasted_iota(jnp.int32, (HEAD_PAD, T), 0)
    ones_row = row == HEAD_PAD - 1
    valid = row < dh
    ng = ng_ref[...]

    def gates(gc, gr, bwd):
        i = 2 if bwd else 0
        return gr[i:i + 1, :], gr[i + 1:i + 2, :], gc[:, i:i + 1] - gc[:, i + 1:i + 2]

    def finish(h, o):
        h = jnp.where(valid, h, 0.0)
        mu = jnp.sum(h, axis=0, keepdims=True) * (1.0 / dh)
        dlt = jnp.where(valid, h - mu, 0.0)
        var = jnp.sum(dlt * dlt, axis=0, keepdims=True) * (1.0 / dh)
        return (dlt * lax.rsqrt(var + LN_EPS) * ng * o.astype(F32)).astype(BF16)

    def aug(vt):
        return jnp.where(ones_row, jnp.ones_like(vt), vt)

    zero_c = jnp.zeros((HEAD_PAD, HEAD_PAD), F32)
    zero_m = jnp.zeros((1, 1), F32)

    q, k, v1t = qc_ref[0], kc_ref[0], aug(vc_ref[0])
    gc, gr = gcc_ref[0, 0], grc_ref[0, 0]
    h_f, c_f, m_f = _mlstm_chunk(q, k, v1t, *gates(gc, gr, False), zero_c, zero_m, False)
    h_b, c_b, m_b = _mlstm_chunk(q, k, v1t, *gates(gc, gr, True), zero_c, zero_m, True)
    outc_ref[0] = finish(h_f + h_b, oc_ref[0])

    nc = L // T
    for ci in range(nc):
        for bwd in (False, True):
            lo = (nc - 1 - ci) * T if bwd else ci * T
            gts = gates(gcl_ref[0, 0, lo:lo + T, :], grl_ref[0, 0, :, lo:lo + T], bwd)
            q, k, v1t = ql_ref[0, lo:lo + T, :], kl_ref[0, lo:lo + T, :], aug(vl_ref[0, :, lo:lo + T])
            if bwd:
                h_b, c_b, m_b = _mlstm_chunk(q, k, v1t, *gts, c_b, m_b, True)
                hb_ref[:, lo:lo + T] = h_b
            else:
                h_f, c_f, m_f = _mlstm_chunk(q, k, v1t, *gts, c_f, m_f, False)
                hf_ref[:, lo:lo + T] = h_f
    for ci in range(nc):
        lo = ci * T
        outl_ref[0, :, lo:lo + T] = finish(hf_ref[:, lo:lo + T] + hb_ref[:, lo:lo + T], ol_ref[0, :, lo:lo + T])


def mlstm_branch(qk_c, vt_c, ot_c, gate_c, qk_l, vt_l, ot_l, gate_l, norm_g_col, dh):
    bv, lc, _ = qk_c.shape
    L = qk_l.shape[1]
    H = ML_HEADS
    assert lc == ML_CHUNK and L % ML_CHUNK == 0

    def per_head(g):
        g = g[:, :, :4 * H].reshape(bv, -1, 4, H)
        return g.transpose(0, 3, 1, 2), g.transpose(0, 3, 2, 1)

    gcc, grc = per_head(gate_c)
    gcl, grl = per_head(gate_l)

    def qk_blk(length, part):
        return pl.BlockSpec((1, length, HEAD_PAD), lambda b, h: (b, 0, part * H + h))

    def t_blk(length):
        return pl.BlockSpec((1, HEAD_PAD, length), lambda b, h: (b, h, 0))

    def gspecs(length):
        return [pl.BlockSpec((1, 1, length, 4), lambda b, h: (b, h, 0, 0)),
                pl.BlockSpec((1, 1, 4, length), lambda b, h: (b, h, 0, 0))]

    def specs(length):
        return [qk_blk(length, 0), qk_blk(length, 1), t_blk(length), t_blk(length)] + gspecs(length)

    return pl.pallas_call(
        functools.partial(_mlstm_kernel, dh=dh),
        out_shape=(jax.ShapeDtypeStruct((bv, H * HEAD_PAD, lc), BF16),
                   jax.ShapeDtypeStruct((bv, H * HEAD_PAD, L), BF16)),
        grid=(bv, H),
        in_specs=specs(lc) + specs(L) + [pl.BlockSpec((HEAD_PAD, 1), lambda b, h: (h, 0))],
        out_specs=(t_blk(lc), t_blk(L)),
        scratch_shapes=[pltpu.VMEM((HEAD_PAD, L), F32), pltpu.VMEM((HEAD_PAD, L), F32)],
        compiler_params=_cparams(("parallel", "parallel")),
        name="mlstm_scan",
    )(qk_c, qk_c, vt_c, ot_c, gcc, grc, qk_l, qk_l, vt_l, ot_l, gcl, grl, norm_g_col)


def _s5_kernel(u_ref, are_ref, aim_ref, ldt_ref, bre_ref, bim_ref, cre_ref, cim_ref, y_ref,
               wb_ref, wc_ref, a_ref, st_ref, v_ref, *, nb, tt, ns):
    d = pl.program_id(0)
    tb = pl.program_id(2)

    @pl.when(tb == 0)
    def _():
        a_re, a_im = are_ref[0, 0], aim_ref[0, 0]
        dt = jnp.exp(ldt_ref[0, 0])
        mag = jnp.exp(dt * a_re)
        ab_re, ab_im = mag * jnp.cos(dt * a_im), mag * jnp.sin(dt * a_im)
        den = a_re * a_re + a_im * a_im
        co_re = ((ab_re - 1.0) * a_re + ab_im * a_im) / den
        co_im = (ab_im * a_re - (ab_re - 1.0) * a_im) / den
        b_re, b_im = bre_ref[0, 0], bim_ref[0, 0]
        wb_ref[:, :ns] = (co_re * b_re - co_im * b_im).astype(BF16)
        wb_ref[:, ns:] = (co_re * b_im + co_im * b_re).astype(BF16)
        wc_ref[:ns, :] = cre_ref[0, 0].astype(BF16)
        wc_ref[ns:, :] = (-cim_ref[0, 0]).astype(BF16)
        a_ref[0] = jnp.broadcast_to(ab_re, (nb, ns))
        a_ref[1] = jnp.broadcast_to(ab_im, (nb, ns))
        st_ref[...] = jnp.zeros_like(st_ref)

    u = u_ref[...].reshape(tt * nb, u_ref.shape[2])
    v_ref[...] = jnp.dot(u, wb_ref[...], preferred_element_type=F32)
    a_re, a_im = a_ref[0], a_ref[1]

    def step(t, carry):
        x_re, x_im = carry
        te = jnp.where(d == 0, t, tt - 1 - t)
        off = pl.multiple_of(te * nb, nb)
        n_re = a_re * x_re - a_im * x_im + v_ref[pl.ds(off, nb), :ns]
        n_im = a_re * x_im + a_im * x_re + v_ref[pl.ds(off, nb), ns:]
        v_ref[pl.ds(off, nb), :ns] = n_re
        v_ref[pl.ds(off, nb), ns:] = n_im
        return n_re, n_im

    x_re, x_im = lax.fori_loop(0, tt, step, (st_ref[0], st_ref[1]), unroll=8)
    st_ref[0] = x_re
    st_ref[1] = x_im
    y = jnp.dot(v_ref[...].astype(BF16), wc_ref[...], preferred_element_type=F32)
    y_ref[0] = y.astype(y_ref.dtype).reshape(y_ref.shape[1:])


def s5_scan(u_tm, a_re, a_im, log_dt, b_re, b_im, c_re, c_im, ctx_len):
    lt, nb, w = u_tm.shape
    G = a_re.shape[1]
    P, CG = S5_STATE, S5_GROUP
    gl = S5_LANE_BLOCK // CG
    nj = G // gl
    ns = gl * P
    tt = S5_TIME_BLOCK
    nctx, ntot = ctx_len // tt, lt // tt

    eye = jnp.eye(gl, dtype=F32)

    def rows(a):
        return a.astype(F32).reshape(2, nj, 1, ns)

    def bdiag_in(b):
        b = b.astype(F32).reshape(2, nj, gl, P, CG)
        return jnp.einsum("djgpc,gh->djgchp", b, eye).reshape(2, nj, gl * CG, ns)

    def bdiag_out(c):
        c = c.astype(F32).reshape(2, nj, gl, CG, P)
        return jnp.einsum("djgcp,gh->djgphc", c, eye).reshape(2, nj, ns, gl * CG)

    ldt = jnp.broadcast_to(log_dt.astype(F32)[:, :, None], (2, G, P))

    def tmap(d, j, tb):
        bwd = jnp.where(tb < nctx, nctx - 1 - tb, ntot + nctx - 1 - tb)
        return jnp.where(d == 0, tb, bwd)

    pspec = lambda r, c: pl.BlockSpec((1, 1, r, c), lambda d, j, tb: (d, j, 0, 0))
    return pl.pallas_call(
        functools.partial(_s5_kernel, nb=nb, tt=tt, ns=ns),
        out_shape=jax.ShapeDtypeStruct((2, lt, nb, w), BF16),
        grid=(2, nj, ntot),
        in_specs=[pl.BlockSpec((tt, nb, S5_LANE_BLOCK), lambda d, j, tb: (tmap(d, j, tb), 0, j)),
                  pspec(1, ns), pspec(1, ns), pspec(1, ns),
                  pspec(gl * CG, ns), pspec(gl * CG, ns), pspec(ns, gl * CG), pspec(ns, gl * CG)],
        out_specs=pl.BlockSpec((1, tt, nb, S5_LANE_BLOCK), lambda d, j, tb: (d, tmap(d, j, tb), 0, j)),
        scratch_shapes=[pltpu.VMEM((gl * CG, 2 * ns), BF16), pltpu.VMEM((2 * ns, gl * CG), BF16),
                        pltpu.VMEM((2, nb, ns), F32), pltpu.VMEM((2, nb, ns), F32),
                        pltpu.VMEM((tt * nb, 2 * ns), F32)],
        compiler_params=_cparams(("parallel", "parallel", "arbitrary")),
        name="s5_scan",
    )(u_tm, rows(a_re), rows(a_im), rows(ldt), bdiag_in(b_re), bdiag_in(b_im), bdiag_out(c_re), bdiag_out(c_im))


def _gelu_tanh(x):
    return 0.5 * x * (1.0 + jnp.tanh(math.sqrt(2.0 / math.pi) * (x + 0.044715 * (x * x * x))))


def _merge_kernel(yhy_ref, yml_ref, ysf_ref, ysb_ref, u_ref, ghy_ref, gml_ref, gs5_ref,
                  d_ref, gw_ref, gb_ref, why_ref, wml_ref, ws5_ref, o_ref):
    ys = ysf_ref[0].astype(F32) + ysb_ref[0].astype(F32) + d_ref[...] * u_ref[...].astype(F32)
    z = _gelu_tanh(ys).astype(BF16)
    glu = jnp.dot(z, gw_ref[...], preferred_element_type=F32) + gb_ref[...]
    y_s5 = (z.astype(F32) * _sigmoid(glu)).astype(BF16)
    acc = ghy_ref[0].astype(F32) * jnp.dot(yhy_ref[0], why_ref[...], preferred_element_type=F32)
    yml = lax.dot_general(yml_ref[0], wml_ref[...], (((0,), (0,)), ((), ())), preferred_element_type=F32)
    acc = acc + gml_ref[0].astype(F32) * yml
    acc = acc + gs5_ref[0].astype(F32) * jnp.dot(y_s5, ws5_ref[...], preferred_element_type=F32)
    o_ref[0] = acc.astype(o_ref.dtype)


def merge_branches(y_hy, y_ml, ys, u_tm2, t_off, gate_arr, s5_d, glu_w, glu_b, w_hy, w_ml, w_s5):
    bv, lv, w = y_hy.shape
    dm = w_hy.shape[1]
    tm = ML_CHUNK
    full = lambda a: pl.BlockSpec(a.shape, lambda b, i: (0,) * a.ndim)
    gate = lambda c: pl.BlockSpec((1, tm, dm), lambda b, i: (b, i, c))
    return pl.pallas_call(
        _merge_kernel,
        out_shape=jax.ShapeDtypeStruct((bv, lv, dm), BF16),
        grid=(bv, lv // tm),
        in_specs=[pl.BlockSpec((1, tm, w), lambda b, i: (b, i, 0)),
                  pl.BlockSpec((1, y_ml.shape[1], tm), lambda b, i: (b, 0, i)),
                  pl.BlockSpec((1, tm, w), lambda b, i: (0, i + t_off, b)),
                  pl.BlockSpec((1, tm, w), lambda b, i: (1, i + t_off, b)),
                  pl.BlockSpec((tm, w), lambda b, i: (i + t_off, b)),
                  gate(0), gate(1), gate(2),
                  full(s5_d), full(glu_w), full(glu_b), full(w_hy), full(w_ml), full(w_s5)],
        out_specs=pl.BlockSpec((1, tm, dm), lambda b, i: (b, i, 0)),
        compiler_params=_cparams(("parallel", "parallel")),
        name="merge_branches",
    )(y_hy, y_ml, ys, ys, u_tm2, gate_arr, gate_arr, gate_arr, s5_d, glu_w, glu_b, w_hy, w_ml, w_s5)


def _outproj_kernel(m_ref, w_ref, x_ref, gate_ref, g_ref, b_ref, o_ref, *, alpha):
    y = jnp.dot(m_ref[0], w_ref[...], preferred_element_type=F32)
    r = alpha * x_ref[0] + gate_ref[0] * y
    o_ref[0] = _layer_norm(r, g_ref[...], b_ref[...])


def out_projection(merged, w_out, x, gate, ln_g, ln_b, alpha, tm):
    bv, lv, d = x.shape
    bm = gate.shape[0]
    mod_map = (lambda b, i: (b, 0, 0)) if bm == bv else (lambda b, i: (0, 0, 0))
    vec = lambda: pl.BlockSpec((1, d), lambda b, i: (0, 0))
    return pl.pallas_call(
        functools.partial(_outproj_kernel, alpha=alpha),
        out_shape=jax.ShapeDtypeStruct((bv, lv, d), F32),
        grid=(bv, lv // tm),
        in_specs=[pl.BlockSpec((1, tm, d), lambda b, i: (b, i, 0)),
                  pl.BlockSpec((d, d), lambda b, i: (0, 0)),
                  pl.BlockSpec((1, tm, d), lambda b, i: (b, i, 0)),
                  pl.BlockSpec((1, 1, d), mod_map), vec(), vec()],
        out_specs=pl.BlockSpec((1, tm, d), lambda b, i: (b, i, 0)),
        compiler_params=_cparams(("parallel", "parallel")),
        name="out_projection_ln",
    )(merged, w_out, x, gate, ln_g, ln_b)


def _mlp_kernel(x_ref, sh_ref, sc_ref, gate_ref, w1_ref, w2_ref, g_ref, b_ref, o_ref, xm_ref, acc_ref, *, alpha):
    k = pl.program_id(2)

    @pl.when(k == 0)
    def _():
        xm_ref[...] = (x_ref[0] * (1.0 + sc_ref[0]) + sh_ref[0]).astype(BF16)
        acc_ref[...] = jnp.zeros_like(acc_ref)

    h = jnp.maximum(jnp.dot(xm_ref[...], w1_ref[...], preferred_element_type=F32), 0.0)
    acc_ref[...] += jnp.dot((h * h).astype(BF16), w2_ref[...], preferred_element_type=F32)

    @pl.when(k == pl.num_programs(2) - 1)
    def _():
        r = alpha * x_ref[0] + gate_ref[0] * acc_ref[...]
        o_ref[0] = _layer_norm(r, g_ref[...], b_ref[...])


def mlp_block(x, shift, scale, gate, w1, w2, ln_g, ln_b, alpha, tm, kf):
    bv, lv, d = x.shape
    dff = w1.shape[1]
    bm = gate.shape[0]
    mod_map = (lambda b, i, k: (b, 0, 0)) if bm == bv else (lambda b, i, k: (0, 0, 0))
    vec = lambda: pl.BlockSpec((1, d), lambda b, i, k: (0, 0))
    return pl.pallas_call(
        functools.partial(_mlp_kernel, alpha=alpha),
        out_shape=jax.ShapeDtypeStruct((bv, lv, d), F32),
        grid=(bv, lv // tm, dff // kf),
        in_specs=[pl.BlockSpec((1, tm, d), lambda b, i, k: (b, i, 0)),
                  pl.BlockSpec((1, 1, d), mod_map), pl.BlockSpec((1, 1, d), mod_map),
                  pl.BlockSpec((1, 1, d), mod_map),
                  pl.BlockSpec((d, kf), lambda b, i, k: (0, k)),
                  pl.BlockSpec((kf, d), lambda b, i, k: (k, 0)),
                  vec(), vec()],
        out_specs=pl.BlockSpec((1, tm, d), lambda b, i, k: (b, i, 0)),
        scratch_shapes=[pltpu.VMEM((tm, d), BF16), pltpu.VMEM((tm, d), F32)],
        compiler_params=_cparams(("parallel", "parallel", "arbitrary")),
        name="mlp_ln",
    )(x, shift, scale, gate, w1, w2, ln_g, ln_b)


def _pad_heads(a, axis, dh):
    shp = a.shape
    a = a.reshape(shp[:axis] + (ML_HEADS, dh) + shp[axis + 1:])
    pad = [(0, 0)] * a.ndim
    pad[axis + 1] = (0, HEAD_PAD - dh)
    a = jnp.pad(a, pad)
    return a.reshape(shp[:axis] + (ML_HEADS * HEAD_PAD,) + shp[axis + 1:])


def _layer_params(l, p, w_hy, w_ml, w_s5, d_model):
    dh = w_ml // ML_HEADS
    sizes = ((HY_ORDER + 1) * w_hy, 2 * w_ml, w_ml, w_ml, 4 * ML_HEADS, w_s5, N_BRANCH * d_model)
    pts = [0]
    for s in sizes:
        pts.append(pts[-1] + s)
    w_in = p["w_in"][l]
    hy, qk, v, o, gt, u, mg = (w_in[:, pts[i]:pts[i + 1]] for i in range(7))
    ph = lambda a: _pad_heads(a, a.ndim - 1, dh)
    none = lambda n: (jnp.zeros((3, n), F32), jnp.zeros((1, n), F32), jnp.ones((1, n), F32))
    half = ML_HEADS * HEAD_PAD
    cw, cb = p["ml_conv_w"][l], p["ml_conv_b"][l]
    out = {}
    out["mg"] = (mg.astype(BF16),) + none(mg.shape[1])
    out["hy"] = (hy.astype(BF16), p["hy_conv_w"][l], p["hy_conv_b"][l].reshape(1, -1), jnp.ones((1, hy.shape[1]), F32))
    out["qk"] = (jnp.concatenate([ph(qk[:, :w_ml]), ph(qk[:, w_ml:])], axis=1).astype(BF16),
                 jnp.concatenate([ph(cw[:, :w_ml]), ph(cw[:, w_ml:])], axis=1),
                 jnp.concatenate([ph(cb[:w_ml]), ph(cb[w_ml:])]).reshape(1, -1),
                 jnp.concatenate([jnp.ones((half,), F32), jnp.full((half,), dh ** -0.5, F32)]).reshape(1, -1))
    out["vt"] = (ph(v).T.astype(BF16),) + none(half)
    out["ot"] = (ph(o).T.astype(BF16),) + none(half)
    ngt = gt.shape[1]
    out["ug"] = (u.astype(BF16), jnp.pad(gt, ((0, 0), (0, LANES - ngt))).astype(BF16),
                 jnp.pad(p["ml_gate_b"][l].reshape(-1), ((0, LANES - ngt),)).reshape(1, -1))
    out["norm_g"] = ph(p["ml_norm_g"][l]).reshape(-1, 1)
    out["w_ml_out"] = _pad_heads(p["w_ml_out"][l], 0, dh).astype(BF16)
    w3 = p["hy_ffn_w3"][l]
    dec = p["hy_decay"][l]
    dec_cols = jnp.broadcast_to(dec[:, None, :], (HY_ORDER, 2, w_hy)).reshape(1, -1)
    hpad = LANES - HY_HIDDEN
    row = lambda a: jnp.pad(a.reshape(1, -1), ((0, 0), (0, hpad)))
    out["hy_ffn"] = (jnp.pad(p["hy_ffn_w1"][l], ((0, LANES - HY_FEAT), (0, hpad))), row(p["hy_ffn_b1"][l]),
                     jnp.pad(p["hy_ffn_w2"][l], ((0, hpad), (0, hpad))), row(p["hy_ffn_b2"][l]),
                     jnp.pad(w3, ((0, hpad), (0, 0))), row(p["hy_sin_freq"][l]), dec_cols)
    return out


def kernel(x, c, ctx, c_ctx, w_mod, b_mod, w_in, hy_conv_w, hy_conv_b, hy_ffn_w1, hy_ffn_b1, hy_ffn_w2,
           hy_ffn_b2, hy_ffn_w3, hy_sin_freq, hy_decay, hy_bias, ml_conv_w, ml_conv_b, ml_gate_b, ml_norm_g,
           s5_a_re, s5_a_im, s5_log_dt, s5_b_re, s5_b_im, s5_c_re, s5_c_im, s5_d, s5_glu_w, s5_glu_b,
           w_hy_out, w_ml_out, w_s5_out, w_out, ln1_g, ln1_b, ln2_g, ln2_b, w_ff1, w_ff2):
    p = dict(w_in=w_in, hy_conv_w=hy_conv_w, hy_conv_b=hy_conv_b, hy_ffn_w1=hy_ffn_w1, hy_ffn_b1=hy_ffn_b1,
             hy_ffn_w2=hy_ffn_w2, hy_ffn_b2=hy_ffn_b2, hy_ffn_w3=hy_ffn_w3, hy_sin_freq=hy_sin_freq,
             hy_decay=hy_decay, ml_conv_w=ml_conv_w, ml_conv_b=ml_conv_b, ml_gate_b=ml_gate_b,
             ml_norm_g=ml_norm_g, w_ml_out=w_ml_out)
    B, L, D = x.shape
    LC = ctx.shape[1]
    depth = w_in.shape[0]
    w_hy = hy_bias.shape[-1]
    w_ml = ml_norm_g.shape[-1]
    w_s5 = s5_d.shape[-1]
    dh = w_ml // ML_HEADS
    alpha = (2 * depth) ** 0.25
    half = ML_HEADS * HEAD_PAD
    tm_l, tm_c = 1024, LC

    tab_l, tab_c = dft_tables(L), dft_tables(LC)
    feats_l, feats_c = hyena_features(L), hyena_features(LC)

    mrows = 8 * ((B + 1 + 7) // 8)
    c_rows = jnp.zeros((mrows, D), F32).at[:B].set(c).at[B].set(c_ctx)

    for l in range(depth):
        need_ctx = l < depth - 1
        lp = _layer_params(l, p, w_hy, w_ml, w_s5, D)
        mod = mod_vectors(c_rows, w_mod[l], b_mod[l])
        mod_l = [mod[:B, k * D:(k + 1) * D].reshape(B, 1, D) for k in range(6)]
        mod_c = [mod[B:B + 1, k * D:(k + 1) * D].reshape(1, 1, D) for k in range(6)]

        def mixer_inputs(xv, md, row_len, tm, full):
            proj = functools.partial(projection, xv, md[0], md[1], tm=tm, row_len=row_len)
            r = {}
            r["qk"] = proj(*lp["qk"], kind="convsilu", tn=half)
            r["vt"] = proj(*lp["vt"], kind="plain", tn=half, transposed=True)
            r["ot"] = proj(*lp["ot"], kind="sigmoid", tn=half, transposed=True)
            r["u"], r["g"] = u_and_gates(xv, md[0], md[1], *lp["ug"], tm=tm)
            if full:
                r["mg"] = proj(*lp["mg"], kind="sigmoid", tn=w_hy)
                r["z"] = proj(*lp["hy"], kind="conv", tn=w_hy)
            return r

        r_l = mixer_inputs(x, mod_l, GRID_W, tm_l, True)
        r_c = mixer_inputs(ctx, mod_c, LC, tm_c, need_ctx)

        hm_c, hm_l = mlstm_branch(r_c["qk"], r_c["vt"], r_c["ot"], r_c["g"],
                                  r_l["qk"], r_l["vt"], r_l["ot"], r_l["g"], lp["norm_g"], dh)

        u_tm2 = jnp.concatenate([r_c["u"], r_l["u"]], axis=0)
        ys = s5_scan(u_tm2.reshape(LC + L, B, w_s5), s5_a_re[l], s5_a_im[l], s5_log_dt[l],
                     s5_b_re[l], s5_b_im[l], s5_c_re[l], s5_c_im[l], LC)
        ys = ys.reshape(2, LC + L, B * w_s5)

        s5_dv = s5_d[l].reshape(1, -1)
        glu_w = s5_glu_w[l].astype(BF16)
        glu_b = s5_glu_b[l].reshape(1, -1)
        w_hy_o = w_hy_out[l].astype(BF16)
        w_s5_o = w_s5_out[l].astype(BF16)
        w_o = w_out[l].astype(BF16)
        w1 = w_ff1[l].astype(BF16)
        w2 = w_ff2[l].astype(BF16)
        g1, b1 = ln1_g[l].reshape(1, -1), ln1_b[l].reshape(1, -1)
        g2, b2 = ln2_g[l].reshape(1, -1), ln2_b[l].reshape(1, -1)

        def finish(xv, r, hm, t_off, tables, feats, md, tm):
            y_hy = hyena_branch(r["z"], (0, 1, 2), tables, feats, lp["hy_ffn"], hy_bias[l])
            merged = merge_branches(y_hy, hm, ys, u_tm2, t_off, r["mg"], s5_dv, glu_w, glu_b,
                                    w_hy_o, lp["w_ml_out"], w_s5_o)
            if md[2].shape[0] == 1:
                merged, xv = merged.reshape(1, -1, D), xv.reshape(1, -1, D)
            x1 = out_projection(merged, w_o, xv, md[2], g1, b1, alpha, tm)
            return mlp_block(x1, md[3], md[4], md[5], w1, w2, g2, b2, alpha, tm, 512)

        x = finish(x, r_l, hm_l, LC // ML_CHUNK, tab_l, feats_l, mod_l, 512)
        if need_ctx:
            ctx = finish(ctx, r_c, hm_c, 0, tab_c, feats_c, mod_c, 512).reshape(B, LC, D)
    return x
```

```python
import functools
import math

import jax
import jax.numpy as jnp
from jax import lax
from jax.experimental import pallas as pl
from jax.experimental.pallas import tpu as pltpu

F32 = jnp.float32
BF16 = jnp.bfloat16

GRID_W = 64
HY_ORDER = 2
HY_BANDS = 16
HY_FEAT = 1 + 2 * HY_BANDS
HY_HIDDEN = 64
ML_HEADS = 4
ML_CHUNK = 256
S5_GROUP = 16
S5_STATE = 64
N_BRANCH = 3
LN_EPS = 1e-5

LANES = 128
MXU_DIM = 256
VMEM_LIMIT = 56 * 1024 * 1024

HEAD_PAD = MXU_DIM
S5_CHUNK = LANES


def _cparams(sem):
    return pltpu.CompilerParams(dimension_semantics=sem, vmem_limit_bytes=VMEM_LIMIT)


def _sigmoid(x):
    return 1.0 / (1.0 + jnp.exp(-x))


def _layer_norm(r, g, b):
    mu = jnp.mean(r, axis=-1, keepdims=True)
    d = r - mu
    var = jnp.mean(d * d, axis=-1, keepdims=True)
    return d * lax.rsqrt(var + LN_EPS) * g + b


def _mod_kernel(c_ref, w_ref, b_ref, o_ref):
    c = c_ref[...]
    a = (c * _sigmoid(c)).astype(BF16)
    o_ref[...] = jnp.dot(a, w_ref[...].astype(BF16), preferred_element_type=F32) + b_ref[...]


def mod_vectors(c_rows, w, b):
    m, d = c_rows.shape
    n = w.shape[1]
    tn = 1024
    return pl.pallas_call(
        _mod_kernel,
        out_shape=jax.ShapeDtypeStruct((m, n), F32),
        grid=(n // tn,),
        in_specs=[pl.BlockSpec((m, d), lambda j: (0, 0)),
                  pl.BlockSpec((d, tn), lambda j: (0, j)),
                  pl.BlockSpec((1, tn), lambda j: (0, j))],
        out_specs=pl.BlockSpec((m, tn), lambda j: (0, j)),
        compiler_params=_cparams(("parallel",)),
        name="mod_vectors",
    )(c_rows, w, b.reshape(1, n))


def _short_conv(acc, w, b, row_len):
    rows = acc.shape[0]
    t = lax.broadcasted_iota(jnp.int32, acc.shape, 0) & (row_len - 1)
    prev = jnp.where(t == 0, 0.0, pltpu.roll(acc, 1, 0))
    nxt = jnp.where(t == row_len - 1, 0.0, pltpu.roll(acc, rows - 1, 0))
    return prev * w[0:1] + acc * w[1:2] + nxt * w[2:3] + b


def _modulate_into(xm_ref, x_ref, sh_ref, sc_ref):
    xm_ref[...] = (x_ref[0] * (1.0 + sc_ref[0]) + sh_ref[0]).astype(BF16)


def _proj_kernel(x_ref, sh_ref, sc_ref, w_ref, cw_ref, cb_ref, cs_ref, o_ref, xm_ref, *, kind, row_len, rb,
                 transposed):
    @pl.when(pl.program_id(2) == 0)
    def _():
        _modulate_into(xm_ref, x_ref, sh_ref, sc_ref)

    def epilogue(acc):
        if kind == "conv":
            return _short_conv(acc, cw_ref[...], cb_ref[...], row_len)
        if kind == "convsilu":
            y = _short_conv(acc, cw_ref[...], cb_ref[...], row_len)
            return y * _sigmoid(y) * cs_ref[...]
        if kind == "sigmoid":
            return _sigmoid(acc)
        return acc

    for r in range(xm_ref.shape[0] // rb):
        xs = xm_ref[r * rb:(r + 1) * rb, :]
        if transposed:
            acc = lax.dot_general(w_ref[...], xs, (((1,), (1,)), ((), ())), preferred_element_type=F32)
            o_ref[0, :, r * rb:(r + 1) * rb] = epilogue(acc).astype(o_ref.dtype)
        else:
            acc = jnp.dot(xs, w_ref[...], preferred_element_type=F32)
            o_ref[0, r * rb:(r + 1) * rb, :] = epilogue(acc).astype(o_ref.dtype)


def projection(x, shift, scale, w, cw, cb, cs, *, kind, tn, tm, row_len, transposed=False):
    bv, lv, d = x.shape
    n = w.shape[0] if transposed else w.shape[1]
    rb = min(tm, ML_CHUNK)
    assert n % tn == 0 and lv % tm == 0 and rb % row_len == 0
    bm = shift.shape[0]
    mod_map = (lambda b, i, j: (b, 0, 0)) if bm == bv else (lambda b, i, j: (0, 0, 0))
    if transposed:
        assert kind in ("plain", "sigmoid")
        w_spec = pl.BlockSpec((tn, d), lambda b, i, j: (j, 0))
        out_shape = jax.ShapeDtypeStruct((bv, n, lv), BF16)
        out_spec = pl.BlockSpec((1, tn, tm), lambda b, i, j: (b, j, i))
    else:
        w_spec = pl.BlockSpec((d, tn), lambda b, i, j: (0, j))
        out_shape = jax.ShapeDtypeStruct((bv, lv, n), BF16)
        out_spec = pl.BlockSpec((1, tm, tn), lambda b, i, j: (b, i, j))
    col = lambda rows: pl.BlockSpec((rows, tn), lambda b, i, j: (0, j))
    return pl.pallas_call(
        functools.partial(_proj_kernel, kind=kind, row_len=row_len, rb=rb, transposed=transposed),
        out_shape=out_shape,
        grid=(bv, lv // tm, n // tn),
        in_specs=[pl.BlockSpec((1, tm, d), lambda b, i, j: (b, i, 0)),
                  pl.BlockSpec((1, 1, d), mod_map), pl.BlockSpec((1, 1, d), mod_map),
                  w_spec, col(3), col(1), col(1)],
        out_specs=out_spec,
        scratch_shapes=[pltpu.VMEM((tm, d), BF16)],
        compiler_params=_cparams(("parallel", "parallel", "arbitrary")),
        name="projection_" + kind + ("_t" if transposed else ""),
    )(x, shift, scale, w, cw, cb, cs)


def _log_sigmoid(x):
    return jnp.minimum(x, 0.0) - jnp.log(1.0 + jnp.exp(-jnp.abs(x)))


def _gate_kernel(x_ref, sh_ref, sc_ref, wg_ref, gb_ref, g_ref, xm_ref):
    _modulate_into(xm_ref, x_ref, sh_ref, sc_ref)
    T = ML_CHUNK
    hp = lax.Precision.HIGHEST
    r = lax.broadcasted_iota(jnp.int32, (T, T), 0)
    s = lax.broadcasted_iota(jnp.int32, (T, T), 1)
    tri_f = (s <= r).astype(F32)
    tri_b = (s >= r).astype(F32)
    kind = lax.broadcasted_iota(jnp.int32, (T, LANES), 1) // ML_HEADS
    for c in range(xm_ref.shape[0] // T):
        g = jnp.dot(xm_ref[c * T:(c + 1) * T, :], wg_ref[...], preferred_element_type=F32) + gb_ref[...]
        ls = _log_sigmoid(g)
        cum_f = jnp.dot(tri_f, ls, precision=hp, preferred_element_type=F32)
        cum_b = jnp.dot(tri_b, ls, precision=hp, preferred_element_type=F32)
        g_ref[0, c * T:(c + 1) * T, :] = jnp.where(kind == 1, cum_f, jnp.where(kind == 3, cum_b, g))


def gate_projection(x, shift, scale, wg, gb, *, tm):
    bv, lv, d = x.shape
    assert lv % tm == 0 and tm % ML_CHUNK == 0
    bm = shift.shape[0]
    mod_map = (lambda b, i: (b, 0, 0)) if bm == bv else (lambda b, i: (0, 0, 0))
    full = lambda a: pl.BlockSpec(a.shape, lambda b, i: (0, 0))
    return pl.pallas_call(
        _gate_kernel,
        out_shape=jax.ShapeDtypeStruct((bv, lv, LANES), F32),
        grid=(bv, lv // tm),
        in_specs=[pl.BlockSpec((1, tm, d), lambda b, i: (b, i, 0)),
                  pl.BlockSpec((1, 1, d), mod_map), pl.BlockSpec((1, 1, d), mod_map),
                  full(wg), full(gb)],
        out_specs=pl.BlockSpec((1, tm, LANES), lambda b, i: (b, i, 0)),
        scratch_shapes=[pltpu.VMEM((tm, d), BF16)],
        compiler_params=_cparams(("parallel", "parallel")),
        name="gate_projection",
    )(x, shift, scale, wg, gb)


def _hy_ffn_kernel(feat_ref, w1_ref, b1_ref, w2_ref, b2_ref, w3_ref, fr_ref, dec_ref, o_ref):
    hp = lax.Precision.HIGHEST
    feats = feat_ref[...]
    fr = fr_ref[...]
    h = jnp.sin(fr * (jnp.dot(feats, w1_ref[...], precision=hp, preferred_element_type=F32) + b1_ref[...]))
    h = jnp.sin(fr * (jnp.dot(h, w2_ref[...], precision=hp, preferred_element_type=F32) + b2_ref[...]))
    h = jnp.dot(h, w3_ref[...], precision=hp, preferred_element_type=F32)
    t = feats[:, 0:1]
    o_ref[...] = h * jnp.exp(-t * jnp.abs(dec_ref[...]))


def hyena_filter_taps(feats, w1p, b1, w2, b2, w3, freq, decay_cols):
    L = feats.shape[0]
    n = w3.shape[1]
    tl = min(L, 256)
    full = lambda a: pl.BlockSpec(a.shape, lambda i: (0, 0))
    return pl.pallas_call(
        _hy_ffn_kernel,
        out_shape=jax.ShapeDtypeStruct((L, n), F32),
        grid=(L // tl,),
        in_specs=[pl.BlockSpec((tl, feats.shape[1]), lambda i: (i, 0)),
                  full(w1p), full(b1), full(w2), full(b2), full(w3), full(freq), full(decay_cols)],
        out_specs=pl.BlockSpec((tl, n), lambda i: (i, 0)),
        compiler_params=_cparams(("parallel",)),
        name="hyena_filter_taps",
    )(feats, w1p, b1, w2, b2, w3, freq, decay_cols)


def _hy_spec_kernel(fp_ref, fq_ref, hf_ref, hb_ref, kp_ref, kq_ref, *, L):
    i = pl.program_id(1)
    hf = hf_ref[...]
    hb = hb_ref[...]
    n = lax.broadcasted_iota(jnp.int32, hb.shape, 0)
    hb = jnp.where(n == 0, 0.0, hb)
    s = (hf + hb).astype(BF16)
    dlt = (hf - hb).astype(BF16)
    kp = jnp.dot(fp_ref[...], s, preferred_element_type=F32)
    kq = jnp.dot(fq_ref[...], dlt, preferred_element_type=F32)
    sign = (1 - 2 * (n & 1)).astype(F32)
    corr = 2.0 * jnp.sum(hb * sign, axis=0, keepdims=True)
    f = lax.broadcasted_iota(jnp.int32, kq.shape, 0) + i * kq.shape[0]
    kq = kq + jnp.where(f == 0, corr, 0.0)
    kp_ref[0] = kp * (1.0 / L)
    kq_ref[0] = kq * (1.0 / L)


def hyena_filter_spectrum(fp, fq, taps, w):
    L = fp.shape[0]
    tf = min(L, 512)
    out = jax.ShapeDtypeStruct((HY_ORDER, L, w), F32)
    return pl.pallas_call(
        functools.partial(_hy_spec_kernel, L=L),
        out_shape=(out, out),
        grid=(HY_ORDER, L // tf),
        in_specs=[pl.BlockSpec((tf, L), lambda o, i: (i, 0)),
                  pl.BlockSpec((tf, L), lambda o, i: (i, 0)),
                  pl.BlockSpec((L, w), lambda o, i: (0, 2 * o)),
                  pl.BlockSpec((L, w), lambda o, i: (0, 2 * o + 1))],
        out_specs=(pl.BlockSpec((1, tf, w), lambda o, i: (o, i, 0)),
                   pl.BlockSpec((1, tf, w), lambda o, i: (o, i, 0))),
        compiler_params=_cparams(("parallel", "parallel")),
        name="hyena_filter_spectrum",
    )(fp, fq, taps, taps)


def _hy_fwd_kernel(fp_ref, fq_ref, u_ref, kp_ref, kq_ref, yp_ref, yq_ref):
    i = pl.program_id(1)
    u = u_ref[0]
    pu = jnp.dot(fp_ref[...], u, preferred_element_type=F32)
    qu = jnp.dot(fq_ref[...], u, preferred_element_type=F32)
    kp = kp_ref[0]
    kq = kq_ref[0]
    f = lax.broadcasted_iota(jnp.int32, pu.shape, 0) + i * pu.shape[0]
    dc = f == 0
    pp = pu * kp
    qq = qu * kq
    yp_ref[0] = jnp.where(dc, 0.5 * pp, pp - qq).astype(yp_ref.dtype)
    yq_ref[0] = jnp.where(dc, 0.5 * qq, pu * kq + qu * kp).astype(yq_ref.dtype)


def hyena_forward(fp, fq, u_arr, u_col, kp, kq, order):
    bv, L, _ = u_arr.shape
    w = kp.shape[2]
    tf = min(L, 512)
    out = jax.ShapeDtypeStruct((bv, L, w), BF16)
    return pl.pallas_call(
        _hy_fwd_kernel,
        out_shape=(out, out),
        grid=(bv, L // tf),
        in_specs=[pl.BlockSpec((tf, L), lambda b, i: (i, 0)),
                  pl.BlockSpec((tf, L), lambda b, i: (i, 0)),
                  pl.BlockSpec((1, L, w), lambda b, i: (b, 0, u_col)),
                  pl.BlockSpec((1, tf, w), lambda b, i: (order, i, 0)),
                  pl.BlockSpec((1, tf, w), lambda b, i: (order, i, 0))],
        out_specs=(pl.BlockSpec((1, tf, w), lambda b, i: (b, i, 0)),
                   pl.BlockSpec((1, tf, w), lambda b, i: (b, i, 0))),
        compiler_params=_cparams(("parallel", "parallel")),
        name="hyena_forward_dft",
    )(fp, fq, u_arr, kp, kq)


def _hy_inv_kernel(fpt_ref, fqt_ref, yp_ref, yq_ref, u_ref, g_ref, bias_ref, o_ref):
    y = jnp.dot(fpt_ref[...], yp_ref[0], preferred_element_type=F32)
    y = y + jnp.dot(fqt_ref[...], yq_ref[0], preferred_element_type=F32)
    u = u_ref[0].astype(F32)
    o_ref[0] = (g_ref[0].astype(F32) * (y + u * bias_ref[0])).astype(o_ref.dtype)


def hyena_inverse(fpt, fqt, yp, yq, u_arr, u_col, g_arr, g_col, bias, order):
    bv, L, w = yp.shape
    tt = min(L, 512)
    return pl.pallas_call(
        _hy_inv_kernel,
        out_shape=jax.ShapeDtypeStruct((bv, L, w), BF16),
        grid=(bv, L // tt),
        in_specs=[pl.BlockSpec((tt, L), lambda b, i: (i, 0)),
                  pl.BlockSpec((tt, L), lambda b, i: (i, 0)),
                  pl.BlockSpec((1, L, w), lambda b, i: (b, 0, 0)),
                  pl.BlockSpec((1, L, w), lambda b, i: (b, 0, 0)),
                  pl.BlockSpec((1, tt, w), lambda b, i: (b, i, u_col)),
                  pl.BlockSpec((1, tt, w), lambda b, i: (b, i, g_col)),
                  pl.BlockSpec((1, 1, w), lambda b, i: (order, 0, 0))],
        out_specs=pl.BlockSpec((1, tt, w), lambda b, i: (b, i, 0)),
        compiler_params=_cparams(("parallel", "parallel")),
        name="hyena_inverse_dft",
    )(fpt, fqt, yp, yq, u_arr, g_arr, bias)


def dft_tables(L):
    f = jnp.arange(L, dtype=jnp.int32)[:, None]
    n = jnp.arange(L, dtype=jnp.int32)[None, :]
    ang = ((f * n) % (2 * L)).astype(F32) * (math.pi / L)
    fp = jnp.cos(ang)
    nyq = (1 - 2 * (n % 2)).astype(F32)
    fq = jnp.where(f == 0, nyq, jnp.sin(ang))
    fp, fq = fp.astype(BF16), fq.astype(BF16)
    return fp, fq, fp.T, fq.T


def hyena_features(L):
    pos = jnp.arange(L, dtype=F32)
    t = pos / (L - 1)
    bands = jnp.linspace(1e-4, HY_BANDS - 1, HY_BANDS, dtype=F32)
    ang = (2.0 * math.pi / L) * pos[:, None] * bands[None, :]
    feats = jnp.concatenate([t[:, None], jnp.cos(ang), jnp.sin(ang)], axis=-1)
    return jnp.pad(feats, ((0, 0), (0, LANES - HY_FEAT)))


def hyena_branch(z_arr, cols, tables, feats, ffn, bias):
    fp, fq, fpt, fqt = tables
    w = bias.shape[-1]
    taps = hyena_filter_taps(feats, *ffn)
    kp, kq = hyena_filter_spectrum(fp, fq, taps, w)
    bias3 = bias.reshape(HY_ORDER, 1, w)
    yp, yq = hyena_forward(fp, fq, z_arr, cols[0], kp, kq, 0)
    y1 = hyena_inverse(fpt, fqt, yp, yq, z_arr, cols[0], z_arr, cols[1], bias3, 0)
    yp, yq = hyena_forward(fp, fq, y1, 0, kp, kq, 1)
    return hyena_inverse(fpt, fqt, yp, yq, y1, 0, z_arr, cols[2], bias3, 1)


def _mlstm_chunk(q, k, v1t, li_row, b_row, r_col, c, m, reverse):
    T = q.shape[0]
    s_idx = lax.broadcasted_iota(jnp.int32, (T, T), 0)
    t_idx = lax.broadcasted_iota(jnp.int32, (T, T), 1)
    mask = (s_idx >= t_idx) if reverse else (s_idx <= t_idx)
    nt = (((1,), (1,)), ((), ()))
    kq = lax.dot_general(k, q, nt, preferred_element_type=F32)
    d = jnp.where(mask, r_col + b_row, -1e30)
    m_loc = jnp.max(d, axis=0, keepdims=True)
    st = (kq * jnp.exp(d - m_loc)).astype(BF16)
    intra = jnp.dot(v1t, st, preferred_element_type=F32)
    g = b_row + m
    m_t = jnp.maximum(g, m_loc)
    inter = lax.dot_general(c.astype(BF16), q, nt, preferred_element_type=F32)
    num = jnp.exp(g - m_t) * inter + jnp.exp(m_loc - m_t) * intra
    den = num[HEAD_PAD - 1:HEAD_PAD, :]
    h = num * (1.0 / jnp.maximum(jnp.abs(den), jnp.exp(-m_t)))
    btot = b_row[:, 0:1] if reverse else b_row[:, T - 1:T]
    a = btot - b_row + li_row
    a_max = jnp.max(a, axis=1, keepdims=True)
    wv = (v1t.astype(F32) * jnp.exp(a - a_max)).astype(BF16)
    delta = jnp.dot(wv, k, preferred_element_type=F32)
    m_new = jnp.maximum(btot + m, a_max)
    c_new = jnp.exp(btot + m - m_new) * c + jnp.exp(a_max - m_new) * delta
    return h, c_new, m_new


def _mlstm_kernel(qc_ref, kc_ref, vc_ref, oc_ref, gcc_ref, grc_ref,
                  ql_ref, kl_ref, vl_ref, ol_ref, gcl_ref, grl_ref, ng_ref,
                  outc_ref, outl_ref, hf_ref, hb_ref, *, dh):
    T = ML_CHUNK
    L = ql_ref.shape[1]
    row = lax.broadcasted_iota(jnp.int32, (HEAD_PAD, T), 0)
    ones_row = row == HEAD_PAD - 1
    valid = row < dh
    ng = ng_ref[...]

    def gates(gc, gr, bwd):
        i = 2 if bwd else 0
        return gr[i:i + 1, :], gr[i + 1:i + 2, :], gc[:, i:i + 1] - gc[:, i + 1:i + 2]

    def finish(h, o):
        h = jnp.where(valid, h, 0.0)
        mu = jnp.sum(h, axis=0, keepdims=True) * (1.0 / dh)
        dlt = jnp.where(valid, h - mu, 0.0)
        var = jnp.sum(dlt * dlt, axis=0, keepdims=True) * (1.0 / dh)
        return (dlt * lax.rsqrt(var + LN_EPS) * ng * o.astype(F32)).astype(BF16)

    def aug(vt):
        return jnp.where(ones_row, jnp.ones_like(vt), vt)

    zero_c = jnp.zeros((HEAD_PAD, HEAD_PAD), F32)
    zero_m = jnp.zeros((1, 1), F32)

    q, k, v1t = qc_ref[0], kc_ref[0], aug(vc_ref[0])
    gc, gr = gcc_ref[0, 0], grc_ref[0, 0]
    h_f, c_f, m_f = _mlstm_chunk(q, k, v1t, *gates(gc, gr, False), zero_c, zero_m, False)
    h_b, c_b, m_b = _mlstm_chunk(q, k, v1t, *gates(gc, gr, True), zero_c, zero_m, True)
    outc_ref[0] = finish(h_f + h_b, oc_ref[0])

    nc = L // T
    for ci in range(nc):
        for bwd in (False, True):
            lo = (nc - 1 - ci) * T if bwd else ci * T
            gts = gates(gcl_ref[0, 0, lo:lo + T, :], grl_ref[0, 0, :, lo:lo + T], bwd)
            q, k, v1t = ql_ref[0, lo:lo + T, :], kl_ref[0, lo:lo + T, :], aug(vl_ref[0, :, lo:lo + T])
            if bwd:
                h_b, c_b, m_b = _mlstm_chunk(q, k, v1t, *gts, c_b, m_b, True)
                hb_ref[:, lo:lo + T] = h_b
            else:
                h_f, c_f, m_f = _mlstm_chunk(q, k, v1t, *gts, c_f, m_f, False)
                hf_ref[:, lo:lo + T] = h_f
    for ci in range(nc):
        lo = ci * T
        outl_ref[0, :, lo:lo + T] = finish(hf_ref[:, lo:lo + T] + hb_ref[:, lo:lo + T], ol_ref[0, :, lo:lo + T])


def mlstm_branch(qk_c, vt_c, ot_c, gate_c, qk_l, vt_l, ot_l, gate_l, norm_g_col, dh):
    bv, lc, _ = qk_c.shape
    L = qk_l.shape[1]
    H = ML_HEADS
    assert lc == ML_CHUNK and L % ML_CHUNK == 0

    def per_head(g):
        g = g[:, :, :4 * H].reshape(bv, -1, 4, H)
        return g.transpose(0, 3, 1, 2), g.transpose(0, 3, 2, 1)

    gcc, grc = per_head(gate_c)
    gcl, grl = per_head(gate_l)

    def qk_blk(length, part):
        return pl.BlockSpec((1, length, HEAD_PAD), lambda b, h: (b, 0, part * H + h))

    def t_blk(length):
        return pl.BlockSpec((1, HEAD_PAD, length), lambda b, h: (b, h, 0))

    def gspecs(length):
        return [pl.BlockSpec((1, 1, length, 4), lambda b, h: (b, h, 0, 0)),
                pl.BlockSpec((1, 1, 4, length), lambda b, h: (b, h, 0, 0))]

    def specs(length):
        return [qk_blk(length, 0), qk_blk(length, 1), t_blk(length), t_blk(length)] + gspecs(length)

    return pl.pallas_call(
        functools.partial(_mlstm_kernel, dh=dh),
        out_shape=(jax.ShapeDtypeStruct((bv, H * HEAD_PAD, lc), BF16),
                   jax.ShapeDtypeStruct((bv, H * HEAD_PAD, L), BF16)),
        grid=(bv, H),
        in_specs=specs(lc) + specs(L) + [pl.BlockSpec((HEAD_PAD, 1), lambda b, h: (h, 0))],
        out_specs=(t_blk(lc), t_blk(L)),
        scratch_shapes=[pltpu.VMEM((HEAD_PAD, L), F32), pltpu.VMEM((HEAD_PAD, L), F32)],
        compiler_params=_cparams(("parallel", "parallel")),
        name="mlstm_scan",
    )(qk_c, qk_c, vt_c, ot_c, gcc, grc, qk_l, qk_l, vt_l, ot_l, gcl, grl, norm_g_col)


def _s5_kernel(u_ref, prow_ref, pcol_ref, bt_ref, cr_ref, cc_ref, y_ref,
               toep_ref, min_ref, mout_ref, kv_ref, s_ref, x_ref, *, nb, nctx, nchunk):
    T, CG, P = S5_CHUNK, S5_GROUP, S5_STATE
    hp = lax.Precision.HIGHEST
    lane = lax.broadcasted_iota(jnp.int32, (1, 2 * P), 1)
    re_lane = lane < P
    re_row = lax.broadcasted_iota(jnp.int32, (2 * P, 1), 0) < P

    def powers_rows(n, mag1, th1):
        m = jnp.exp(n * mag1)
        return m * jnp.cos(n * th1), m * jnp.sin(n * th1)

    def powers_cols(n, mag1, th1):
        m = jnp.exp(n * mag1)
        return m * jnp.cos(n * th1), m * jnp.sin(n * th1)

    at = []
    kv = jnp.zeros((CG * CG, 2 * T), F32)
    for d in range(2):
        a_re, a_im, ldt = prow_ref[0, d, 0:1, :], prow_ref[0, d, 1:2, :], prow_ref[0, d, 2:3, :]
        dt = jnp.exp(ldt)
        mag1, th1 = dt * a_re, dt * a_im
        ab_re, ab_im = jnp.exp(mag1) * jnp.cos(th1), jnp.exp(mag1) * jnp.sin(th1)
        den = a_re * a_re + a_im * a_im
        co_re = ((ab_re - 1.0) * a_re + ab_im * a_im) / den
        co_im = (ab_im * a_re - (ab_re - 1.0) * a_im) / den
        b_re, b_im = bt_ref[0, d, 0], bt_ref[0, d, 1]
        bb_re = co_re * b_re - co_im * b_im
        bb_im = co_re * b_im + co_im * b_re
        c_re, c_im = cr_ref[0, d, 0], cr_ref[0, d, 1]
        a_re_c, a_im_c = pcol_ref[0, d, :, 0:1], pcol_ref[0, d, :, 1:2]
        dt_c = jnp.exp(pcol_ref[0, d, :, 2:3])
        mag1_c, th1_c = dt_c * a_re_c, dt_c * a_im_c

        rep = lambda a: jnp.broadcast_to(a[:, None, :], (CG, CG, 2 * P)).reshape(CG * CG, 2 * P)
        til = lambda a: jnp.broadcast_to(a[None, :, :], (CG, CG, 2 * P)).reshape(CG * CG, 2 * P)
        w_re = rep(bb_re) * til(c_re) - rep(bb_im) * til(c_im)
        w_im = rep(bb_re) * til(c_im) + rep(bb_im) * til(c_re)
        w = jnp.where(re_lane, w_re, -w_im)
        lagp = lax.broadcasted_iota(jnp.int32, (1, 2 * T), 1)
        n = (T - 1 - lagp) if d else (lagp - (T - 1))
        ok = (n >= 0) & (lagp < 2 * T - 1)
        pc, ps = powers_cols(jnp.maximum(n, 0).astype(F32), mag1_c, th1_c)
        pw = jnp.where(ok, jnp.where(re_row, pc, ps), 0.0)
        kv = kv + jnp.dot(w, pw, precision=hp, preferred_element_type=F32)

        s_col = lax.broadcasted_iota(jnp.int32, (T, 1), 0)
        pc, ps = powers_rows((s_col if d else T - 1 - s_col).astype(F32), mag1, th1)
        for ci in range(CG):
            br, bi = bb_re[ci:ci + 1, :], bb_im[ci:ci + 1, :]
            blk = jnp.where(re_lane, br * pc - bi * ps, br * ps + bi * pc)
            min_ref[d, ci * T:(ci + 1) * T, :] = blk.astype(BF16)

        t_row = lax.broadcasted_iota(jnp.int32, (1, T), 1)
        pc, ps = powers_cols((T - t_row if d else t_row + 1).astype(F32), mag1_c, th1_c)
        for co in range(CG):
            cr, ci_ = cc_ref[0, d, 0, :, co:co + 1], cc_ref[0, d, 1, :, co:co + 1]
            blk = jnp.where(re_row, cr * pc - ci_ * ps, -(cr * ps + ci_ * pc))
            mout_ref[d, :, co * T:(co + 1) * T] = blk.astype(BF16)

        mt = jnp.exp(T * mag1)
        at.append(((mt * jnp.cos(T * th1))[:, :P], (mt * jnp.sin(T * th1))[:, :P]))

    kv_ref[...] = kv

    def build(ci, carry):
        r0 = pl.multiple_of(ci * T, T)
        for co in range(CG):
            vec = kv_ref[pl.ds(ci * CG + co, 1), :]
            rolled = pltpu.roll(jnp.broadcast_to(vec, (T, 2 * T)), T + 1, 1, stride=1, stride_axis=0)
            toep_ref[pl.ds(r0, T), co * T:(co + 1) * T] = rolled[:, :T].astype(BF16)
        return carry

    lax.fori_loop(0, CG, build, 0)

    u = u_ref[0]
    y = jnp.dot(u, toep_ref[...], preferred_element_type=F32)
    for d in range(2):
        s_ref[d] = jnp.dot(u, min_ref[d], preferred_element_type=F32)

    orders = (list(range(nchunk)), list(range(nctx - 1, -1, -1)) + list(range(nchunk - 1, nctx - 1, -1)))
    for d in range(2):
        at_re, at_im = at[d]
        x_re = jnp.zeros((nb, P), F32)
        x_im = jnp.zeros((nb, P), F32)
        for ch in orders[d]:
            x_ref[d, ch * nb:(ch + 1) * nb, :] = jnp.concatenate([x_re, x_im], axis=1)
            s = s_ref[d, ch * nb:(ch + 1) * nb, :]
            x_re, x_im = (at_re * x_re - at_im * x_im + s[:, :P], at_re * x_im + at_im * x_re + s[:, P:])
        y = y + jnp.dot(x_ref[d].astype(BF16), mout_ref[d], preferred_element_type=F32)
    y_ref[0] = y.astype(y_ref.dtype)


def s5_mix(ut_c, ut_l, a_re, a_im, log_dt, b_re, b_im, c_re, c_im):
    nb, w, lc = ut_c.shape
    L = ut_l.shape[2]
    T, CG, P = S5_CHUNK, S5_GROUP, S5_STATE
    G = w // CG
    nctx, nlat = lc // T, L // T
    nchunk = nctx + nlat

    def to_groups(ut, n):
        return ut.reshape(nb, G, CG, n, T).transpose(1, 3, 0, 2, 4).reshape(G, n, nb, CG * T)

    ug = jnp.concatenate([to_groups(ut_c, nctx), to_groups(ut_l, nlat)], axis=1).reshape(G, nchunk * nb, CG * T)

    dup = lambda a: jnp.concatenate([a, a], axis=-1)
    f32 = lambda a: a.astype(F32)
    ldt = jnp.broadcast_to(f32(log_dt)[:, :, None], (2, G, P))
    prow = jnp.stack([dup(f32(a_re)), dup(f32(a_im)), dup(ldt)], axis=2)
    prow = jnp.pad(prow, ((0, 0), (0, 0), (0, 5), (0, 0))).transpose(1, 0, 2, 3)
    pcol = prow.transpose(0, 1, 3, 2)
    bt = jnp.stack([dup(f32(b_re).transpose(0, 1, 3, 2)), dup(f32(b_im).transpose(0, 1, 3, 2))], axis=2)
    bt = bt.transpose(1, 0, 2, 3, 4)
    cr = jnp.stack([dup(f32(c_re)), dup(f32(c_im))], axis=2).transpose(1, 0, 2, 3, 4)
    cc = cr.transpose(0, 1, 2, 4, 3)

    blk = lambda a: pl.BlockSpec((1,) + a.shape[1:], lambda g: (g,) + (0,) * (a.ndim - 1))
    r = nchunk * nb
    y = pl.pallas_call(
        functools.partial(_s5_kernel, nb=nb, nctx=nctx, nchunk=nchunk),
        out_shape=jax.ShapeDtypeStruct((G, r, CG * T), BF16),
        grid=(G,),
        in_specs=[blk(ug), blk(prow), blk(pcol), blk(bt), blk(cr), blk(cc)],
        out_specs=pl.BlockSpec((1, r, CG * T), lambda g: (g, 0, 0)),
        scratch_shapes=[pltpu.VMEM((CG * T, CG * T), BF16), pltpu.VMEM((2, CG * T, 2 * P), BF16),
                        pltpu.VMEM((2, 2 * P, CG * T), BF16), pltpu.VMEM((CG * CG, 2 * T), F32),
                        pltpu.VMEM((2, r, 2 * P), F32), pltpu.VMEM((2, r, 2 * P), F32)],
        compiler_params=_cparams(("parallel",)),
        name="s5_mix",
    )(ug, prow, pcol, bt, cr, cc)
    return y.reshape(G, nchunk, nb, CG, T).transpose(2, 0, 3, 1, 4).reshape(nb, w, nchunk * T)


def _gelu_tanh(x):
    return 0.5 * x * (1.0 + jnp.tanh(math.sqrt(2.0 / math.pi) * (x + 0.044715 * (x * x * x))))


def _merge_kernel(yhy_ref, yml_ref, ys_ref, u_ref, ghy_ref, gml_ref, gs5_ref,
                  d_ref, gw_ref, gb_ref, why_ref, wml_ref, ws5_ref, o_ref):
    tn = (((0,), (0,)), ((), ()))
    ys = ys_ref[0].astype(F32) + d_ref[...] * u_ref[0].astype(F32)
    z = _gelu_tanh(ys).astype(BF16)
    glu = jnp.dot(gw_ref[...], z, preferred_element_type=F32) + gb_ref[...]
    y_s5 = (z.astype(F32) * _sigmoid(glu)).astype(BF16)
    acc = ghy_ref[0].astype(F32) * jnp.dot(yhy_ref[0], why_ref[...], preferred_element_type=F32)
    acc = acc + gml_ref[0].astype(F32) * lax.dot_general(yml_ref[0], wml_ref[...], tn, preferred_element_type=F32)
    acc = acc + gs5_ref[0].astype(F32) * lax.dot_general(y_s5, ws5_ref[...], tn, preferred_element_type=F32)
    o_ref[0] = acc.astype(o_ref.dtype)


def merge_branches(y_hy, y_ml, ys_t, u_t, t_off, gate_arr, s5_d, glu_wt, glu_b, w_hy, w_ml, w_s5):
    bv, lv, w = y_hy.shape
    dm = w_hy.shape[1]
    tm = ML_CHUNK
    full = lambda a: pl.BlockSpec(a.shape, lambda b, i: (0,) * a.ndim)
    gate = lambda c: pl.BlockSpec((1, tm, dm), lambda b, i: (b, i, c))
    return pl.pallas_call(
        _merge_kernel,
        out_shape=jax.ShapeDtypeStruct((bv, lv, dm), BF16),
        grid=(bv, lv // tm),
        in_specs=[pl.BlockSpec((1, tm, w), lambda b, i: (b, i, 0)),
                  pl.BlockSpec((1, y_ml.shape[1], tm), lambda b, i: (b, 0, i)),
                  pl.BlockSpec((1, w, tm), lambda b, i: (b, 0, i + t_off)),
                  pl.BlockSpec((1, w, tm), lambda b, i: (b, 0, i)),
                  gate(0), gate(1), gate(2),
                  full(s5_d), full(glu_wt), full(glu_b), full(w_hy), full(w_ml), full(w_s5)],
        out_specs=pl.BlockSpec((1, tm, dm), lambda b, i: (b, i, 0)),
        compiler_params=_cparams(("parallel", "parallel")),
        name="merge_branches",
    )(y_hy, y_ml, ys_t, u_t, gate_arr, gate_arr, gate_arr, s5_d, glu_wt, glu_b, w_hy, w_ml, w_s5)


def _outproj_kernel(m_ref, w_ref, x_ref, gate_ref, g_ref, b_ref, o_ref, *, alpha):
    y = jnp.dot(m_ref[0], w_ref[...], preferred_element_type=F32)
    r = alpha * x_ref[0] + gate_ref[0] * y
    o_ref[0] = _layer_norm(r, g_ref[...], b_ref[...])


def out_projection(merged, w_out, x, gate, ln_g, ln_b, alpha, tm):
    bv, lv, d = x.shape
    bm = gate.shape[0]
    mod_map = (lambda b, i: (b, 0, 0)) if bm == bv else (lambda b, i: (0, 0, 0))
    vec = lambda: pl.BlockSpec((1, d), lambda b, i: (0, 0))
    return pl.pallas_call(
        functools.partial(_outproj_kernel, alpha=alpha),
        out_shape=jax.ShapeDtypeStruct((bv, lv, d), F32),
        grid=(bv, lv // tm),
        in_specs=[pl.BlockSpec((1, tm, d), lambda b, i: (b, i, 0)),
                  pl.BlockSpec((d, d), lambda b, i: (0, 0)),
                  pl.BlockSpec((1, tm, d), lambda b, i: (b, i, 0)),
                  pl.BlockSpec((1, 1, d), mod_map), vec(), vec()],
        out_specs=pl.BlockSpec((1, tm, d), lambda b, i: (b, i, 0)),
        compiler_params=_cparams(("parallel", "parallel")),
        name="out_projection_ln",
    )(merged, w_out, x, gate, ln_g, ln_b)


def _mlp_kernel(x_ref, sh_ref, sc_ref, gate_ref, w1_ref, w2_ref, g_ref, b_ref, o_ref, xm_ref, acc_ref, *, alpha):
    k = pl.program_id(2)

    @pl.when(k == 0)
    def _():
        xm_ref[...] = (x_ref[0] * (1.0 + sc_ref[0]) + sh_ref[0]).astype(BF16)
        acc_ref[...] = jnp.zeros_like(acc_ref)

    h = jnp.maximum(jnp.dot(xm_ref[...], w1_ref[...], preferred_element_type=F32), 0.0)
    acc_ref[...] += jnp.dot((h * h).astype(BF16), w2_ref[...], preferred_element_type=F32)

    @pl.when(k == pl.num_programs(2) - 1)
    def _():
        r = alpha * x_ref[0] + gate_ref[0] * acc_ref[...]
        o_ref[0] = _layer_norm(r, g_ref[...], b_ref[...])


def mlp_block(x, shift, scale, gate, w1, w2, ln_g, ln_b, alpha, tm, kf):
    bv, lv, d = x.shape
    dff = w1.shape[1]
    bm = gate.shape[0]
    mod_map = (lambda b, i, k: (b, 0, 0)) if bm == bv else (lambda b, i, k: (0, 0, 0))
    vec = lambda: pl.BlockSpec((1, d), lambda b, i, k: (0, 0))
    return pl.pallas_call(
        functools.partial(_mlp_kernel, alpha=alpha),
        out_shape=jax.ShapeDtypeStruct((bv, lv, d), F32),
        grid=(bv, lv // tm, dff // kf),
        in_specs=[pl.BlockSpec((1, tm, d), lambda b, i, k: (b, i, 0)),
                  pl.BlockSpec((1, 1, d), mod_map), pl.BlockSpec((1, 1, d), mod_map),
                  pl.BlockSpec((1, 1, d), mod_map),
                  pl.BlockSpec((d, kf), lambda b, i, k: (0, k)),
                  pl.BlockSpec((kf, d), lambda b, i, k: (k, 0)),
                  vec(), vec()],
        out_specs=pl.BlockSpec((1, tm, d), lambda b, i, k: (b, i, 0)),
        scratch_shapes=[pltpu.VMEM((tm, d), BF16), pltpu.VMEM((tm, d), F32)],
        compiler_params=_cparams(("parallel", "parallel", "arbitrary")),
        name="mlp_ln",
    )(x, shift, scale, gate, w1, w2, ln_g, ln_b)


def _pad_heads(a, axis, dh):
    shp = a.shape
    a = a.reshape(shp[:axis] + (ML_HEADS, dh) + shp[axis + 1:])
    pad = [(0, 0)] * a.ndim
    pad[axis + 1] = (0, HEAD_PAD - dh)
    a = jnp.pad(a, pad)
    return a.reshape(shp[:axis] + (ML_HEADS * HEAD_PAD,) + shp[axis + 1:])


def _layer_params(l, p, w_hy, w_ml, w_s5, d_model):
    dh = w_ml // ML_HEADS
    sizes = ((HY_ORDER + 1) * w_hy, 2 * w_ml, w_ml, w_ml, 4 * ML_HEADS, w_s5, N_BRANCH * d_model)
    pts = [0]
    for s in sizes:
        pts.append(pts[-1] + s)
    w_in = p["w_in"][l]
    hy, qk, v, o, gt, u, mg = (w_in[:, pts[i]:pts[i + 1]] for i in range(7))
    ph = lambda a: _pad_heads(a, a.ndim - 1, dh)
    none = lambda n: (jnp.zeros((3, n), F32), jnp.zeros((1, n), F32), jnp.ones((1, n), F32))
    half = ML_HEADS * HEAD_PAD
    cw, cb = p["ml_conv_w"][l], p["ml_conv_b"][l]
    out = {}
    out["mg"] = (mg.astype(BF16),) + none(mg.shape[1])
    out["hy"] = (hy.astype(BF16), p["hy_conv_w"][l], p["hy_conv_b"][l].reshape(1, -1), jnp.ones((1, hy.shape[1]), F32))
    out["qk"] = (jnp.concatenate([ph(qk[:, :w_ml]), ph(qk[:, w_ml:])], axis=1).astype(BF16),
                 jnp.concatenate([ph(cw[:, :w_ml]), ph(cw[:, w_ml:])], axis=1),
                 jnp.concatenate([ph(cb[:w_ml]), ph(cb[w_ml:])]).reshape(1, -1),
                 jnp.concatenate([jnp.ones((half,), F32), jnp.full((half,), dh ** -0.5, F32)]).reshape(1, -1))
    out["vt"] = (ph(v).T.astype(BF16),) + none(half)
    out["ot"] = (ph(o).T.astype(BF16),) + none(half)
    ngt = gt.shape[1]
    out["ut"] = (u.T.astype(BF16),) + none(u.shape[1])
    out["gt"] = (jnp.pad(gt, ((0, 0), (0, LANES - ngt))).astype(BF16),
                 jnp.pad(p["ml_gate_b"][l].reshape(-1), ((0, LANES - ngt),)).reshape(1, -1))
    out["norm_g"] = ph(p["ml_norm_g"][l]).reshape(-1, 1)
    out["w_ml_out"] = _pad_heads(p["w_ml_out"][l], 0, dh).astype(BF16)
    w3 = p["hy_ffn_w3"][l]
    dec = p["hy_decay"][l]
    dec_cols = jnp.broadcast_to(dec[:, None, :], (HY_ORDER, 2, w_hy)).reshape(1, -1)
    hpad = LANES - HY_HIDDEN
    row = lambda a: jnp.pad(a.reshape(1, -1), ((0, 0), (0, hpad)))
    out["hy_ffn"] = (jnp.pad(p["hy_ffn_w1"][l], ((0, LANES - HY_FEAT), (0, hpad))), row(p["hy_ffn_b1"][l]),
                     jnp.pad(p["hy_ffn_w2"][l], ((0, hpad), (0, hpad))), row(p["hy_ffn_b2"][l]),
                     jnp.pad(w3, ((0, hpad), (0, 0))), row(p["hy_sin_freq"][l]), dec_cols)
    return out


def kernel(x, c, ctx, c_ctx, w_mod, b_mod, w_in, hy_conv_w, hy_conv_b, hy_ffn_w1, hy_ffn_b1, hy_ffn_w2,
           hy_ffn_b2, hy_ffn_w3, hy_sin_freq, hy_decay, hy_bias, ml_conv_w, ml_conv_b, ml_gate_b, ml_norm_g,
           s5_a_re, s5_a_im, s5_log_dt, s5_b_re, s5_b_im, s5_c_re, s5_c_im, s5_d, s5_glu_w, s5_glu_b,
           w_hy_out, w_ml_out, w_s5_out, w_out, ln1_g, ln1_b, ln2_g, ln2_b, w_ff1, w_ff2):
    p = dict(w_in=w_in, hy_conv_w=hy_conv_w, hy_conv_b=hy_conv_b, hy_ffn_w1=hy_ffn_w1, hy_ffn_b1=hy_ffn_b1,
             hy_ffn_w2=hy_ffn_w2, hy_ffn_b2=hy_ffn_b2, hy_ffn_w3=hy_ffn_w3, hy_sin_freq=hy_sin_freq,
             hy_decay=hy_decay, ml_conv_w=ml_conv_w, ml_conv_b=ml_conv_b, ml_gate_b=ml_gate_b,
             ml_norm_g=ml_norm_g, w_ml_out=w_ml_out)
    B, L, D = x.shape
    LC = ctx.shape[1]
    depth = w_in.shape[0]
    w_hy = hy_bias.shape[-1]
    w_ml = ml_norm_g.shape[-1]
    w_s5 = s5_d.shape[-1]
    dh = w_ml // ML_HEADS
    alpha = (2 * depth) ** 0.25
    half = ML_HEADS * HEAD_PAD
    tm_l, tm_c = 1024, LC

    tab_l, tab_c = dft_tables(L), dft_tables(LC)
    feats_l, feats_c = hyena_features(L), hyena_features(LC)

    mrows = 8 * ((B + 1 + 7) // 8)
    c_rows = jnp.zeros((mrows, D), F32).at[:B].set(c).at[B].set(c_ctx)

    for l in range(depth):
        need_ctx = l < depth - 1
        lp = _layer_params(l, p, w_hy, w_ml, w_s5, D)
        mod = mod_vectors(c_rows, w_mod[l], b_mod[l])
        mod_l = [mod[:B, k * D:(k + 1) * D].reshape(B, 1, D) for k in range(6)]
        mod_c = [mod[B:B + 1, k * D:(k + 1) * D].reshape(1, 1, D) for k in range(6)]

        def mixer_inputs(xv, md, row_len, tm, full):
            proj = functools.partial(projection, xv, md[0], md[1], tm=tm, row_len=row_len)
            r = {}
            r["qk"] = proj(*lp["qk"], kind="convsilu", tn=half)
            r["vt"] = proj(*lp["vt"], kind="plain", tn=half, transposed=True)
            r["ot"] = proj(*lp["ot"], kind="sigmoid", tn=half, transposed=True)
            r["ut"] = proj(*lp["ut"], kind="plain", tn=w_s5, transposed=True)
            r["g"] = gate_projection(xv, md[0], md[1], *lp["gt"], tm=tm)
            if full:
                r["mg"] = proj(*lp["mg"], kind="sigmoid", tn=w_hy)
                r["z"] = proj(*lp["hy"], kind="conv", tn=w_hy)
            return r

        r_l = mixer_inputs(x, mod_l, GRID_W, tm_l, True)
        r_c = mixer_inputs(ctx, mod_c, LC, tm_c, need_ctx)

        hm_c, hm_l = mlstm_branch(r_c["qk"], r_c["vt"], r_c["ot"], r_c["g"],
                                  r_l["qk"], r_l["vt"], r_l["ot"], r_l["g"], lp["norm_g"], dh)

        ys_t = s5_mix(r_c["ut"], r_l["ut"], s5_a_re[l], s5_a_im[l], s5_log_dt[l],
                      s5_b_re[l], s5_b_im[l], s5_c_re[l], s5_c_im[l])

        s5_dv = s5_d[l].reshape(-1, 1)
        glu_wt = s5_glu_w[l].T.astype(BF16)
        glu_b = s5_glu_b[l].reshape(-1, 1)
        w_hy_o = w_hy_out[l].astype(BF16)
        w_s5_o = w_s5_out[l].astype(BF16)
        w_o = w_out[l].astype(BF16)
        w1 = w_ff1[l].astype(BF16)
        w2 = w_ff2[l].astype(BF16)
        g1, b1 = ln1_g[l].reshape(1, -1), ln1_b[l].reshape(1, -1)
        g2, b2 = ln2_g[l].reshape(1, -1), ln2_b[l].reshape(1, -1)

        def finish(xv, r, hm, t_off, tables, feats, md, tm):
            y_hy = hyena_branch(r["z"], (0, 1, 2), tables, feats, lp["hy_ffn"], hy_bias[l])
            merged = merge_branches(y_hy, hm, ys_t, r["ut"], t_off, r["mg"], s5_dv, glu_wt, glu_b,
                                    w_hy_o, lp["w_ml_out"], w_s5_o)
            if md[2].shape[0] == 1:
                merged, xv = merged.reshape(1, -1, D), xv.reshape(1, -1, D)
            x1 = out_projection(merged, w_o, xv, md[2], g1, b1, alpha, tm)
            return mlp_block(x1, md[3], md[4], md[5], w1, w2, g2, b2, alpha, tm, 1024)

        x = finish(x, r_l, hm_l, LC // ML_CHUNK, tab_l, feats_l, mod_l, 512)
        if need_ctx:
            ctx = finish(ctx, r_c, hm_c, 0, tab_c, feats_c, mod_c, 512).reshape(B, LC, D)
    return x
```

```python
import functools
import math

import jax
import jax.numpy as jnp
from jax import lax
from jax.experimental import pallas as pl
from jax.experimental.pallas import tpu as pltpu

F32 = jnp.float32
BF16 = jnp.bfloat16

GRID_W = 64
HY_ORDER = 2
HY_BANDS = 16
HY_FEAT = 1 + 2 * HY_BANDS
HY_HIDDEN = 64
ML_HEADS = 4
ML_CHUNK = 256
S5_GROUP = 16
S5_STATE = 64
N_BRANCH = 3
LN_EPS = 1e-5

LANES = 128
MXU_DIM = 256
VMEM_LIMIT = 56 * 1024 * 1024

HEAD_PAD = MXU_DIM
S5_CHUNK = LANES


def _cparams(sem):
    return pltpu.CompilerParams(dimension_semantics=sem, vmem_limit_bytes=VMEM_LIMIT)


def _sigmoid(x):
    return 1.0 / (1.0 + jnp.exp(-x))


def _layer_norm(r, g, b):
    mu = jnp.mean(r, axis=-1, keepdims=True)
    d = r - mu
    var = jnp.mean(d * d, axis=-1, keepdims=True)
    return d * lax.rsqrt(var + LN_EPS) * g + b


def _mod_kernel(c_ref, w_ref, b_ref, o_ref):
    c = c_ref[...]
    a = (c * _sigmoid(c)).astype(BF16)
    o_ref[...] = jnp.dot(a, w_ref[...].astype(BF16), preferred_element_type=F32) + b_ref[...]


def mod_vectors(c_rows, w, b):
    m, d = c_rows.shape
    n = w.shape[1]
    tn = 1024
    return pl.pallas_call(
        _mod_kernel,
        out_shape=jax.ShapeDtypeStruct((m, n), F32),
        grid=(n // tn,),
        in_specs=[pl.BlockSpec((m, d), lambda j: (0, 0)),
                  pl.BlockSpec((d, tn), lambda j: (0, j)),
                  pl.BlockSpec((1, tn), lambda j: (0, j))],
        out_specs=pl.BlockSpec((m, tn), lambda j: (0, j)),
        compiler_params=_cparams(("parallel",)),
        name="mod_vectors",
    )(c_rows, w, b.reshape(1, n))


def _short_conv(acc, w, b, row_len):
    rows = acc.shape[0]
    t = lax.broadcasted_iota(jnp.int32, acc.shape, 0) & (row_len - 1)
    prev = jnp.where(t == 0, 0.0, pltpu.roll(acc, 1, 0))
    nxt = jnp.where(t == row_len - 1, 0.0, pltpu.roll(acc, rows - 1, 0))
    return prev * w[0:1] + acc * w[1:2] + nxt * w[2:3] + b


def _modulate_into(xm_ref, x_ref, sh_ref, sc_ref):
    xm_ref[...] = (x_ref[0] * (1.0 + sc_ref[0]) + sh_ref[0]).astype(BF16)


def _proj_kernel(x_ref, sh_ref, sc_ref, w_ref, cw_ref, cb_ref, cs_ref, o_ref, xm_ref, *, kind, row_len, rb,
                 transposed):
    @pl.when(pl.program_id(2) == 0)
    def _():
        _modulate_into(xm_ref, x_ref, sh_ref, sc_ref)

    def epilogue(acc):
        if kind == "conv":
            return _short_conv(acc, cw_ref[...], cb_ref[...], row_len)
        if kind == "convsilu":
            y = _short_conv(acc, cw_ref[...], cb_ref[...], row_len)
            return y * _sigmoid(y) * cs_ref[...]
        if kind == "sigmoid":
            return _sigmoid(acc)
        return acc

    for r in range(xm_ref.shape[0] // rb):
        xs = xm_ref[r * rb:(r + 1) * rb, :]
        if transposed:
            acc = lax.dot_general(w_ref[...], xs, (((1,), (1,)), ((), ())), preferred_element_type=F32)
            o_ref[0, :, r * rb:(r + 1) * rb] = epilogue(acc).astype(o_ref.dtype)
        else:
            acc = jnp.dot(xs, w_ref[...], preferred_element_type=F32)
            o_ref[0, r * rb:(r + 1) * rb, :] = epilogue(acc).astype(o_ref.dtype)


def projection(x, shift, scale, w, cw, cb, cs, *, kind, tn, tm, row_len, transposed=False):
    bv, lv, d = x.shape
    n = w.shape[0] if transposed else w.shape[1]
    rb = min(tm, ML_CHUNK)
    assert n % tn == 0 and lv % tm == 0 and rb % row_len == 0
    bm = shift.shape[0]
    mod_map = (lambda b, i, j: (b, 0, 0)) if bm == bv else (lambda b, i, j: (0, 0, 0))
    if transposed:
        assert kind in ("plain", "sigmoid")
        w_spec = pl.BlockSpec((tn, d), lambda b, i, j: (j, 0))
        out_shape = jax.ShapeDtypeStruct((bv, n, lv), BF16)
        out_spec = pl.BlockSpec((1, tn, tm), lambda b, i, j: (b, j, i))
    else:
        w_spec = pl.BlockSpec((d, tn), lambda b, i, j: (0, j))
        out_shape = jax.ShapeDtypeStruct((bv, lv, n), BF16)
        out_spec = pl.BlockSpec((1, tm, tn), lambda b, i, j: (b, i, j))
    col = lambda rows: pl.BlockSpec((rows, tn), lambda b, i, j: (0, j))
    return pl.pallas_call(
        functools.partial(_proj_kernel, kind=kind, row_len=row_len, rb=rb, transposed=transposed),
        out_shape=out_shape,
        grid=(bv, lv // tm, n // tn),
        in_specs=[pl.BlockSpec((1, tm, d), lambda b, i, j: (b, i, 0)),
                  pl.BlockSpec((1, 1, d), mod_map), pl.BlockSpec((1, 1, d), mod_map),
                  w_spec, col(3), col(1), col(1)],
        out_specs=out_spec,
        scratch_shapes=[pltpu.VMEM((tm, d), BF16)],
        compiler_params=_cparams(("parallel", "parallel", "arbitrary")),
        name="projection_" + kind + ("_t" if transposed else ""),
    )(x, shift, scale, w, cw, cb, cs)


def _log_sigmoid(x):
    return jnp.minimum(x, 0.0) - jnp.log(1.0 + jnp.exp(-jnp.abs(x)))


def _gate_kernel(x_ref, sh_ref, sc_ref, wg_ref, gb_ref, g_ref, xm_ref):
    _modulate_into(xm_ref, x_ref, sh_ref, sc_ref)
    T = ML_CHUNK
    hp = lax.Precision.HIGHEST
    r = lax.broadcasted_iota(jnp.int32, (T, T), 0)
    s = lax.broadcasted_iota(jnp.int32, (T, T), 1)
    tri_f = (s <= r).astype(F32)
    tri_b = (s >= r).astype(F32)
    kind = lax.broadcasted_iota(jnp.int32, (T, LANES), 1) // ML_HEADS
    for c in range(xm_ref.shape[0] // T):
        g = jnp.dot(xm_ref[c * T:(c + 1) * T, :], wg_ref[...], preferred_element_type=F32) + gb_ref[...]
        ls = _log_sigmoid(g)
        cum_f = jnp.dot(tri_f, ls, precision=hp, preferred_element_type=F32)
        cum_b = jnp.dot(tri_b, ls, precision=hp, preferred_element_type=F32)
        g_ref[0, c * T:(c + 1) * T, :] = jnp.where(kind == 1, cum_f, jnp.where(kind == 3, cum_b, g))


def gate_projection(x, shift, scale, wg, gb, *, tm):
    bv, lv, d = x.shape
    assert lv % tm == 0 and tm % ML_CHUNK == 0
    bm = shift.shape[0]
    mod_map = (lambda b, i: (b, 0, 0)) if bm == bv else (lambda b, i: (0, 0, 0))
    full = lambda a: pl.BlockSpec(a.shape, lambda b, i: (0, 0))
    return pl.pallas_call(
        _gate_kernel,
        out_shape=jax.ShapeDtypeStruct((bv, lv, LANES), F32),
        grid=(bv, lv // tm),
        in_specs=[pl.BlockSpec((1, tm, d), lambda b, i: (b, i, 0)),
                  pl.BlockSpec((1, 1, d), mod_map), pl.BlockSpec((1, 1, d), mod_map),
                  full(wg), full(gb)],
        out_specs=pl.BlockSpec((1, tm, LANES), lambda b, i: (b, i, 0)),
        scratch_shapes=[pltpu.VMEM((tm, d), BF16)],
        compiler_params=_cparams(("parallel", "parallel")),
        name="gate_projection",
    )(x, shift, scale, wg, gb)


def _hy_ffn_kernel(feat_ref, w1_ref, b1_ref, w2_ref, b2_ref, w3_ref, fr_ref, dec_ref, o_ref):
    hp = lax.Precision.HIGHEST
    feats = feat_ref[...]
    fr = fr_ref[...]
    h = jnp.sin(fr * (jnp.dot(feats, w1_ref[...], precision=hp, preferred_element_type=F32) + b1_ref[...]))
    h = jnp.sin(fr * (jnp.dot(h, w2_ref[...], precision=hp, preferred_element_type=F32) + b2_ref[...]))
    h = jnp.dot(h, w3_ref[...], precision=hp, preferred_element_type=F32)
    t = feats[:, 0:1]
    o_ref[...] = h * jnp.exp(-t * jnp.abs(dec_ref[...]))


def hyena_filter_taps(feats, w1p, b1, w2, b2, w3, freq, decay_cols):
    L = feats.shape[0]
    n = w3.shape[1]
    tl = min(L, 256)
    full = lambda a: pl.BlockSpec(a.shape, lambda i: (0, 0))
    return pl.pallas_call(
        _hy_ffn_kernel,
        out_shape=jax.ShapeDtypeStruct((L, n), F32),
        grid=(L // tl,),
        in_specs=[pl.BlockSpec((tl, feats.shape[1]), lambda i: (i, 0)),
                  full(w1p), full(b1), full(w2), full(b2), full(w3), full(freq), full(decay_cols)],
        out_specs=pl.BlockSpec((tl, n), lambda i: (i, 0)),
        compiler_params=_cparams(("parallel",)),
        name="hyena_filter_taps",
    )(feats, w1p, b1, w2, b2, w3, freq, decay_cols)


def _hy_spec_kernel(fp_ref, fq_ref, hf_ref, hb_ref, kp_ref, kq_ref, *, L):
    i = pl.program_id(1)
    hf = hf_ref[...]
    hb = hb_ref[...]
    n = lax.broadcasted_iota(jnp.int32, hb.shape, 0)
    hb = jnp.where(n == 0, 0.0, hb)
    s = (hf + hb).astype(BF16)
    dlt = (hf - hb).astype(BF16)
    kp = jnp.dot(fp_ref[...], s, preferred_element_type=F32)
    kq = jnp.dot(fq_ref[...], dlt, preferred_element_type=F32)
    sign = (1 - 2 * (n & 1)).astype(F32)
    corr = 2.0 * jnp.sum(hb * sign, axis=0, keepdims=True)
    f = lax.broadcasted_iota(jnp.int32, kq.shape, 0) + i * kq.shape[0]
    kq = kq + jnp.where(f == 0, corr, 0.0)
    kp_ref[0] = kp * (1.0 / L)
    kq_ref[0] = kq * (1.0 / L)


def hyena_filter_spectrum(fp, fq, taps, w):
    L = fp.shape[0]
    tf = min(L, 512)
    out = jax.ShapeDtypeStruct((HY_ORDER, L, w), F32)
    return pl.pallas_call(
        functools.partial(_hy_spec_kernel, L=L),
        out_shape=(out, out),
        grid=(HY_ORDER, L // tf),
        in_specs=[pl.BlockSpec((tf, L), lambda o, i: (i, 0)),
                  pl.BlockSpec((tf, L), lambda o, i: (i, 0)),
                  pl.BlockSpec((L, w), lambda o, i: (0, 2 * o)),
                  pl.BlockSpec((L, w), lambda o, i: (0, 2 * o + 1))],
        out_specs=(pl.BlockSpec((1, tf, w), lambda o, i: (o, i, 0)),
                   pl.BlockSpec((1, tf, w), lambda o, i: (o, i, 0))),
        compiler_params=_cparams(("parallel", "parallel")),
        name="hyena_filter_spectrum",
    )(fp, fq, taps, taps)


def _hy_fwd_kernel(fp_ref, fq_ref, u_ref, kp_ref, kq_ref, yp_ref, yq_ref):
    i = pl.program_id(1)
    u = u_ref[0]
    pu = jnp.dot(fp_ref[...], u, preferred_element_type=F32)
    qu = jnp.dot(fq_ref[...], u, preferred_element_type=F32)
    kp = kp_ref[0]
    kq = kq_ref[0]
    f = lax.broadcasted_iota(jnp.int32, pu.shape, 0) + i * pu.shape[0]
    dc = f == 0
    pp = pu * kp
    qq = qu * kq
    yp_ref[0] = jnp.where(dc, 0.5 * pp, pp - qq).astype(yp_ref.dtype)
    yq_ref[0] = jnp.where(dc, 0.5 * qq, pu * kq + qu * kp).astype(yq_ref.dtype)


def hyena_forward(fp, fq, u_arr, u_col, kp, kq, order):
    bv, L, _ = u_arr.shape
    w = kp.shape[2]
    tf = min(L, 512)
    out = jax.ShapeDtypeStruct((bv, L, w), BF16)
    return pl.pallas_call(
        _hy_fwd_kernel,
        out_shape=(out, out),
        grid=(bv, L // tf),
        in_specs=[pl.BlockSpec((tf, L), lambda b, i: (i, 0)),
                  pl.BlockSpec((tf, L), lambda b, i: (i, 0)),
                  pl.BlockSpec((1, L, w), lambda b, i: (b, 0, u_col)),
                  pl.BlockSpec((1, tf, w), lambda b, i: (order, i, 0)),
                  pl.BlockSpec((1, tf, w), lambda b, i: (order, i, 0))],
        out_specs=(pl.BlockSpec((1, tf, w), lambda b, i: (b, i, 0)),
                   pl.BlockSpec((1, tf, w), lambda b, i: (b, i, 0))),
        compiler_params=_cparams(("parallel", "parallel")),
        name="hyena_forward_dft",
    )(fp, fq, u_arr, kp, kq)


def _hy_inv_kernel(fpt_ref, fqt_ref, yp_ref, yq_ref, u_ref, g_ref, bias_ref, o_ref):
    y = jnp.dot(fpt_ref[...], yp_ref[0], preferred_element_type=F32)
    y = y + jnp.dot(fqt_ref[...], yq_ref[0], preferred_element_type=F32)
    u = u_ref[0].astype(F32)
    o_ref[0] = (g_ref[0].astype(F32) * (y + u * bias_ref[0])).astype(o_ref.dtype)


def hyena_inverse(fpt, fqt, yp, yq, u_arr, u_col, g_arr, g_col, bias, order):
    bv, L, w = yp.shape
    tt = min(L, 512)
    return pl.pallas_call(
        _hy_inv_kernel,
        out_shape=jax.ShapeDtypeStruct((bv, L, w), BF16),
        grid=(bv, L // tt),
        in_specs=[pl.BlockSpec((tt, L), lambda b, i: (i, 0)),
                  pl.BlockSpec((tt, L), lambda b, i: (i, 0)),
                  pl.BlockSpec((1, L, w), lambda b, i: (b, 0, 0)),
                  pl.BlockSpec((1, L, w), lambda b, i: (b, 0, 0)),
                  pl.BlockSpec((1, tt, w), lambda b, i: (b, i, u_col)),
                  pl.BlockSpec((1, tt, w), lambda b, i: (b, i, g_col)),
                  pl.BlockSpec((1, 1, w), lambda b, i: (order, 0, 0))],
        out_specs=pl.BlockSpec((1, tt, w), lambda b, i: (b, i, 0)),
        compiler_params=_cparams(("parallel", "parallel")),
        name="hyena_inverse_dft",
    )(fpt, fqt, yp, yq, u_arr, g_arr, bias)


def dft_tables(L):
    f = jnp.arange(L, dtype=jnp.int32)[:, None]
    n = jnp.arange(L, dtype=jnp.int32)[None, :]
    ang = ((f * n) % (2 * L)).astype(F32) * (math.pi / L)
    fp = jnp.cos(ang)
    nyq = (1 - 2 * (n % 2)).astype(F32)
    fq = jnp.where(f == 0, nyq, jnp.sin(ang))
    fp, fq = fp.astype(BF16), fq.astype(BF16)
    return fp, fq, fp.T, fq.T


def hyena_features(L):
    pos = jnp.arange(L, dtype=F32)
    t = pos / (L - 1)
    bands = jnp.linspace(1e-4, HY_BANDS - 1, HY_BANDS, dtype=F32)
    ang = (2.0 * math.pi / L) * pos[:, None] * bands[None, :]
    feats = jnp.concatenate([t[:, None], jnp.cos(ang), jnp.sin(ang)], axis=-1)
    return jnp.pad(feats, ((0, 0), (0, LANES - HY_FEAT)))


def hyena_branch(z_arr, cols, tables, feats, ffn, bias):
    fp, fq, fpt, fqt = tables
    w = bias.shape[-1]
    taps = hyena_filter_taps(feats, *ffn)
    kp, kq = hyena_filter_spectrum(fp, fq, taps, w)
    bias3 = bias.reshape(HY_ORDER, 1, w)
    yp, yq = hyena_forward(fp, fq, z_arr, cols[0], kp, kq, 0)
    y1 = hyena_inverse(fpt, fqt, yp, yq, z_arr, cols[0], z_arr, cols[1], bias3, 0)
    yp, yq = hyena_forward(fp, fq, y1, 0, kp, kq, 1)
    return hyena_inverse(fpt, fqt, yp, yq, y1, 0, z_arr, cols[2], bias3, 1)


def _mlstm_chunk(q, k, v1t, li_row, b_row, r_col, c, m, reverse):
    T = q.shape[0]
    s_idx = lax.broadcasted_iota(jnp.int32, (T, T), 0)
    t_idx = lax.broadcasted_iota(jnp.int32, (T, T), 1)
    mask = (s_idx >= t_idx) if reverse else (s_idx <= t_idx)
    nt = (((1,), (1,)), ((), ()))
    kq = lax.dot_general(k, q, nt, preferred_element_type=F32)
    d = jnp.where(mask, r_col + b_row, -1e30)
    m_loc = jnp.max(d, axis=0, keepdims=True)
    st = (kq * jnp.exp(d - m_loc)).astype(BF16)
    intra = jnp.dot(v1t, st, preferred_element_type=F32)
    g = b_row + m
    m_t = jnp.maximum(g, m_loc)
    inter = lax.dot_general(c.astype(BF16), q, nt, preferred_element_type=F32)
    num = jnp.exp(g - m_t) * inter + jnp.exp(m_loc - m_t) * intra
    den = num[HEAD_PAD - 1:HEAD_PAD, :]
    h = num * (1.0 / jnp.maximum(jnp.abs(den), jnp.exp(-m_t)))
    btot = b_row[:, 0:1] if reverse else b_row[:, T - 1:T]
    a = btot - b_row + li_row
    a_max = jnp.max(a, axis=1, keepdims=True)
    wv = (v1t.astype(F32) * jnp.exp(a - a_max)).astype(BF16)
    delta = jnp.dot(wv, k, preferred_element_type=F32)
    m_new = jnp.maximum(btot + m, a_max)
    c_new = jnp.exp(btot + m - m_new) * c + jnp.exp(a_max - m_new) * delta
    return h, c_new, m_new


def _mlstm_kernel(qc_ref, kc_ref, vc_ref, oc_ref, gcc_ref, grc_ref,
                  ql_ref, kl_ref, vl_ref, ol_ref, gcl_ref, grl_ref, ng_ref,
                  outc_ref, outl_ref, hf_ref, hb_ref, *, dh):
    T = ML_CHUNK
    L = ql_ref.shape[1]
    row = lax.broadcasted_iota(jnp.int32, (HEAD_PAD, T), 0)
    ones_row = row == HEAD_PAD - 1
    valid = row < dh
    ng = ng_ref[...]

    def gates(gc, gr, bwd):
        i = 2 if bwd else 0
        return gr[i:i + 1, :], gr[i + 1:i + 2, :], gc[:, i:i + 1] - gc[:, i + 1:i + 2]

    def finish(h, o):
        h = jnp.where(valid, h, 0.0)
        mu = jnp.sum(h, axis=0, keepdims=True) * (1.0 / dh)
        dlt = jnp.where(valid, h - mu, 0.0)
        var = jnp.sum(dlt * dlt, axis=0, keepdims=True) * (1.0 / dh)
        return (dlt * lax.rsqrt(var + LN_EPS) * ng * o.astype(F32)).astype(BF16)

    def aug(vt):
        return jnp.where(ones_row, jnp.ones_like(vt), vt)

    zero_c = jnp.zeros((HEAD_PAD, HEAD_PAD), F32)
    zero_m = jnp.zeros((1, 1), F32)

    q, k, v1t = qc_ref[0], kc_ref[0], aug(vc_ref[0])
    gc, gr = gcc_ref[0, 0], grc_ref[0, 0]
    h_f, c_f, m_f = _mlstm_chunk(q, k, v1t, *gates(gc, gr, False), zero_c, zero_m, False)
    h_b, c_b, m_b = _mlstm_chunk(q, k, v1t, *gates(gc, gr, True), zero_c, zero_m, True)
    outc_ref[0] = finish(h_f + h_b, oc_ref[0])

    nc = L // T
    for ci in range(nc):
        for bwd in (False, True):
            lo = (nc - 1 - ci) * T if bwd else ci * T
            gts = gates(gcl_ref[0, 0, lo:lo + T, :], grl_ref[0, 0, :, lo:lo + T], bwd)
            q, k, v1t = ql_ref[0, lo:lo + T, :], kl_ref[0, lo:lo + T, :], aug(vl_ref[0, :, lo:lo + T])
            if bwd:
                h_b, c_b, m_b = _mlstm_chunk(q, k, v1t, *gts, c_b, m_b, True)
                hb_ref[:, lo:lo + T] = h_b
            else:
                h_f, c_f, m_f = _mlstm_chunk(q, k, v1t, *gts, c_f, m_f, False)
                hf_ref[:, lo:lo + T] = h_f
    for ci in range(nc):
        lo = ci * T
        outl_ref[0, :, lo:lo + T] = finish(hf_ref[:, lo:lo + T] + hb_ref[:, lo:lo + T], ol_ref[0, :, lo:lo + T])


def mlstm_branch(qk_c, vt_c, ot_c, gate_c, qk_l, vt_l, ot_l, gate_l, norm_g_col, dh):
    bv, lc, _ = qk_c.shape
    L = qk_l.shape[1]
    H = ML_HEADS
    assert lc == ML_CHUNK and L % ML_CHUNK == 0

    def per_head(g):
        g = g[:, :, :4 * H].reshape(bv, -1, 4, H)
        return g.transpose(0, 3, 1, 2), g.transpose(0, 3, 2, 1)

    gcc, grc = per_head(gate_c)
    gcl, grl = per_head(gate_l)

    def qk_blk(length, part):
        return pl.BlockSpec((1, length, HEAD_PAD), lambda b, h: (b, 0, part * H + h))

    def t_blk(length):
        return pl.BlockSpec((1, HEAD_PAD, length), lambda b, h: (b, h, 0))

    def gspecs(length):
        return [pl.BlockSpec((1, 1, length, 4), lambda b, h: (b, h, 0, 0)),
                pl.BlockSpec((1, 1, 4, length), lambda b, h: (b, h, 0, 0))]

    def specs(length):
        return [qk_blk(length, 0), qk_blk(length, 1), t_blk(length), t_blk(length)] + gspecs(length)

    return pl.pallas_call(
        functools.partial(_mlstm_kernel, dh=dh),
        out_shape=(jax.ShapeDtypeStruct((bv, H * HEAD_PAD, lc), BF16),
                   jax.ShapeDtypeStruct((bv, H * HEAD_PAD, L), BF16)),
        grid=(bv, H),
        in_specs=specs(lc) + specs(L) + [pl.BlockSpec((HEAD_PAD, 1), lambda b, h: (h, 0))],
        out_specs=(t_blk(lc), t_blk(L)),
        scratch_shapes=[pltpu.VMEM((HEAD_PAD, L), F32), pltpu.VMEM((HEAD_PAD, L), F32)],
        compiler_params=_cparams(("parallel", "parallel")),
        name="mlstm_scan",
    )(qk_c, qk_c, vt_c, ot_c, gcc, grc, qk_l, qk_l, vt_l, ot_l, gcl, grl, norm_g_col)


def _s5_kernel(uc_ref, ul_ref, prow_ref, pcol_ref, bt_ref, cr_ref, cc_ref, y_ref,
               toep_ref, min_ref, mout_ref, kv_ref, s_ref, x_ref, u_ref, yv_ref, *, nb, nctx, nchunk):
    T, CG, P = S5_CHUNK, S5_GROUP, S5_STATE

    ch0 = 0
    for src in (uc_ref, ul_ref):
        xs = pltpu.einshape("bct->cbt", src[...].astype(F32))
        for ch in range(xs.shape[2] // T):
            for ci in range(CG):
                u_ref[(ch0 + ch) * nb:(ch0 + ch + 1) * nb, ci * T:(ci + 1) * T] = (
                    xs[ci, :, ch * T:(ch + 1) * T].astype(BF16))
        ch0 += xs.shape[2] // T

    hp = lax.Precision.HIGHEST
    lane = lax.broadcasted_iota(jnp.int32, (1, 2 * P), 1)
    re_lane = lane < P
    re_row = lax.broadcasted_iota(jnp.int32, (2 * P, 1), 0) < P

    def powers_rows(n, mag1, th1):
        m = jnp.exp(n * mag1)
        return m * jnp.cos(n * th1), m * jnp.sin(n * th1)

    def powers_cols(n, mag1, th1):
        m = jnp.exp(n * mag1)
        return m * jnp.cos(n * th1), m * jnp.sin(n * th1)

    at = []
    kv = jnp.zeros((CG * CG, 2 * T), F32)
    for d in range(2):
        a_re, a_im, ldt = prow_ref[0, d, 0:1, :], prow_ref[0, d, 1:2, :], prow_ref[0, d, 2:3, :]
        dt = jnp.exp(ldt)
        mag1, th1 = dt * a_re, dt * a_im
        ab_re, ab_im = jnp.exp(mag1) * jnp.cos(th1), jnp.exp(mag1) * jnp.sin(th1)
        den = a_re * a_re + a_im * a_im
        co_re = ((ab_re - 1.0) * a_re + ab_im * a_im) / den
        co_im = (ab_im * a_re - (ab_re - 1.0) * a_im) / den
        b_re, b_im = bt_ref[0, d, 0], bt_ref[0, d, 1]
        bb_re = co_re * b_re - co_im * b_im
        bb_im = co_re * b_im + co_im * b_re
        c_re, c_im = cr_ref[0, d, 0], cr_ref[0, d, 1]
        a_re_c, a_im_c = pcol_ref[0, d, :, 0:1], pcol_ref[0, d, :, 1:2]
        dt_c = jnp.exp(pcol_ref[0, d, :, 2:3])
        mag1_c, th1_c = dt_c * a_re_c, dt_c * a_im_c

        rep = lambda a: jnp.broadcast_to(a[:, None, :], (CG, CG, 2 * P)).reshape(CG * CG, 2 * P)
        til = lambda a: jnp.broadcast_to(a[None, :, :], (CG, CG, 2 * P)).reshape(CG * CG, 2 * P)
        w_re = rep(bb_re) * til(c_re) - rep(bb_im) * til(c_im)
        w_im = rep(bb_re) * til(c_im) + rep(bb_im) * til(c_re)
        w = jnp.where(re_lane, w_re, -w_im)
        lagp = lax.broadcasted_iota(jnp.int32, (1, 2 * T), 1)
        n = (T - 1 - lagp) if d else (lagp - (T - 1))
        ok = (n >= 0) & (lagp < 2 * T - 1)
        pc, ps = powers_cols(jnp.maximum(n, 0).astype(F32), mag1_c, th1_c)
        pw = jnp.where(ok, jnp.where(re_row, pc, ps), 0.0)
        kv = kv + jnp.dot(w, pw, precision=hp, preferred_element_type=F32)

        s_col = lax.broadcasted_iota(jnp.int32, (T, 1), 0)
        pc, ps = powers_rows((s_col if d else T - 1 - s_col).astype(F32), mag1, th1)
        for ci in range(CG):
            br, bi = bb_re[ci:ci + 1, :], bb_im[ci:ci + 1, :]
            blk = jnp.where(re_lane, br * pc - bi * ps, br * ps + bi * pc)
            min_ref[d, ci * T:(ci + 1) * T, :] = blk.astype(BF16)

        t_row = lax.broadcasted_iota(jnp.int32, (1, T), 1)
        pc, ps = powers_cols((T - t_row if d else t_row + 1).astype(F32), mag1_c, th1_c)
        for co in range(CG):
            cr, ci_ = cc_ref[0, d, 0, :, co:co + 1], cc_ref[0, d, 1, :, co:co + 1]
            blk = jnp.where(re_row, cr * pc - ci_ * ps, -(cr * ps + ci_ * pc))
            mout_ref[d, :, co * T:(co + 1) * T] = blk.astype(BF16)

        mt = jnp.exp(T * mag1)
        at.append(((mt * jnp.cos(T * th1))[:, :P], (mt * jnp.sin(T * th1))[:, :P]))

    kv_ref[...] = kv

    def build(ci, carry):
        r0 = pl.multiple_of(ci * T, T)
        for co in range(CG):
            vec = kv_ref[pl.ds(ci * CG + co, 1), :]
            rolled = pltpu.roll(jnp.broadcast_to(vec, (T, 2 * T)), T + 1, 1, stride=1, stride_axis=0)
            toep_ref[pl.ds(r0, T), co * T:(co + 1) * T] = rolled[:, :T].astype(BF16)
        return carry

    lax.fori_loop(0, CG, build, 0)

    u = u_ref[...]
    y = jnp.dot(u, toep_ref[...], preferred_element_type=F32)
    for d in range(2):
        s_ref[d] = jnp.dot(u, min_ref[d], preferred_element_type=F32)

    orders = (list(range(nchunk)), list(range(nctx - 1, -1, -1)) + list(range(nchunk - 1, nctx - 1, -1)))
    for d in range(2):
        at_re, at_im = at[d]
        x_re = jnp.zeros((nb, P), F32)
        x_im = jnp.zeros((nb, P), F32)
        for ch in orders[d]:
            x_ref[d, ch * nb:(ch + 1) * nb, :] = jnp.concatenate([x_re, x_im], axis=1)
            s = s_ref[d, ch * nb:(ch + 1) * nb, :]
            x_re, x_im = (at_re * x_re - at_im * x_im + s[:, :P], at_re * x_im + at_im * x_re + s[:, P:])
        y = y + jnp.dot(x_ref[d].astype(BF16), mout_ref[d], preferred_element_type=F32)

    for ch in range(nchunk):
        for co in range(CG):
            yv_ref[co, :, ch * T:(ch + 1) * T] = y[ch * nb:(ch + 1) * nb, co * T:(co + 1) * T]
    y_ref[...] = pltpu.einshape("cbt->bct", yv_ref[...]).astype(y_ref.dtype)


def s5_mix(ut_c, ut_l, a_re, a_im, log_dt, b_re, b_im, c_re, c_im):
    nb, w, lc = ut_c.shape
    L = ut_l.shape[2]
    T, CG, P = S5_CHUNK, S5_GROUP, S5_STATE
    G = w // CG
    nctx, nlat = lc // T, L // T
    nchunk = nctx + nlat

    dup = lambda a: jnp.concatenate([a, a], axis=-1)
    f32 = lambda a: a.astype(F32)
    ldt = jnp.broadcast_to(f32(log_dt)[:, :, None], (2, G, P))
    prow = jnp.stack([dup(f32(a_re)), dup(f32(a_im)), dup(ldt)], axis=2)
    prow = jnp.pad(prow, ((0, 0), (0, 0), (0, 5), (0, 0))).transpose(1, 0, 2, 3)
    pcol = prow.transpose(0, 1, 3, 2)
    bt = jnp.stack([dup(f32(b_re).transpose(0, 1, 3, 2)), dup(f32(b_im).transpose(0, 1, 3, 2))], axis=2)
    bt = bt.transpose(1, 0, 2, 3, 4)
    cr = jnp.stack([dup(f32(c_re)), dup(f32(c_im))], axis=2).transpose(1, 0, 2, 3, 4)
    cc = cr.transpose(0, 1, 2, 4, 3)

    blk = lambda a: pl.BlockSpec((1,) + a.shape[1:], lambda g: (g,) + (0,) * (a.ndim - 1))
    grp = lambda length: pl.BlockSpec((nb, CG, length), lambda g: (0, g, 0))
    r = nchunk * nb
    return pl.pallas_call(
        functools.partial(_s5_kernel, nb=nb, nctx=nctx, nchunk=nchunk),
        out_shape=jax.ShapeDtypeStruct((nb, w, lc + L), BF16),
        grid=(G,),
        in_specs=[grp(lc), grp(L), blk(prow), blk(pcol), blk(bt), blk(cr), blk(cc)],
        out_specs=grp(lc + L),
        scratch_shapes=[pltpu.VMEM((CG * T, CG * T), BF16), pltpu.VMEM((2, CG * T, 2 * P), BF16),
                        pltpu.VMEM((2, 2 * P, CG * T), BF16), pltpu.VMEM((CG * CG, 2 * T), F32),
                        pltpu.VMEM((2, r, 2 * P), F32), pltpu.VMEM((2, r, 2 * P), F32),
                        pltpu.VMEM((r, CG * T), BF16), pltpu.VMEM((CG, nb, lc + L), F32)],
        compiler_params=_cparams(("parallel",)),
        name="s5_mix",
    )(ut_c, ut_l, prow, pcol, bt, cr, cc)


def _gelu_tanh(x):
    return 0.5 * x * (1.0 + jnp.tanh(math.sqrt(2.0 / math.pi) * (x + 0.044715 * (x * x * x))))


def _merge_kernel(yhy_ref, yml_ref, ys_ref, u_ref, ghy_ref, gml_ref, gs5_ref,
                  d_ref, gw_ref, gb_ref, why_ref, wml_ref, ws5_ref, o_ref):
    tn = (((0,), (0,)), ((), ()))
    ys = ys_ref[0].astype(F32) + d_ref[...] * u_ref[0].astype(F32)
    z = _gelu_tanh(ys).astype(BF16)
    glu = jnp.dot(gw_ref[...], z, preferred_element_type=F32) + gb_ref[...]
    y_s5 = (z.astype(F32) * _sigmoid(glu)).astype(BF16)
    acc = ghy_ref[0].astype(F32) * jnp.dot(yhy_ref[0], why_ref[...], preferred_element_type=F32)
    acc = acc + gml_ref[0].astype(F32) * lax.dot_general(yml_ref[0], wml_ref[...], tn, preferred_element_type=F32)
    acc = acc + gs5_ref[0].astype(F32) * lax.dot_general(y_s5, ws5_ref[...], tn, preferred_element_type=F32)
    o_ref[0] = acc.astype(o_ref.dtype)


def merge_branches(y_hy, y_ml, ys_t, u_t, t_off, gate_arr, s5_d, glu_wt, glu_b, w_hy, w_ml, w_s5):
    bv, lv, w = y_hy.shape
    dm = w_hy.shape[1]
    tm = ML_CHUNK
    full = lambda a: pl.BlockSpec(a.shape, lambda b, i: (0,) * a.ndim)
    gate = lambda c: pl.BlockSpec((1, tm, dm), lambda b, i: (b, i, c))
    return pl.pallas_call(
        _merge_kernel,
        out_shape=jax.ShapeDtypeStruct((bv, lv, dm), BF16),
        grid=(bv, lv // tm),
        in_specs=[pl.BlockSpec((1, tm, w), lambda b, i: (b, i, 0)),
                  pl.BlockSpec((1, y_ml.shape[1], tm), lambda b, i: (b, 0, i)),
                  pl.BlockSpec((1, w, tm), lambda b, i: (b, 0, i + t_off)),
                  pl.BlockSpec((1, w, tm), lambda b, i: (b, 0, i)),
                  gate(0), gate(1), gate(2),
                  full(s5_d), full(glu_wt), full(glu_b), full(w_hy), full(w_ml), full(w_s5)],
        out_specs=pl.BlockSpec((1, tm, dm), lambda b, i: (b, i, 0)),
        compiler_params=_cparams(("parallel", "parallel")),
        name="merge_branches",
    )(y_hy, y_ml, ys_t, u_t, gate_arr, gate_arr, gate_arr, s5_d, glu_wt, glu_b, w_hy, w_ml, w_s5)


def _outproj_kernel(m_ref, w_ref, x_ref, gate_ref, g_ref, b_ref, o_ref, *, alpha):
    rb = min(m_ref.shape[1], LANES)
    for s in range(m_ref.shape[1] // rb):
        rows = slice(s * rb, (s + 1) * rb)
        y = jnp.dot(m_ref[0, rows, :], w_ref[...], preferred_element_type=F32)
        r = alpha * x_ref[0, rows, :] + gate_ref[0] * y
        o_ref[0, rows, :] = _layer_norm(r, g_ref[...], b_ref[...])


def out_projection(merged, w_out, x, gate, ln_g, ln_b, alpha, tm):
    bv, lv, d = x.shape
    bm = gate.shape[0]
    mod_map = (lambda b, i: (b, 0, 0)) if bm == bv else (lambda b, i: (0, 0, 0))
    vec = lambda: pl.BlockSpec((1, d), lambda b, i: (0, 0))
    return pl.pallas_call(
        functools.partial(_outproj_kernel, alpha=alpha),
        out_shape=jax.ShapeDtypeStruct((bv, lv, d), F32),
        grid=(bv, lv // tm),
        in_specs=[pl.BlockSpec((1, tm, d), lambda b, i: (b, i, 0)),
                  pl.BlockSpec((d, d), lambda b, i: (0, 0)),
                  pl.BlockSpec((1, tm, d), lambda b, i: (b, i, 0)),
                  pl.BlockSpec((1, 1, d), mod_map), vec(), vec()],
        out_specs=pl.BlockSpec((1, tm, d), lambda b, i: (b, i, 0)),
        compiler_params=_cparams(("parallel", "parallel")),
        name="out_projection_ln",
    )(merged, w_out, x, gate, ln_g, ln_b)


def _mlp_kernel(x_ref, sh_ref, sc_ref, gate_ref, w1_ref, w2_ref, g_ref, b_ref, o_ref, xm_ref, acc_ref, *, alpha):
    k = pl.program_id(2)

    @pl.when(k == 0)
    def _():
        xm_ref[...] = (x_ref[0] * (1.0 + sc_ref[0]) + sh_ref[0]).astype(BF16)
        acc_ref[...] = jnp.zeros_like(acc_ref)

    h = jnp.maximum(jnp.dot(xm_ref[...], w1_ref[...], preferred_element_type=F32), 0.0)
    acc_ref[...] += jnp.dot((h * h).astype(BF16), w2_ref[...], preferred_element_type=F32)

    @pl.when(k == pl.num_programs(2) - 1)
    def _():
        r = alpha * x_ref[0] + gate_ref[0] * acc_ref[...]
        o_ref[0] = _layer_norm(r, g_ref[...], b_ref[...])


def mlp_block(x, shift, scale, gate, w1, w2, ln_g, ln_b, alpha, tm, kf):
    bv, lv, d = x.shape
    dff = w1.shape[1]
    bm = gate.shape[0]
    mod_map = (lambda b, i, k: (b, 0, 0)) if bm == bv else (lambda b, i, k: (0, 0, 0))
    vec = lambda: pl.BlockSpec((1, d), lambda b, i, k: (0, 0))
    return pl.pallas_call(
        functools.partial(_mlp_kernel, alpha=alpha),
        out_shape=jax.ShapeDtypeStruct((bv, lv, d), F32),
        grid=(bv, lv // tm, dff // kf),
        in_specs=[pl.BlockSpec((1, tm, d), lambda b, i, k: (b, i, 0)),
                  pl.BlockSpec((1, 1, d), mod_map), pl.BlockSpec((1, 1, d), mod_map),
                  pl.BlockSpec((1, 1, d), mod_map),
                  pl.BlockSpec((d, kf), lambda b, i, k: (0, k)),
                  pl.BlockSpec((kf, d), lambda b, i, k: (k, 0)),
                  vec(), vec()],
        out_specs=pl.BlockSpec((1, tm, d), lambda b, i, k: (b, i, 0)),
        scratch_shapes=[pltpu.VMEM((tm, d), BF16), pltpu.VMEM((tm, d), F32)],
        compiler_params=_cparams(("parallel", "parallel", "arbitrary")),
        name="mlp_ln",
    )(x, shift, scale, gate, w1, w2, ln_g, ln_b)


def _pad_heads(a, axis, dh):
    shp = a.shape
    a = a.reshape(shp[:axis] + (ML_HEADS, dh) + shp[axis + 1:])
    pad = [(0, 0)] * a.ndim
    pad[axis + 1] = (0, HEAD_PAD - dh)
    a = jnp.pad(a, pad)
    return a.reshape(shp[:axis] + (ML_HEADS * HEAD_PAD,) + shp[axis + 1:])


def _layer_params(l, p, w_hy, w_ml, w_s5, d_model):
    dh = w_ml // ML_HEADS
    sizes = ((HY_ORDER + 1) * w_hy, 2 * w_ml, w_ml, w_ml, 4 * ML_HEADS, w_s5, N_BRANCH * d_model)
    pts = [0]
    for s in sizes:
        pts.append(pts[-1] + s)
    w_in = p["w_in"][l]
    hy, qk, v, o, gt, u, mg = (w_in[:, pts[i]:pts[i + 1]] for i in range(7))
    ph = lambda a: _pad_heads(a, a.ndim - 1, dh)
    none = lambda n: (jnp.zeros((3, n), F32), jnp.zeros((1, n), F32), jnp.ones((1, n), F32))
    half = ML_HEADS * HEAD_PAD
    cw, cb = p["ml_conv_w"][l], p["ml_conv_b"][l]
    out = {}
    out["mg"] = (mg.astype(BF16),) + none(mg.shape[1])
    out["hy"] = (hy.astype(BF16), p["hy_conv_w"][l], p["hy_conv_b"][l].reshape(1, -1), jnp.ones((1, hy.shape[1]), F32))
    out["qk"] = (jnp.concatenate([ph(qk[:, :w_ml]), ph(qk[:, w_ml:])], axis=1).astype(BF16),
                 jnp.concatenate([ph(cw[:, :w_ml]), ph(cw[:, w_ml:])], axis=1),
                 jnp.concatenate([ph(cb[:w_ml]), ph(cb[w_ml:])]).reshape(1, -1),
                 jnp.concatenate([jnp.ones((half,), F32), jnp.full((half,), dh ** -0.5, F32)]).reshape(1, -1))
    out["vt"] = (ph(v).T.astype(BF16),) + none(half)
    out["ot"] = (ph(o).T.astype(BF16),) + none(half)
    ngt = gt.shape[1]
    out["ut"] = (u.T.astype(BF16),) + none(u.shape[1])
    out["gt"] = (jnp.pad(gt, ((0, 0), (0, LANES - ngt))).astype(BF16),
                 jnp.pad(p["ml_gate_b"][l].reshape(-1), ((0, LANES - ngt),)).reshape(1, -1))
    out["norm_g"] = ph(p["ml_norm_g"][l]).reshape(-1, 1)
    out["w_ml_out"] = _pad_heads(p["w_ml_out"][l], 0, dh).astype(BF16)
    w3 = p["hy_ffn_w3"][l]
    dec = p["hy_decay"][l]
    dec_cols = jnp.broadcast_to(dec[:, None, :], (HY_ORDER, 2, w_hy)).reshape(1, -1)
    hpad = LANES - HY_HIDDEN
    row = lambda a: jnp.pad(a.reshape(1, -1), ((0, 0), (0, hpad)))
    out["hy_ffn"] = (jnp.pad(p["hy_ffn_w1"][l], ((0, LANES - HY_FEAT), (0, hpad))), row(p["hy_ffn_b1"][l]),
                     jnp.pad(p["hy_ffn_w2"][l], ((0, hpad), (0, hpad))), row(p["hy_ffn_b2"][l]),
                     jnp.pad(w3, ((0, hpad), (0, 0))), row(p["hy_sin_freq"][l]), dec_cols)
    return out


def kernel(x, c, ctx, c_ctx, w_mod, b_mod, w_in, hy_conv_w, hy_conv_b, hy_ffn_w1, hy_ffn_b1, hy_ffn_w2,
           hy_ffn_b2, hy_ffn_w3, hy_sin_freq, hy_decay, hy_bias, ml_conv_w, ml_conv_b, ml_gate_b, ml_norm_g,
           s5_a_re, s5_a_im, s5_log_dt, s5_b_re, s5_b_im, s5_c_re, s5_c_im, s5_d, s5_glu_w, s5_glu_b,
           w_hy_out, w_ml_out, w_s5_out, w_out, ln1_g, ln1_b, ln2_g, ln2_b, w_ff1, w_ff2):
    p = dict(w_in=w_in, hy_conv_w=hy_conv_w, hy_conv_b=hy_conv_b, hy_ffn_w1=hy_ffn_w1, hy_ffn_b1=hy_ffn_b1,
             hy_ffn_w2=hy_ffn_w2, hy_ffn_b2=hy_ffn_b2, hy_ffn_w3=hy_ffn_w3, hy_sin_freq=hy_sin_freq,
             hy_decay=hy_decay, ml_conv_w=ml_conv_w, ml_conv_b=ml_conv_b, ml_gate_b=ml_gate_b,
             ml_norm_g=ml_norm_g, w_ml_out=w_ml_out)
    B, L, D = x.shape
    LC = ctx.shape[1]
    depth = w_in.shape[0]
    w_hy = hy_bias.shape[-1]
    w_ml = ml_norm_g.shape[-1]
    w_s5 = s5_d.shape[-1]
    dh = w_ml // ML_HEADS
    alpha = (2 * depth) ** 0.25
    half = ML_HEADS * HEAD_PAD
    tm_l, tm_c = 1024, LC

    tab_l, tab_c = dft_tables(L), dft_tables(LC)
    feats_l, feats_c = hyena_features(L), hyena_features(LC)

    mrows = 8 * ((B + 1 + 7) // 8)
    c_rows = jnp.zeros((mrows, D), F32).at[:B].set(c).at[B].set(c_ctx)

    for l in range(depth):
        need_ctx = l < depth - 1
        lp = _layer_params(l, p, w_hy, w_ml, w_s5, D)
        mod = mod_vectors(c_rows, w_mod[l], b_mod[l])
        mod_l = [mod[:B, k * D:(k + 1) * D].reshape(B, 1, D) for k in range(6)]
        mod_c = [mod[B:B + 1, k * D:(k + 1) * D].reshape(1, 1, D) for k in range(6)]

        def mixer_inputs(xv, md, row_len, tm, full):
            proj = functools.partial(projection, xv, md[0], md[1], tm=tm, row_len=row_len)
            r = {}
            r["qk"] = proj(*lp["qk"], kind="convsilu", tn=half)
            r["vt"] = proj(*lp["vt"], kind="plain", tn=half, transposed=True)
            r["ot"] = proj(*lp["ot"], kind="sigmoid", tn=half, transposed=True)
            r["ut"] = proj(*lp["ut"], kind="plain", tn=w_s5, transposed=True)
            r["g"] = gate_projection(xv, md[0], md[1], *lp["gt"], tm=tm)
            if full:
                r["mg"] = proj(*lp["mg"], kind="sigmoid", tn=w_hy)
                r["z"] = proj(*lp["hy"], kind="conv", tn=w_hy)
            return r

        r_l = mixer_inputs(x, mod_l, GRID_W, tm_l, True)
        r_c = mixer_inputs(ctx, mod_c, LC, tm_c, need_ctx)

        hm_c, hm_l = mlstm_branch(r_c["qk"], r_c["vt"], r_c["ot"], r_c["g"],
                                  r_l["qk"], r_l["vt"], r_l["ot"], r_l["g"], lp["norm_g"], dh)

        ys_t = s5_mix(r_c["ut"], r_l["ut"], s5_a_re[l], s5_a_im[l], s5_log_dt[l],
                      s5_b_re[l], s5_b_im[l], s5_c_re[l], s5_c_im[l])

        s5_dv = s5_d[l].reshape(-1, 1)
        glu_wt = s5_glu_w[l].T.astype(BF16)
        glu_b = s5_glu_b[l].reshape(-1, 1)
        w_hy_o = w_hy_out[l].astype(BF16)
        w_s5_o = w_s5_out[l].astype(BF16)
        w_o = w_out[l].astype(BF16)
        w1 = w_ff1[l].astype(BF16)
        w2 = w_ff2[l].astype(BF16)
        g1, b1 = ln1_g[l].reshape(1, -1), ln1_b[l].reshape(1, -1)
        g2, b2 = ln2_g[l].reshape(1, -1), ln2_b[l].reshape(1, -1)

        def finish(xv, r, hm, t_off, tables, feats, md, tm):
            y_hy = hyena_branch(r["z"], (0, 1, 2), tables, feats, lp["hy_ffn"], hy_bias[l])
            merged = merge_branches(y_hy, hm, ys_t, r["ut"], t_off, r["mg"], s5_dv, glu_wt, glu_b,
                                    w_hy_o, lp["w_ml_out"], w_s5_o)
            if md[2].shape[0] == 1:
                merged, xv = merged.reshape(1, -1, D), xv.reshape(1, -1, D)
            x1 = out_projection(merged, w_o, xv, md[2], g1, b1, alpha, tm)
            return mlp_block(x1, md[3], md[4], md[5], w1, w2, g2, b2, alpha, tm, 1024)

        x = finish(x, r_l, hm_l, LC // ML_CHUNK, tab_l, feats_l, mod_l, 512)
        if need_ctx:
            ctx = finish(ctx, r_c, hm_c, 0, tab_c, feats_c, mod_c, 512).reshape(B, LC, D)
    return x
```

```python
import functools
import math

import jax
import jax.numpy as jnp
from jax import lax
from jax.experimental import pallas as pl
from jax.experimental.pallas import tpu as pltpu

F32 = jnp.float32
BF16 = jnp.bfloat16

GRID_W = 64
HY_ORDER = 2
HY_BANDS = 16
HY_FEAT = 1 + 2 * HY_BANDS
HY_HIDDEN = 64
ML_HEADS = 4
ML_CHUNK = 256
S5_GROUP = 16
S5_STATE = 64
N_BRANCH = 3
LN_EPS = 1e-5

LANES = 128
MXU_DIM = 256
VMEM_LIMIT = 56 * 1024 * 1024

HEAD_PAD = MXU_DIM
S5_CHUNK = LANES


def _cparams(sem):
    return pltpu.CompilerParams(dimension_semantics=sem, vmem_limit_bytes=VMEM_LIMIT)


def _sigmoid(x):
    return 1.0 / (1.0 + jnp.exp(-x))


def _layer_norm(r, g, b):
    mu = jnp.mean(r, axis=-1, keepdims=True)
    d = r - mu
    var = jnp.mean(d * d, axis=-1, keepdims=True)
    return d * lax.rsqrt(var + LN_EPS) * g + b


def _mod_kernel(c_ref, w_ref, b_ref, o_ref):
    c = c_ref[...]
    a = (c * _sigmoid(c)).astype(BF16)
    o_ref[...] = jnp.dot(a, w_ref[0].astype(BF16), preferred_element_type=F32) + b_ref[0]


def mod_vectors(c_rows, w, b, layer):
    m, d = c_rows.shape
    n = w.shape[2]
    tn = 1024
    return pl.pallas_call(
        _mod_kernel,
        out_shape=jax.ShapeDtypeStruct((m, n), F32),
        grid=(n // tn,),
        in_specs=[pl.BlockSpec((m, d), lambda j: (0, 0)),
                  pl.BlockSpec((1, d, tn), lambda j: (layer, 0, j)),
                  pl.BlockSpec((1, 1, tn), lambda j: (layer, 0, j))],
        out_specs=pl.BlockSpec((m, tn), lambda j: (0, j)),
        compiler_params=_cparams(("parallel",)),
        name="mod_vectors",
    )(c_rows, w, b)


def _short_conv(acc, w, b, row_len):
    rows = acc.shape[0]
    t = lax.broadcasted_iota(jnp.int32, acc.shape, 0) & (row_len - 1)
    prev = jnp.where(t == 0, 0.0, pltpu.roll(acc, 1, 0))
    nxt = jnp.where(t == row_len - 1, 0.0, pltpu.roll(acc, rows - 1, 0))
    return prev * w[0:1] + acc * w[1:2] + nxt * w[2:3] + b


def _modulate_rows(xm_ref, x_ref, sh_ref, sc_ref, rows):
    xm_ref[rows, :] = (x_ref[0, rows, :] * (1.0 + sc_ref[0]) + sh_ref[0]).astype(BF16)


def _modulate_into(xm_ref, x_ref, sh_ref, sc_ref):
    _modulate_rows(xm_ref, x_ref, sh_ref, sc_ref, slice(None))


def _proj_kernel(x_ref, sh_ref, sc_ref, w_ref, cw_ref, cb_ref, cs_ref, o_ref, xm_ref, *, kind, row_len, rb,
                 transposed, single_tile):
    def epilogue(acc):
        if kind == "conv":
            return _short_conv(acc, cw_ref[...], cb_ref[...], row_len)
        if kind == "convsilu":
            y = _short_conv(acc, cw_ref[...], cb_ref[...], row_len)
            return y * _sigmoid(y) * cs_ref[...]
        if kind == "sigmoid":
            return _sigmoid(acc)
        return acc

    def body(modulate):
        for r in range(xm_ref.shape[0] // rb):
            rows = slice(r * rb, (r + 1) * rb)
            if modulate:
                _modulate_rows(xm_ref, x_ref, sh_ref, sc_ref, rows)
            xs = xm_ref[rows, :]
            if transposed:
                acc = lax.dot_general(w_ref[...], xs, (((1,), (1,)), ((), ())), preferred_element_type=F32)
                o_ref[0, :, rows] = epilogue(acc).astype(o_ref.dtype)
            else:
                acc = jnp.dot(xs, w_ref[...], preferred_element_type=F32)
                o_ref[0, rows, :] = epilogue(acc).astype(o_ref.dtype)

    if single_tile:
        body(True)
    else:
        first = pl.program_id(2) == 0
        pl.when(first)(lambda: body(True))
        pl.when(jnp.logical_not(first))(lambda: body(False))


def projection(x, shift, scale, w, cw, cb, cs, *, kind, tn, tm, row_len, transposed=False):
    bv, lv, d = x.shape
    n = w.shape[0] if transposed else w.shape[1]
    rb = min(tm, ML_CHUNK)
    assert n % tn == 0 and lv % tm == 0 and rb % row_len == 0
    bm = shift.shape[0]
    mod_map = (lambda b, i, j: (b, 0, 0)) if bm == bv else (lambda b, i, j: (0, 0, 0))
    if transposed:
        assert kind in ("plain", "sigmoid")
        w_spec = pl.BlockSpec((tn, d), lambda b, i, j: (j, 0))
        out_shape = jax.ShapeDtypeStruct((bv, n, lv), BF16)
        out_spec = pl.BlockSpec((1, tn, tm), lambda b, i, j: (b, j, i))
    else:
        w_spec = pl.BlockSpec((d, tn), lambda b, i, j: (0, j))
        out_shape = jax.ShapeDtypeStruct((bv, lv, n), BF16)
        out_spec = pl.BlockSpec((1, tm, tn), lambda b, i, j: (b, i, j))
    col = lambda rows: pl.BlockSpec((rows, tn), lambda b, i, j: (0, j))
    return pl.pallas_call(
        functools.partial(_proj_kernel, kind=kind, row_len=row_len, rb=rb, transposed=transposed,
                          single_tile=(n == tn)),
        out_shape=out_shape,
        grid=(bv, lv // tm, n // tn),
        in_specs=[pl.BlockSpec((1, tm, d), lambda b, i, j: (b, i, 0)),
                  pl.BlockSpec((1, 1, d), mod_map), pl.BlockSpec((1, 1, d), mod_map),
                  w_spec, col(3), col(1), col(1)],
        out_specs=out_spec,
        scratch_shapes=[pltpu.VMEM((tm, d), BF16)],
        compiler_params=_cparams(("parallel", "parallel", "arbitrary")),
        name="projection_" + kind + ("_t" if transposed else ""),
    )(x, shift, scale, w, cw, cb, cs)


def _log_sigmoid(x):
    return jnp.minimum(x, 0.0) - jnp.log(1.0 + jnp.exp(-jnp.abs(x)))


def _gate_kernel(x_ref, sh_ref, sc_ref, wg_ref, gb_ref, g_ref, xm_ref):
    _modulate_into(xm_ref, x_ref, sh_ref, sc_ref)
    T = ML_CHUNK
    hp = lax.Precision.HIGHEST
    r = lax.broadcasted_iota(jnp.int32, (T, T), 0)
    s = lax.broadcasted_iota(jnp.int32, (T, T), 1)
    tri_f = (s <= r).astype(F32)
    tri_b = (s >= r).astype(F32)
    kind = lax.broadcasted_iota(jnp.int32, (T, LANES), 1) // ML_HEADS
    for c in range(xm_ref.shape[0] // T):
        g = jnp.dot(xm_ref[c * T:(c + 1) * T, :], wg_ref[...], preferred_element_type=F32) + gb_ref[...]
        ls = _log_sigmoid(g)
        cum_f = jnp.dot(tri_f, ls, precision=hp, preferred_element_type=F32)
        cum_b = jnp.dot(tri_b, ls, precision=hp, preferred_element_type=F32)
        g_ref[0, c * T:(c + 1) * T, :] = jnp.where(kind == 1, cum_f, jnp.where(kind == 3, cum_b, g))


def gate_projection(x, shift, scale, wg, gb, *, tm):
    bv, lv, d = x.shape
    assert lv % tm == 0 and tm % ML_CHUNK == 0
    bm = shift.shape[0]
    mod_map = (lambda b, i: (b, 0, 0)) if bm == bv else (lambda b, i: (0, 0, 0))
    full = lambda a: pl.BlockSpec(a.shape, lambda b, i: (0, 0))
    return pl.pallas_call(
        _gate_kernel,
        out_shape=jax.ShapeDtypeStruct((bv, lv, LANES), F32),
        grid=(bv, lv // tm),
        in_specs=[pl.BlockSpec((1, tm, d), lambda b, i: (b, i, 0)),
                  pl.BlockSpec((1, 1, d), mod_map), pl.BlockSpec((1, 1, d), mod_map),
                  full(wg), full(gb)],
        out_specs=pl.BlockSpec((1, tm, LANES), lambda b, i: (b, i, 0)),
        scratch_shapes=[pltpu.VMEM((tm, d), BF16)],
        compiler_params=_cparams(("parallel", "parallel")),
        name="gate_projection",
    )(x, shift, scale, wg, gb)


def _hy_ffn_kernel(feat_ref, w1_ref, b1_ref, w2_ref, b2_ref, w3_ref, fr_ref, dec_ref, o_ref):
    hp = lax.Precision.HIGHEST
    feats = feat_ref[...]
    fr = fr_ref[...]
    h = jnp.sin(fr * (jnp.dot(feats, w1_ref[...], precision=hp, preferred_element_type=F32) + b1_ref[...]))
    h = jnp.sin(fr * (jnp.dot(h, w2_ref[...], precision=hp, preferred_element_type=F32) + b2_ref[...]))
    h = jnp.dot(h, w3_ref[...], precision=hp, preferred_element_type=F32)
    t = feats[:, 0:1]
    o_ref[...] = h * jnp.exp(-t * jnp.abs(dec_ref[...]))


def hyena_filter_taps(feats, w1p, b1, w2, b2, w3, freq, decay_cols):
    L = feats.shape[0]
    n = w3.shape[1]
    tl = min(L, 256)
    full = lambda a: pl.BlockSpec(a.shape, lambda i: (0, 0))
    return pl.pallas_call(
        _hy_ffn_kernel,
        out_shape=jax.ShapeDtypeStruct((L, n), F32),
        grid=(L // tl,),
        in_specs=[pl.BlockSpec((tl, feats.shape[1]), lambda i: (i, 0)),
                  full(w1p), full(b1), full(w2), full(b2), full(w3), full(freq), full(decay_cols)],
        out_specs=pl.BlockSpec((tl, n), lambda i: (i, 0)),
        compiler_params=_cparams(("parallel",)),
        name="hyena_filter_taps",
    )(feats, w1p, b1, w2, b2, w3, freq, decay_cols)


def _hy_spec_kernel(fp_ref, fq_ref, hf_ref, hb_ref, kp_ref, kq_ref, *, L):
    i = pl.program_id(1)
    hf = hf_ref[...]
    hb = hb_ref[...]
    n = lax.broadcasted_iota(jnp.int32, hb.shape, 0)
    hb = jnp.where(n == 0, 0.0, hb)
    s = (hf + hb).astype(BF16)
    dlt = (hf - hb).astype(BF16)
    kp = jnp.dot(fp_ref[...], s, preferred_element_type=F32)
    kq = jnp.dot(fq_ref[...], dlt, preferred_element_type=F32)
    sign = (1 - 2 * (n & 1)).astype(F32)
    corr = 2.0 * jnp.sum(hb * sign, axis=0, keepdims=True)
    f = lax.broadcasted_iota(jnp.int32, kq.shape, 0) + i * kq.shape[0]
    kq = kq + jnp.where(f == 0, corr, 0.0)
    kp_ref[0] = kp * (1.0 / L)
    kq_ref[0] = kq * (1.0 / L)


def hyena_filter_spectrum(fp, fq, taps, w):
    L = fp.shape[0]
    tf = min(L, 512)
    out = jax.ShapeDtypeStruct((HY_ORDER, L, w), F32)
    return pl.pallas_call(
        functools.partial(_hy_spec_kernel, L=L),
        out_shape=(out, out),
        grid=(HY_ORDER, L // tf),
        in_specs=[pl.BlockSpec((tf, L), lambda o, i: (i, 0)),
                  pl.BlockSpec((tf, L), lambda o, i: (i, 0)),
                  pl.BlockSpec((L, w), lambda o, i: (0, 2 * o)),
                  pl.BlockSpec((L, w), lambda o, i: (0, 2 * o + 1))],
        out_specs=(pl.BlockSpec((1, tf, w), lambda o, i: (o, i, 0)),
                   pl.BlockSpec((1, tf, w), lambda o, i: (o, i, 0))),
        compiler_params=_cparams(("parallel", "parallel")),
        name="hyena_filter_spectrum",
    )(fp, fq, taps, taps)


def _hy_fwd_kernel(fp_ref, fq_ref, u_ref, kp_ref, kq_ref, yp_ref, yq_ref):
    i = pl.program_id(1)
    u = u_ref[0]
    pu = jnp.dot(fp_ref[...], u, preferred_element_type=F32)
    qu = jnp.dot(fq_ref[...], u, preferred_element_type=F32)
    kp = kp_ref[0]
    kq = kq_ref[0]
    f = lax.broadcasted_iota(jnp.int32, pu.shape, 0) + i * pu.shape[0]
    dc = f == 0
    pp = pu * kp
    qq = qu * kq
    yp_ref[0] = jnp.where(dc, 0.5 * pp, pp - qq).astype(yp_ref.dtype)
    yq_ref[0] = jnp.where(dc, 0.5 * qq, pu * kq + qu * kp).astype(yq_ref.dtype)


def hyena_forward(fp, fq, u_arr, u_col, kp, kq, order):
    bv, L, _ = u_arr.shape
    w = kp.shape[2]
    tf = min(L, 512)
    out = jax.ShapeDtypeStruct((bv, L, w), BF16)
    return pl.pallas_call(
        _hy_fwd_kernel,
        out_shape=(out, out),
        grid=(bv, L // tf),
        in_specs=[pl.BlockSpec((tf, L), lambda b, i: (i, 0)),
                  pl.BlockSpec((tf, L), lambda b, i: (i, 0)),
                  pl.BlockSpec((1, L, w), lambda b, i: (b, 0, u_col)),
                  pl.BlockSpec((1, tf, w), lambda b, i: (order, i, 0)),
                  pl.BlockSpec((1, tf, w), lambda b, i: (order, i, 0))],
        out_specs=(pl.BlockSpec((1, tf, w), lambda b, i: (b, i, 0)),
                   pl.BlockSpec((1, tf, w), lambda b, i: (b, i, 0))),
        compiler_params=_cparams(("parallel", "parallel")),
        name="hyena_forward_dft",
    )(fp, fq, u_arr, kp, kq)


def _hy_inv_kernel(fpt_ref, fqt_ref, yp_ref, yq_ref, u_ref, g_ref, bias_ref, o_ref):
    y = jnp.dot(fpt_ref[...], yp_ref[0], preferred_element_type=F32)
    y = y + jnp.dot(fqt_ref[...], yq_ref[0], preferred_element_type=F32)
    u = u_ref[0].astype(F32)
    o_ref[0] = (g_ref[0].astype(F32) * (y + u * bias_ref[0])).astype(o_ref.dtype)


def hyena_inverse(fpt, fqt, yp, yq, u_arr, u_col, g_arr, g_col, bias, order):
    bv, L, w = yp.shape
    tt = min(L, 512)
    return pl.pallas_call(
        _hy_inv_kernel,
        out_shape=jax.ShapeDtypeStruct((bv, L, w), BF16),
        grid=(bv, L // tt),
        in_specs=[pl.BlockSpec((tt, L), lambda b, i: (i, 0)),
                  pl.BlockSpec((tt, L), lambda b, i: (i, 0)),
                  pl.BlockSpec((1, L, w), lambda b, i: (b, 0, 0)),
                  pl.BlockSpec((1, L, w), lambda b, i: (b, 0, 0)),
                  pl.BlockSpec((1, tt, w), lambda b, i: (b, i, u_col)),
                  pl.BlockSpec((1, tt, w), lambda b, i: (b, i, g_col)),
                  pl.BlockSpec((1, 1, w), lambda b, i: (order, 0, 0))],
        out_specs=pl.BlockSpec((1, tt, w), lambda b, i: (b, i, 0)),
        compiler_params=_cparams(("parallel", "parallel")),
        name="hyena_inverse_dft",
    )(fpt, fqt, yp, yq, u_arr, g_arr, bias)


def dft_tables(L):
    f = jnp.arange(L, dtype=jnp.int32)[:, None]
    n = jnp.arange(L, dtype=jnp.int32)[None, :]
    ang = ((f * n) % (2 * L)).astype(F32) * (math.pi / L)
    fp = jnp.cos(ang).astype(BF16)
    sn = jnp.sin(ang)
    fq = jnp.where(f == 0, (1 - 2 * (n % 2)).astype(F32), sn).astype(BF16)
    fqt = jnp.where(n == 0, (1 - 2 * (f % 2)).astype(F32), sn).astype(BF16)
    return fp, fq, fp, fqt


def hyena_features(L):
    pos = jnp.arange(L, dtype=F32)
    t = pos / (L - 1)
    bands = jnp.linspace(1e-4, HY_BANDS - 1, HY_BANDS, dtype=F32)
    ang = (2.0 * math.pi / L) * pos[:, None] * bands[None, :]
    feats = jnp.concatenate([t[:, None], jnp.cos(ang), jnp.sin(ang)], axis=-1)
    return jnp.pad(feats, ((0, 0), (0, LANES - HY_FEAT)))


def hyena_branch(z_arr, cols, tables, feats, ffn, bias):
    fp, fq, fpt, fqt = tables
    w = bias.shape[-1]
    taps = hyena_filter_taps(feats, *ffn)
    kp, kq = hyena_filter_spectrum(fp, fq, taps, w)
    bias3 = bias.reshape(HY_ORDER, 1, w)
    yp, yq = hyena_forward(fp, fq, z_arr, cols[0], kp, kq, 0)
    y1 = hyena_inverse(fpt, fqt, yp, yq, z_arr, cols[0], z_arr, cols[1], bias3, 0)
    yp, yq = hyena_forward(fp, fq, y1, 0, kp, kq, 1)
    return hyena_inverse(fpt, fqt, yp, yq, y1, 0, z_arr, cols[2], bias3, 1)


def _mlstm_chunk(q, k, v1t, li_row, b_row, r_col, c, m, reverse):
    T = q.shape[0]
    s_idx = lax.broadcasted_iota(jnp.int32, (T, T), 0)
    t_idx = lax.broadcasted_iota(jnp.int32, (T, T), 1)
    mask = (s_idx >= t_idx) if reverse else (s_idx <= t_idx)
    nt = (((1,), (1,)), ((), ()))
    kq = lax.dot_general(k, q, nt, preferred_element_type=F32)
    d = jnp.where(mask, r_col + b_row, -1e30)
    m_loc = jnp.max(d, axis=0, keepdims=True)
    st = (kq * jnp.exp(d - m_loc)).astype(BF16)
    intra = jnp.dot(v1t, st, preferred_element_type=F32)
    g = b_row + m
    m_t = jnp.maximum(g, m_loc)
    inter = lax.dot_general(c.astype(BF16), q, nt, preferred_element_type=F32)
    num = jnp.exp(g - m_t) * inter + jnp.exp(m_loc - m_t) * intra
    den = num[HEAD_PAD - 1:HEAD_PAD, :]
    h = num * (1.0 / jnp.maximum(jnp.abs(den), jnp.exp(-m_t)))
    btot = b_row[:, 0:1] if reverse else b_row[:, T - 1:T]
    a = btot - b_row + li_row
    a_max = jnp.max(a, axis=1, keepdims=True)
    wv = (v1t.astype(F32) * jnp.exp(a - a_max)).astype(BF16)
    delta = jnp.dot(wv, k, preferred_element_type=F32)
    m_new = jnp.maximum(btot + m, a_max)
    c_new = jnp.exp(btot + m - m_new) * c + jnp.exp(a_max - m_new) * delta
    return h, c_new, m_new


def _mlstm_kernel(qc_ref, kc_ref, vc_ref, oc_ref, gcc_ref, grc_ref,
                  ql_ref, kl_ref, vl_ref, ol_ref, gcl_ref, grl_ref, ng_ref,
                  outc_ref, outl_ref, hf_ref, hb_ref, *, dh):
    T = ML_CHUNK
    L = ql_ref.shape[1]
    row = lax.broadcasted_iota(jnp.int32, (HEAD_PAD, T), 0)
    ones_row = row == HEAD_PAD - 1
    valid = row < dh
    ng = ng_ref[...]

    def gates(gc, gr, bwd):
        i = 2 if bwd else 0
        return gr[i:i + 1, :], gr[i + 1:i + 2, :], gc[:, i:i + 1] - gc[:, i + 1:i + 2]

    def finish(h, o):
        h = jnp.where(valid, h, 0.0)
        mu = jnp.sum(h, axis=0, keepdims=True) * (1.0 / dh)
        dlt = jnp.where(valid, h - mu, 0.0)
        var = jnp.sum(dlt * dlt, axis=0, keepdims=True) * (1.0 / dh)
        return (dlt * lax.rsqrt(var + LN_EPS) * ng * o.astype(F32)).astype(BF16)

    def aug(vt):
        return jnp.where(ones_row, jnp.ones_like(vt), vt)

    zero_c = jnp.zeros((HEAD_PAD, HEAD_PAD), F32)
    zero_m = jnp.zeros((1, 1), F32)

    q, k, v1t = qc_ref[0], kc_ref[0], aug(vc_ref[0])
    gc, gr = gcc_ref[0, 0], grc_ref[0, 0]
    h_f, c_f, m_f = _mlstm_chunk(q, k, v1t, *gates(gc, gr, False), zero_c, zero_m, False)
    h_b, c_b, m_b = _mlstm_chunk(q, k, v1t, *gates(gc, gr, True), zero_c, zero_m, True)
    outc_ref[0] = finish(h_f + h_b, oc_ref[0])

    nc = L // T
    for ci in range(nc):
        for bwd in (False, True):
            lo = (nc - 1 - ci) * T if bwd else ci * T
            gts = gates(gcl_ref[0, 0, lo:lo + T, :], grl_ref[0, 0, :, lo:lo + T], bwd)
            q, k, v1t = ql_ref[0, lo:lo + T, :], kl_ref[0, lo:lo + T, :], aug(vl_ref[0, :, lo:lo + T])
            if bwd:
                h_b, c_b, m_b = _mlstm_chunk(q, k, v1t, *gts, c_b, m_b, True)
                hb_ref[:, lo:lo + T] = h_b
            else:
                h_f, c_f, m_f = _mlstm_chunk(q, k, v1t, *gts, c_f, m_f, False)
                hf_ref[:, lo:lo + T] = h_f
    for ci in range(nc):
        lo = ci * T
        outl_ref[0, :, lo:lo + T] = finish(hf_ref[:, lo:lo + T] + hb_ref[:, lo:lo + T], ol_ref[0, :, lo:lo + T])


def mlstm_branch(qk_c, vt_c, ot_c, gate_c, qk_l, vt_l, ot_l, gate_l, norm_g_col, dh):
    bv, lc, _ = qk_c.shape
    L = qk_l.shape[1]
    H = ML_HEADS
    assert lc == ML_CHUNK and L % ML_CHUNK == 0

    def per_head(g):
        g = g[:, :, :4 * H].reshape(bv, -1, 4, H)
        return g.transpose(0, 3, 1, 2), g.transpose(0, 3, 2, 1)

    gcc, grc = per_head(gate_c)
    gcl, grl = per_head(gate_l)

    def qk_blk(length, part):
        return pl.BlockSpec((1, length, HEAD_PAD), lambda b, h: (b, 0, part * H + h))

    def t_blk(length):
        return pl.BlockSpec((1, HEAD_PAD, length), lambda b, h: (b, h, 0))

    def gspecs(length):
        return [pl.BlockSpec((1, 1, length, 4), lambda b, h: (b, h, 0, 0)),
                pl.BlockSpec((1, 1, 4, length), lambda b, h: (b, h, 0, 0))]

    def specs(length):
        return [qk_blk(length, 0), qk_blk(length, 1), t_blk(length), t_blk(length)] + gspecs(length)

    return pl.pallas_call(
        functools.partial(_mlstm_kernel, dh=dh),
        out_shape=(jax.ShapeDtypeStruct((bv, H * HEAD_PAD, lc), BF16),
                   jax.ShapeDtypeStruct((bv, H * HEAD_PAD, L), BF16)),
        grid=(bv, H),
        in_specs=specs(lc) + specs(L) + [pl.BlockSpec((HEAD_PAD, 1), lambda b, h: (h, 0))],
        out_specs=(t_blk(lc), t_blk(L)),
        scratch_shapes=[pltpu.VMEM((HEAD_PAD, L), F32), pltpu.VMEM((HEAD_PAD, L), F32)],
        compiler_params=_cparams(("parallel", "parallel")),
        name="mlstm_scan",
    )(qk_c, qk_c, vt_c, ot_c, gcc, grc, qk_l, qk_l, vt_l, ot_l, gcl, grl, norm_g_col)


def _s5_kernel(uc_ref, ul_ref, prow_ref, pcol_ref, bt_ref, cr_ref, cc_ref, y_ref,
               toep_ref, min_ref, mout_ref, kv_ref, s_ref, x_ref, u_ref, yv_ref, *, nb, nctx, nchunk):
    T, CG, P = S5_CHUNK, S5_GROUP, S5_STATE

    ch0 = 0
    for src in (uc_ref, ul_ref):
        xs = pltpu.einshape("bct->cbt", src[...].astype(F32))
        for ch in range(xs.shape[2] // T):
            for ci in range(CG):
                u_ref[(ch0 + ch) * nb:(ch0 + ch + 1) * nb, ci * T:(ci + 1) * T] = (
                    xs[ci, :, ch * T:(ch + 1) * T].astype(BF16))
        ch0 += xs.shape[2] // T

    hp = lax.Precision.HIGHEST
    lane = lax.broadcasted_iota(jnp.int32, (1, 2 * P), 1)
    re_lane = lane < P
    re_row = lax.broadcasted_iota(jnp.int32, (2 * P, 1), 0) < P

    def powers_rows(n, mag1, th1):
        m = jnp.exp(n * mag1)
        return m * jnp.cos(n * th1), m * jnp.sin(n * th1)

    def powers_cols(n, mag1, th1):
        m = jnp.exp(n * mag1)
        return m * jnp.cos(n * th1), m * jnp.sin(n * th1)

    at = []
    kv = jnp.zeros((CG * CG, 2 * T), F32)
    for d in range(2):
        a_re, a_im, ldt = prow_ref[0, d, 0:1, :], prow_ref[0, d, 1:2, :], prow_ref[0, d, 2:3, :]
        dt = jnp.exp(ldt)
        mag1, th1 = dt * a_re, dt * a_im
        ab_re, ab_im = jnp.exp(mag1) * jnp.cos(th1), jnp.exp(mag1) * jnp.sin(th1)
        den = a_re * a_re + a_im * a_im
        co_re = ((ab_re - 1.0) * a_re + ab_im * a_im) / den
        co_im = (ab_im * a_re - (ab_re - 1.0) * a_im) / den
        b_re, b_im = bt_ref[0, d, 0], bt_ref[0, d, 1]
        bb_re = co_re * b_re - co_im * b_im
        bb_im = co_re * b_im + co_im * b_re
        c_re, c_im = cr_ref[0, d, 0], cr_ref[0, d, 1]
        a_re_c, a_im_c = pcol_ref[0, d, :, 0:1], pcol_ref[0, d, :, 1:2]
        dt_c = jnp.exp(pcol_ref[0, d, :, 2:3])
        mag1_c, th1_c = dt_c * a_re_c, dt_c * a_im_c

        rep = lambda a: jnp.broadcast_to(a[:, None, :], (CG, CG, 2 * P)).reshape(CG * CG, 2 * P)
        til = lambda a: jnp.broadcast_to(a[None, :, :], (CG, CG, 2 * P)).reshape(CG * CG, 2 * P)
        w_re = rep(bb_re) * til(c_re) - rep(bb_im) * til(c_im)
        w_im = rep(bb_re) * til(c_im) + rep(bb_im) * til(c_re)
        w = jnp.where(re_lane, w_re, -w_im)
        lagp = lax.broadcasted_iota(jnp.int32, (1, 2 * T), 1)
        n = (T - 1 - lagp) if d else (lagp - (T - 1))
        ok = (n >= 0) & (lagp < 2 * T - 1)
        pc, ps = powers_cols(jnp.maximum(n, 0).astype(F32), mag1_c, th1_c)
        pw = jnp.where(ok, jnp.where(re_row, pc, ps), 0.0)
        kv = kv + jnp.dot(w, pw, precision=hp, preferred_element_type=F32)

        s_col = lax.broadcasted_iota(jnp.int32, (T, 1), 0)
        pc, ps = powers_rows((s_col if d else T - 1 - s_col).astype(F32), mag1, th1)
        for ci in range(CG):
            br, bi = bb_re[ci:ci + 1, :], bb_im[ci:ci + 1, :]
            blk = jnp.where(re_lane, br * pc - bi * ps, br * ps + bi * pc)
            min_ref[d, ci * T:(ci + 1) * T, :] = blk.astype(BF16)

        t_row = lax.broadcasted_iota(jnp.int32, (1, T), 1)
        pc, ps = powers_cols((T - t_row if d else t_row + 1).astype(F32), mag1_c, th1_c)
        for co in range(CG):
            cr, ci_ = cc_ref[0, d, 0, :, co:co + 1], cc_ref[0, d, 1, :, co:co + 1]
            blk = jnp.where(re_row, cr * pc - ci_ * ps, -(cr * ps + ci_ * pc))
            mout_ref[d, :, co * T:(co + 1) * T] = blk.astype(BF16)

        mt = jnp.exp(T * mag1)
        at.append(((mt * jnp.cos(T * th1))[:, :P], (mt * jnp.sin(T * th1))[:, :P]))

    kv_ref[...] = kv

    def build(ci, carry):
        r0 = pl.multiple_of(ci * T, T)
        for co in range(CG):
            vec = kv_ref[pl.ds(ci * CG + co, 1), :]
            rolled = pltpu.roll(jnp.broadcast_to(vec, (T, 2 * T)), T + 1, 1, stride=1, stride_axis=0)
            toep_ref[pl.ds(r0, T), co * T:(co + 1) * T] = rolled[:, :T].astype(BF16)
        return carry

    lax.fori_loop(0, CG, build, 0)

    u = u_ref[...]
    y = jnp.dot(u, toep_ref[...], preferred_element_type=F32)
    for d in range(2):
        s_ref[d] = jnp.dot(u, min_ref[d], preferred_element_type=F32)

    orders = (list(range(nchunk)), list(range(nctx - 1, -1, -1)) + list(range(nchunk - 1, nctx - 1, -1)))
    for d in range(2):
        at_re, at_im = at[d]
        x_re = jnp.zeros((nb, P), F32)
        x_im = jnp.zeros((nb, P), F32)
        for ch in orders[d]:
            x_ref[d, ch * nb:(ch + 1) * nb, :] = jnp.concatenate([x_re, x_im], axis=1)
            s = s_ref[d, ch * nb:(ch + 1) * nb, :]
            x_re, x_im = (at_re * x_re - at_im * x_im + s[:, :P], at_re * x_im + at_im * x_re + s[:, P:])
        y = y + jnp.dot(x_ref[d].astype(BF16), mout_ref[d], preferred_element_type=F32)

    for ch in range(nchunk):
        for co in range(CG):
            yv_ref[co, :, ch * T:(ch + 1) * T] = y[ch * nb:(ch + 1) * nb, co * T:(co + 1) * T]
    y_ref[...] = pltpu.einshape("cbt->bct", yv_ref[...]).astype(y_ref.dtype)


def s5_mix(ut_c, ut_l, a_re, a_im, log_dt, b_re, b_im, c_re, c_im):
    nb, w, lc = ut_c.shape
    L = ut_l.shape[2]
    T, CG, P = S5_CHUNK, S5_GROUP, S5_STATE
    G = w // CG
    nctx, nlat = lc // T, L // T
    nchunk = nctx + nlat

    dup = lambda a: jnp.concatenate([a, a], axis=-1)
    f32 = lambda a: a.astype(F32)
    ldt = jnp.broadcast_to(f32(log_dt)[:, :, None], (2, G, P))
    prow = jnp.stack([dup(f32(a_re)), dup(f32(a_im)), dup(ldt)], axis=2)
    prow = jnp.pad(prow, ((0, 0), (0, 0), (0, 5), (0, 0))).transpose(1, 0, 2, 3)
    pcol = prow.transpose(0, 1, 3, 2)
    bt = jnp.stack([dup(f32(b_re).transpose(0, 1, 3, 2)), dup(f32(b_im).transpose(0, 1, 3, 2))], axis=2)
    bt = bt.transpose(1, 0, 2, 3, 4)
    cr = jnp.stack([dup(f32(c_re)), dup(f32(c_im))], axis=2).transpose(1, 0, 2, 3, 4)
    cc = cr.transpose(0, 1, 2, 4, 3)

    blk = lambda a: pl.BlockSpec((1,) + a.shape[1:], lambda g: (g,) + (0,) * (a.ndim - 1))
    grp = lambda length: pl.BlockSpec((nb, CG, length), lambda g: (0, g, 0))
    r = nchunk * nb
    return pl.pallas_call(
        functools.partial(_s5_kernel, nb=nb, nctx=nctx, nchunk=nchunk),
        out_shape=jax.ShapeDtypeStruct((nb, w, lc + L), BF16),
        grid=(G,),
        in_specs=[grp(lc), grp(L), blk(prow), blk(pcol), blk(bt), blk(cr), blk(cc)],
        out_specs=grp(lc + L),
        scratch_shapes=[pltpu.VMEM((CG * T, CG * T), BF16), pltpu.VMEM((2, CG * T, 2 * P), BF16),
                        pltpu.VMEM((2, 2 * P, CG * T), BF16), pltpu.VMEM((CG * CG, 2 * T), F32),
                        pltpu.VMEM((2, r, 2 * P), F32), pltpu.VMEM((2, r, 2 * P), F32),
                        pltpu.VMEM((r, CG * T), BF16), pltpu.VMEM((CG, nb, lc + L), F32)],
        compiler_params=_cparams(("parallel",)),
        name="s5_mix",
    )(ut_c, ut_l, prow, pcol, bt, cr, cc)


def _gelu_tanh(x):
    return 0.5 * x * (1.0 + jnp.tanh(math.sqrt(2.0 / math.pi) * (x + 0.044715 * (x * x * x))))


def _merge_kernel(yhy_ref, yml_ref, ys_ref, u_ref, ghy_ref, gml_ref, gs5_ref,
                  d_ref, gw_ref, gb_ref, why_ref, wml_ref, ws5_ref, o_ref):
    tn = (((0,), (0,)), ((), ()))
    ys = ys_ref[0].astype(F32) + d_ref[...] * u_ref[0].astype(F32)
    z = _gelu_tanh(ys).astype(BF16)
    glu = jnp.dot(gw_ref[...], z, preferred_element_type=F32) + gb_ref[...]
    y_s5 = (z.astype(F32) * _sigmoid(glu)).astype(BF16)
    acc = ghy_ref[0].astype(F32) * jnp.dot(yhy_ref[0], why_ref[...], preferred_element_type=F32)
    acc = acc + gml_ref[0].astype(F32) * lax.dot_general(yml_ref[0], wml_ref[...], tn, preferred_element_type=F32)
    acc = acc + gs5_ref[0].astype(F32) * lax.dot_general(y_s5, ws5_ref[...], tn, preferred_element_type=F32)
    o_ref[0] = acc.astype(o_ref.dtype)


def merge_branches(y_hy, y_ml, ys_t, u_t, t_off, gate_arr, s5_d, glu_wt, glu_b, w_hy, w_ml, w_s5):
    bv, lv, w = y_hy.shape
    dm = w_hy.shape[1]
    tm = ML_CHUNK
    full = lambda a: pl.BlockSpec(a.shape, lambda b, i: (0,) * a.ndim)
    gate = lambda c: pl.BlockSpec((1, tm, dm), lambda b, i: (b, i, c))
    return pl.pallas_call(
        _merge_kernel,
        out_shape=jax.ShapeDtypeStruct((bv, lv, dm), BF16),
        grid=(bv, lv // tm),
        in_specs=[pl.BlockSpec((1, tm, w), lambda b, i: (b, i, 0)),
                  pl.BlockSpec((1, y_ml.shape[1], tm), lambda b, i: (b, 0, i)),
                  pl.BlockSpec((1, w, tm), lambda b, i: (b, 0, i + t_off)),
                  pl.BlockSpec((1, w, tm), lambda b, i: (b, 0, i)),
                  gate(0), gate(1), gate(2),
                  full(s5_d), full(glu_wt), full(glu_b), full(w_hy), full(w_ml), full(w_s5)],
        out_specs=pl.BlockSpec((1, tm, dm), lambda b, i: (b, i, 0)),
        compiler_params=_cparams(("parallel", "parallel")),
        name="merge_branches",
    )(y_hy, y_ml, ys_t, u_t, gate_arr, gate_arr, gate_arr, s5_d, glu_wt, glu_b, w_hy, w_ml, w_s5)


def _outproj_kernel(m_ref, w_ref, x_ref, gate_ref, g_ref, b_ref, o_ref, *, alpha):
    rb = min(m_ref.shape[1], LANES)
    for s in range(m_ref.shape[1] // rb):
        rows = slice(s * rb, (s + 1) * rb)
        y = jnp.dot(m_ref[0, rows, :], w_ref[...], preferred_element_type=F32)
        r = alpha * x_ref[0, rows, :] + gate_ref[0] * y
        o_ref[0, rows, :] = _layer_norm(r, g_ref[...], b_ref[...])


def out_projection(merged, w_out, x, gate, ln_g, ln_b, alpha, tm):
    bv, lv, d = x.shape
    bm = gate.shape[0]
    mod_map = (lambda b, i: (b, 0, 0)) if bm == bv else (lambda b, i: (0, 0, 0))
    vec = lambda: pl.BlockSpec((1, d), lambda b, i: (0, 0))
    return pl.pallas_call(
        functools.partial(_outproj_kernel, alpha=alpha),
        out_shape=jax.ShapeDtypeStruct((bv, lv, d), F32),
        grid=(bv, lv // tm),
        in_specs=[pl.BlockSpec((1, tm, d), lambda b, i: (b, i, 0)),
                  pl.BlockSpec((d, d), lambda b, i: (0, 0)),
                  pl.BlockSpec((1, tm, d), lambda b, i: (b, i, 0)),
                  pl.BlockSpec((1, 1, d), mod_map), vec(), vec()],
        out_specs=pl.BlockSpec((1, tm, d), lambda b, i: (b, i, 0)),
        compiler_params=_cparams(("parallel", "parallel")),
        name="out_projection_ln",
    )(merged, w_out, x, gate, ln_g, ln_b)


def _mlp_kernel(x_ref, sh_ref, sc_ref, gate_ref, w1_ref, w2_ref, g_ref, b_ref, o_ref, xm_ref, acc_ref, *, alpha):
    k = pl.program_id(2)

    @pl.when(k == 0)
    def _():
        xm_ref[...] = (x_ref[0] * (1.0 + sc_ref[0]) + sh_ref[0]).astype(BF16)
        acc_ref[...] = jnp.zeros_like(acc_ref)

    h = jnp.maximum(jnp.dot(xm_ref[...], w1_ref[...], preferred_element_type=F32), 0.0)
    acc_ref[...] += jnp.dot((h * h).astype(BF16), w2_ref[...], preferred_element_type=F32)

    @pl.when(k == pl.num_programs(2) - 1)
    def _():
        r = alpha * x_ref[0] + gate_ref[0] * acc_ref[...]
        o_ref[0] = _layer_norm(r, g_ref[...], b_ref[...])


def mlp_block(x, shift, scale, gate, w1, w2, ln_g, ln_b, alpha, tm, kf):
    bv, lv, d = x.shape
    dff = w1.shape[1]
    bm = gate.shape[0]
    mod_map = (lambda b, i, k: (b, 0, 0)) if bm == bv else (lambda b, i, k: (0, 0, 0))
    vec = lambda: pl.BlockSpec((1, d), lambda b, i, k: (0, 0))
    return pl.pallas_call(
        functools.partial(_mlp_kernel, alpha=alpha),
        out_shape=jax.ShapeDtypeStruct((bv, lv, d), F32),
        grid=(bv, lv // tm, dff // kf),
        in_specs=[pl.BlockSpec((1, tm, d), lambda b, i, k: (b, i, 0)),
                  pl.BlockSpec((1, 1, d), mod_map), pl.BlockSpec((1, 1, d), mod_map),
                  pl.BlockSpec((1, 1, d), mod_map),
                  pl.BlockSpec((d, kf), lambda b, i, k: (0, k)),
                  pl.BlockSpec((kf, d), lambda b, i, k: (k, 0)),
                  vec(), vec()],
        out_specs=pl.BlockSpec((1, tm, d), lambda b, i, k: (b, i, 0)),
        scratch_shapes=[pltpu.VMEM((tm, d), BF16), pltpu.VMEM((tm, d), F32)],
        compiler_params=_cparams(("parallel", "parallel", "arbitrary")),
        name="mlp_ln",
    )(x, shift, scale, gate, w1, w2, ln_g, ln_b)


def _pad_heads(a, axis, dh):
    shp = a.shape
    a = a.reshape(shp[:axis] + (ML_HEADS, dh) + shp[axis + 1:])
    pad = [(0, 0)] * a.ndim
    pad[axis + 1] = (0, HEAD_PAD - dh)
    a = jnp.pad(a, pad)
    return a.reshape(shp[:axis] + (ML_HEADS * HEAD_PAD,) + shp[axis + 1:])


def _layer_params(l, p, w_hy, w_ml, w_s5, d_model):
    dh = w_ml // ML_HEADS
    sizes = ((HY_ORDER + 1) * w_hy, 2 * w_ml, w_ml, w_ml, 4 * ML_HEADS, w_s5, N_BRANCH * d_model)
    pts = [0]
    for s in sizes:
        pts.append(pts[-1] + s)
    w_in = p["w_in"][l]
    hy, qk, v, o, gt, u, mg = (w_in[:, pts[i]:pts[i + 1]] for i in range(7))
    ph = lambda a: _pad_heads(a, a.ndim - 1, dh)
    none = lambda n: (jnp.zeros((3, n), F32), jnp.zeros((1, n), F32), jnp.ones((1, n), F32))
    half = ML_HEADS * HEAD_PAD
    cw, cb = p["ml_conv_w"][l], p["ml_conv_b"][l]
    out = {}
    out["mg"] = (mg.astype(BF16),) + none(mg.shape[1])
    out["hy"] = (hy.astype(BF16), p["hy_conv_w"][l], p["hy_conv_b"][l].reshape(1, -1), jnp.ones((1, hy.shape[1]), F32))
    out["qk"] = (jnp.concatenate([ph(qk[:, :w_ml]), ph(qk[:, w_ml:])], axis=1).astype(BF16),
                 jnp.concatenate([ph(cw[:, :w_ml]), ph(cw[:, w_ml:])], axis=1),
                 jnp.concatenate([ph(cb[:w_ml]), ph(cb[w_ml:])]).reshape(1, -1),
                 jnp.concatenate([jnp.ones((half,), F32), jnp.full((half,), dh ** -0.5, F32)]).reshape(1, -1))
    out["vt"] = (ph(v).T.astype(BF16),) + none(half)
    out["ot"] = (ph(o).T.astype(BF16),) + none(half)
    ngt = gt.shape[1]
    out["ut"] = (u.T.astype(BF16),) + none(u.shape[1])
    out["gt"] = (jnp.pad(gt, ((0, 0), (0, LANES - ngt))).astype(BF16),
                 jnp.pad(p["ml_gate_b"][l].reshape(-1), ((0, LANES - ngt),)).reshape(1, -1))
    out["norm_g"] = ph(p["ml_norm_g"][l]).reshape(-1, 1)
    out["w_ml_out"] = _pad_heads(p["w_ml_out"][l], 0, dh).astype(BF16)
    w3 = p["hy_ffn_w3"][l]
    dec = p["hy_decay"][l]
    dec_cols = jnp.broadcast_to(dec[:, None, :], (HY_ORDER, 2, w_hy)).reshape(1, -1)
    hpad = LANES - HY_HIDDEN
    row = lambda a: jnp.pad(a.reshape(1, -1), ((0, 0), (0, hpad)))
    out["hy_ffn"] = (jnp.pad(p["hy_ffn_w1"][l], ((0, LANES - HY_FEAT), (0, hpad))), row(p["hy_ffn_b1"][l]),
                     jnp.pad(p["hy_ffn_w2"][l], ((0, hpad), (0, hpad))), row(p["hy_ffn_b2"][l]),
                     jnp.pad(w3, ((0, hpad), (0, 0))), row(p["hy_sin_freq"][l]), dec_cols)
    return out


def kernel(x, c, ctx, c_ctx, w_mod, b_mod, w_in, hy_conv_w, hy_conv_b, hy_ffn_w1, hy_ffn_b1, hy_ffn_w2,
           hy_ffn_b2, hy_ffn_w3, hy_sin_freq, hy_decay, hy_bias, ml_conv_w, ml_conv_b, ml_gate_b, ml_norm_g,
           s5_a_re, s5_a_im, s5_log_dt, s5_b_re, s5_b_im, s5_c_re, s5_c_im, s5_d, s5_glu_w, s5_glu_b,
           w_hy_out, w_ml_out, w_s5_out, w_out, ln1_g, ln1_b, ln2_g, ln2_b, w_ff1, w_ff2):
    p = dict(w_in=w_in, hy_conv_w=hy_conv_w, hy_conv_b=hy_conv_b, hy_ffn_w1=hy_ffn_w1, hy_ffn_b1=hy_ffn_b1,
             hy_ffn_w2=hy_ffn_w2, hy_ffn_b2=hy_ffn_b2, hy_ffn_w3=hy_ffn_w3, hy_sin_freq=hy_sin_freq,
             hy_decay=hy_decay, ml_conv_w=ml_conv_w, ml_conv_b=ml_conv_b, ml_gate_b=ml_gate_b,
             ml_norm_g=ml_norm_g, w_ml_out=w_ml_out)
    B, L, D = x.shape
    LC = ctx.shape[1]
    depth = w_in.shape[0]
    w_hy = hy_bias.shape[-1]
    w_ml = ml_norm_g.shape[-1]
    w_s5 = s5_d.shape[-1]
    dh = w_ml // ML_HEADS
    alpha = (2 * depth) ** 0.25
    half = ML_HEADS * HEAD_PAD
    tm_l, tm_c = 1024, LC

    tab_l, tab_c = dft_tables(L), dft_tables(LC)
    feats_l, feats_c = hyena_features(L), hyena_features(LC)

    mrows = 8 * ((B + 1 + 7) // 8)
    c_rows = jnp.zeros((mrows, D), F32).at[:B].set(c).at[B].set(c_ctx)

    for l in range(depth):
        need_ctx = l < depth - 1
        lp = _layer_params(l, p, w_hy, w_ml, w_s5, D)
        mod = mod_vectors(c_rows, w_mod, b_mod.reshape(depth, 1, -1), l)
        mod_l = [mod[:B, k * D:(k + 1) * D].reshape(B, 1, D) for k in range(6)]
        mod_c = [mod[B:B + 1, k * D:(k + 1) * D].reshape(1, 1, D) for k in range(6)]

        def mixer_inputs(xv, md, row_len, tm, full):
            proj = functools.partial(projection, xv, md[0], md[1], tm=tm, row_len=row_len)
            r = {}
            r["qk"] = proj(*lp["qk"], kind="convsilu", tn=half)
            r["vt"] = proj(*lp["vt"], kind="plain", tn=half, transposed=True)
            r["ot"] = proj(*lp["ot"], kind="sigmoid", tn=half, transposed=True)
            r["ut"] = proj(*lp["ut"], kind="plain", tn=w_s5, transposed=True)
            r["g"] = gate_projection(xv, md[0], md[1], *lp["gt"], tm=tm)
            if full:
                r["mg"] = proj(*lp["mg"], kind="sigmoid", tn=2 * w_hy)
                r["z"] = proj(*lp["hy"], kind="conv", tn=w_hy)
            return r

        r_l = mixer_inputs(x, mod_l, GRID_W, tm_l, True)
        r_c = mixer_inputs(ctx, mod_c, LC, tm_c, need_ctx)

        hm_c, hm_l = mlstm_branch(r_c["qk"], r_c["vt"], r_c["ot"], r_c["g"],
                                  r_l["qk"], r_l["vt"], r_l["ot"], r_l["g"], lp["norm_g"], dh)

        ys_t = s5_mix(r_c["ut"], r_l["ut"], s5_a_re[l], s5_a_im[l], s5_log_dt[l],
                      s5_b_re[l], s5_b_im[l], s5_c_re[l], s5_c_im[l])

        s5_dv = s5_d[l].reshape(-1, 1)
        glu_wt = s5_glu_w[l].T.astype(BF16)
        glu_b = s5_glu_b[l].reshape(-1, 1)
        w_hy_o = w_hy_out[l].astype(BF16)
        w_s5_o = w_s5_out[l].astype(BF16)
        w_o = w_out[l].astype(BF16)
        w1 = w_ff1[l].astype(BF16)
        w2 = w_ff2[l].astype(BF16)
        g1, b1 = ln1_g[l].reshape(1, -1), ln1_b[l].reshape(1, -1)
        g2, b2 = ln2_g[l].reshape(1, -1), ln2_b[l].reshape(1, -1)

        def finish(xv, r, hm, t_off, tables, feats, md, tm):
            y_hy = hyena_branch(r["z"], (0, 1, 2), tables, feats, lp["hy_ffn"], hy_bias[l])
            merged = merge_branches(y_hy, hm, ys_t, r["ut"], t_off, r["mg"], s5_dv, glu_wt, glu_b,
                                    w_hy_o, lp["w_ml_out"], w_s5_o)
            if md[2].shape[0] == 1:
                merged, xv = merged.reshape(1, -1, D), xv.reshape(1, -1, D)
            x1 = out_projection(merged, w_o, xv, md[2], g1, b1, alpha, tm)
            return mlp_block(x1, md[3], md[4], md[5], w1, w2, g2, b2, alpha, tm, 1024)

        x = finish(x, r_l, hm_l, LC // ML_CHUNK, tab_l, feats_l, mod_l, 512)
        if need_ctx:
            ctx = finish(ctx, r_c, hm_c, 0, tab_c, feats_c, mod_c, 512).reshape(B, LC, D)
    return x
```

```python
import functools
import math

import jax
import jax.numpy as jnp
from jax import lax
from jax.experimental import pallas as pl
from jax.experimental.pallas import tpu as pltpu

F32 = jnp.float32
BF16 = jnp.bfloat16

GRID_W = 64
HY_ORDER = 2
HY_BANDS = 16
HY_FEAT = 1 + 2 * HY_BANDS
HY_HIDDEN = 64
ML_HEADS = 4
ML_CHUNK = 256
S5_GROUP = 16
S5_STATE = 64
N_BRANCH = 3
LN_EPS = 1e-5

LANES = 128
MXU_DIM = 256
VMEM_LIMIT = 56 * 1024 * 1024

HEAD_PAD = MXU_DIM
S5_CHUNK = LANES


def _cparams(sem):
    return pltpu.CompilerParams(dimension_semantics=sem, vmem_limit_bytes=VMEM_LIMIT)


def _sigmoid(x):
    return 1.0 / (1.0 + jnp.exp(-x))


def _layer_norm(r, g, b):
    mu = jnp.mean(r, axis=-1, keepdims=True)
    d = r - mu
    var = jnp.mean(d * d, axis=-1, keepdims=True)
    return d * lax.rsqrt(var + LN_EPS) * g + b


def _mod_kernel(c_ref, w_ref, b_ref, o_ref):
    c = c_ref[...]
    a = (c * _sigmoid(c)).astype(BF16)
    o_ref[...] = jnp.dot(a, w_ref[0].astype(BF16), preferred_element_type=F32) + b_ref[0]


def mod_vectors(c_rows, w, b, layer):
    m, d = c_rows.shape
    n = w.shape[2]
    tn = 1024
    return pl.pallas_call(
        _mod_kernel,
        out_shape=jax.ShapeDtypeStruct((m, n), F32),
        grid=(n // tn,),
        in_specs=[pl.BlockSpec((m, d), lambda j: (0, 0)),
                  pl.BlockSpec((1, d, tn), lambda j: (layer, 0, j)),
                  pl.BlockSpec((1, 1, tn), lambda j: (layer, 0, j))],
        out_specs=pl.BlockSpec((m, tn), lambda j: (0, j)),
        compiler_params=_cparams(("parallel",)),
        name="mod_vectors",
    )(c_rows, w, b)


def _short_conv(acc, w, b, row_len):
    rows = acc.shape[0]
    t = lax.broadcasted_iota(jnp.int32, acc.shape, 0) & (row_len - 1)
    prev = jnp.where(t == 0, 0.0, pltpu.roll(acc, 1, 0))
    nxt = jnp.where(t == row_len - 1, 0.0, pltpu.roll(acc, rows - 1, 0))
    return prev * w[0:1] + acc * w[1:2] + nxt * w[2:3] + b


def _modulate_rows(xm_ref, x_ref, sh_ref, sc_ref, rows):
    xm_ref[rows, :] = (x_ref[0, rows, :] * (1.0 + sc_ref[0]) + sh_ref[0]).astype(BF16)


def _modulate_into(xm_ref, x_ref, sh_ref, sc_ref):
    _modulate_rows(xm_ref, x_ref, sh_ref, sc_ref, slice(None))


def _proj_kernel(x_ref, sh_ref, sc_ref, w_ref, cw_ref, cb_ref, cs_ref, o_ref, xm_ref, *, kind, row_len, rb,
                 transposed, single_tile):
    def epilogue(acc):
        if kind == "conv":
            return _short_conv(acc, cw_ref[...], cb_ref[...], row_len)
        if kind == "convsilu":
            y = _short_conv(acc, cw_ref[...], cb_ref[...], row_len)
            return y * _sigmoid(y) * cs_ref[...]
        if kind == "sigmoid":
            return _sigmoid(acc)
        return acc

    def body(modulate):
        for r in range(xm_ref.shape[0] // rb):
            rows = slice(r * rb, (r + 1) * rb)
            if modulate:
                _modulate_rows(xm_ref, x_ref, sh_ref, sc_ref, rows)
            xs = xm_ref[rows, :]
            if transposed:
                acc = lax.dot_general(w_ref[...], xs, (((1,), (1,)), ((), ())), preferred_element_type=F32)
                o_ref[0, :, rows] = epilogue(acc).astype(o_ref.dtype)
            else:
                acc = jnp.dot(xs, w_ref[...], preferred_element_type=F32)
                o_ref[0, rows, :] = epilogue(acc).astype(o_ref.dtype)

    if single_tile:
        body(True)
    else:
        first = pl.program_id(2) == 0
        pl.when(first)(lambda: body(True))
        pl.when(jnp.logical_not(first))(lambda: body(False))


def projection(x, shift, scale, w, cw, cb, cs, *, kind, tn, tm, row_len, transposed=False):
    bv, lv, d = x.shape
    n = w.shape[0] if transposed else w.shape[1]
    rb = min(tm, ML_CHUNK)
    assert n % tn == 0 and lv % tm == 0 and rb % row_len == 0
    bm = shift.shape[0]
    mod_map = (lambda b, i, j: (b, 0, 0)) if bm == bv else (lambda b, i, j: (0, 0, 0))
    if transposed:
        assert kind in ("plain", "sigmoid")
        w_spec = pl.BlockSpec((tn, d), lambda b, i, j: (j, 0))
        out_shape = jax.ShapeDtypeStruct((bv, n, lv), BF16)
        out_spec = pl.BlockSpec((1, tn, tm), lambda b, i, j: (b, j, i))
    else:
        w_spec = pl.BlockSpec((d, tn), lambda b, i, j: (0, j))
        out_shape = jax.ShapeDtypeStruct((bv, lv, n), BF16)
        out_spec = pl.BlockSpec((1, tm, tn), lambda b, i, j: (b, i, j))
    col = lambda rows: pl.BlockSpec((rows, tn), lambda b, i, j: (0, j))
    return pl.pallas_call(
        functools.partial(_proj_kernel, kind=kind, row_len=row_len, rb=rb, transposed=transposed,
                          single_tile=(n == tn)),
        out_shape=out_shape,
        grid=(bv, lv // tm, n // tn),
        in_specs=[pl.BlockSpec((1, tm, d), lambda b, i, j: (b, i, 0)),
                  pl.BlockSpec((1, 1, d), mod_map), pl.BlockSpec((1, 1, d), mod_map),
                  w_spec, col(3), col(1), col(1)],
        out_specs=out_spec,
        scratch_shapes=[pltpu.VMEM((tm, d), BF16)],
        compiler_params=_cparams(("parallel", "parallel", "arbitrary")),
        name="projection_" + kind + ("_t" if transposed else ""),
    )(x, shift, scale, w, cw, cb, cs)


def _proj_eo_kernel(x_ref, sh_ref, sc_ref, w_ref, cw_ref, cb_ref, oe_ref, oo_ref, xm_ref, tmp_ref, *, row_len, rb):
    nlc = w_ref.shape[1] // LANES

    def body(modulate):
        for r in range(xm_ref.shape[0] // rb):
            rows = slice(r * rb, (r + 1) * rb)
            if modulate:
                _modulate_rows(xm_ref, x_ref, sh_ref, sc_ref, rows)
            acc = jnp.dot(xm_ref[rows, :], w_ref[...], preferred_element_type=F32)
            val = _short_conv(acc, cw_ref[...], cb_ref[...], row_len)
            half = slice(r * rb // 2, (r + 1) * rb // 2)
            for c in range(nlc):
                lanes = slice(c * LANES, (c + 1) * LANES)
                tmp_ref[r * nlc + c] = val[:, lanes]
                oe_ref[0, half, lanes] = tmp_ref[r * nlc + c, pl.ds(0, rb // 2, stride=2), :].astype(oe_ref.dtype)
                oo_ref[0, half, lanes] = tmp_ref[r * nlc + c, pl.ds(1, rb // 2, stride=2), :].astype(oo_ref.dtype)

    first = pl.program_id(2) == 0
    pl.when(first)(lambda: body(True))
    pl.when(jnp.logical_not(first))(lambda: body(False))


def projection_even_odd(x, shift, scale, w, cw, cb, *, tn, tm, row_len):
    bv, lv, d = x.shape
    n = w.shape[1]
    rb = min(tm, ML_CHUNK)
    assert n % tn == 0 and n > tn and lv % tm == 0 and rb % row_len == 0
    bm = shift.shape[0]
    mod_map = (lambda b, i, j: (b, 0, 0)) if bm == bv else (lambda b, i, j: (0, 0, 0))
    col = lambda rows: pl.BlockSpec((rows, tn), lambda b, i, j: (0, j))
    out = jax.ShapeDtypeStruct((bv, lv // 2, n), BF16)
    half = lambda: pl.BlockSpec((1, tm // 2, tn), lambda b, i, j: (b, i, j))
    return pl.pallas_call(
        functools.partial(_proj_eo_kernel, row_len=row_len, rb=rb),
        out_shape=(out, out),
        grid=(bv, lv // tm, n // tn),
        in_specs=[pl.BlockSpec((1, tm, d), lambda b, i, j: (b, i, 0)),
                  pl.BlockSpec((1, 1, d), mod_map), pl.BlockSpec((1, 1, d), mod_map),
                  pl.BlockSpec((d, tn), lambda b, i, j: (0, j)), col(3), col(1)],
        out_specs=(half(), half()),
        scratch_shapes=[pltpu.VMEM((tm, d), BF16), pltpu.VMEM((tm // rb * (tn // LANES), rb, LANES), F32)],
        compiler_params=_cparams(("parallel", "parallel", "arbitrary")),
        name="projection_conv_even_odd",
    )(x, shift, scale, w, cw, cb)


def _log_sigmoid(x):
    return jnp.minimum(x, 0.0) - jnp.log(1.0 + jnp.exp(-jnp.abs(x)))


def _gate_kernel(x_ref, sh_ref, sc_ref, wg_ref, gb_ref, g_ref, xm_ref):
    _modulate_into(xm_ref, x_ref, sh_ref, sc_ref)
    T = ML_CHUNK
    hp = lax.Precision.HIGHEST
    r = lax.broadcasted_iota(jnp.int32, (T, T), 0)
    s = lax.broadcasted_iota(jnp.int32, (T, T), 1)
    tri_f = (s <= r).astype(F32)
    tri_b = (s >= r).astype(F32)
    kind = lax.broadcasted_iota(jnp.int32, (T, LANES), 1) // ML_HEADS
    for c in range(xm_ref.shape[0] // T):
        g = jnp.dot(xm_ref[c * T:(c + 1) * T, :], wg_ref[...], preferred_element_type=F32) + gb_ref[...]
        ls = _log_sigmoid(g)
        cum_f = jnp.dot(tri_f, ls, precision=hp, preferred_element_type=F32)
        cum_b = jnp.dot(tri_b, ls, precision=hp, preferred_element_type=F32)
        g_ref[0, c * T:(c + 1) * T, :] = jnp.where(kind == 1, cum_f, jnp.where(kind == 3, cum_b, g))


def gate_projection(x, shift, scale, wg, gb, *, tm):
    bv, lv, d = x.shape
    assert lv % tm == 0 and tm % ML_CHUNK == 0
    bm = shift.shape[0]
    mod_map = (lambda b, i: (b, 0, 0)) if bm == bv else (lambda b, i: (0, 0, 0))
    full = lambda a: pl.BlockSpec(a.shape, lambda b, i: (0, 0))
    return pl.pallas_call(
        _gate_kernel,
        out_shape=jax.ShapeDtypeStruct((bv, lv, LANES), F32),
        grid=(bv, lv // tm),
        in_specs=[pl.BlockSpec((1, tm, d), lambda b, i: (b, i, 0)),
                  pl.BlockSpec((1, 1, d), mod_map), pl.BlockSpec((1, 1, d), mod_map),
                  full(wg), full(gb)],
        out_specs=pl.BlockSpec((1, tm, LANES), lambda b, i: (b, i, 0)),
        scratch_shapes=[pltpu.VMEM((tm, d), BF16)],
        compiler_params=_cparams(("parallel", "parallel")),
        name="gate_projection",
    )(x, shift, scale, wg, gb)


def _hy_ffn_kernel(feat_ref, w1_ref, b1_ref, w2_ref, b2_ref, w3_ref, fr_ref, dec_ref, o_ref):
    hp = lax.Precision.HIGHEST
    feats = feat_ref[...]
    fr = fr_ref[...]
    h = jnp.sin(fr * (jnp.dot(feats, w1_ref[...], precision=hp, preferred_element_type=F32) + b1_ref[...]))
    h = jnp.sin(fr * (jnp.dot(h, w2_ref[...], precision=hp, preferred_element_type=F32) + b2_ref[...]))
    h = jnp.dot(h, w3_ref[...], precision=hp, preferred_element_type=F32)
    t = feats[:, 0:1]
    o_ref[...] = h * jnp.exp(-t * jnp.abs(dec_ref[...]))


def hyena_filter_taps(feats, w1p, b1, w2, b2, w3, freq, decay_cols):
    L = feats.shape[0]
    n = w3.shape[1]
    tl = min(L, 256)
    full = lambda a: pl.BlockSpec(a.shape, lambda i: (0, 0))
    return pl.pallas_call(
        _hy_ffn_kernel,
        out_shape=jax.ShapeDtypeStruct((L, n), F32),
        grid=(L // tl,),
        in_specs=[pl.BlockSpec((tl, feats.shape[1]), lambda i: (i, 0)),
                  full(w1p), full(b1), full(w2), full(b2), full(w3), full(freq), full(decay_cols)],
        out_specs=pl.BlockSpec((tl, n), lambda i: (i, 0)),
        compiler_params=_cparams(("parallel",)),
        name="hyena_filter_taps",
    )(feats, w1p, b1, w2, b2, w3, freq, decay_cols)


def _row0(shape, i):
    return (lax.broadcasted_iota(jnp.int32, shape, 0) + i * shape[0]) == 0


def _hy_spec_kernel(ce_ref, se_ref, co_ref, so_ref, hfe_ref, hbe_ref, hfo_ref, hbo_ref,
                    kpa_ref, kqa_ref, kpb_ref, kqb_ref, *, L):
    i = pl.program_id(1)
    hf_e, hf_o = hfe_ref[...], hfo_ref[...]
    hb_e, hb_o = hbe_ref[...], hbo_ref[...]
    m = lax.broadcasted_iota(jnp.int32, hb_e.shape, 0)
    hb_e = jnp.where(m == 0, 0.0, hb_e)
    s_e, s_o, d_e, d_o = hf_e + hb_e, hf_o + hb_o, hf_e - hb_e, hf_o - hb_o
    dot = lambda t, x: jnp.dot(t[...], x.astype(BF16), preferred_element_type=F32)
    pe, po, qe, qo = dot(ce_ref, s_e), dot(co_ref, s_o), dot(se_ref, d_e), dot(so_ref, d_o)
    r0 = _row0(pe.shape, i)
    p_half = jnp.sum(s_e * (1 - 2 * (m & 1)).astype(F32), axis=0, keepdims=True)
    scale = 1.0 / L
    kpa_ref[0] = (pe + po) * scale
    kpb_ref[0] = (pe - po) * scale
    kqa_ref[0] = jnp.where(r0, p_half, qe + qo) * scale
    kqb_ref[0] = jnp.where(r0, qo, qo - qe) * scale


def hyena_filter_spectrum(tables, taps, w):
    ce, se, co, so = tables[:4]
    H = ce.shape[0]
    tf = min(H, 512)
    taps_e, taps_o = taps[0::2], taps[1::2]
    out = jax.ShapeDtypeStruct((HY_ORDER, H, w), F32)
    tab = lambda: pl.BlockSpec((tf, H), lambda o, i: (i, 0))
    slot = lambda: pl.BlockSpec((1, tf, w), lambda o, i: (o, i, 0))
    tap = lambda direction: pl.BlockSpec((H, w), lambda o, i: (0, 2 * o + direction))
    return pl.pallas_call(
        functools.partial(_hy_spec_kernel, L=2 * H),
        out_shape=(out,) * 4,
        grid=(HY_ORDER, H // tf),
        in_specs=[tab(), tab(), tab(), tab(), tap(0), tap(1), tap(0), tap(1)],
        out_specs=(slot(),) * 4,
        compiler_params=_cparams(("parallel", "parallel")),
        name="hyena_filter_spectrum",
    )(ce, se, co, so, taps_e, taps_e, taps_o, taps_o)


def _hy_fwd_kernel(ce_ref, se_ref, co_ref, so_ref, ue_ref, uo_ref, kpa_ref, kqa_ref, kpb_ref, kqb_ref,
                   eep_ref, eeq_ref, eop_ref, eoq_ref):
    i = pl.program_id(1)
    ue, uo = ue_ref[0], uo_ref[0]
    dot = lambda t, x: jnp.dot(t[...], x, preferred_element_type=F32)
    pe, qe, po, qo = dot(ce_ref, ue), dot(se_ref, ue), dot(co_ref, uo), dot(so_ref, uo)
    r0 = _row0(pe.shape, i)
    pa, pb = pe + po, pe - po
    qa, qb = jnp.where(r0, qe, qe + qo), jnp.where(r0, qo, qo - qe)
    kpa, kqa, kpb, kqb = kpa_ref[0], kqa_ref[0], kpb_ref[0], kqb_ref[0]
    ypa = jnp.where(r0, 0.5 * pa * kpa, pa * kpa - qa * kqa)
    ypb = jnp.where(r0, 0.5 * pb * kpb, pb * kpb - qb * kqb)
    yqa = jnp.where(r0, qa * kqa - qb * kqb, pa * kqa + qa * kpa)
    yqb = jnp.where(r0, qa * kqb + qb * kqa, pb * kqb + qb * kpb)
    eep_ref[0] = (ypa + ypb).astype(eep_ref.dtype)
    eop_ref[0] = (ypa - ypb).astype(eop_ref.dtype)
    eeq_ref[0] = jnp.where(r0, yqa, yqa - yqb).astype(eeq_ref.dtype)
    eoq_ref[0] = jnp.where(r0, yqb, yqa + yqb).astype(eoq_ref.dtype)


def hyena_forward(tables, u_e, u_o, u_col, ks, order):
    ce, se, co, so = tables[:4]
    bv, H, _ = u_e.shape
    w = ks[0].shape[2]
    tf = min(H, 512)
    out = jax.ShapeDtypeStruct((bv, H, w), BF16)
    tab = lambda: pl.BlockSpec((tf, H), lambda b, i: (i, 0))
    sig = lambda: pl.BlockSpec((1, H, w), lambda b, i: (b, 0, u_col))
    slot = lambda: pl.BlockSpec((1, tf, w), lambda b, i: (order, i, 0))
    res = lambda: pl.BlockSpec((1, tf, w), lambda b, i: (b, i, 0))
    return pl.pallas_call(
        _hy_fwd_kernel,
        out_shape=(out,) * 4,
        grid=(bv, H // tf),
        in_specs=[tab(), tab(), tab(), tab(), sig(), sig(), slot(), slot(), slot(), slot()],
        out_specs=(res(),) * 4,
        compiler_params=_cparams(("parallel", "parallel")),
        name="hyena_forward_dft",
    )(ce, se, co, so, u_e, u_o, *ks)


def _hy_inv_kernel(cet_ref, set_ref, cot_ref, sot_ref, eep_ref, eeq_ref, eop_ref, eoq_ref,
                   ue_ref, uo_ref, ge_ref, go_ref, bias_ref, *o_refs, interleave):
    dot = lambda t, x: jnp.dot(t[...], x[0], preferred_element_type=F32)
    ye = dot(cet_ref, eep_ref) + dot(set_ref, eeq_ref)
    yo = dot(cot_ref, eop_ref) + dot(sot_ref, eoq_ref)
    bias = bias_ref[0]
    oe = ge_ref[0].astype(F32) * (ye + ue_ref[0].astype(F32) * bias)
    oo = go_ref[0].astype(F32) * (yo + uo_ref[0].astype(F32) * bias)
    if interleave:
        o_ref, mix_ref = o_refs
        n = oe.shape[0]
        for c in range(oe.shape[1] // LANES):
            lanes = slice(c * LANES, (c + 1) * LANES)
            mix_ref[c, pl.ds(0, n, stride=2), :] = oe[:, lanes]
            mix_ref[c, pl.ds(1, n, stride=2), :] = oo[:, lanes]
            o_ref[0, :, lanes] = mix_ref[c].astype(o_ref.dtype)
    else:
        oe_ref, oo_ref = o_refs
        oe_ref[0] = oe.astype(oe_ref.dtype)
        oo_ref[0] = oo.astype(oo_ref.dtype)


def hyena_inverse(tables, es, u_e, u_o, u_col, g_e, g_o, g_col, bias, order, interleave):
    cet, set_, cot, sot = tables[0], tables[4], tables[5], tables[6]
    bv, H, w = es[0].shape
    tt = min(H, 512)
    tab = lambda: pl.BlockSpec((tt, H), lambda b, i: (i, 0))
    spec = lambda: pl.BlockSpec((1, H, w), lambda b, i: (b, 0, 0))
    tile = lambda col: pl.BlockSpec((1, tt, w), lambda b, i: (b, i, col))
    if interleave:
        out_shape = jax.ShapeDtypeStruct((bv, 2 * H, w), BF16)
        out_specs = pl.BlockSpec((1, 2 * tt, w), lambda b, i: (b, i, 0))
        scratch = [pltpu.VMEM((w // LANES, 2 * tt, LANES), F32)]
    else:
        out_shape = (jax.ShapeDtypeStruct((bv, H, w), BF16),) * 2
        out_specs = (tile(0), tile(0))
        scratch = []
    return pl.pallas_call(
        functools.partial(_hy_inv_kernel, interleave=interleave),
        out_shape=out_shape,
        grid=(bv, H // tt),
        in_specs=[tab(), tab(), tab(), tab(), spec(), spec(), spec(), spec(),
                  tile(u_col), tile(u_col), tile(g_col), tile(g_col),
                  pl.BlockSpec((1, 1, w), lambda b, i: (order, 0, 0))],
        out_specs=out_specs,
        scratch_shapes=scratch,
        compiler_params=_cparams(("parallel", "parallel")),
        name="hyena_inverse_dft",
    )(cet, set_, cot, sot, *es, u_e, u_o, g_e, g_o, bias)


def dft_tables(L):
    H = L // 2
    r = jnp.arange(H, dtype=jnp.int32)[:, None]
    c = jnp.arange(H, dtype=jnp.int32)[None, :]
    ang = lambda k: (k % (2 * L)).astype(F32) * (math.pi / L)
    alt_r, alt_c = (1 - 2 * (r % 2)).astype(F32), (1 - 2 * (c % 2)).astype(F32)
    a_e, a_o, a_ot = ang(2 * r * c), ang(r * (2 * c + 1)), ang(c * (2 * r + 1))
    bf = lambda a: a.astype(BF16)
    ce = bf(jnp.cos(a_e))
    se = bf(jnp.where(r == 0, alt_c, jnp.sin(a_e)))
    co = bf(jnp.cos(a_o))
    so = bf(jnp.where(r == 0, alt_c, jnp.sin(a_o)))
    set_ = bf(jnp.where(c == 0, alt_r, jnp.sin(a_e)))
    cot = bf(jnp.cos(a_ot))
    sot = bf(jnp.where(c == 0, alt_r, jnp.sin(a_ot)))
    return ce, se, co, so, set_, cot, sot


def hyena_features(L):
    pos = jnp.arange(L, dtype=F32)
    t = pos / (L - 1)
    bands = jnp.linspace(1e-4, HY_BANDS - 1, HY_BANDS, dtype=F32)
    ang = (2.0 * math.pi / L) * pos[:, None] * bands[None, :]
    feats = jnp.concatenate([t[:, None], jnp.cos(ang), jnp.sin(ang)], axis=-1)
    return jnp.pad(feats, ((0, 0), (0, LANES - HY_FEAT)))


def hyena_branch(z_e, z_o, tables, feats, ffn, bias):
    w = bias.shape[-1]
    taps = hyena_filter_taps(feats, *ffn)
    ks = hyena_filter_spectrum(tables, taps, w)
    bias3 = bias.reshape(HY_ORDER, 1, w)
    es = hyena_forward(tables, z_e, z_o, 0, ks, 0)
    y1_e, y1_o = hyena_inverse(tables, es, z_e, z_o, 0, z_e, z_o, 1, bias3, 0, False)
    es = hyena_forward(tables, y1_e, y1_o, 0, ks, 1)
    return hyena_inverse(tables, es, y1_e, y1_o, 0, z_e, z_o, 2, bias3, 1, True)


def _mlstm_chunk(q, k, v1t, li_row, b_row, r_col, c, m, reverse):
    T = q.shape[0]
    s_idx = lax.broadcasted_iota(jnp.int32, (T, T), 0)
    t_idx = lax.broadcasted_iota(jnp.int32, (T, T), 1)
    mask = (s_idx >= t_idx) if reverse else (s_idx <= t_idx)
    nt = (((1,), (1,)), ((), ()))
    kq = lax.dot_general(k, q, nt, preferred_element_type=F32)
    d = jnp.where(mask, r_col + b_row, -1e30)
    m_loc = jnp.max(d, axis=0, keepdims=True)
    st = (kq * jnp.exp(d - m_loc)).astype(BF16)
    intra = jnp.dot(v1t, st, preferred_element_type=F32)
    g = b_row + m
    m_t = jnp.maximum(g, m_loc)
    inter = lax.dot_general(c.astype(BF16), q, nt, preferred_element_type=F32)
    num = jnp.exp(g - m_t) * inter + jnp.exp(m_loc - m_t) * intra
    den = num[HEAD_PAD - 1:HEAD_PAD, :]
    h = num * (1.0 / jnp.maximum(jnp.abs(den), jnp.exp(-m_t)))
    btot = b_row[:, 0:1] if reverse else b_row[:, T - 1:T]
    a = btot - b_row + li_row
    a_max = jnp.max(a, axis=1, keepdims=True)
    wv = (v1t.astype(F32) * jnp.exp(a - a_max)).astype(BF16)
    delta = jnp.dot(wv, k, preferred_element_type=F32)
    m_new = jnp.maximum(btot + m, a_max)
    c_new = jnp.exp(btot + m - m_new) * c + jnp.exp(a_max - m_new) * delta
    return h, c_new, m_new


def _mlstm_kernel(qc_ref, kc_ref, vc_ref, oc_ref, gcc_ref, grc_ref,
                  ql_ref, kl_ref, vl_ref, ol_ref, gcl_ref, grl_ref, ng_ref,
                  outc_ref, outl_ref, hf_ref, hb_ref, *, dh):
    T = ML_CHUNK
    L = ql_ref.shape[1]
    row = lax.broadcasted_iota(jnp.int32, (HEAD_PAD, T), 0)
    ones_row = row == HEAD_PAD - 1
    valid = row < dh
    ng = ng_ref[...]

    def gates(gc, gr, bwd):
        i = 2 if bwd else 0
        return gr[i:i + 1, :], gr[i + 1:i + 2, :], gc[:, i:i + 1] - gc[:, i + 1:i + 2]

    def finish(h, o):
        h = jnp.where(valid, h, 0.0)
        mu = jnp.sum(h, axis=0, keepdims=True) * (1.0 / dh)
        dlt = jnp.where(valid, h - mu, 0.0)
        var = jnp.sum(dlt * dlt, axis=0, keepdims=True) * (1.0 / dh)
        return (dlt * lax.rsqrt(var + LN_EPS) * ng * o.astype(F32)).astype(BF16)

    def aug(vt):
        return jnp.where(ones_row, jnp.ones_like(vt), vt)

    zero_c = jnp.zeros((HEAD_PAD, HEAD_PAD), F32)
    zero_m = jnp.zeros((1, 1), F32)

    q, k, v1t = qc_ref[0], kc_ref[0], aug(vc_ref[0])
    gc, gr = gcc_ref[0, 0], grc_ref[0, 0]
    h_f, c_f, m_f = _mlstm_chunk(q, k, v1t, *gates(gc, gr, False), zero_c, zero_m, False)
    h_b, c_b, m_b = _mlstm_chunk(q, k, v1t, *gates(gc, gr, True), zero_c, zero_m, True)
    outc_ref[0] = finish(h_f + h_b, oc_ref[0])

    nc = L // T
    for ci in range(nc):
        for bwd in (False, True):
            lo = (nc - 1 - ci) * T if bwd else ci * T
            gts = gates(gcl_ref[0, 0, lo:lo + T, :], grl_ref[0, 0, :, lo:lo + T], bwd)
            q, k, v1t = ql_ref[0, lo:lo + T, :], kl_ref[0, lo:lo + T, :], aug(vl_ref[0, :, lo:lo + T])
            if bwd:
                h_b, c_b, m_b = _mlstm_chunk(q, k, v1t, *gts, c_b, m_b, True)
                hb_ref[:, lo:lo + T] = h_b
            else:
                h_f, c_f, m_f = _mlstm_chunk(q, k, v1t, *gts, c_f, m_f, False)
                hf_ref[:, lo:lo + T] = h_f
    for ci in range(nc):
        lo = ci * T
        outl_ref[0, :, lo:lo + T] = finish(hf_ref[:, lo:lo + T] + hb_ref[:, lo:lo + T], ol_ref[0, :, lo:lo + T])


def mlstm_branch(qk_c, vt_c, ot_c, gate_c, qk_l, vt_l, ot_l, gate_l, norm_g_col, dh):
    bv, lc, _ = qk_c.shape
    L = qk_l.shape[1]
    H = ML_HEADS
    assert lc == ML_CHUNK and L % ML_CHUNK == 0

    def per_head(g):
        g = g[:, :, :4 * H].reshape(bv, -1, 4, H)
        return g.transpose(0, 3, 1, 2), g.transpose(0, 3, 2, 1)

    gcc, grc = per_head(gate_c)
    gcl, grl = per_head(gate_l)

    def qk_blk(length, part):
        return pl.BlockSpec((1, length, HEAD_PAD), lambda b, h: (b, 0, part * H + h))

    def t_blk(length):
        return pl.BlockSpec((1, HEAD_PAD, length), lambda b, h: (b, h, 0))

    def gspecs(length):
        return [pl.BlockSpec((1, 1, length, 4), lambda b, h: (b, h, 0, 0)),
                pl.BlockSpec((1, 1, 4, length), lambda b, h: (b, h, 0, 0))]

    def specs(length):
        return [qk_blk(length, 0), qk_blk(length, 1), t_blk(length), t_blk(length)] + gspecs(length)

    return pl.pallas_call(
        functools.partial(_mlstm_kernel, dh=dh),
        out_shape=(jax.ShapeDtypeStruct((bv, H * HEAD_PAD, lc), BF16),
                   jax.ShapeDtypeStruct((bv, H * HEAD_PAD, L), BF16)),
        grid=(bv, H),
        in_specs=specs(lc) + specs(L) + [pl.BlockSpec((HEAD_PAD, 1), lambda b, h: (h, 0))],
        out_specs=(t_blk(lc), t_blk(L)),
        scratch_shapes=[pltpu.VMEM((HEAD_PAD, L), F32), pltpu.VMEM((HEAD_PAD, L), F32)],
        compiler_params=_cparams(("parallel", "parallel")),
        name="mlstm_scan",
    )(qk_c, qk_c, vt_c, ot_c, gcc, grc, qk_l, qk_l, vt_l, ot_l, gcl, grl, norm_g_col)


def _s5_kernel(uc_ref, ul_ref, prow_ref, pcol_ref, bt_ref, cr_ref, cc_ref, y_ref,
               toep_ref, min_ref, mout_ref, kv_ref, s_ref, x_ref, u_ref, yv_ref, *, nb, nctx, nchunk):
    T, CG, P = S5_CHUNK, S5_GROUP, S5_STATE

    ch0 = 0
    for src in (uc_ref, ul_ref):
        xs = pltpu.einshape("bct->cbt", src[...].astype(F32))
        for ch in range(xs.shape[2] // T):
            for ci in range(CG):
                u_ref[(ch0 + ch) * nb:(ch0 + ch + 1) * nb, ci * T:(ci + 1) * T] = (
                    xs[ci, :, ch * T:(ch + 1) * T].astype(BF16))
        ch0 += xs.shape[2] // T

    hp = lax.Precision.HIGHEST
    lane = lax.broadcasted_iota(jnp.int32, (1, 2 * P), 1)
    re_lane = lane < P
    re_row = lax.broadcasted_iota(jnp.int32, (2 * P, 1), 0) < P

    def powers_rows(n, mag1, th1):
        m = jnp.exp(n * mag1)
        return m * jnp.cos(n * th1), m * jnp.sin(n * th1)

    def powers_cols(n, mag1, th1):
        m = jnp.exp(n * mag1)
        return m * jnp.cos(n * th1), m * jnp.sin(n * th1)

    at = []
    kv = jnp.zeros((CG * CG, 2 * T), F32)
    for d in range(2):
        a_re, a_im, ldt = prow_ref[0, d, 0:1, :], prow_ref[0, d, 1:2, :], prow_ref[0, d, 2:3, :]
        dt = jnp.exp(ldt)
        mag1, th1 = dt * a_re, dt * a_im
        ab_re, ab_im = jnp.exp(mag1) * jnp.cos(th1), jnp.exp(mag1) * jnp.sin(th1)
        den = a_re * a_re + a_im * a_im
        co_re = ((ab_re - 1.0) * a_re + ab_im * a_im) / den
        co_im = (ab_im * a_re - (ab_re - 1.0) * a_im) / den
        b_re, b_im = bt_ref[0, d, 0], bt_ref[0, d, 1]
        bb_re = co_re * b_re - co_im * b_im
        bb_im = co_re * b_im + co_im * b_re
        c_re, c_im = cr_ref[0, d, 0], cr_ref[0, d, 1]
        a_re_c, a_im_c = pcol_ref[0, d, :, 0:1], pcol_ref[0, d, :, 1:2]
        dt_c = jnp.exp(pcol_ref[0, d, :, 2:3])
        mag1_c, th1_c = dt_c * a_re_c, dt_c * a_im_c

        rep = lambda a: jnp.broadcast_to(a[:, None, :], (CG, CG, 2 * P)).reshape(CG * CG, 2 * P)
        til = lambda a: jnp.broadcast_to(a[None, :, :], (CG, CG, 2 * P)).reshape(CG * CG, 2 * P)
        w_re = rep(bb_re) * til(c_re) - rep(bb_im) * til(c_im)
        w_im = rep(bb_re) * til(c_im) + rep(bb_im) * til(c_re)
        w = jnp.where(re_lane, w_re, -w_im)
        lagp = lax.broadcasted_iota(jnp.int32, (1, 2 * T), 1)
        n = (T - 1 - lagp) if d else (lagp - (T - 1))
        ok = (n >= 0) & (lagp < 2 * T - 1)
        pc, ps = powers_cols(jnp.maximum(n, 0).astype(F32), mag1_c, th1_c)
        pw = jnp.where(ok, jnp.where(re_row, pc, ps), 0.0)
        kv = kv + jnp.dot(w, pw, precision=hp, preferred_element_type=F32)

        s_col = lax.broadcasted_iota(jnp.int32, (T, 1), 0)
        pc, ps = powers_rows((s_col if d else T - 1 - s_col).astype(F32), mag1, th1)
        for ci in range(CG):
            br, bi = bb_re[ci:ci + 1, :], bb_im[ci:ci + 1, :]
            blk = jnp.where(re_lane, br * pc - bi * ps, br * ps + bi * pc)
            min_ref[d, ci * T:(ci + 1) * T, :] = blk.astype(BF16)

        t_row = lax.broadcasted_iota(jnp.int32, (1, T), 1)
        pc, ps = powers_cols((T - t_row if d else t_row + 1).astype(F32), mag1_c, th1_c)
        for co in range(CG):
            cr, ci_ = cc_ref[0, d, 0, :, co:co + 1], cc_ref[0, d, 1, :, co:co + 1]
            blk = jnp.where(re_row, cr * pc - ci_ * ps, -(cr * ps + ci_ * pc))
            mout_ref[d, :, co * T:(co + 1) * T] = blk.astype(BF16)

        mt = jnp.exp(T * mag1)
        at.append(((mt * jnp.cos(T * th1))[:, :P], (mt * jnp.sin(T * th1))[:, :P]))

    kv_ref[...] = kv

    def build(ci, carry):
        r0 = pl.multiple_of(ci * T, T)
        for co in range(CG):
            vec = kv_ref[pl.ds(ci * CG + co, 1), :]
            rolled = pltpu.roll(jnp.broadcast_to(vec, (T, 2 * T)), T + 1, 1, stride=1, stride_axis=0)
            toep_ref[pl.ds(r0, T), co * T:(co + 1) * T] = rolled[:, :T].astype(BF16)
        return carry

    lax.fori_loop(0, CG, build, 0)

    u = u_ref[...]
    y = jnp.dot(u, toep_ref[...], preferred_element_type=F32)
    for d in range(2):
        s_ref[d] = jnp.dot(u, min_ref[d], preferred_element_type=F32)

    orders = (list(range(nchunk)), list(range(nctx - 1, -1, -1)) + list(range(nchunk - 1, nctx - 1, -1)))
    for d in range(2):
        at_re, at_im = at[d]
        x_re = jnp.zeros((nb, P), F32)
        x_im = jnp.zeros((nb, P), F32)
        for ch in orders[d]:
            x_ref[d, ch * nb:(ch + 1) * nb, :] = jnp.concatenate([x_re, x_im], axis=1)
            s = s_ref[d, ch * nb:(ch + 1) * nb, :]
            x_re, x_im = (at_re * x_re - at_im * x_im + s[:, :P], at_re * x_im + at_im * x_re + s[:, P:])
        y = y + jnp.dot(x_ref[d].astype(BF16), mout_ref[d], preferred_element_type=F32)

    for ch in range(nchunk):
        for co in range(CG):
            yv_ref[co, :, ch * T:(ch + 1) * T] = y[ch * nb:(ch + 1) * nb, co * T:(co + 1) * T]
    y_ref[...] = pltpu.einshape("cbt->bct", yv_ref[...]).astype(y_ref.dtype)


def s5_mix(ut_c, ut_l, a_re, a_im, log_dt, b_re, b_im, c_re, c_im):
    nb, w, lc = ut_c.shape
    L = ut_l.shape[2]
    T, CG, P = S5_CHUNK, S5_GROUP, S5_STATE
    G = w // CG
    nctx, nlat = lc // T, L // T
    nchunk = nctx + nlat

    dup = lambda a: jnp.concatenate([a, a], axis=-1)
    f32 = lambda a: a.astype(F32)
    ldt = jnp.broadcast_to(f32(log_dt)[:, :, None], (2, G, P))
    prow = jnp.stack([dup(f32(a_re)), dup(f32(a_im)), dup(ldt)], axis=2)
    prow = jnp.pad(prow, ((0, 0), (0, 0), (0, 5), (0, 0))).transpose(1, 0, 2, 3)
    pcol = prow.transpose(0, 1, 3, 2)
    bt = jnp.stack([dup(f32(b_re).transpose(0, 1, 3, 2)), dup(f32(b_im).transpose(0, 1, 3, 2))], axis=2)
    bt = bt.transpose(1, 0, 2, 3, 4)
    cr = jnp.stack([dup(f32(c_re)), dup(f32(c_im))], axis=2).transpose(1, 0, 2, 3, 4)
    cc = cr.transpose(0, 1, 2, 4, 3)

    blk = lambda a: pl.BlockSpec((1,) + a.shape[1:], lambda g: (g,) + (0,) * (a.ndim - 1))
    grp = lambda length: pl.BlockSpec((nb, CG, length), lambda g: (0, g, 0))
    r = nchunk * nb
    return pl.pallas_call(
        functools.partial(_s5_kernel, nb=nb, nctx=nctx, nchunk=nchunk),
        out_shape=jax.ShapeDtypeStruct((nb, w, lc + L), BF16),
        grid=(G,),
        in_specs=[grp(lc), grp(L), blk(prow), blk(pcol), blk(bt), blk(cr), blk(cc)],
        out_specs=grp(lc + L),
        scratch_shapes=[pltpu.VMEM((CG * T, CG * T), BF16), pltpu.VMEM((2, CG * T, 2 * P), BF16),
                        pltpu.VMEM((2, 2 * P, CG * T), BF16), pltpu.VMEM((CG * CG, 2 * T), F32),
                        pltpu.VMEM((2, r, 2 * P), F32), pltpu.VMEM((2, r, 2 * P), F32),
                        pltpu.VMEM((r, CG * T), BF16), pltpu.VMEM((CG, nb, lc + L), F32)],
        compiler_params=_cparams(("parallel",)),
        name="s5_mix",
    )(ut_c, ut_l, prow, pcol, bt, cr, cc)


def _gelu_tanh(x):
    return 0.5 * x * (1.0 + jnp.tanh(math.sqrt(2.0 / math.pi) * (x + 0.044715 * (x * x * x))))


def _merge_kernel(yhy_ref, yml_ref, ys_ref, u_ref, ghy_ref, gml_ref, gs5_ref,
                  d_ref, gw_ref, gb_ref, why_ref, wml_ref, ws5_ref, o_ref):
    tn = (((0,), (0,)), ((), ()))
    ys = ys_ref[0].astype(F32) + d_ref[...] * u_ref[0].astype(F32)
    z = _gelu_tanh(ys).astype(BF16)
    glu = jnp.dot(gw_ref[...], z, preferred_element_type=F32) + gb_ref[...]
    y_s5 = (z.astype(F32) * _sigmoid(glu)).astype(BF16)
    acc = ghy_ref[0].astype(F32) * jnp.dot(yhy_ref[0], why_ref[...], preferred_element_type=F32)
    acc = acc + gml_ref[0].astype(F32) * lax.dot_general(yml_ref[0], wml_ref[...], tn, preferred_element_type=F32)
    acc = acc + gs5_ref[0].astype(F32) * lax.dot_general(y_s5, ws5_ref[...], tn, preferred_element_type=F32)
    o_ref[0] = acc.astype(o_ref.dtype)


def merge_branches(y_hy, y_ml, ys_t, u_t, t_off, gate_arr, s5_d, glu_wt, glu_b, w_hy, w_ml, w_s5):
    bv, lv, w = y_hy.shape
    dm = w_hy.shape[1]
    tm = ML_CHUNK
    full = lambda a: pl.BlockSpec(a.shape, lambda b, i: (0,) * a.ndim)
    gate = lambda c: pl.BlockSpec((1, tm, dm), lambda b, i: (b, i, c))
    return pl.pallas_call(
        _merge_kernel,
        out_shape=jax.ShapeDtypeStruct((bv, lv, dm), BF16),
        grid=(bv, lv // tm),
        in_specs=[pl.BlockSpec((1, tm, w), lambda b, i: (b, i, 0)),
                  pl.BlockSpec((1, y_ml.shape[1], tm), lambda b, i: (b, 0, i)),
                  pl.BlockSpec((1, w, tm), lambda b, i: (b, 0, i + t_off)),
                  pl.BlockSpec((1, w, tm), lambda b, i: (b, 0, i)),
                  gate(0), gate(1), gate(2),
                  full(s5_d), full(glu_wt), full(glu_b), full(w_hy), full(w_ml), full(w_s5)],
        out_specs=pl.BlockSpec((1, tm, dm), lambda b, i: (b, i, 0)),
        compiler_params=_cparams(("parallel", "parallel")),
        name="merge_branches",
    )(y_hy, y_ml, ys_t, u_t, gate_arr, gate_arr, gate_arr, s5_d, glu_wt, glu_b, w_hy, w_ml, w_s5)


def _outproj_kernel(m_ref, w_ref, x_ref, gate_ref, g_ref, b_ref, o_ref, *, alpha):
    rb = min(m_ref.shape[1], LANES)
    for s in range(m_ref.shape[1] // rb):
        rows = slice(s * rb, (s + 1) * rb)
        y = jnp.dot(m_ref[0, rows, :], w_ref[...], preferred_element_type=F32)
        r = alpha * x_ref[0, rows, :] + gate_ref[0] * y
        o_ref[0, rows, :] = _layer_norm(r, g_ref[...], b_ref[...])


def out_projection(merged, w_out, x, gate, ln_g, ln_b, alpha, tm):
    bv, lv, d = x.shape
    bm = gate.shape[0]
    mod_map = (lambda b, i: (b, 0, 0)) if bm == bv else (lambda b, i: (0, 0, 0))
    vec = lambda: pl.BlockSpec((1, d), lambda b, i: (0, 0))
    return pl.pallas_call(
        functools.partial(_outproj_kernel, alpha=alpha),
        out_shape=jax.ShapeDtypeStruct((bv, lv, d), F32),
        grid=(bv, lv // tm),
        in_specs=[pl.BlockSpec((1, tm, d), lambda b, i: (b, i, 0)),
                  pl.BlockSpec((d, d), lambda b, i: (0, 0)),
                  pl.BlockSpec((1, tm, d), lambda b, i: (b, i, 0)),
                  pl.BlockSpec((1, 1, d), mod_map), vec(), vec()],
        out_specs=pl.BlockSpec((1, tm, d), lambda b, i: (b, i, 0)),
        compiler_params=_cparams(("parallel", "parallel")),
        name="out_projection_ln",
    )(merged, w_out, x, gate, ln_g, ln_b)


def _mlp_kernel(x_ref, sh_ref, sc_ref, gate_ref, w1_ref, w2_ref, g_ref, b_ref, o_ref, xm_ref, acc_ref, *, alpha):
    k = pl.program_id(2)

    @pl.when(k == 0)
    def _():
        xm_ref[...] = (x_ref[0] * (1.0 + sc_ref[0]) + sh_ref[0]).astype(BF16)
        acc_ref[...] = jnp.zeros_like(acc_ref)

    h = jnp.maximum(jnp.dot(xm_ref[...], w1_ref[...], preferred_element_type=F32), 0.0)
    acc_ref[...] += jnp.dot((h * h).astype(BF16), w2_ref[...], preferred_element_type=F32)

    @pl.when(k == pl.num_programs(2) - 1)
    def _():
        r = alpha * x_ref[0] + gate_ref[0] * acc_ref[...]
        o_ref[0] = _layer_norm(r, g_ref[...], b_ref[...])


def mlp_block(x, shift, scale, gate, w1, w2, ln_g, ln_b, alpha, tm, kf):
    bv, lv, d = x.shape
    dff = w1.shape[1]
    bm = gate.shape[0]
    mod_map = (lambda b, i, k: (b, 0, 0)) if bm == bv else (lambda b, i, k: (0, 0, 0))
    vec = lambda: pl.BlockSpec((1, d), lambda b, i, k: (0, 0))
    return pl.pallas_call(
        functools.partial(_mlp_kernel, alpha=alpha),
        out_shape=jax.ShapeDtypeStruct((bv, lv, d), F32),
        grid=(bv, lv // tm, dff // kf),
        in_specs=[pl.BlockSpec((1, tm, d), lambda b, i, k: (b, i, 0)),
                  pl.BlockSpec((1, 1, d), mod_map), pl.BlockSpec((1, 1, d), mod_map),
                  pl.BlockSpec((1, 1, d), mod_map),
                  pl.BlockSpec((d, kf), lambda b, i, k: (0, k)),
                  pl.BlockSpec((kf, d), lambda b, i, k: (k, 0)),
                  vec(), vec()],
        out_specs=pl.BlockSpec((1, tm, d), lambda b, i, k: (b, i, 0)),
        scratch_shapes=[pltpu.VMEM((tm, d), BF16), pltpu.VMEM((tm, d), F32)],
        compiler_params=_cparams(("parallel", "parallel", "arbitrary")),
        name="mlp_ln",
    )(x, shift, scale, gate, w1, w2, ln_g, ln_b)


def _pad_heads(a, axis, dh):
    shp = a.shape
    a = a.reshape(shp[:axis] + (ML_HEADS, dh) + shp[axis + 1:])
    pad = [(0, 0)] * a.ndim
    pad[axis + 1] = (0, HEAD_PAD - dh)
    a = jnp.pad(a, pad)
    return a.reshape(shp[:axis] + (ML_HEADS * HEAD_PAD,) + shp[axis + 1:])


def _layer_params(l, p, w_hy, w_ml, w_s5, d_model):
    dh = w_ml // ML_HEADS
    sizes = ((HY_ORDER + 1) * w_hy, 2 * w_ml, w_ml, w_ml, 4 * ML_HEADS, w_s5, N_BRANCH * d_model)
    pts = [0]
    for s in sizes:
        pts.append(pts[-1] + s)
    w_in = p["w_in"][l]
    hy, qk, v, o, gt, u, mg = (w_in[:, pts[i]:pts[i + 1]] for i in range(7))
    ph = lambda a: _pad_heads(a, a.ndim - 1, dh)
    none = lambda n: (jnp.zeros((3, n), F32), jnp.zeros((1, n), F32), jnp.ones((1, n), F32))
    half = ML_HEADS * HEAD_PAD
    cw, cb = p["ml_conv_w"][l], p["ml_conv_b"][l]
    out = {}
    out["mg"] = (mg.astype(BF16),) + none(mg.shape[1])
    out["hy"] = (hy.astype(BF16), p["hy_conv_w"][l], p["hy_conv_b"][l].reshape(1, -1), jnp.ones((1, hy.shape[1]), F32))
    out["qk"] = (jnp.concatenate([ph(qk[:, :w_ml]), ph(qk[:, w_ml:])], axis=1).astype(BF16),
                 jnp.concatenate([ph(cw[:, :w_ml]), ph(cw[:, w_ml:])], axis=1),
                 jnp.concatenate([ph(cb[:w_ml]), ph(cb[w_ml:])]).reshape(1, -1),
                 jnp.concatenate([jnp.ones((half,), F32), jnp.full((half,), dh ** -0.5, F32)]).reshape(1, -1))
    out["vt"] = (ph(v).T.astype(BF16),) + none(half)
    out["ot"] = (ph(o).T.astype(BF16),) + none(half)
    ngt = gt.shape[1]
    out["ut"] = (u.T.astype(BF16),) + none(u.shape[1])
    out["gt"] = (jnp.pad(gt, ((0, 0), (0, LANES - ngt))).astype(BF16),
                 jnp.pad(p["ml_gate_b"][l].reshape(-1), ((0, LANES - ngt),)).reshape(1, -1))
    out["norm_g"] = ph(p["ml_norm_g"][l]).reshape(-1, 1)
    out["w_ml_out"] = _pad_heads(p["w_ml_out"][l], 0, dh).astype(BF16)
    w3 = p["hy_ffn_w3"][l]
    dec = p["hy_decay"][l]
    dec_cols = jnp.broadcast_to(dec[:, None, :], (HY_ORDER, 2, w_hy)).reshape(1, -1)
    hpad = LANES - HY_HIDDEN
    row = lambda a: jnp.pad(a.reshape(1, -1), ((0, 0), (0, hpad)))
    out["hy_ffn"] = (jnp.pad(p["hy_ffn_w1"][l], ((0, LANES - HY_FEAT), (0, hpad))), row(p["hy_ffn_b1"][l]),
                     jnp.pad(p["hy_ffn_w2"][l], ((0, hpad), (0, hpad))), row(p["hy_ffn_b2"][l]),
                     jnp.pad(w3, ((0, hpad), (0, 0))), row(p["hy_sin_freq"][l]), dec_cols)
    return out


def kernel(x, c, ctx, c_ctx, w_mod, b_mod, w_in, hy_conv_w, hy_conv_b, hy_ffn_w1, hy_ffn_b1, hy_ffn_w2,
           hy_ffn_b2, hy_ffn_w3, hy_sin_freq, hy_decay, hy_bias, ml_conv_w, ml_conv_b, ml_gate_b, ml_norm_g,
           s5_a_re, s5_a_im, s5_log_dt, s5_b_re, s5_b_im, s5_c_re, s5_c_im, s5_d, s5_glu_w, s5_glu_b,
           w_hy_out, w_ml_out, w_s5_out, w_out, ln1_g, ln1_b, ln2_g, ln2_b, w_ff1, w_ff2):
    p = dict(w_in=w_in, hy_conv_w=hy_conv_w, hy_conv_b=hy_conv_b, hy_ffn_w1=hy_ffn_w1, hy_ffn_b1=hy_ffn_b1,
             hy_ffn_w2=hy_ffn_w2, hy_ffn_b2=hy_ffn_b2, hy_ffn_w3=hy_ffn_w3, hy_sin_freq=hy_sin_freq,
             hy_decay=hy_decay, ml_conv_w=ml_conv_w, ml_conv_b=ml_conv_b, ml_gate_b=ml_gate_b,
             ml_norm_g=ml_norm_g, w_ml_out=w_ml_out)
    B, L, D = x.shape
    LC = ctx.shape[1]
    depth = w_in.shape[0]
    w_hy = hy_bias.shape[-1]
    w_ml = ml_norm_g.shape[-1]
    w_s5 = s5_d.shape[-1]
    dh = w_ml // ML_HEADS
    alpha = (2 * depth) ** 0.25
    half = ML_HEADS * HEAD_PAD
    tm_l, tm_c = 1024, LC

    tab_l, tab_c = dft_tables(L), dft_tables(LC)
    feats_l, feats_c = hyena_features(L), hyena_features(LC)

    mrows = 8 * ((B + 1 + 7) // 8)
    c_rows = jnp.zeros((mrows, D), F32).at[:B].set(c).at[B].set(c_ctx)

    for l in range(depth):
        need_ctx = l < depth - 1
        lp = _layer_params(l, p, w_hy, w_ml, w_s5, D)
        mod = mod_vectors(c_rows, w_mod, b_mod.reshape(depth, 1, -1), l)
        mod_l = [mod[:B, k * D:(k + 1) * D].reshape(B, 1, D) for k in range(6)]
        mod_c = [mod[B:B + 1, k * D:(k + 1) * D].reshape(1, 1, D) for k in range(6)]

        def mixer_inputs(xv, md, row_len, tm, full):
            proj = functools.partial(projection, xv, md[0], md[1], tm=tm, row_len=row_len)
            r = {}
            r["qk"] = proj(*lp["qk"], kind="convsilu", tn=half)
            r["vt"] = proj(*lp["vt"], kind="plain", tn=half, transposed=True)
            r["ot"] = proj(*lp["ot"], kind="sigmoid", tn=half, transposed=True)
            r["ut"] = proj(*lp["ut"], kind="plain", tn=w_s5, transposed=True)
            r["g"] = gate_projection(xv, md[0], md[1], *lp["gt"], tm=tm)
            if full:
                r["mg"] = proj(*lp["mg"], kind="sigmoid", tn=2 * w_hy)
                r["z"] = projection_even_odd(xv, md[0], md[1], *lp["hy"][:3], tn=w_hy, tm=tm, row_len=row_len)
            return r

        r_l = mixer_inputs(x, mod_l, GRID_W, tm_l, True)
        r_c = mixer_inputs(ctx, mod_c, LC, tm_c, need_ctx)

        hm_c, hm_l = mlstm_branch(r_c["qk"], r_c["vt"], r_c["ot"], r_c["g"],
                                  r_l["qk"], r_l["vt"], r_l["ot"], r_l["g"], lp["norm_g"], dh)

        ys_t = s5_mix(r_c["ut"], r_l["ut"], s5_a_re[l], s5_a_im[l], s5_log_dt[l],
                      s5_b_re[l], s5_b_im[l], s5_c_re[l], s5_c_im[l])

        s5_dv = s5_d[l].reshape(-1, 1)
        glu_wt = s5_glu_w[l].T.astype(BF16)
        glu_b = s5_glu_b[l].reshape(-1, 1)
        w_hy_o = w_hy_out[l].astype(BF16)
        w_s5_o = w_s5_out[l].astype(BF16)
        w_o = w_out[l].astype(BF16)
        w1 = w_ff1[l].astype(BF16)
        w2 = w_ff2[l].astype(BF16)
        g1, b1 = ln1_g[l].reshape(1, -1), ln1_b[l].reshape(1, -1)
        g2, b2 = ln2_g[l].reshape(1, -1), ln2_b[l].reshape(1, -1)

        def finish(xv, r, hm, t_off, tables, feats, md, tm):
            y_hy = hyena_branch(*r["z"], tables, feats, lp["hy_ffn"], hy_bias[l])
            merged = merge_branches(y_hy, hm, ys_t, r["ut"], t_off, r["mg"], s5_dv, glu_wt, glu_b,
                                    w_hy_o, lp["w_ml_out"], w_s5_o)
            if md[2].shape[0] == 1:
                merged, xv = merged.reshape(1, -1, D), xv.reshape(1, -1, D)
            x1 = out_projection(merged, w_o, xv, md[2], g1, b1, alpha, tm)
            return mlp_block(x1, md[3], md[4], md[5], w1, w2, g2, b2, alpha, tm, 1024)

        x = finish(x, r_l, hm_l, LC // ML_CHUNK, tab_l, feats_l, mod_l, 512)
        if need_ctx:
            ctx = finish(ctx, r_c, hm_c, 0, tab_c, feats_c, mod_c, 512).reshape(B, LC, D)
    return x
```

```python
import functools
import math

import jax
import jax.numpy as jnp
from jax import lax
from jax.experimental import pallas as pl
from jax.experimental.pallas import tpu as pltpu

F32 = jnp.float32
BF16 = jnp.bfloat16

GRID_W = 64
HY_ORDER = 2
HY_BANDS = 16
HY_FEAT = 1 + 2 * HY_BANDS
HY_HIDDEN = 64
ML_HEADS = 4
ML_CHUNK = 256
S5_GROUP = 16
S5_STATE = 64
N_BRANCH = 3
LN_EPS = 1e-5

LANES = 128
MXU_DIM = 256
VMEM_LIMIT = 56 * 1024 * 1024

HEAD_PAD = MXU_DIM
S5_CHUNK = LANES


def _cparams(sem):
    return pltpu.CompilerParams(dimension_semantics=sem, vmem_limit_bytes=VMEM_LIMIT)


def _sigmoid(x):
    return 1.0 / (1.0 + jnp.exp(-x))


def _layer_norm(r, g, b):
    mu = jnp.mean(r, axis=-1, keepdims=True)
    d = r - mu
    var = jnp.mean(d * d, axis=-1, keepdims=True)
    return d * lax.rsqrt(var + LN_EPS) * g + b


def _mod_kernel(c_ref, w_ref, b_ref, o_ref):
    c = c_ref[...]
    a = (c * _sigmoid(c)).astype(BF16)
    o_ref[...] = jnp.dot(a, w_ref[0].astype(BF16), preferred_element_type=F32) + b_ref[0]


def mod_vectors(c_rows, w, b, layer):
    m, d = c_rows.shape
    n = w.shape[2]
    tn = 1024
    return pl.pallas_call(
        _mod_kernel,
        out_shape=jax.ShapeDtypeStruct((m, n), F32),
        grid=(n // tn,),
        in_specs=[pl.BlockSpec((m, d), lambda j: (0, 0)),
                  pl.BlockSpec((1, d, tn), lambda j: (layer, 0, j)),
                  pl.BlockSpec((1, 1, tn), lambda j: (layer, 0, j))],
        out_specs=pl.BlockSpec((m, tn), lambda j: (0, j)),
        compiler_params=_cparams(("parallel",)),
        name="mod_vectors",
    )(c_rows, w, b)


def _short_conv(acc, w, b, row_len):
    rows = acc.shape[0]
    t = lax.broadcasted_iota(jnp.int32, acc.shape, 0) & (row_len - 1)
    prev = jnp.where(t == 0, 0.0, pltpu.roll(acc, 1, 0))
    nxt = jnp.where(t == row_len - 1, 0.0, pltpu.roll(acc, rows - 1, 0))
    return prev * w[0:1] + acc * w[1:2] + nxt * w[2:3] + b


def _modulate_rows(xm_ref, x_ref, sh_ref, sc_ref, rows):
    xm_ref[rows, :] = (x_ref[0, rows, :] * (1.0 + sc_ref[0]) + sh_ref[0]).astype(BF16)


def _modulate_into(xm_ref, x_ref, sh_ref, sc_ref):
    _modulate_rows(xm_ref, x_ref, sh_ref, sc_ref, slice(None))


def _proj_kernel(x_ref, sh_ref, sc_ref, w_ref, cw_ref, cb_ref, cs_ref, o_ref, xm_ref, *, kind, row_len, rb,
                 transposed, single_tile):
    def epilogue(acc):
        if kind == "conv":
            return _short_conv(acc, cw_ref[...], cb_ref[...], row_len)
        if kind == "convsilu":
            y = _short_conv(acc, cw_ref[...], cb_ref[...], row_len)
            return y * _sigmoid(y) * cs_ref[...]
        if kind == "sigmoid":
            return _sigmoid(acc)
        return acc

    def body(modulate):
        for r in range(xm_ref.shape[0] // rb):
            rows = slice(r * rb, (r + 1) * rb)
            if modulate:
                _modulate_rows(xm_ref, x_ref, sh_ref, sc_ref, rows)
            xs = xm_ref[rows, :]
            if transposed:
                acc = lax.dot_general(w_ref[...], xs, (((1,), (1,)), ((), ())), preferred_element_type=F32)
                o_ref[0, :, rows] = epilogue(acc).astype(o_ref.dtype)
            else:
                acc = jnp.dot(xs, w_ref[...], preferred_element_type=F32)
                o_ref[0, rows, :] = epilogue(acc).astype(o_ref.dtype)

    if single_tile:
        body(True)
    else:
        first = pl.program_id(2) == 0
        pl.when(first)(lambda: body(True))
        pl.when(jnp.logical_not(first))(lambda: body(False))


def projection(x, shift, scale, w, cw, cb, cs, *, kind, tn, tm, row_len, transposed=False):
    bv, lv, d = x.shape
    n = w.shape[0] if transposed else w.shape[1]
    rb = min(tm, ML_CHUNK)
    assert n % tn == 0 and lv % tm == 0 and rb % row_len == 0
    bm = shift.shape[0]
    mod_map = (lambda b, i, j: (b, 0, 0)) if bm == bv else (lambda b, i, j: (0, 0, 0))
    if transposed:
        assert kind in ("plain", "sigmoid")
        w_spec = pl.BlockSpec((tn, d), lambda b, i, j: (j, 0))
        out_shape = jax.ShapeDtypeStruct((bv, n, lv), BF16)
        out_spec = pl.BlockSpec((1, tn, tm), lambda b, i, j: (b, j, i))
    else:
        w_spec = pl.BlockSpec((d, tn), lambda b, i, j: (0, j))
        out_shape = jax.ShapeDtypeStruct((bv, lv, n), BF16)
        out_spec = pl.BlockSpec((1, tm, tn), lambda b, i, j: (b, i, j))
    col = lambda rows: pl.BlockSpec((rows, tn), lambda b, i, j: (0, j))
    return pl.pallas_call(
        functools.partial(_proj_kernel, kind=kind, row_len=row_len, rb=rb, transposed=transposed,
                          single_tile=(n == tn)),
        out_shape=out_shape,
        grid=(bv, lv // tm, n // tn),
        in_specs=[pl.BlockSpec((1, tm, d), lambda b, i, j: (b, i, 0)),
                  pl.BlockSpec((1, 1, d), mod_map), pl.BlockSpec((1, 1, d), mod_map),
                  w_spec, col(3), col(1), col(1)],
        out_specs=out_spec,
        scratch_shapes=[pltpu.VMEM((tm, d), BF16)],
        compiler_params=_cparams(("parallel", "parallel", "arbitrary")),
        name="projection_" + kind + ("_t" if transposed else ""),
    )(x, shift, scale, w, cw, cb, cs)


def _proj_eo_kernel(x_ref, sh_ref, sc_ref, w_ref, cw_ref, cb_ref, oe_ref, oo_ref, xm_ref, tmp_ref, *, row_len, rb):
    nlc = w_ref.shape[1] // LANES

    def body(modulate):
        for r in range(xm_ref.shape[0] // rb):
            rows = slice(r * rb, (r + 1) * rb)
            if modulate:
                _modulate_rows(xm_ref, x_ref, sh_ref, sc_ref, rows)
            acc = jnp.dot(xm_ref[rows, :], w_ref[...], preferred_element_type=F32)
            val = _short_conv(acc, cw_ref[...], cb_ref[...], row_len)
            half = slice(r * rb // 2, (r + 1) * rb // 2)
            for c in range(nlc):
                lanes = slice(c * LANES, (c + 1) * LANES)
                tmp_ref[r * nlc + c] = val[:, lanes]
                oe_ref[0, half, lanes] = tmp_ref[r * nlc + c, pl.ds(0, rb // 2, stride=2), :].astype(oe_ref.dtype)
                oo_ref[0, half, lanes] = tmp_ref[r * nlc + c, pl.ds(1, rb // 2, stride=2), :].astype(oo_ref.dtype)

    first = pl.program_id(2) == 0
    pl.when(first)(lambda: body(True))
    pl.when(jnp.logical_not(first))(lambda: body(False))


def projection_even_odd(x, shift, scale, w, cw, cb, *, tn, tm, row_len):
    bv, lv, d = x.shape
    n = w.shape[1]
    rb = min(tm, ML_CHUNK)
    assert n % tn == 0 and n > tn and lv % tm == 0 and rb % row_len == 0
    bm = shift.shape[0]
    mod_map = (lambda b, i, j: (b, 0, 0)) if bm == bv else (lambda b, i, j: (0, 0, 0))
    col = lambda rows: pl.BlockSpec((rows, tn), lambda b, i, j: (0, j))
    out = jax.ShapeDtypeStruct((bv, lv // 2, n), BF16)
    half = lambda: pl.BlockSpec((1, tm // 2, tn), lambda b, i, j: (b, i, j))
    return pl.pallas_call(
        functools.partial(_proj_eo_kernel, row_len=row_len, rb=rb),
        out_shape=(out, out),
        grid=(bv, lv // tm, n // tn),
        in_specs=[pl.BlockSpec((1, tm, d), lambda b, i, j: (b, i, 0)),
                  pl.BlockSpec((1, 1, d), mod_map), pl.BlockSpec((1, 1, d), mod_map),
                  pl.BlockSpec((d, tn), lambda b, i, j: (0, j)), col(3), col(1)],
        out_specs=(half(), half()),
        scratch_shapes=[pltpu.VMEM((tm, d), BF16), pltpu.VMEM((tm // rb * (tn // LANES), rb, LANES), F32)],
        compiler_params=_cparams(("parallel", "parallel", "arbitrary")),
        name="projection_conv_even_odd",
    )(x, shift, scale, w, cw, cb)


def _log_sigmoid(x):
    return jnp.minimum(x, 0.0) - jnp.log(1.0 + jnp.exp(-jnp.abs(x)))


def _gate_kernel(x_ref, sh_ref, sc_ref, wg_ref, gb_ref, g_ref, xm_ref):
    _modulate_into(xm_ref, x_ref, sh_ref, sc_ref)
    T = ML_CHUNK
    hp = lax.Precision.HIGHEST
    r = lax.broadcasted_iota(jnp.int32, (T, T), 0)
    s = lax.broadcasted_iota(jnp.int32, (T, T), 1)
    tri_f = (s <= r).astype(F32)
    tri_b = (s >= r).astype(F32)
    kind = lax.broadcasted_iota(jnp.int32, (T, LANES), 1) // ML_HEADS
    for c in range(xm_ref.shape[0] // T):
        g = jnp.dot(xm_ref[c * T:(c + 1) * T, :], wg_ref[...], preferred_element_type=F32) + gb_ref[...]
        ls = _log_sigmoid(g)
        cum_f = jnp.dot(tri_f, ls, precision=hp, preferred_element_type=F32)
        cum_b = jnp.dot(tri_b, ls, precision=hp, preferred_element_type=F32)
        g_ref[0, c * T:(c + 1) * T, :] = jnp.where(kind == 1, cum_f, jnp.where(kind == 3, cum_b, g))


def gate_projection(x, shift, scale, wg, gb, *, tm):
    bv, lv, d = x.shape
    assert lv % tm == 0 and tm % ML_CHUNK == 0
    bm = shift.shape[0]
    mod_map = (lambda b, i: (b, 0, 0)) if bm == bv else (lambda b, i: (0, 0, 0))
    full = lambda a: pl.BlockSpec(a.shape, lambda b, i: (0, 0))
    return pl.pallas_call(
        _gate_kernel,
        out_shape=jax.ShapeDtypeStruct((bv, lv, LANES), F32),
        grid=(bv, lv // tm),
        in_specs=[pl.BlockSpec((1, tm, d), lambda b, i: (b, i, 0)),
                  pl.BlockSpec((1, 1, d), mod_map), pl.BlockSpec((1, 1, d), mod_map),
                  full(wg), full(gb)],
        out_specs=pl.BlockSpec((1, tm, LANES), lambda b, i: (b, i, 0)),
        scratch_shapes=[pltpu.VMEM((tm, d), BF16)],
        compiler_params=_cparams(("parallel", "parallel")),
        name="gate_projection",
    )(x, shift, scale, wg, gb)


def _hy_ffn_kernel(feat_ref, w1_ref, b1_ref, w2_ref, b2_ref, w3_ref, fr_ref, dec_ref, o_ref):
    hp = lax.Precision.HIGHEST
    feats = feat_ref[...]
    fr = fr_ref[...]
    h = jnp.sin(fr * (jnp.dot(feats, w1_ref[...], precision=hp, preferred_element_type=F32) + b1_ref[...]))
    h = jnp.sin(fr * (jnp.dot(h, w2_ref[...], precision=hp, preferred_element_type=F32) + b2_ref[...]))
    h = jnp.dot(h, w3_ref[...], precision=hp, preferred_element_type=F32)
    t = feats[:, 0:1]
    o_ref[...] = h * jnp.exp(-t * jnp.abs(dec_ref[...]))


def hyena_filter_taps(feats, w1p, b1, w2, b2, w3, freq, decay_cols):
    L = feats.shape[0]
    n = w3.shape[1]
    tl = min(L, 256)
    full = lambda a: pl.BlockSpec(a.shape, lambda i: (0, 0))
    return pl.pallas_call(
        _hy_ffn_kernel,
        out_shape=jax.ShapeDtypeStruct((L, n), F32),
        grid=(L // tl,),
        in_specs=[pl.BlockSpec((tl, feats.shape[1]), lambda i: (i, 0)),
                  full(w1p), full(b1), full(w2), full(b2), full(w3), full(freq), full(decay_cols)],
        out_specs=pl.BlockSpec((tl, n), lambda i: (i, 0)),
        compiler_params=_cparams(("parallel",)),
        name="hyena_filter_taps",
    )(feats, w1p, b1, w2, b2, w3, freq, decay_cols)


def _row0(shape, i):
    return (lax.broadcasted_iota(jnp.int32, shape, 0) + i * shape[0]) == 0


def _hy_spec_kernel(ce_ref, se_ref, co_ref, so_ref, hfe_ref, hbe_ref, hfo_ref, hbo_ref,
                    kpa_ref, kqa_ref, kpb_ref, kqb_ref, *, L):
    i = pl.program_id(1)
    hf_e, hf_o = hfe_ref[...], hfo_ref[...]
    hb_e, hb_o = hbe_ref[...], hbo_ref[...]
    m = lax.broadcasted_iota(jnp.int32, hb_e.shape, 0)
    hb_e = jnp.where(m == 0, 0.0, hb_e)
    s_e, s_o, d_e, d_o = hf_e + hb_e, hf_o + hb_o, hf_e - hb_e, hf_o - hb_o
    dot = lambda t, x: jnp.dot(t[...], x.astype(BF16), preferred_element_type=F32)
    pe, po, qe, qo = dot(ce_ref, s_e), dot(co_ref, s_o), dot(se_ref, d_e), dot(so_ref, d_o)
    r0 = _row0(pe.shape, i)
    p_half = jnp.sum(s_e * (1 - 2 * (m & 1)).astype(F32), axis=0, keepdims=True)
    scale = 1.0 / L
    kpa_ref[0] = (pe + po) * scale
    kpb_ref[0] = (pe - po) * scale
    kqa_ref[0] = jnp.where(r0, p_half, qe + qo) * scale
    kqb_ref[0] = jnp.where(r0, qo, qo - qe) * scale


def hyena_filter_spectrum(tables, taps, w):
    ce, se, co, so = tables[:4]
    H = ce.shape[0]
    tf = min(H, 512)
    out = jax.ShapeDtypeStruct((HY_ORDER, H, w), F32)
    tab = lambda: pl.BlockSpec((tf, H), lambda o, i: (i, 0))
    slot = lambda: pl.BlockSpec((1, tf, w), lambda o, i: (o, i, 0))
    tap = lambda parity, direction: pl.BlockSpec((H, w), lambda o, i: (parity, 2 * o + direction))
    return pl.pallas_call(
        functools.partial(_hy_spec_kernel, L=2 * H),
        out_shape=(out,) * 4,
        grid=(HY_ORDER, H // tf),
        in_specs=[tab(), tab(), tab(), tab(), tap(0, 0), tap(0, 1), tap(1, 0), tap(1, 1)],
        out_specs=(slot(),) * 4,
        compiler_params=_cparams(("parallel", "parallel")),
        name="hyena_filter_spectrum",
    )(ce, se, co, so, taps, taps, taps, taps)


def _hy_fwd_kernel(ce_ref, se_ref, co_ref, so_ref, ue_ref, uo_ref, kpa_ref, kqa_ref, kpb_ref, kqb_ref,
                   eep_ref, eeq_ref, eop_ref, eoq_ref):
    i = pl.program_id(1)
    ue, uo = ue_ref[0], uo_ref[0]
    dot = lambda t, x: jnp.dot(t[...], x, preferred_element_type=F32)
    pe, qe, po, qo = dot(ce_ref, ue), dot(se_ref, ue), dot(co_ref, uo), dot(so_ref, uo)
    r0 = _row0(pe.shape, i)
    pa, pb = pe + po, pe - po
    qa, qb = jnp.where(r0, qe, qe + qo), jnp.where(r0, qo, qo - qe)
    kpa, kqa, kpb, kqb = kpa_ref[0], kqa_ref[0], kpb_ref[0], kqb_ref[0]
    ypa = jnp.where(r0, 0.5 * pa * kpa, pa * kpa - qa * kqa)
    ypb = jnp.where(r0, 0.5 * pb * kpb, pb * kpb - qb * kqb)
    yqa = jnp.where(r0, qa * kqa - qb * kqb, pa * kqa + qa * kpa)
    yqb = jnp.where(r0, qa * kqb + qb * kqa, pb * kqb + qb * kpb)
    eep_ref[0] = (ypa + ypb).astype(eep_ref.dtype)
    eop_ref[0] = (ypa - ypb).astype(eop_ref.dtype)
    eeq_ref[0] = jnp.where(r0, yqa, yqa - yqb).astype(eeq_ref.dtype)
    eoq_ref[0] = jnp.where(r0, yqb, yqa + yqb).astype(eoq_ref.dtype)


def hyena_forward(tables, u_e, u_o, u_col, ks, order):
    ce, se, co, so = tables[:4]
    bv, H, _ = u_e.shape
    w = ks[0].shape[2]
    tf = min(H, 512)
    out = jax.ShapeDtypeStruct((bv, H, w), BF16)
    tab = lambda: pl.BlockSpec((tf, H), lambda b, i: (i, 0))
    sig = lambda: pl.BlockSpec((1, H, w), lambda b, i: (b, 0, u_col))
    slot = lambda: pl.BlockSpec((1, tf, w), lambda b, i: (order, i, 0))
    res = lambda: pl.BlockSpec((1, tf, w), lambda b, i: (b, i, 0))
    return pl.pallas_call(
        _hy_fwd_kernel,
        out_shape=(out,) * 4,
        grid=(bv, H // tf),
        in_specs=[tab(), tab(), tab(), tab(), sig(), sig(), slot(), slot(), slot(), slot()],
        out_specs=(res(),) * 4,
        compiler_params=_cparams(("parallel", "parallel")),
        name="hyena_forward_dft",
    )(ce, se, co, so, u_e, u_o, *ks)


def _hy_inv_kernel(cet_ref, set_ref, cot_ref, sot_ref, eep_ref, eeq_ref, eop_ref, eoq_ref,
                   ue_ref, uo_ref, ge_ref, go_ref, bias_ref, *o_refs, interleave):
    dot = lambda t, x: jnp.dot(t[...], x[0], preferred_element_type=F32)
    ye = dot(cet_ref, eep_ref) + dot(set_ref, eeq_ref)
    yo = dot(cot_ref, eop_ref) + dot(sot_ref, eoq_ref)
    bias = bias_ref[0]
    oe = ge_ref[0].astype(F32) * (ye + ue_ref[0].astype(F32) * bias)
    oo = go_ref[0].astype(F32) * (yo + uo_ref[0].astype(F32) * bias)
    if interleave:
        o_ref, mix_ref = o_refs
        n = oe.shape[0]
        for c in range(oe.shape[1] // LANES):
            lanes = slice(c * LANES, (c + 1) * LANES)
            mix_ref[c, pl.ds(0, n, stride=2), :] = oe[:, lanes]
            mix_ref[c, pl.ds(1, n, stride=2), :] = oo[:, lanes]
            o_ref[0, :, lanes] = mix_ref[c].astype(o_ref.dtype)
    else:
        oe_ref, oo_ref = o_refs
        oe_ref[0] = oe.astype(oe_ref.dtype)
        oo_ref[0] = oo.astype(oo_ref.dtype)


def hyena_inverse(tables, es, u_e, u_o, u_col, g_e, g_o, g_col, bias, order, interleave):
    cet, set_, cot, sot = tables[0], tables[4], tables[5], tables[6]
    bv, H, w = es[0].shape
    tt = min(H, 512)
    tab = lambda: pl.BlockSpec((tt, H), lambda b, i: (i, 0))
    spec = lambda: pl.BlockSpec((1, H, w), lambda b, i: (b, 0, 0))
    tile = lambda col: pl.BlockSpec((1, tt, w), lambda b, i: (b, i, col))
    if interleave:
        out_shape = jax.ShapeDtypeStruct((bv, 2 * H, w), BF16)
        out_specs = pl.BlockSpec((1, 2 * tt, w), lambda b, i: (b, i, 0))
        scratch = [pltpu.VMEM((w // LANES, 2 * tt, LANES), F32)]
    else:
        out_shape = (jax.ShapeDtypeStruct((bv, H, w), BF16),) * 2
        out_specs = (tile(0), tile(0))
        scratch = []
    return pl.pallas_call(
        functools.partial(_hy_inv_kernel, interleave=interleave),
        out_shape=out_shape,
        grid=(bv, H // tt),
        in_specs=[tab(), tab(), tab(), tab(), spec(), spec(), spec(), spec(),
                  tile(u_col), tile(u_col), tile(g_col), tile(g_col),
                  pl.BlockSpec((1, 1, w), lambda b, i: (order, 0, 0))],
        out_specs=out_specs,
        scratch_shapes=scratch,
        compiler_params=_cparams(("parallel", "parallel")),
        name="hyena_inverse_dft",
    )(cet, set_, cot, sot, *es, u_e, u_o, g_e, g_o, bias)


def dft_tables(L):
    H = L // 2
    r = jnp.arange(H, dtype=jnp.int32)[:, None]
    c = jnp.arange(H, dtype=jnp.int32)[None, :]
    ang = lambda k: (k % (2 * L)).astype(F32) * (math.pi / L)
    alt_r, alt_c = (1 - 2 * (r % 2)).astype(F32), (1 - 2 * (c % 2)).astype(F32)
    a_e, a_o = ang(2 * r * c), ang(r * (2 * c + 1))
    bf = lambda a: a.astype(BF16)
    ce = bf(jnp.cos(a_e))
    se = bf(jnp.where(r == 0, alt_c, jnp.sin(a_e)))
    co = bf(jnp.cos(a_o))
    so = bf(jnp.where(r == 0, alt_c, jnp.sin(a_o)))
    set_ = bf(jnp.where(c == 0, alt_r, jnp.sin(a_e)))
    return ce, se, co, so, set_, co.T, so.T


def hyena_features(L):
    pos = jnp.arange(L, dtype=F32)
    t = pos / (L - 1)
    bands = jnp.linspace(1e-4, HY_BANDS - 1, HY_BANDS, dtype=F32)
    ang = (2.0 * math.pi / L) * pos[:, None] * bands[None, :]
    feats = jnp.concatenate([t[:, None], jnp.cos(ang), jnp.sin(ang)], axis=-1)
    feats = jnp.concatenate([feats[0::2], feats[1::2]], axis=0)
    return jnp.pad(feats, ((0, 0), (0, LANES - HY_FEAT)))


def hyena_branch(z_e, z_o, tables, feats, ffn, bias):
    w = bias.shape[-1]
    taps = hyena_filter_taps(feats, *ffn)
    ks = hyena_filter_spectrum(tables, taps, w)
    bias3 = bias.reshape(HY_ORDER, 1, w)
    es = hyena_forward(tables, z_e, z_o, 0, ks, 0)
    y1_e, y1_o = hyena_inverse(tables, es, z_e, z_o, 0, z_e, z_o, 1, bias3, 0, False)
    es = hyena_forward(tables, y1_e, y1_o, 0, ks, 1)
    return hyena_inverse(tables, es, y1_e, y1_o, 0, z_e, z_o, 2, bias3, 1, True)


def _mlstm_chunk(q, k, v1t, li_row, b_row, r_col, c, m, reverse):
    T = q.shape[0]
    s_idx = lax.broadcasted_iota(jnp.int32, (T, T), 0)
    t_idx = lax.broadcasted_iota(jnp.int32, (T, T), 1)
    mask = (s_idx >= t_idx) if reverse else (s_idx <= t_idx)
    nt = (((1,), (1,)), ((), ()))
    kq = lax.dot_general(k, q, nt, preferred_element_type=F32)
    d = jnp.where(mask, r_col + b_row, -1e30)
    m_loc = jnp.max(d, axis=0, keepdims=True)
    st = (kq * jnp.exp(d - m_loc)).astype(BF16)
    intra = jnp.dot(v1t, st, preferred_element_type=F32)
    g = b_row + m
    m_t = jnp.maximum(g, m_loc)
    inter = lax.dot_general(c.astype(BF16), q, nt, preferred_element_type=F32)
    num = jnp.exp(g - m_t) * inter + jnp.exp(m_loc - m_t) * intra
    den = num[HEAD_PAD - 1:HEAD_PAD, :]
    h = num * (1.0 / jnp.maximum(jnp.abs(den), jnp.exp(-m_t)))
    btot = b_row[:, 0:1] if reverse else b_row[:, T - 1:T]
    a = btot - b_row + li_row
    a_max = jnp.max(a, axis=1, keepdims=True)
    wv = (v1t.astype(F32) * jnp.exp(a - a_max)).astype(BF16)
    delta = jnp.dot(wv, k, preferred_element_type=F32)
    m_new = jnp.maximum(btot + m, a_max)
    c_new = jnp.exp(btot + m - m_new) * c + jnp.exp(a_max - m_new) * delta
    return h, c_new, m_new


def _mlstm_kernel(qc_ref, kc_ref, vc_ref, oc_ref, gcc_ref, grc_ref,
                  ql_ref, kl_ref, vl_ref, ol_ref, gcl_ref, grl_ref, ng_ref,
                  outc_ref, outl_ref, hf_ref, hb_ref, *, dh):
    T = ML_CHUNK
    L = ql_ref.shape[1]
    row = lax.broadcasted_iota(jnp.int32, (HEAD_PAD, T), 0)
    ones_row = row == HEAD_PAD - 1
    valid = row < dh
    ng = ng_ref[...]

    def gates(gc, gr, bwd):
        i = 2 if bwd else 0
        return gr[i:i + 1, :], gr[i + 1:i + 2, :], gc[:, i:i + 1] - gc[:, i + 1:i + 2]

    def finish(h, o):
        h = jnp.where(valid, h, 0.0)
        mu = jnp.sum(h, axis=0, keepdims=True) * (1.0 / dh)
        dlt = jnp.where(valid, h - mu, 0.0)
        var = jnp.sum(dlt * dlt, axis=0, keepdims=True) * (1.0 / dh)
        return (dlt * lax.rsqrt(var + LN_EPS) * ng * o.astype(F32)).astype(BF16)

    def aug(vt):
        return jnp.where(ones_row, jnp.ones_like(vt), vt)

    zero_c = jnp.zeros((HEAD_PAD, HEAD_PAD), F32)
    zero_m = jnp.zeros((1, 1), F32)

    q, k, v1t = qc_ref[0], kc_ref[0], aug(vc_ref[0])
    gc, gr = gcc_ref[0, 0], grc_ref[0, 0]
    h_f, c_f, m_f = _mlstm_chunk(q, k, v1t, *gates(gc, gr, False), zero_c, zero_m, False)
    h_b, c_b, m_b = _mlstm_chunk(q, k, v1t, *gates(gc, gr, True), zero_c, zero_m, True)
    outc_ref[0] = finish(h_f + h_b, oc_ref[0])

    nc = L // T
    for ci in range(nc):
        for bwd in (False, True):
            lo = (nc - 1 - ci) * T if bwd else ci * T
            gts = gates(gcl_ref[0, 0, lo:lo + T, :], grl_ref[0, 0, :, lo:lo + T], bwd)
            q, k, v1t = ql_ref[0, lo:lo + T, :], kl_ref[0, lo:lo + T, :], aug(vl_ref[0, :, lo:lo + T])
            if bwd:
                h_b, c_b, m_b = _mlstm_chunk(q, k, v1t, *gts, c_b, m_b, True)
                hb_ref[:, lo:lo + T] = h_b
            else:
                h_f, c_f, m_f = _mlstm_chunk(q, k, v1t, *gts, c_f, m_f, False)
                hf_ref[:, lo:lo + T] = h_f
    for ci in range(nc):
        lo = ci * T
        outl_ref[0, :, lo:lo + T] = finish(hf_ref[:, lo:lo + T] + hb_ref[:, lo:lo + T], ol_ref[0, :, lo:lo + T])


def mlstm_branch(qk_c, vt_c, ot_c, gate_c, qk_l, vt_l, ot_l, gate_l, norm_g_col, dh):
    bv, lc, _ = qk_c.shape
    L = qk_l.shape[1]
    H = ML_HEADS
    assert lc == ML_CHUNK and L % ML_CHUNK == 0

    def per_head(g):
        g = g[:, :, :4 * H].reshape(bv, -1, 4, H)
        return g.transpose(0, 3, 1, 2), g.transpose(0, 3, 2, 1)

    gcc, grc = per_head(gate_c)
    gcl, grl = per_head(gate_l)

    def qk_blk(length, part):
        return pl.BlockSpec((1, length, HEAD_PAD), lambda b, h: (b, 0, part * H + h))

    def t_blk(length):
        return pl.BlockSpec((1, HEAD_PAD, length), lambda b, h: (b, h, 0))

    def gspecs(length):
        return [pl.BlockSpec((1, 1, length, 4), lambda b, h: (b, h, 0, 0)),
                pl.BlockSpec((1, 1, 4, length), lambda b, h: (b, h, 0, 0))]

    def specs(length):
        return [qk_blk(length, 0), qk_blk(length, 1), t_blk(length), t_blk(length)] + gspecs(length)

    return pl.pallas_call(
        functools.partial(_mlstm_kernel, dh=dh),
        out_shape=(jax.ShapeDtypeStruct((bv, H * HEAD_PAD, lc), BF16),
                   jax.ShapeDtypeStruct((bv, H * HEAD_PAD, L), BF16)),
        grid=(bv, H),
        in_specs=specs(lc) + specs(L) + [pl.BlockSpec((HEAD_PAD, 1), lambda b, h: (h, 0))],
        out_specs=(t_blk(lc), t_blk(L)),
        scratch_shapes=[pltpu.VMEM((HEAD_PAD, L), F32), pltpu.VMEM((HEAD_PAD, L), F32)],
        compiler_params=_cparams(("parallel", "parallel")),
        name="mlstm_scan",
    )(qk_c, qk_c, vt_c, ot_c, gcc, grc, qk_l, qk_l, vt_l, ot_l, gcl, grl, norm_g_col)


def _s5_kernel(uc_ref, ul_ref, prow_ref, pcol_ref, bt_ref, cr_ref, cc_ref, y_ref,
               toep_ref, min_ref, mout_ref, kv_ref, s_ref, x_ref, u_ref, yv_ref, *, nb, nctx, nchunk):
    T, CG, P = S5_CHUNK, S5_GROUP, S5_STATE

    ch0 = 0
    for src in (uc_ref, ul_ref):
        xs = pltpu.einshape("bct->cbt", src[...].astype(F32))
        for ch in range(xs.shape[2] // T):
            for ci in range(CG):
                u_ref[(ch0 + ch) * nb:(ch0 + ch + 1) * nb, ci * T:(ci + 1) * T] = (
                    xs[ci, :, ch * T:(ch + 1) * T].astype(BF16))
        ch0 += xs.shape[2] // T

    hp = lax.Precision.HIGHEST
    lane = lax.broadcasted_iota(jnp.int32, (1, 2 * P), 1)
    re_lane = lane < P
    re_row = lax.broadcasted_iota(jnp.int32, (2 * P, 1), 0) < P

    def powers_rows(n, mag1, th1):
        m = jnp.exp(n * mag1)
        return m * jnp.cos(n * th1), m * jnp.sin(n * th1)

    def powers_cols(n, mag1, th1):
        m = jnp.exp(n * mag1)
        return m * jnp.cos(n * th1), m * jnp.sin(n * th1)

    at = []
    kv = jnp.zeros((CG * CG, 2 * T), F32)
    for d in range(2):
        a_re, a_im, ldt = prow_ref[0, d, 0:1, :], prow_ref[0, d, 1:2, :], prow_ref[0, d, 2:3, :]
        dt = jnp.exp(ldt)
        mag1, th1 = dt * a_re, dt * a_im
        ab_re, ab_im = jnp.exp(mag1) * jnp.cos(th1), jnp.exp(mag1) * jnp.sin(th1)
        den = a_re * a_re + a_im * a_im
        co_re = ((ab_re - 1.0) * a_re + ab_im * a_im) / den
        co_im = (ab_im * a_re - (ab_re - 1.0) * a_im) / den
        b_re, b_im = bt_ref[0, d, 0], bt_ref[0, d, 1]
        bb_re = co_re * b_re - co_im * b_im
        bb_im = co_re * b_im + co_im * b_re
        c_re, c_im = cr_ref[0, d, 0], cr_ref[0, d, 1]
        a_re_c, a_im_c = pcol_ref[0, d, :, 0:1], pcol_ref[0, d, :, 1:2]
        dt_c = jnp.exp(pcol_ref[0, d, :, 2:3])
        mag1_c, th1_c = dt_c * a_re_c, dt_c * a_im_c

        rep = lambda a: jnp.broadcast_to(a[:, None, :], (CG, CG, 2 * P)).reshape(CG * CG, 2 * P)
        til = lambda a: jnp.broadcast_to(a[None, :, :], (CG, CG, 2 * P)).reshape(CG * CG, 2 * P)
        w_re = rep(bb_re) * til(c_re) - rep(bb_im) * til(c_im)
        w_im = rep(bb_re) * til(c_im) + rep(bb_im) * til(c_re)
        w = jnp.where(re_lane, w_re, -w_im)
        lagp = lax.broadcasted_iota(jnp.int32, (1, 2 * T), 1)
        n = (T - 1 - lagp) if d else (lagp - (T - 1))
        ok = (n >= 0) & (lagp < 2 * T - 1)
        pc, ps = powers_cols(jnp.maximum(n, 0).astype(F32), mag1_c, th1_c)
        pw = jnp.where(ok, jnp.where(re_row, pc, ps), 0.0)
        kv = kv + jnp.dot(w, pw, precision=hp, preferred_element_type=F32)

        s_col = lax.broadcasted_iota(jnp.int32, (T, 1), 0)
        pc, ps = powers_rows((s_col if d else T - 1 - s_col).astype(F32), mag1, th1)
        for ci in range(CG):
            br, bi = bb_re[ci:ci + 1, :], bb_im[ci:ci + 1, :]
            blk = jnp.where(re_lane, br * pc - bi * ps, br * ps + bi * pc)
            min_ref[d, ci * T:(ci + 1) * T, :] = blk.astype(BF16)

        t_row = lax.broadcasted_iota(jnp.int32, (1, T), 1)
        pc, ps = powers_cols((T - t_row if d else t_row + 1).astype(F32), mag1_c, th1_c)
        for co in range(CG):
            cr, ci_ = cc_ref[0, d, 0, :, co:co + 1], cc_ref[0, d, 1, :, co:co + 1]
            blk = jnp.where(re_row, cr * pc - ci_ * ps, -(cr * ps + ci_ * pc))
            mout_ref[d, :, co * T:(co + 1) * T] = blk.astype(BF16)

        mt = jnp.exp(T * mag1)
        at.append(((mt * jnp.cos(T * th1))[:, :P], (mt * jnp.sin(T * th1))[:, :P]))

    kv_ref[...] = kv

    def build(ci, carry):
        r0 = pl.multiple_of(ci * T, T)
        for co in range(CG):
            vec = kv_ref[pl.ds(ci * CG + co, 1), :]
            rolled = pltpu.roll(jnp.broadcast_to(vec, (T, 2 * T)), T + 1, 1, stride=1, stride_axis=0)
            toep_ref[pl.ds(r0, T), co * T:(co + 1) * T] = rolled[:, :T].astype(BF16)
        return carry

    lax.fori_loop(0, CG, build, 0)

    u = u_ref[...]
    y = jnp.dot(u, toep_ref[...], preferred_element_type=F32)
    for d in range(2):
        s_ref[d] = jnp.dot(u, min_ref[d], preferred_element_type=F32)

    orders = (list(range(nchunk)), list(range(nctx - 1, -1, -1)) + list(range(nchunk - 1, nctx - 1, -1)))
    for d in range(2):
        at_re, at_im = at[d]
        x_re = jnp.zeros((nb, P), F32)
        x_im = jnp.zeros((nb, P), F32)
        for ch in orders[d]:
            x_ref[d, ch * nb:(ch + 1) * nb, :] = jnp.concatenate([x_re, x_im], axis=1)
            s = s_ref[d, ch * nb:(ch + 1) * nb, :]
            x_re, x_im = (at_re * x_re - at_im * x_im + s[:, :P], at_re * x_im + at_im * x_re + s[:, P:])
        y = y + jnp.dot(x_ref[d].astype(BF16), mout_ref[d], preferred_element_type=F32)

    for ch in range(nchunk):
        for co in range(CG):
            yv_ref[co, :, ch * T:(ch + 1) * T] = y[ch * nb:(ch + 1) * nb, co * T:(co + 1) * T]
    y_ref[...] = pltpu.einshape("cbt->bct", yv_ref[...]).astype(y_ref.dtype)


def s5_mix(ut_c, ut_l, a_re, a_im, log_dt, b_re, b_im, c_re, c_im):
    nb, w, lc = ut_c.shape
    L = ut_l.shape[2]
    T, CG, P = S5_CHUNK, S5_GROUP, S5_STATE
    G = w // CG
    nctx, nlat = lc // T, L // T
    nchunk = nctx + nlat

    dup = lambda a: jnp.concatenate([a, a], axis=-1)
    f32 = lambda a: a.astype(F32)
    ldt = jnp.broadcast_to(f32(log_dt)[:, :, None], (2, G, P))
    prow = jnp.stack([dup(f32(a_re)), dup(f32(a_im)), dup(ldt)], axis=2)
    prow = jnp.pad(prow, ((0, 0), (0, 0), (0, 5), (0, 0))).transpose(1, 0, 2, 3)
    pcol = prow.transpose(0, 1, 3, 2)
    bt = jnp.stack([dup(f32(b_re).transpose(0, 1, 3, 2)), dup(f32(b_im).transpose(0, 1, 3, 2))], axis=2)
    bt = bt.transpose(1, 0, 2, 3, 4)
    cr = jnp.stack([dup(f32(c_re)), dup(f32(c_im))], axis=2).transpose(1, 0, 2, 3, 4)
    cc = cr.transpose(0, 1, 2, 4, 3)

    blk = lambda a: pl.BlockSpec((1,) + a.shape[1:], lambda g: (g,) + (0,) * (a.ndim - 1))
    grp = lambda length: pl.BlockSpec((nb, CG, length), lambda g: (0, g, 0))
    r = nchunk * nb
    return pl.pallas_call(
        functools.partial(_s5_kernel, nb=nb, nctx=nctx, nchunk=nchunk),
        out_shape=jax.ShapeDtypeStruct((nb, w, lc + L), BF16),
        grid=(G,),
        in_specs=[grp(lc), grp(L), blk(prow), blk(pcol), blk(bt), blk(cr), blk(cc)],
        out_specs=grp(lc + L),
        scratch_shapes=[pltpu.VMEM((CG * T, CG * T), BF16), pltpu.VMEM((2, CG * T, 2 * P), BF16),
                        pltpu.VMEM((2, 2 * P, CG * T), BF16), pltpu.VMEM((CG * CG, 2 * T), F32),
                        pltpu.VMEM((2, r, 2 * P), F32), pltpu.VMEM((2, r, 2 * P), F32),
                        pltpu.VMEM((r, CG * T), BF16), pltpu.VMEM((CG, nb, lc + L), F32)],
        compiler_params=_cparams(("parallel",)),
        name="s5_mix",
    )(ut_c, ut_l, prow, pcol, bt, cr, cc)


def _gelu_tanh(x):
    return 0.5 * x * (1.0 + jnp.tanh(math.sqrt(2.0 / math.pi) * (x + 0.044715 * (x * x * x))))


def _merge_kernel(yhy_ref, yml_ref, ys_ref, u_ref, ghy_ref, gml_ref, gs5_ref,
                  d_ref, gw_ref, gb_ref, why_ref, wml_ref, ws5_ref, o_ref):
    tn = (((0,), (0,)), ((), ()))
    ys = ys_ref[0].astype(F32) + d_ref[...] * u_ref[0].astype(F32)
    z = _gelu_tanh(ys).astype(BF16)
    glu = jnp.dot(gw_ref[...], z, preferred_element_type=F32) + gb_ref[...]
    y_s5 = (z.astype(F32) * _sigmoid(glu)).astype(BF16)
    acc = ghy_ref[0].astype(F32) * jnp.dot(yhy_ref[0], why_ref[...], preferred_element_type=F32)
    acc = acc + gml_ref[0].astype(F32) * lax.dot_general(yml_ref[0], wml_ref[...], tn, preferred_element_type=F32)
    acc = acc + gs5_ref[0].astype(F32) * lax.dot_general(y_s5, ws5_ref[...], tn, preferred_element_type=F32)
    o_ref[0] = acc.astype(o_ref.dtype)


def merge_branches(y_hy, y_ml, ys_t, u_t, t_off, gate_arr, s5_d, glu_wt, glu_b, w_hy, w_ml, w_s5):
    bv, lv, w = y_hy.shape
    dm = w_hy.shape[1]
    tm = ML_CHUNK
    full = lambda a: pl.BlockSpec(a.shape, lambda b, i: (0,) * a.ndim)
    gate = lambda c: pl.BlockSpec((1, tm, dm), lambda b, i: (b, i, c))
    return pl.pallas_call(
        _merge_kernel,
        out_shape=jax.ShapeDtypeStruct((bv, lv, dm), BF16),
        grid=(bv, lv // tm),
        in_specs=[pl.BlockSpec((1, tm, w), lambda b, i: (b, i, 0)),
                  pl.BlockSpec((1, y_ml.shape[1], tm), lambda b, i: (b, 0, i)),
                  pl.BlockSpec((1, w, tm), lambda b, i: (b, 0, i + t_off)),
                  pl.BlockSpec((1, w, tm), lambda b, i: (b, 0, i)),
                  gate(0), gate(1), gate(2),
                  full(s5_d), full(glu_wt), full(glu_b), full(w_hy), full(w_ml), full(w_s5)],
        out_specs=pl.BlockSpec((1, tm, dm), lambda b, i: (b, i, 0)),
        compiler_params=_cparams(("parallel", "parallel")),
        name="merge_branches",
    )(y_hy, y_ml, ys_t, u_t, gate_arr, gate_arr, gate_arr, s5_d, glu_wt, glu_b, w_hy, w_ml, w_s5)


def _outproj_kernel(m_ref, w_ref, x_ref, gate_ref, g_ref, b_ref, o_ref, *, alpha):
    rb = min(m_ref.shape[1], LANES)
    for s in range(m_ref.shape[1] // rb):
        rows = slice(s * rb, (s + 1) * rb)
        y = jnp.dot(m_ref[0, rows, :], w_ref[...], preferred_element_type=F32)
        r = alpha * x_ref[0, rows, :] + gate_ref[0] * y
        o_ref[0, rows, :] = _layer_norm(r, g_ref[...], b_ref[...])


def out_projection(merged, w_out, x, gate, ln_g, ln_b, alpha, tm):
    bv, lv, d = x.shape
    bm = gate.shape[0]
    mod_map = (lambda b, i: (b, 0, 0)) if bm == bv else (lambda b, i: (0, 0, 0))
    vec = lambda: pl.BlockSpec((1, d), lambda b, i: (0, 0))
    return pl.pallas_call(
        functools.partial(_outproj_kernel, alpha=alpha),
        out_shape=jax.ShapeDtypeStruct((bv, lv, d), F32),
        grid=(bv, lv // tm),
        in_specs=[pl.BlockSpec((1, tm, d), lambda b, i: (b, i, 0)),
                  pl.BlockSpec((d, d), lambda b, i: (0, 0)),
                  pl.BlockSpec((1, tm, d), lambda b, i: (b, i, 0)),
                  pl.BlockSpec((1, 1, d), mod_map), vec(), vec()],
        out_specs=pl.BlockSpec((1, tm, d), lambda b, i: (b, i, 0)),
        compiler_params=_cparams(("parallel", "parallel")),
        name="out_projection_ln",
    )(merged, w_out, x, gate, ln_g, ln_b)


def _mlp_kernel(x_ref, sh_ref, sc_ref, gate_ref, w1_ref, w2_ref, g_ref, b_ref, o_ref, xm_ref, acc_ref, *, alpha):
    k = pl.program_id(2)

    @pl.when(k == 0)
    def _():
        xm_ref[...] = (x_ref[0] * (1.0 + sc_ref[0]) + sh_ref[0]).astype(BF16)
        acc_ref[...] = jnp.zeros_like(acc_ref)

    h = jnp.maximum(jnp.dot(xm_ref[...], w1_ref[...], preferred_element_type=F32), 0.0)
    acc_ref[...] += jnp.dot((h * h).astype(BF16), w2_ref[...], preferred_element_type=F32)

    @pl.when(k == pl.num_programs(2) - 1)
    def _():
        r = alpha * x_ref[0] + gate_ref[0] * acc_ref[...]
        o_ref[0] = _layer_norm(r, g_ref[...], b_ref[...])


def mlp_block(x, shift, scale, gate, w1, w2, ln_g, ln_b, alpha, tm, kf):
    bv, lv, d = x.shape
    dff = w1.shape[1]
    bm = gate.shape[0]
    mod_map = (lambda b, i, k: (b, 0, 0)) if bm == bv else (lambda b, i, k: (0, 0, 0))
    vec = lambda: pl.BlockSpec((1, d), lambda b, i, k: (0, 0))
    return pl.pallas_call(
        functools.partial(_mlp_kernel, alpha=alpha),
        out_shape=jax.ShapeDtypeStruct((bv, lv, d), F32),
        grid=(bv, lv // tm, dff // kf),
        in_specs=[pl.BlockSpec((1, tm, d), lambda b, i, k: (b, i, 0)),
                  pl.BlockSpec((1, 1, d), mod_map), pl.BlockSpec((1, 1, d), mod_map),
                  pl.BlockSpec((1, 1, d), mod_map),
                  pl.BlockSpec((d, kf), lambda b, i, k: (0, k)),
                  pl.BlockSpec((kf, d), lambda b, i, k: (k, 0)),
                  vec(), vec()],
        out_specs=pl.BlockSpec((1, tm, d), lambda b, i, k: (b, i, 0)),
        scratch_shapes=[pltpu.VMEM((tm, d), BF16), pltpu.VMEM((tm, d), F32)],
        compiler_params=_cparams(("parallel", "parallel", "arbitrary")),
        name="mlp_ln",
    )(x, shift, scale, gate, w1, w2, ln_g, ln_b)


def _pad_heads(a, axis, dh):
    shp = a.shape
    a = a.reshape(shp[:axis] + (ML_HEADS, dh) + shp[axis + 1:])
    pad = [(0, 0)] * a.ndim
    pad[axis + 1] = (0, HEAD_PAD - dh)
    a = jnp.pad(a, pad)
    return a.reshape(shp[:axis] + (ML_HEADS * HEAD_PAD,) + shp[axis + 1:])


def _layer_params(l, p, w_hy, w_ml, w_s5, d_model):
    dh = w_ml // ML_HEADS
    sizes = ((HY_ORDER + 1) * w_hy, 2 * w_ml, w_ml, w_ml, 4 * ML_HEADS, w_s5, N_BRANCH * d_model)
    pts = [0]
    for s in sizes:
        pts.append(pts[-1] + s)
    w_in = p["w_in"][l]
    hy, qk, v, o, gt, u, mg = (w_in[:, pts[i]:pts[i + 1]] for i in range(7))
    ph = lambda a: _pad_heads(a, a.ndim - 1, dh)
    none = lambda n: (jnp.zeros((3, n), F32), jnp.zeros((1, n), F32), jnp.ones((1, n), F32))
    half = ML_HEADS * HEAD_PAD
    cw, cb = p["ml_conv_w"][l], p["ml_conv_b"][l]
    out = {}
    out["mg"] = (mg.astype(BF16),) + none(mg.shape[1])
    out["hy"] = (hy.astype(BF16), p["hy_conv_w"][l], p["hy_conv_b"][l].reshape(1, -1), jnp.ones((1, hy.shape[1]), F32))
    out["qk"] = (jnp.concatenate([ph(qk[:, :w_ml]), ph(qk[:, w_ml:])], axis=1).astype(BF16),
                 jnp.concatenate([ph(cw[:, :w_ml]), ph(cw[:, w_ml:])], axis=1),
                 jnp.concatenate([ph(cb[:w_ml]), ph(cb[w_ml:])]).reshape(1, -1),
                 jnp.concatenate([jnp.ones((half,), F32), jnp.full((half,), dh ** -0.5, F32)]).reshape(1, -1))
    out["vt"] = (ph(v).T.astype(BF16),) + none(half)
    out["ot"] = (ph(o).T.astype(BF16),) + none(half)
    ngt = gt.shape[1]
    out["ut"] = (u.T.astype(BF16),) + none(u.shape[1])
    out["gt"] = (jnp.pad(gt, ((0, 0), (0, LANES - ngt))).astype(BF16),
                 jnp.pad(p["ml_gate_b"][l].reshape(-1), ((0, LANES - ngt),)).reshape(1, -1))
    out["norm_g"] = ph(p["ml_norm_g"][l]).reshape(-1, 1)
    out["w_ml_out"] = _pad_heads(p["w_ml_out"][l], 0, dh).astype(BF16)
    w3 = p["hy_ffn_w3"][l]
    dec = p["hy_decay"][l]
    dec_cols = jnp.broadcast_to(dec[:, None, :], (HY_ORDER, 2, w_hy)).reshape(1, -1)
    hpad = LANES - HY_HIDDEN
    row = lambda a: jnp.pad(a.reshape(1, -1), ((0, 0), (0, hpad)))
    out["hy_ffn"] = (jnp.pad(p["hy_ffn_w1"][l], ((0, LANES - HY_FEAT), (0, hpad))), row(p["hy_ffn_b1"][l]),
                     jnp.pad(p["hy_ffn_w2"][l], ((0, hpad), (0, hpad))), row(p["hy_ffn_b2"][l]),
                     jnp.pad(w3, ((0, hpad), (0, 0))), row(p["hy_sin_freq"][l]), dec_cols)
    return out


def kernel(x, c, ctx, c_ctx, w_mod, b_mod, w_in, hy_conv_w, hy_conv_b, hy_ffn_w1, hy_ffn_b1, hy_ffn_w2,
           hy_ffn_b2, hy_ffn_w3, hy_sin_freq, hy_decay, hy_bias, ml_conv_w, ml_conv_b, ml_gate_b, ml_norm_g,
           s5_a_re, s5_a_im, s5_log_dt, s5_b_re, s5_b_im, s5_c_re, s5_c_im, s5_d, s5_glu_w, s5_glu_b,
           w_hy_out, w_ml_out, w_s5_out, w_out, ln1_g, ln1_b, ln2_g, ln2_b, w_ff1, w_ff2):
    p = dict(w_in=w_in, hy_conv_w=hy_conv_w, hy_conv_b=hy_conv_b, hy_ffn_w1=hy_ffn_w1, hy_ffn_b1=hy_ffn_b1,
             hy_ffn_w2=hy_ffn_w2, hy_ffn_b2=hy_ffn_b2, hy_ffn_w3=hy_ffn_w3, hy_sin_freq=hy_sin_freq,
             hy_decay=hy_decay, ml_conv_w=ml_conv_w, ml_conv_b=ml_conv_b, ml_gate_b=ml_gate_b,
             ml_norm_g=ml_norm_g, w_ml_out=w_ml_out)
    B, L, D = x.shape
    LC = ctx.shape[1]
    depth = w_in.shape[0]
    w_hy = hy_bias.shape[-1]
    w_ml = ml_norm_g.shape[-1]
    w_s5 = s5_d.shape[-1]
    dh = w_ml // ML_HEADS
    alpha = (2 * depth) ** 0.25
    half = ML_HEADS * HEAD_PAD
    tm_l, tm_c = 1024, LC

    tab_l, tab_c = dft_tables(L), dft_tables(LC)
    feats_l, feats_c = hyena_features(L), hyena_features(LC)

    mrows = 8 * ((B + 1 + 7) // 8)
    c_rows = jnp.zeros((mrows, D), F32).at[:B].set(c).at[B].set(c_ctx)

    for l in range(depth):
        need_ctx = l < depth - 1
        lp = _layer_params(l, p, w_hy, w_ml, w_s5, D)
        mod = mod_vectors(c_rows, w_mod, b_mod.reshape(depth, 1, -1), l)
        mod_l = [mod[:B, k * D:(k + 1) * D].reshape(B, 1, D) for k in range(6)]
        mod_c = [mod[B:B + 1, k * D:(k + 1) * D].reshape(1, 1, D) for k in range(6)]

        def mixer_inputs(xv, md, row_len, tm, full):
            proj = functools.partial(projection, xv, md[0], md[1], tm=tm, row_len=row_len)
            r = {}
            r["qk"] = proj(*lp["qk"], kind="convsilu", tn=half)
            r["vt"] = proj(*lp["vt"], kind="plain", tn=half, transposed=True)
            r["ot"] = proj(*lp["ot"], kind="sigmoid", tn=half, transposed=True)
            r["ut"] = proj(*lp["ut"], kind="plain", tn=w_s5, transposed=True)
            r["g"] = gate_projection(xv, md[0], md[1], *lp["gt"], tm=tm)
            if full:
                r["mg"] = proj(*lp["mg"], kind="sigmoid", tn=2 * w_hy)
                r["z"] = projection_even_odd(xv, md[0], md[1], *lp["hy"][:3], tn=w_hy, tm=tm, row_len=row_len)
            return r

        r_l = mixer_inputs(x, mod_l, GRID_W, tm_l, True)
        r_c = mixer_inputs(ctx, mod_c, LC, tm_c, need_ctx)

        hm_c, hm_l = mlstm_branch(r_c["qk"], r_c["vt"], r_c["ot"], r_c["g"],
                                  r_l["qk"], r_l["vt"], r_l["ot"], r_l["g"], lp["norm_g"], dh)

        ys_t = s5_mix(r_c["ut"], r_l["ut"], s5_a_re[l], s5_a_im[l], s5_log_dt[l],
                      s5_b_re[l], s5_b_im[l], s5_c_re[l], s5_c_im[l])

        s5_dv = s5_d[l].reshape(-1, 1)
        glu_wt = s5_glu_w[l].T.astype(BF16)
        glu_b = s5_glu_b[l].reshape(-1, 1)
        w_hy_o = w_hy_out[l].astype(BF16)
        w_s5_o = w_s5_out[l].astype(BF16)
        w_o = w_out[l].astype(BF16)
        w1 = w_ff1[l].astype(BF16)
        w2 = w_ff2[l].astype(BF16)
        g1, b1 = ln1_g[l].reshape(1, -1), ln1_b[l].reshape(1, -1)
        g2, b2 = ln2_g[l].reshape(1, -1), ln2_b[l].reshape(1, -1)

        def finish(xv, r, hm, t_off, tables, feats, md, tm):
            y_hy = hyena_branch(*r["z"], tables, feats, lp["hy_ffn"], hy_bias[l])
            merged = merge_branches(y_hy, hm, ys_t, r["ut"], t_off, r["mg"], s5_dv, glu_wt, glu_b,
                                    w_hy_o, lp["w_ml_out"], w_s5_o)
            if md[2].shape[0] == 1:
                merged, xv = merged.reshape(1, -1, D), xv.reshape(1, -1, D)
            x1 = out_projection(merged, w_o, xv, md[2], g1, b1, alpha, tm)
            return mlp_block(x1, md[3], md[4], md[5], w1, w2, g2, b2, alpha, tm, 1024)

        x = finish(x, r_l, hm_l, LC // ML_CHUNK, tab_l, feats_l, mod_l, 512)
        if need_ctx:
            ctx = finish(ctx, r_c, hm_c, 0, tab_c, feats_c, mod_c, 512).reshape(B, LC, D)
    return x
```

```python
import functools
import math

import jax
import jax.numpy as jnp
from jax import lax
from jax.experimental import pallas as pl
from jax.experimental.pallas import tpu as pltpu

F32 = jnp.float32
BF16 = jnp.bfloat16

GRID_W = 64
HY_ORDER = 2
HY_BANDS = 16
HY_FEAT = 1 + 2 * HY_BANDS
HY_HIDDEN = 64
ML_HEADS = 4
ML_CHUNK = 256
S5_GROUP = 16
S5_STATE = 64
N_BRANCH = 3
LN_EPS = 1e-5

LANES = 128
BF16_TILE_ROWS = 16
MXU_DIM = 256
VMEM_LIMIT = 56 * 1024 * 1024

HEAD_PAD = MXU_DIM
S5_CHUNK = LANES


def _cparams(sem):
    return pltpu.CompilerParams(dimension_semantics=sem, vmem_limit_bytes=VMEM_LIMIT)


def _sigmoid(x):
    return 0.5 * jnp.tanh(0.5 * x) + 0.5


def _layer_norm(r, g, b):
    mu = jnp.mean(r, axis=-1, keepdims=True)
    d = r - mu
    var = jnp.mean(d * d, axis=-1, keepdims=True)
    return d * lax.rsqrt(var + LN_EPS) * g + b


def _mod_kernel(c_ref, w_ref, b_ref, o_ref):
    c = c_ref[...]
    a = (c * _sigmoid(c)).astype(BF16)
    o_ref[...] = jnp.dot(a, w_ref[0].astype(BF16), preferred_element_type=F32) + b_ref[0]


def mod_vectors(c_rows, w, b, layer):
    m, d = c_rows.shape
    n = w.shape[2]
    tn = 1024
    return pl.pallas_call(
        _mod_kernel,
        out_shape=jax.ShapeDtypeStruct((m, n), F32),
        grid=(n // tn,),
        in_specs=[pl.BlockSpec((m, d), lambda j: (0, 0)),
                  pl.BlockSpec((1, d, tn), lambda j: (layer, 0, j)),
                  pl.BlockSpec((1, 1, tn), lambda j: (layer, 0, j))],
        out_specs=pl.BlockSpec((m, tn), lambda j: (0, j)),
        compiler_params=_cparams(("parallel",)),
        name="mod_vectors",
    )(c_rows, w, b)


def _short_conv(acc, w, b, row_len):
    rows = acc.shape[0]
    t = lax.broadcasted_iota(jnp.int32, acc.shape, 0) & (row_len - 1)
    prev = jnp.where(t == 0, 0.0, pltpu.roll(acc, 1, 0))
    nxt = jnp.where(t == row_len - 1, 0.0, pltpu.roll(acc, rows - 1, 0))
    return prev * w[0:1] + acc * w[1:2] + nxt * w[2:3] + b


def _modulate_rows(xm_ref, x_ref, sh_ref, sc_ref, rows):
    xm_ref[rows, :] = (x_ref[0, rows, :] * (1.0 + sc_ref[0]) + sh_ref[0]).astype(BF16)


def _modulate_into(xm_ref, x_ref, sh_ref, sc_ref):
    _modulate_rows(xm_ref, x_ref, sh_ref, sc_ref, slice(None))


def _proj_kernel(x_ref, sh_ref, sc_ref, w_ref, cw_ref, cb_ref, cs_ref, o_ref, xm_ref, *, kind, row_len, rb,
                 transposed, single_tile):
    def epilogue(acc):
        if kind == "conv":
            return _short_conv(acc, cw_ref[...], cb_ref[...], row_len)
        if kind == "convsilu":
            y = _short_conv(acc, cw_ref[...], cb_ref[...], row_len)
            return y * _sigmoid(y) * cs_ref[...]
        if kind == "sigmoid":
            return _sigmoid(acc)
        return acc

    def body(modulate):
        for r in range(xm_ref.shape[0] // rb):
            rows = slice(r * rb, (r + 1) * rb)
            if modulate:
                _modulate_rows(xm_ref, x_ref, sh_ref, sc_ref, rows)
            xs = xm_ref[rows, :]
            if transposed:
                acc = lax.dot_general(w_ref[...], xs, (((1,), (1,)), ((), ())), preferred_element_type=F32)
                o_ref[0, :, rows] = epilogue(acc).astype(o_ref.dtype)
            else:
                acc = jnp.dot(xs, w_ref[...], preferred_element_type=F32)
                o_ref[0, rows, :] = epilogue(acc).astype(o_ref.dtype)

    if single_tile:
        body(True)
    else:
        first = pl.program_id(2) == 0
        pl.when(first)(lambda: body(True))
        pl.when(jnp.logical_not(first))(lambda: body(False))


def projection(x, shift, scale, w, cw, cb, cs, *, kind, tn, tm, row_len, transposed=False):
    bv, lv, d = x.shape
    n = w.shape[0] if transposed else w.shape[1]
    rb = min(tm, ML_CHUNK)
    assert n % tn == 0 and lv % tm == 0 and rb % row_len == 0
    bm = shift.shape[0]
    mod_map = (lambda b, i, j: (b, 0, 0)) if bm == bv else (lambda b, i, j: (0, 0, 0))
    if transposed:
        assert kind in ("plain", "sigmoid")
        w_spec = pl.BlockSpec((tn, d), lambda b, i, j: (j, 0))
        out_shape = jax.ShapeDtypeStruct((bv, n, lv), BF16)
        out_spec = pl.BlockSpec((1, tn, tm), lambda b, i, j: (b, j, i))
    else:
        w_spec = pl.BlockSpec((d, tn), lambda b, i, j: (0, j))
        out_shape = jax.ShapeDtypeStruct((bv, lv, n), BF16)
        out_spec = pl.BlockSpec((1, tm, tn), lambda b, i, j: (b, i, j))
    col = lambda rows: pl.BlockSpec((rows, tn), lambda b, i, j: (0, j))
    return pl.pallas_call(
        functools.partial(_proj_kernel, kind=kind, row_len=row_len, rb=rb, transposed=transposed,
                          single_tile=(n == tn)),
        out_shape=out_shape,
        grid=(bv, lv // tm, n // tn),
        in_specs=[pl.BlockSpec((1, tm, d), lambda b, i, j: (b, i, 0)),
                  pl.BlockSpec((1, 1, d), mod_map), pl.BlockSpec((1, 1, d), mod_map),
                  w_spec, col(3), col(1), col(1)],
        out_specs=out_spec,
        scratch_shapes=[pltpu.VMEM((tm, d), BF16)],
        compiler_params=_cparams(("parallel", "parallel", "arbitrary")),
        name="projection_" + kind + ("_t" if transposed else ""),
    )(x, shift, scale, w, cw, cb, cs)


def _proj_eo_kernel(x_ref, sh_ref, sc_ref, w_ref, cw_ref, cb_ref, oe_ref, oo_ref, xm_ref, tmp_ref, *, row_len, rb):
    nlc = w_ref.shape[1] // LANES

    def body(modulate):
        for r in range(xm_ref.shape[0] // rb):
            rows = slice(r * rb, (r + 1) * rb)
            if modulate:
                _modulate_rows(xm_ref, x_ref, sh_ref, sc_ref, rows)
            acc = jnp.dot(xm_ref[rows, :], w_ref[...], preferred_element_type=F32)
            val = _short_conv(acc, cw_ref[...], cb_ref[...], row_len)
            half = slice(r * rb // 2, (r + 1) * rb // 2)
            for c in range(nlc):
                lanes = slice(c * LANES, (c + 1) * LANES)
                tmp_ref[r * nlc + c] = val[:, lanes]
                oe_ref[0, half, lanes] = tmp_ref[r * nlc + c, pl.ds(0, rb // 2, stride=2), :].astype(oe_ref.dtype)
                oo_ref[0, half, lanes] = tmp_ref[r * nlc + c, pl.ds(1, rb // 2, stride=2), :].astype(oo_ref.dtype)

    first = pl.program_id(2) == 0
    pl.when(first)(lambda: body(True))
    pl.when(jnp.logical_not(first))(lambda: body(False))


def projection_even_odd(x, shift, scale, w, cw, cb, *, tn, tm, row_len):
    bv, lv, d = x.shape
    n = w.shape[1]
    rb = min(tm, ML_CHUNK)
    assert n % tn == 0 and n > tn and lv % tm == 0 and rb % row_len == 0
    bm = shift.shape[0]
    mod_map = (lambda b, i, j: (b, 0, 0)) if bm == bv else (lambda b, i, j: (0, 0, 0))
    col = lambda rows: pl.BlockSpec((rows, tn), lambda b, i, j: (0, j))
    out = jax.ShapeDtypeStruct((bv, lv // 2, n), BF16)
    half = lambda: pl.BlockSpec((1, tm // 2, tn), lambda b, i, j: (b, i, j))
    return pl.pallas_call(
        functools.partial(_proj_eo_kernel, row_len=row_len, rb=rb),
        out_shape=(out, out),
        grid=(bv, lv // tm, n // tn),
        in_specs=[pl.BlockSpec((1, tm, d), lambda b, i, j: (b, i, 0)),
                  pl.BlockSpec((1, 1, d), mod_map), pl.BlockSpec((1, 1, d), mod_map),
                  pl.BlockSpec((d, tn), lambda b, i, j: (0, j)), col(3), col(1)],
        out_specs=(half(), half()),
        scratch_shapes=[pltpu.VMEM((tm, d), BF16), pltpu.VMEM((tm // rb * (tn // LANES), rb, LANES), F32)],
        compiler_params=_cparams(("parallel", "parallel", "arbitrary")),
        name="projection_conv_even_odd",
    )(x, shift, scale, w, cw, cb)


def _log_sigmoid(x):
    return jnp.minimum(x, 0.0) - jnp.log(1.0 + jnp.exp(-jnp.abs(x)))


def _gate_kernel(x_ref, sh_ref, sc_ref, wg_ref, gb_ref, g_ref, xm_ref):
    _modulate_into(xm_ref, x_ref, sh_ref, sc_ref)
    T = ML_CHUNK
    hp = lax.Precision.HIGHEST
    r = lax.broadcasted_iota(jnp.int32, (T, T), 0)
    s = lax.broadcasted_iota(jnp.int32, (T, T), 1)
    tri_f = (s <= r).astype(F32)
    tri_b = (s >= r).astype(F32)
    kind = lax.broadcasted_iota(jnp.int32, (T, LANES), 1) // ML_HEADS
    for c in range(xm_ref.shape[0] // T):
        g = jnp.dot(xm_ref[c * T:(c + 1) * T, :], wg_ref[...], preferred_element_type=F32) + gb_ref[...]
        ls = _log_sigmoid(g)
        cum_f = jnp.dot(tri_f, ls, precision=hp, preferred_element_type=F32)
        cum_b = jnp.dot(tri_b, ls, precision=hp, preferred_element_type=F32)
        g_ref[0, c * T:(c + 1) * T, :] = jnp.where(kind == 1, cum_f, jnp.where(kind == 3, cum_b, g))


def gate_projection(x, shift, scale, wg, gb, *, tm):
    bv, lv, d = x.shape
    assert lv % tm == 0 and tm % ML_CHUNK == 0
    bm = shift.shape[0]
    mod_map = (lambda b, i: (b, 0, 0)) if bm == bv else (lambda b, i: (0, 0, 0))
    full = lambda a: pl.BlockSpec(a.shape, lambda b, i: (0, 0))
    return pl.pallas_call(
        _gate_kernel,
        out_shape=jax.ShapeDtypeStruct((bv, lv, LANES), F32),
        grid=(bv, lv // tm),
        in_specs=[pl.BlockSpec((1, tm, d), lambda b, i: (b, i, 0)),
                  pl.BlockSpec((1, 1, d), mod_map), pl.BlockSpec((1, 1, d), mod_map),
                  full(wg), full(gb)],
        out_specs=pl.BlockSpec((1, tm, LANES), lambda b, i: (b, i, 0)),
        scratch_shapes=[pltpu.VMEM((tm, d), BF16)],
        compiler_params=_cparams(("parallel", "parallel")),
        name="gate_projection",
    )(x, shift, scale, wg, gb)


def _hy_ffn_kernel(feat_ref, w1_ref, b1_ref, w2_ref, b2_ref, w3_ref, fr_ref, dec_ref, o_ref):
    hp = lax.Precision.HIGHEST
    feats = feat_ref[...]
    fr = fr_ref[...]
    h = jnp.sin(fr * (jnp.dot(feats, w1_ref[...], precision=hp, preferred_element_type=F32) + b1_ref[...]))
    h = jnp.sin(fr * (jnp.dot(h, w2_ref[...], precision=hp, preferred_element_type=F32) + b2_ref[...]))
    h = jnp.dot(h, w3_ref[...], precision=hp, preferred_element_type=F32)
    t = feats[:, 0:1]
    o_ref[...] = h * jnp.exp(-t * jnp.abs(dec_ref[...]))


def hyena_filter_taps(feats, w1p, b1, w2, b2, w3, freq, decay_cols):
    L = feats.shape[0]
    n = w3.shape[1]
    tl = min(L, 256)
    full = lambda a: pl.BlockSpec(a.shape, lambda i: (0, 0))
    return pl.pallas_call(
        _hy_ffn_kernel,
        out_shape=jax.ShapeDtypeStruct((L, n), F32),
        grid=(L // tl,),
        in_specs=[pl.BlockSpec((tl, feats.shape[1]), lambda i: (i, 0)),
                  full(w1p), full(b1), full(w2), full(b2), full(w3), full(freq), full(decay_cols)],
        out_specs=pl.BlockSpec((tl, n), lambda i: (i, 0)),
        compiler_params=_cparams(("parallel",)),
        name="hyena_filter_taps",
    )(feats, w1p, b1, w2, b2, w3, freq, decay_cols)


def _row0(shape, i):
    return (lax.broadcasted_iota(jnp.int32, shape, 0) + i * shape[0]) == 0


def _hy_spec_kernel(ce_ref, se_ref, co_ref, so_ref, hfe_ref, hbe_ref, hfo_ref, hbo_ref,
                    kpa_ref, kqa_ref, kpb_ref, kqb_ref, *, L):
    i = pl.program_id(1)
    hf_e, hf_o = hfe_ref[...], hfo_ref[...]
    hb_e, hb_o = hbe_ref[...], hbo_ref[...]
    m = lax.broadcasted_iota(jnp.int32, hb_e.shape, 0)
    hb_e = jnp.where(m == 0, 0.0, hb_e)
    s_e, s_o, d_e, d_o = hf_e + hb_e, hf_o + hb_o, hf_e - hb_e, hf_o - hb_o
    dot = lambda t, x: jnp.dot(t[...], x.astype(BF16), preferred_element_type=F32)
    pe, po, qe, qo = dot(ce_ref, s_e), dot(co_ref, s_o), dot(se_ref, d_e), dot(so_ref, d_o)
    r0 = _row0(pe.shape, i)
    p_half = jnp.sum(s_e * (1 - 2 * (m & 1)).astype(F32), axis=0, keepdims=True)
    scale = 1.0 / L
    kpa_ref[0] = (pe + po) * scale
    kpb_ref[0] = (pe - po) * scale
    kqa_ref[0] = jnp.where(r0, p_half, qe + qo) * scale
    kqb_ref[0] = jnp.where(r0, qo, qo - qe) * scale


def hyena_filter_spectrum(tables, taps, w):
    ce, se, co, so = tables[:4]
    H = ce.shape[0]
    tf = min(H, 512)
    out = jax.ShapeDtypeStruct((HY_ORDER, H, w), F32)
    tab = lambda: pl.BlockSpec((tf, H), lambda o, i: (i, 0))
    slot = lambda: pl.BlockSpec((1, tf, w), lambda o, i: (o, i, 0))
    tap = lambda parity, direction: pl.BlockSpec((H, w), lambda o, i: (parity, 2 * o + direction))
    return pl.pallas_call(
        functools.partial(_hy_spec_kernel, L=2 * H),
        out_shape=(out,) * 4,
        grid=(HY_ORDER, H // tf),
        in_specs=[tab(), tab(), tab(), tab(), tap(0, 0), tap(0, 1), tap(1, 0), tap(1, 1)],
        out_specs=(slot(),) * 4,
        compiler_params=_cparams(("parallel", "parallel")),
        name="hyena_filter_spectrum",
    )(ce, se, co, so, taps, taps, taps, taps)


def _hy_fwd_kernel(ce_ref, se_ref, co_ref, so_ref, ue_ref, uo_ref, kpa_ref, kqa_ref, kpb_ref, kqb_ref,
                   eep_ref, eeq_ref, eop_ref, eoq_ref):
    ue, uo = ue_ref[0], uo_ref[0]
    dot = lambda t, x: jnp.dot(t[...], x, preferred_element_type=F32)
    pe, qe, po, qo = dot(ce_ref, ue), dot(se_ref, ue), dot(co_ref, uo), dot(so_ref, uo)

    def emit(rows, pe, qe, po, qo, r0):
        sel = (lambda special, general: general) if r0 is None else (lambda special, general: jnp.where(r0, special, general))
        kpa, kqa, kpb, kqb = kpa_ref[0, rows, :], kqa_ref[0, rows, :], kpb_ref[0, rows, :], kqb_ref[0, rows, :]
        pa, pb = pe + po, pe - po
        qa, qb = sel(qe, qe + qo), sel(qo, qo - qe)
        ypa = sel(0.5 * pa * kpa, pa * kpa - qa * kqa)
        ypb = sel(0.5 * pb * kpb, pb * kpb - qb * kqb)
        yqa = sel(qa * kqa - qb * kqb, pa * kqa + qa * kpa)
        yqb = sel(qa * kqb + qb * kqa, pb * kqb + qb * kpb)
        eep_ref[0, rows, :] = (ypa + ypb).astype(eep_ref.dtype)
        eop_ref[0, rows, :] = (ypa - ypb).astype(eop_ref.dtype)
        eeq_ref[0, rows, :] = sel(yqa, yqa - yqb).astype(eeq_ref.dtype)
        eoq_ref[0, rows, :] = sel(yqb, yqa + yqb).astype(eoq_ref.dtype)

    emit(slice(None), pe, qe, po, qo, None)

    @pl.when(pl.program_id(1) == 0)
    def _():
        top = slice(0, BF16_TILE_ROWS)
        r0 = lax.broadcasted_iota(jnp.int32, (BF16_TILE_ROWS, pe.shape[1]), 0) == 0
        emit(top, pe[top], qe[top], po[top], qo[top], r0)


def hyena_forward(tables, u_e, u_o, u_col, ks, order):
    ce, se, co, so = tables[:4]
    bv, H, _ = u_e.shape
    w = ks[0].shape[2]
    tf = min(H, 512)
    out = jax.ShapeDtypeStruct((bv, H, w), BF16)
    tab = lambda: pl.BlockSpec((tf, H), lambda b, i: (i, 0))
    sig = lambda: pl.BlockSpec((1, H, w), lambda b, i: (b, 0, u_col))
    slot = lambda: pl.BlockSpec((1, tf, w), lambda b, i: (order, i, 0))
    res = lambda: pl.BlockSpec((1, tf, w), lambda b, i: (b, i, 0))
    return pl.pallas_call(
        _hy_fwd_kernel,
        out_shape=(out,) * 4,
        grid=(bv, H // tf),
        in_specs=[tab(), tab(), tab(), tab(), sig(), sig(), slot(), slot(), slot(), slot()],
        out_specs=(res(),) * 4,
        compiler_params=_cparams(("parallel", "parallel")),
        name="hyena_forward_dft",
    )(ce, se, co, so, u_e, u_o, *ks)


def _hy_inv_kernel(cet_ref, set_ref, cot_ref, sot_ref, eep_ref, eeq_ref, eop_ref, eoq_ref,
                   ue_ref, uo_ref, ge_ref, go_ref, bias_ref, *o_refs, interleave):
    dot = lambda t, x: jnp.dot(t[...], x[0], preferred_element_type=F32)
    ye = dot(cet_ref, eep_ref) + dot(set_ref, eeq_ref)
    yo = dot(cot_ref, eop_ref) + dot(sot_ref, eoq_ref)
    bias = bias_ref[0]
    oe = ge_ref[0].astype(F32) * (ye + ue_ref[0].astype(F32) * bias)
    oo = go_ref[0].astype(F32) * (yo + uo_ref[0].astype(F32) * bias)
    if interleave:
        o_ref, mix_ref = o_refs
        n = oe.shape[0]
        for c in range(oe.shape[1] // LANES):
            lanes = slice(c * LANES, (c + 1) * LANES)
            mix_ref[c, pl.ds(0, n, stride=2), :] = oe[:, lanes]
            mix_ref[c, pl.ds(1, n, stride=2), :] = oo[:, lanes]
            o_ref[0, :, lanes] = mix_ref[c].astype(o_ref.dtype)
    else:
        oe_ref, oo_ref = o_refs
        oe_ref[0] = oe.astype(oe_ref.dtype)
        oo_ref[0] = oo.astype(oo_ref.dtype)


def hyena_inverse(tables, es, u_e, u_o, u_col, g_e, g_o, g_col, bias, order, interleave):
    cet, set_, cot, sot = tables[0], tables[4], tables[5], tables[6]
    bv, H, w = es[0].shape
    tt = min(H, 512)
    tab = lambda: pl.BlockSpec((tt, H), lambda b, i: (i, 0))
    spec = lambda: pl.BlockSpec((1, H, w), lambda b, i: (b, 0, 0))
    tile = lambda col: pl.BlockSpec((1, tt, w), lambda b, i: (b, i, col))
    if interleave:
        out_shape = jax.ShapeDtypeStruct((bv, 2 * H, w), BF16)
        out_specs = pl.BlockSpec((1, 2 * tt, w), lambda b, i: (b, i, 0))
        scratch = [pltpu.VMEM((w // LANES, 2 * tt, LANES), F32)]
    else:
        out_shape = (jax.ShapeDtypeStruct((bv, H, w), BF16),) * 2
        out_specs = (tile(0), tile(0))
        scratch = []
    return pl.pallas_call(
        functools.partial(_hy_inv_kernel, interleave=interleave),
        out_shape=out_shape,
        grid=(bv, H // tt),
        in_specs=[tab(), tab(), tab(), tab(), spec(), spec(), spec(), spec(),
                  tile(u_col), tile(u_col), tile(g_col), tile(g_col),
                  pl.BlockSpec((1, 1, w), lambda b, i: (order, 0, 0))],
        out_specs=out_specs,
        scratch_shapes=scratch,
        compiler_params=_cparams(("parallel", "parallel")),
        name="hyena_inverse_dft",
    )(cet, set_, cot, sot, *es, u_e, u_o, g_e, g_o, bias)


def dft_tables(L):
    H = L // 2
    r = jnp.arange(H, dtype=jnp.int32)[:, None]
    c = jnp.arange(H, dtype=jnp.int32)[None, :]
    ang = lambda k: (k % (2 * L)).astype(F32) * (math.pi / L)
    alt_r, alt_c = (1 - 2 * (r % 2)).astype(F32), (1 - 2 * (c % 2)).astype(F32)
    a_e, a_o = ang(2 * r * c), ang(r * (2 * c + 1))
    bf = lambda a: a.astype(BF16)
    ce = bf(jnp.cos(a_e))
    se = bf(jnp.where(r == 0, alt_c, jnp.sin(a_e)))
    co = bf(jnp.cos(a_o))
    so = bf(jnp.where(r == 0, alt_c, jnp.sin(a_o)))
    set_ = bf(jnp.where(c == 0, alt_r, jnp.sin(a_e)))
    return ce, se, co, so, set_, co.T, so.T


def hyena_features(L):
    pos = jnp.arange(L, dtype=F32)
    t = pos / (L - 1)
    bands = jnp.linspace(1e-4, HY_BANDS - 1, HY_BANDS, dtype=F32)
    ang = (2.0 * math.pi / L) * pos[:, None] * bands[None, :]
    feats = jnp.concatenate([t[:, None], jnp.cos(ang), jnp.sin(ang)], axis=-1)
    feats = jnp.concatenate([feats[0::2], feats[1::2]], axis=0)
    return jnp.pad(feats, ((0, 0), (0, LANES - HY_FEAT)))


def hyena_branch(z_e, z_o, tables, feats, ffn, bias):
    w = bias.shape[-1]
    taps = hyena_filter_taps(feats, *ffn)
    ks = hyena_filter_spectrum(tables, taps, w)
    bias3 = bias.reshape(HY_ORDER, 1, w)
    es = hyena_forward(tables, z_e, z_o, 0, ks, 0)
    y1_e, y1_o = hyena_inverse(tables, es, z_e, z_o, 0, z_e, z_o, 1, bias3, 0, False)
    es = hyena_forward(tables, y1_e, y1_o, 0, ks, 1)
    return hyena_inverse(tables, es, y1_e, y1_o, 0, z_e, z_o, 2, bias3, 1, True)


def _mlstm_chunk(q, k, v1t, li_row, b_row, r_col, c, m, reverse):
    T = q.shape[0]
    s_idx = lax.broadcasted_iota(jnp.int32, (T, T), 0)
    t_idx = lax.broadcasted_iota(jnp.int32, (T, T), 1)
    mask = (s_idx >= t_idx) if reverse else (s_idx <= t_idx)
    nt = (((1,), (1,)), ((), ()))
    kq = lax.dot_general(k, q, nt, preferred_element_type=F32)
    d = jnp.where(mask, r_col + b_row, -1e30)
    m_loc = jnp.max(d, axis=0, keepdims=True)
    st = (kq * jnp.exp(d - m_loc)).astype(BF16)
    intra = jnp.dot(v1t, st, preferred_element_type=F32)
    g = b_row + m
    m_t = jnp.maximum(g, m_loc)
    inter = lax.dot_general(c.astype(BF16), q, nt, preferred_element_type=F32)
    num = jnp.exp(g - m_t) * inter + jnp.exp(m_loc - m_t) * intra
    den = num[HEAD_PAD - 1:HEAD_PAD, :]
    h = num * (1.0 / jnp.maximum(jnp.abs(den), jnp.exp(-m_t)))
    btot = b_row[:, 0:1] if reverse else b_row[:, T - 1:T]
    a = btot - b_row + li_row
    a_max = jnp.max(a, axis=1, keepdims=True)
    wv = (v1t.astype(F32) * jnp.exp(a - a_max)).astype(BF16)
    delta = jnp.dot(wv, k, preferred_element_type=F32)
    m_new = jnp.maximum(btot + m, a_max)
    c_new = jnp.exp(btot + m - m_new) * c + jnp.exp(a_max - m_new) * delta
    return h, c_new, m_new


def _mlstm_kernel(qc_ref, kc_ref, vc_ref, oc_ref, gcc_ref, grc_ref,
                  ql_ref, kl_ref, vl_ref, ol_ref, gcl_ref, grl_ref, ng_ref,
                  outc_ref, outl_ref, hf_ref, hb_ref, *, dh):
    T = ML_CHUNK
    L = ql_ref.shape[1]
    row = lax.broadcasted_iota(jnp.int32, (HEAD_PAD, T), 0)
    ones_row = row == HEAD_PAD - 1
    valid = row < dh
    ng = ng_ref[...]

    def gates(gc, gr, bwd):
        i = 2 if bwd else 0
        return gr[i:i + 1, :], gr[i + 1:i + 2, :], gc[:, i:i + 1] - gc[:, i + 1:i + 2]

    def finish(h, o):
        h = jnp.where(valid, h, 0.0)
        mu = jnp.sum(h, axis=0, keepdims=True) * (1.0 / dh)
        dlt = jnp.where(valid, h - mu, 0.0)
        var = jnp.sum(dlt * dlt, axis=0, keepdims=True) * (1.0 / dh)
        return (dlt * lax.rsqrt(var + LN_EPS) * ng * o.astype(F32)).astype(BF16)

    def aug(vt):
        return jnp.where(ones_row, jnp.ones_like(vt), vt)

    zero_c = jnp.zeros((HEAD_PAD, HEAD_PAD), F32)
    zero_m = jnp.zeros((1, 1), F32)

    q, k, v1t = qc_ref[0], kc_ref[0], aug(vc_ref[0])
    gc, gr = gcc_ref[0, 0], grc_ref[0, 0]
    h_f, c_f, m_f = _mlstm_chunk(q, k, v1t, *gates(gc, gr, False), zero_c, zero_m, False)
    h_b, c_b, m_b = _mlstm_chunk(q, k, v1t, *gates(gc, gr, True), zero_c, zero_m, True)
    outc_ref[0] = finish(h_f + h_b, oc_ref[0])

    nc = L // T
    for ci in range(nc):
        for bwd in (False, True):
            lo = (nc - 1 - ci) * T if bwd else ci * T
            gts = gates(gcl_ref[0, 0, lo:lo + T, :], grl_ref[0, 0, :, lo:lo + T], bwd)
            q, k, v1t = ql_ref[0, lo:lo + T, :], kl_ref[0, lo:lo + T, :], aug(vl_ref[0, :, lo:lo + T])
            if bwd:
                h_b, c_b, m_b = _mlstm_chunk(q, k, v1t, *gts, c_b, m_b, True)
                hb_ref[:, lo:lo + T] = h_b
            else:
                h_f, c_f, m_f = _mlstm_chunk(q, k, v1t, *gts, c_f, m_f, False)
                hf_ref[:, lo:lo + T] = h_f
    for ci in range(nc):
        lo = ci * T
        outl_ref[0, :, lo:lo + T] = finish(hf_ref[:, lo:lo + T] + hb_ref[:, lo:lo + T], ol_ref[0, :, lo:lo + T])


def mlstm_branch(qk_c, vt_c, ot_c, gate_c, qk_l, vt_l, ot_l, gate_l, norm_g_col, dh):
    bv, lc, _ = qk_c.shape
    L = qk_l.shape[1]
    H = ML_HEADS
    assert lc == ML_CHUNK and L % ML_CHUNK == 0

    def per_head(g):
        g = g[:, :, :4 * H].reshape(bv, -1, 4, H)
        return g.transpose(0, 3, 1, 2), g.transpose(0, 3, 2, 1)

    gcc, grc = per_head(gate_c)
    gcl, grl = per_head(gate_l)

    def qk_blk(length, part):
        return pl.BlockSpec((1, length, HEAD_PAD), lambda b, h: (b, 0, part * H + h))

    def t_blk(length):
        return pl.BlockSpec((1, HEAD_PAD, length), lambda b, h: (b, h, 0))

    def gspecs(length):
        return [pl.BlockSpec((1, 1, length, 4), lambda b, h: (b, h, 0, 0)),
                pl.BlockSpec((1, 1, 4, length), lambda b, h: (b, h, 0, 0))]

    def specs(length):
        return [qk_blk(length, 0), qk_blk(length, 1), t_blk(length), t_blk(length)] + gspecs(length)

    return pl.pallas_call(
        functools.partial(_mlstm_kernel, dh=dh),
        out_shape=(jax.ShapeDtypeStruct((bv, H * HEAD_PAD, lc), BF16),
                   jax.ShapeDtypeStruct((bv, H * HEAD_PAD, L), BF16)),
        grid=(bv, H),
        in_specs=specs(lc) + specs(L) + [pl.BlockSpec((HEAD_PAD, 1), lambda b, h: (h, 0))],
        out_specs=(t_blk(lc), t_blk(L)),
        scratch_shapes=[pltpu.VMEM((HEAD_PAD, L), F32), pltpu.VMEM((HEAD_PAD, L), F32)],
        compiler_params=_cparams(("parallel", "parallel")),
        name="mlstm_scan",
    )(qk_c, qk_c, vt_c, ot_c, gcc, grc, qk_l, qk_l, vt_l, ot_l, gcl, grl, norm_g_col)


def _s5_kernel(uc_ref, ul_ref, prow_ref, pcol_ref, bt_ref, cr_ref, cc_ref, y_ref,
               toep_ref, min_ref, mout_ref, kv_ref, s_ref, x_ref, u_ref, yv_ref, *, nb, nctx, nchunk):
    T, CG, P = S5_CHUNK, S5_GROUP, S5_STATE

    ch0 = 0
    for src in (uc_ref, ul_ref):
        xs = pltpu.einshape("bct->cbt", src[...].astype(F32))
        for ch in range(xs.shape[2] // T):
            for ci in range(CG):
                u_ref[(ch0 + ch) * nb:(ch0 + ch + 1) * nb, ci * T:(ci + 1) * T] = (
                    xs[ci, :, ch * T:(ch + 1) * T].astype(BF16))
        ch0 += xs.shape[2] // T

    hp = lax.Precision.HIGHEST
    lane = lax.broadcasted_iota(jnp.int32, (1, 2 * P), 1)
    re_lane = lane < P
    re_row = lax.broadcasted_iota(jnp.int32, (2 * P, 1), 0) < P

    def powers_rows(n, mag1, th1):
        m = jnp.exp(n * mag1)
        return m * jnp.cos(n * th1), m * jnp.sin(n * th1)

    def powers_cols(n, mag1, th1):
        m = jnp.exp(n * mag1)
        return m * jnp.cos(n * th1), m * jnp.sin(n * th1)

    at = []
    kv = jnp.zeros((CG * CG, 2 * T), F32)
    for d in range(2):
        a_re, a_im, ldt = prow_ref[0, d, 0:1, :], prow_ref[0, d, 1:2, :], prow_ref[0, d, 2:3, :]
        dt = jnp.exp(ldt)
        mag1, th1 = dt * a_re, dt * a_im
        ab_re, ab_im = jnp.exp(mag1) * jnp.cos(th1), jnp.exp(mag1) * jnp.sin(th1)
        den = a_re * a_re + a_im * a_im
        co_re = ((ab_re - 1.0) * a_re + ab_im * a_im) / den
        co_im = (ab_im * a_re - (ab_re - 1.0) * a_im) / den
        b_re, b_im = bt_ref[0, d, 0], bt_ref[0, d, 1]
        bb_re = co_re * b_re - co_im * b_im
        bb_im = co_re * b_im + co_im * b_re
        c_re, c_im = cr_ref[0, d, 0], cr_ref[0, d, 1]
        a_re_c, a_im_c = pcol_ref[0, d, :, 0:1], pcol_ref[0, d, :, 1:2]
        dt_c = jnp.exp(pcol_ref[0, d, :, 2:3])
        mag1_c, th1_c = dt_c * a_re_c, dt_c * a_im_c

        rep = lambda a: jnp.broadcast_to(a[:, None, :], (CG, CG, 2 * P)).reshape(CG * CG, 2 * P)
        til = lambda a: jnp.broadcast_to(a[None, :, :], (CG, CG, 2 * P)).reshape(CG * CG, 2 * P)
        w_re = rep(bb_re) * til(c_re) - rep(bb_im) * til(c_im)
        w_im = rep(bb_re) * til(c_im) + rep(bb_im) * til(c_re)
        w = jnp.where(re_lane, w_re, -w_im)
        lagp = lax.broadcasted_iota(jnp.int32, (1, 2 * T), 1)
        n = (T - 1 - lagp) if d else (lagp - (T - 1))
        ok = (n >= 0) & (lagp < 2 * T - 1)
        pc, ps = powers_cols(jnp.maximum(n, 0).astype(F32), mag1_c, th1_c)
        pw = jnp.where(ok, jnp.where(re_row, pc, ps), 0.0)
        kv = kv + jnp.dot(w, pw, precision=hp, preferred_element_type=F32)

        s_col = lax.broadcasted_iota(jnp.int32, (T, 1), 0)
        pc, ps = powers_rows((s_col if d else T - 1 - s_col).astype(F32), mag1, th1)
        for ci in range(CG):
            br, bi = bb_re[ci:ci + 1, :], bb_im[ci:ci + 1, :]
            blk = jnp.where(re_lane, br * pc - bi * ps, br * ps + bi * pc)
            min_ref[d, ci * T:(ci + 1) * T, :] = blk.astype(BF16)

        t_row = lax.broadcasted_iota(jnp.int32, (1, T), 1)
        pc, ps = powers_cols((T - t_row if d else t_row + 1).astype(F32), mag1_c, th1_c)
        for co in range(CG):
            cr, ci_ = cc_ref[0, d, 0, :, co:co + 1], cc_ref[0, d, 1, :, co:co + 1]
            blk = jnp.where(re_row, cr * pc - ci_ * ps, -(cr * ps + ci_ * pc))
            mout_ref[d, :, co * T:(co + 1) * T] = blk.astype(BF16)

        mt = jnp.exp(T * mag1)
        at.append(((mt * jnp.cos(T * th1))[:, :P], (mt * jnp.sin(T * th1))[:, :P]))

    kv_ref[...] = kv

    def build(ci, carry):
        r0 = pl.multiple_of(ci * T, T)
        for co in range(CG):
            vec = kv_ref[pl.ds(ci * CG + co, 1), :]
            rolled = pltpu.roll(jnp.broadcast_to(vec, (T, 2 * T)), T + 1, 1, stride=1, stride_axis=0)
            toep_ref[pl.ds(r0, T), co * T:(co + 1) * T] = rolled[:, :T].astype(BF16)
        return carry

    lax.fori_loop(0, CG, build, 0)

    u = u_ref[...]
    y = jnp.dot(u, toep_ref[...], preferred_element_type=F32)
    for d in range(2):
        s_ref[d] = jnp.dot(u, min_ref[d], preferred_element_type=F32)

    orders = (list(range(nchunk)), list(range(nctx - 1, -1, -1)) + list(range(nchunk - 1, nctx - 1, -1)))
    for d in range(2):
        at_re, at_im = at[d]
        x_re = jnp.zeros((nb, P), F32)
        x_im = jnp.zeros((nb, P), F32)
        for ch in orders[d]:
            x_ref[d, ch * nb:(ch + 1) * nb, :] = jnp.concatenate([x_re, x_im], axis=1)
            s = s_ref[d, ch * nb:(ch + 1) * nb, :]
            x_re, x_im = (at_re * x_re - at_im * x_im + s[:, :P], at_re * x_im + at_im * x_re + s[:, P:])
        y = y + jnp.dot(x_ref[d].astype(BF16), mout_ref[d], preferred_element_type=F32)

    for ch in range(nchunk):
        for co in range(CG):
            yv_ref[co, :, ch * T:(ch + 1) * T] = y[ch * nb:(ch + 1) * nb, co * T:(co + 1) * T]
    y_ref[...] = pltpu.einshape("cbt->bct", yv_ref[...]).astype(y_ref.dtype)


def s5_mix(ut_c, ut_l, a_re, a_im, log_dt, b_re, b_im, c_re, c_im):
    nb, w, lc = ut_c.shape
    L = ut_l.shape[2]
    T, CG, P = S5_CHUNK, S5_GROUP, S5_STATE
    G = w // CG
    nctx, nlat = lc // T, L // T
    nchunk = nctx + nlat

    dup = lambda a: jnp.concatenate([a, a], axis=-1)
    f32 = lambda a: a.astype(F32)
    ldt = jnp.broadcast_to(f32(log_dt)[:, :, None], (2, G, P))
    prow = jnp.stack([dup(f32(a_re)), dup(f32(a_im)), dup(ldt)], axis=2)
    prow = jnp.pad(prow, ((0, 0), (0, 0), (0, 5), (0, 0))).transpose(1, 0, 2, 3)
    pcol = prow.transpose(0, 1, 3, 2)
    bt = jnp.stack([dup(f32(b_re).transpose(0, 1, 3, 2)), dup(f32(b_im).transpose(0, 1, 3, 2))], axis=2)
    bt = bt.transpose(1, 0, 2, 3, 4)
    cr = jnp.stack([dup(f32(c_re)), dup(f32(c_im))], axis=2).transpose(1, 0, 2, 3, 4)
    cc = cr.transpose(0, 1, 2, 4, 3)

    blk = lambda a: pl.BlockSpec((1,) + a.shape[1:], lambda g: (g,) + (0,) * (a.ndim - 1))
    grp = lambda length: pl.BlockSpec((nb, CG, length), lambda g: (0, g, 0))
    r = nchunk * nb
    return pl.pallas_call(
        functools.partial(_s5_kernel, nb=nb, nctx=nctx, nchunk=nchunk),
        out_shape=jax.ShapeDtypeStruct((nb, w, lc + L), BF16),
        grid=(G,),
        in_specs=[grp(lc), grp(L), blk(prow), blk(pcol), blk(bt), blk(cr), blk(cc)],
        out_specs=grp(lc + L),
        scratch_shapes=[pltpu.VMEM((CG * T, CG * T), BF16), pltpu.VMEM((2, CG * T, 2 * P), BF16),
                        pltpu.VMEM((2, 2 * P, CG * T), BF16), pltpu.VMEM((CG * CG, 2 * T), F32),
                        pltpu.VMEM((2, r, 2 * P), F32), pltpu.VMEM((2, r, 2 * P), F32),
                        pltpu.VMEM((r, CG * T), BF16), pltpu.VMEM((CG, nb, lc + L), F32)],
        compiler_params=_cparams(("parallel",)),
        name="s5_mix",
    )(ut_c, ut_l, prow, pcol, bt, cr, cc)


def _gelu_tanh(x):
    return 0.5 * x * (1.0 + jnp.tanh(math.sqrt(2.0 / math.pi) * (x + 0.044715 * (x * x * x))))


def _merge_kernel(yhy_ref, yml_ref, ys_ref, u_ref, ghy_ref, gml_ref, gs5_ref,
                  d_ref, gw_ref, gb_ref, why_ref, wml_ref, ws5_ref, o_ref):
    tn = (((0,), (0,)), ((), ()))
    ys = ys_ref[0].astype(F32) + d_ref[...] * u_ref[0].astype(F32)
    z = _gelu_tanh(ys).astype(BF16)
    glu = jnp.dot(gw_ref[...], z, preferred_element_type=F32) + gb_ref[...]
    y_s5 = (z.astype(F32) * _sigmoid(glu)).astype(BF16)
    acc = ghy_ref[0].astype(F32) * jnp.dot(yhy_ref[0], why_ref[...], preferred_element_type=F32)
    acc = acc + gml_ref[0].astype(F32) * lax.dot_general(yml_ref[0], wml_ref[...], tn, preferred_element_type=F32)
    acc = acc + gs5_ref[0].astype(F32) * lax.dot_general(y_s5, ws5_ref[...], tn, preferred_element_type=F32)
    o_ref[0] = acc.astype(o_ref.dtype)


def merge_branches(y_hy, y_ml, ys_t, u_t, t_off, gate_arr, s5_d, glu_wt, glu_b, w_hy, w_ml, w_s5):
    bv, lv, w = y_hy.shape
    dm = w_hy.shape[1]
    tm = ML_CHUNK
    full = lambda a: pl.BlockSpec(a.shape, lambda b, i: (0,) * a.ndim)
    gate = lambda c: pl.BlockSpec((1, tm, dm), lambda b, i: (b, i, c))
    return pl.pallas_call(
        _merge_kernel,
        out_shape=jax.ShapeDtypeStruct((bv, lv, dm), BF16),
        grid=(bv, lv // tm),
        in_specs=[pl.BlockSpec((1, tm, w), lambda b, i: (b, i, 0)),
                  pl.BlockSpec((1, y_ml.shape[1], tm), lambda b, i: (b, 0, i)),
                  pl.BlockSpec((1, w, tm), lambda b, i: (b, 0, i + t_off)),
                  pl.BlockSpec((1, w, tm), lambda b, i: (b, 0, i)),
                  gate(0), gate(1), gate(2),
                  full(s5_d), full(glu_wt), full(glu_b), full(w_hy), full(w_ml), full(w_s5)],
        out_specs=pl.BlockSpec((1, tm, dm), lambda b, i: (b, i, 0)),
        compiler_params=_cparams(("parallel", "parallel")),
        name="merge_branches",
    )(y_hy, y_ml, ys_t, u_t, gate_arr, gate_arr, gate_arr, s5_d, glu_wt, glu_b, w_hy, w_ml, w_s5)


def _outproj_kernel(m_ref, w_ref, x_ref, gate_ref, g_ref, b_ref, o_ref, *, alpha):
    rb = min(m_ref.shape[1], LANES)
    for s in range(m_ref.shape[1] // rb):
        rows = slice(s * rb, (s + 1) * rb)
        y = jnp.dot(m_ref[0, rows, :], w_ref[...], preferred_element_type=F32)
        r = alpha * x_ref[0, rows, :] + gate_ref[0] * y
        o_ref[0, rows, :] = _layer_norm(r, g_ref[...], b_ref[...])


def out_projection(merged, w_out, x, gate, ln_g, ln_b, alpha, tm):
    bv, lv, d = x.shape
    bm = gate.shape[0]
    mod_map = (lambda b, i: (b, 0, 0)) if bm == bv else (lambda b, i: (0, 0, 0))
    vec = lambda: pl.BlockSpec((1, d), lambda b, i: (0, 0))
    return pl.pallas_call(
        functools.partial(_outproj_kernel, alpha=alpha),
        out_shape=jax.ShapeDtypeStruct((bv, lv, d), F32),
        grid=(bv, lv // tm),
        in_specs=[pl.BlockSpec((1, tm, d), lambda b, i: (b, i, 0)),
                  pl.BlockSpec((d, d), lambda b, i: (0, 0)),
                  pl.BlockSpec((1, tm, d), lambda b, i: (b, i, 0)),
                  pl.BlockSpec((1, 1, d), mod_map), vec(), vec()],
        out_specs=pl.BlockSpec((1, tm, d), lambda b, i: (b, i, 0)),
        compiler_params=_cparams(("parallel", "parallel")),
        name="out_projection_ln",
    )(merged, w_out, x, gate, ln_g, ln_b)


def _mlp_kernel(x_ref, sh_ref, sc_ref, gate_ref, w1_ref, w2_ref, g_ref, b_ref, o_ref, xm_ref, acc_ref, *, alpha):
    k = pl.program_id(2)

    @pl.when(k == 0)
    def _():
        xm_ref[...] = (x_ref[0] * (1.0 + sc_ref[0]) + sh_ref[0]).astype(BF16)
        acc_ref[...] = jnp.zeros_like(acc_ref)

    h = jnp.maximum(jnp.dot(xm_ref[...], w1_ref[...], preferred_element_type=F32), 0.0)
    acc_ref[...] += jnp.dot((h * h).astype(BF16), w2_ref[...], preferred_element_type=F32)

    @pl.when(k == pl.num_programs(2) - 1)
    def _():
        r = alpha * x_ref[0] + gate_ref[0] * acc_ref[...]
        o_ref[0] = _layer_norm(r, g_ref[...], b_ref[...])


def mlp_block(x, shift, scale, gate, w1, w2, ln_g, ln_b, alpha, tm, kf):
    bv, lv, d = x.shape
    dff = w1.shape[1]
    bm = gate.shape[0]
    mod_map = (lambda b, i, k: (b, 0, 0)) if bm == bv else (lambda b, i, k: (0, 0, 0))
    vec = lambda: pl.BlockSpec((1, d), lambda b, i, k: (0, 0))
    return pl.pallas_call(
        functools.partial(_mlp_kernel, alpha=alpha),
        out_shape=jax.ShapeDtypeStruct((bv, lv, d), F32),
        grid=(bv, lv // tm, dff // kf),
        in_specs=[pl.BlockSpec((1, tm, d), lambda b, i, k: (b, i, 0)),
                  pl.BlockSpec((1, 1, d), mod_map), pl.BlockSpec((1, 1, d), mod_map),
                  pl.BlockSpec((1, 1, d), mod_map),
                  pl.BlockSpec((d, kf), lambda b, i, k: (0, k)),
                  pl.BlockSpec((kf, d), lambda b, i, k: (k, 0)),
                  vec(), vec()],
        out_specs=pl.BlockSpec((1, tm, d), lambda b, i, k: (b, i, 0)),
        scratch_shapes=[pltpu.VMEM((tm, d), BF16), pltpu.VMEM((tm, d), F32)],
        compiler_params=_cparams(("parallel", "parallel", "arbitrary")),
        name="mlp_ln",
    )(x, shift, scale, gate, w1, w2, ln_g, ln_b)


def _pad_heads(a, axis, dh):
    shp = a.shape
    a = a.reshape(shp[:axis] + (ML_HEADS, dh) + shp[axis + 1:])
    pad = [(0, 0)] * a.ndim
    pad[axis + 1] = (0, HEAD_PAD - dh)
    a = jnp.pad(a, pad)
    return a.reshape(shp[:axis] + (ML_HEADS * HEAD_PAD,) + shp[axis + 1:])


def _layer_params(l, p, w_hy, w_ml, w_s5, d_model):
    dh = w_ml // ML_HEADS
    sizes = ((HY_ORDER + 1) * w_hy, 2 * w_ml, w_ml, w_ml, 4 * ML_HEADS, w_s5, N_BRANCH * d_model)
    pts = [0]
    for s in sizes:
        pts.append(pts[-1] + s)
    w_in = p["w_in"][l]
    hy, qk, v, o, gt, u, mg = (w_in[:, pts[i]:pts[i + 1]] for i in range(7))
    ph = lambda a: _pad_heads(a, a.ndim - 1, dh)
    none = lambda n: (jnp.zeros((3, n), F32), jnp.zeros((1, n), F32), jnp.ones((1, n), F32))
    half = ML_HEADS * HEAD_PAD
    cw, cb = p["ml_conv_w"][l], p["ml_conv_b"][l]
    out = {}
    out["mg"] = (mg.astype(BF16),) + none(mg.shape[1])
    out["hy"] = (hy.astype(BF16), p["hy_conv_w"][l], p["hy_conv_b"][l].reshape(1, -1), jnp.ones((1, hy.shape[1]), F32))
    out["qk"] = (jnp.concatenate([ph(qk[:, :w_ml]), ph(qk[:, w_ml:])], axis=1).astype(BF16),
                 jnp.concatenate([ph(cw[:, :w_ml]), ph(cw[:, w_ml:])], axis=1),
                 jnp.concatenate([ph(cb[:w_ml]), ph(cb[w_ml:])]).reshape(1, -1),
                 jnp.concatenate([jnp.ones((half,), F32), jnp.full((half,), dh ** -0.5, F32)]).reshape(1, -1))
    out["vt"] = (ph(v).T.astype(BF16),) + none(half)
    out["ot"] = (ph(o).T.astype(BF16),) + none(half)
    ngt = gt.shape[1]
    out["ut"] = (u.T.astype(BF16),) + none(u.shape[1])
    out["gt"] = (jnp.pad(gt, ((0, 0), (0, LANES - ngt))).astype(BF16),
                 jnp.pad(p["ml_gate_b"][l].reshape(-1), ((0, LANES - ngt),)).reshape(1, -1))
    out["norm_g"] = ph(p["ml_norm_g"][l]).reshape(-1, 1)
    out["w_ml_out"] = _pad_heads(p["w_ml_out"][l], 0, dh).astype(BF16)
    w3 = p["hy_ffn_w3"][l]
    dec = p["hy_decay"][l]
    dec_cols = jnp.broadcast_to(dec[:, None, :], (HY_ORDER, 2, w_hy)).reshape(1, -1)
    hpad = LANES - HY_HIDDEN
    row = lambda a: jnp.pad(a.reshape(1, -1), ((0, 0), (0, hpad)))
    out["hy_ffn"] = (jnp.pad(p["hy_ffn_w1"][l], ((0, LANES - HY_FEAT), (0, hpad))), row(p["hy_ffn_b1"][l]),
                     jnp.pad(p["hy_ffn_w2"][l], ((0, hpad), (0, hpad))), row(p["hy_ffn_b2"][l]),
                     jnp.pad(w3, ((0, hpad), (0, 0))), row(p["hy_sin_freq"][l]), dec_cols)
    return out


def kernel(x, c, ctx, c_ctx, w_mod, b_mod, w_in, hy_conv_w, hy_conv_b, hy_ffn_w1, hy_ffn_b1, hy_ffn_w2,
           hy_ffn_b2, hy_ffn_w3, hy_sin_freq, hy_decay, hy_bias, ml_conv_w, ml_conv_b, ml_gate_b, ml_norm_g,
           s5_a_re, s5_a_im, s5_log_dt, s5_b_re, s5_b_im, s5_c_re, s5_c_im, s5_d, s5_glu_w, s5_glu_b,
           w_hy_out, w_ml_out, w_s5_out, w_out, ln1_g, ln1_b, ln2_g, ln2_b, w_ff1, w_ff2):
    p = dict(w_in=w_in, hy_conv_w=hy_conv_w, hy_conv_b=hy_conv_b, hy_ffn_w1=hy_ffn_w1, hy_ffn_b1=hy_ffn_b1,
             hy_ffn_w2=hy_ffn_w2, hy_ffn_b2=hy_ffn_b2, hy_ffn_w3=hy_ffn_w3, hy_sin_freq=hy_sin_freq,
             hy_decay=hy_decay, ml_conv_w=ml_conv_w, ml_conv_b=ml_conv_b, ml_gate_b=ml_gate_b,
             ml_norm_g=ml_norm_g, w_ml_out=w_ml_out)
    B, L, D = x.shape
    LC = ctx.shape[1]
    depth = w_in.shape[0]
    w_hy = hy_bias.shape[-1]
    w_ml = ml_norm_g.shape[-1]
    w_s5 = s5_d.shape[-1]
    dh = w_ml // ML_HEADS
    alpha = (2 * depth) ** 0.25
    half = ML_HEADS * HEAD_PAD
    tm_l, tm_c = 1024, LC

    tab_l, tab_c = dft_tables(L), dft_tables(LC)
    feats_l, feats_c = hyena_features(L), hyena_features(LC)

    mrows = 8 * ((B + 1 + 7) // 8)
    c_rows = jnp.zeros((mrows, D), F32).at[:B].set(c).at[B].set(c_ctx)

    for l in range(depth):
        need_ctx = l < depth - 1
        lp = _layer_params(l, p, w_hy, w_ml, w_s5, D)
        mod = mod_vectors(c_rows, w_mod, b_mod.reshape(depth, 1, -1), l)
        mod_l = [mod[:B, k * D:(k + 1) * D].reshape(B, 1, D) for k in range(6)]
        mod_c = [mod[B:B + 1, k * D:(k + 1) * D].reshape(1, 1, D) for k in range(6)]

        def mixer_inputs(xv, md, row_len, tm, full):
            proj = functools.partial(projection, xv, md[0], md[1], tm=tm, row_len=row_len)
            r = {}
            r["qk"] = proj(*lp["qk"], kind="convsilu", tn=half)
            r["vt"] = proj(*lp["vt"], kind="plain", tn=half, transposed=True)
            r["ot"] = proj(*lp["ot"], kind="sigmoid", tn=half, transposed=True)
            r["ut"] = proj(*lp["ut"], kind="plain", tn=w_s5, transposed=True)
            r["g"] = gate_projection(xv, md[0], md[1], *lp["gt"], tm=tm)
            if full:
                r["mg"] = proj(*lp["mg"], kind="sigmoid", tn=2 * w_hy)
                r["z"] = projection_even_odd(xv, md[0], md[1], *lp["hy"][:3], tn=w_hy, tm=tm, row_len=row_len)
            return r

        r_l = mixer_inputs(x, mod_l, GRID_W, tm_l, True)
        r_c = mixer_inputs(ctx, mod_c, LC, tm_c, need_ctx)

        hm_c, hm_l = mlstm_branch(r_c["qk"], r_c["vt"], r_c["ot"], r_c["g"],
                                  r_l["qk"], r_l["vt"], r_l["ot"], r_l["g"], lp["norm_g"], dh)

        ys_t = s5_mix(r_c["ut"], r_l["ut"], s5_a_re[l], s5_a_im[l], s5_log_dt[l],
                      s5_b_re[l], s5_b_im[l], s5_c_re[l], s5_c_im[l])

        s5_dv = s5_d[l].reshape(-1, 1)
        glu_wt = s5_glu_w[l].T.astype(BF16)
        glu_b = s5_glu_b[l].reshape(-1, 1)
        w_hy_o = w_hy_out[l].astype(BF16)
        w_s5_o = w_s5_out[l].astype(BF16)
        w_o = w_out[l].astype(BF16)
        w1 = w_ff1[l].astype(BF16)
        w2 = w_ff2[l].astype(BF16)
        g1, b1 = ln1_g[l].reshape(1, -1), ln1_b[l].reshape(1, -1)
        g2, b2 = ln2_g[l].reshape(1, -1), ln2_b[l].reshape(1, -1)

        def finish(xv, r, hm, t_off, tables, feats, md, tm):
            y_hy = hyena_branch(*r["z"], tables, feats, lp["hy_ffn"], hy_bias[l])
            merged = merge_branches(y_hy, hm, ys_t, r["ut"], t_off, r["mg"], s5_dv, glu_wt, glu_b,
                                    w_hy_o, lp["w_ml_out"], w_s5_o)
            if md[2].shape[0] == 1:
                merged, xv = merged.reshape(1, -1, D), xv.reshape(1, -1, D)
            x1 = out_projection(merged, w_o, xv, md[2], g1, b1, alpha, tm)
            return mlp_block(x1, md[3], md[4], md[5], w1, w2, g2, b2, alpha, tm, 1024)

        x = finish(x, r_l, hm_l, LC // ML_CHUNK, tab_l, feats_l, mod_l, 512)
        if need_ctx:
            ctx = finish(ctx, r_c, hm_c, 0, tab_c, feats_c, mod_c, 512).reshape(B, LC, D)
    return x
```

```python
import functools
import math

import jax
import jax.numpy as jnp
from jax import lax
from jax.experimental import pallas as pl
from jax.experimental.pallas import tpu as pltpu

F32 = jnp.float32
BF16 = jnp.bfloat16

GRID_W = 64
HY_ORDER = 2
HY_BANDS = 16
HY_FEAT = 1 + 2 * HY_BANDS
HY_HIDDEN = 64
ML_HEADS = 4
ML_CHUNK = 256
S5_GROUP = 16
S5_STATE = 64
N_BRANCH = 3
LN_EPS = 1e-5

LANES = 128
BF16_TILE_ROWS = 16
MXU_DIM = 256
VMEM_LIMIT = 56 * 1024 * 1024

PROJ_ROWS = 1024
TOKEN_ROWS = 512
MLP_FF_CHUNK = 1024
MOD_COLS = 1024
DFT_ROWS = 512

HEAD_PAD = MXU_DIM
S5_CHUNK = LANES


def _cparams(sem):
    return pltpu.CompilerParams(dimension_semantics=sem, vmem_limit_bytes=VMEM_LIMIT)


def _sigmoid(x):
    return 0.5 * jnp.tanh(0.5 * x) + 0.5


def _layer_norm(r, g, b):
    mu = jnp.mean(r, axis=-1, keepdims=True)
    d = r - mu
    var = jnp.mean(d * d, axis=-1, keepdims=True)
    return d * lax.rsqrt(var + LN_EPS) * g + b


def _mod_kernel(c_ref, w_ref, b_ref, o_ref):
    c = c_ref[...]
    a = (c * _sigmoid(c)).astype(BF16)
    o_ref[...] = jnp.dot(a, w_ref[0].astype(BF16), preferred_element_type=F32) + b_ref[0]


def mod_vectors(c_rows, w, b, layer):
    m, d = c_rows.shape
    n = w.shape[2]
    tn = MOD_COLS
    return pl.pallas_call(
        _mod_kernel,
        out_shape=jax.ShapeDtypeStruct((m, n), F32),
        grid=(n // tn,),
        in_specs=[pl.BlockSpec((m, d), lambda j: (0, 0)),
                  pl.BlockSpec((1, d, tn), lambda j: (layer, 0, j)),
                  pl.BlockSpec((1, 1, tn), lambda j: (layer, 0, j))],
        out_specs=pl.BlockSpec((m, tn), lambda j: (0, j)),
        compiler_params=_cparams(("parallel",)),
        name="mod_vectors",
    )(c_rows, w, b)


def _short_conv(acc, w, b, row_len):
    rows = acc.shape[0]
    t = lax.broadcasted_iota(jnp.int32, acc.shape, 0) & (row_len - 1)
    prev = jnp.where(t == 0, 0.0, pltpu.roll(acc, 1, 0))
    nxt = jnp.where(t == row_len - 1, 0.0, pltpu.roll(acc, rows - 1, 0))
    return prev * w[0:1] + acc * w[1:2] + nxt * w[2:3] + b


def _modulate_rows(xm_ref, x_ref, sh_ref, sc_ref, rows):
    xm_ref[rows, :] = (x_ref[0, rows, :] * (1.0 + sc_ref[0]) + sh_ref[0]).astype(BF16)


def _modulate_into(xm_ref, x_ref, sh_ref, sc_ref):
    _modulate_rows(xm_ref, x_ref, sh_ref, sc_ref, slice(None))


def _proj_kernel(x_ref, sh_ref, sc_ref, w_ref, cw_ref, cb_ref, cs_ref, o_ref, xm_ref, *, kind, row_len, rb,
                 transposed, single_tile):
    def epilogue(acc):
        if kind == "conv":
            return _short_conv(acc, cw_ref[...], cb_ref[...], row_len)
        if kind == "convsilu":
            y = _short_conv(acc, cw_ref[...], cb_ref[...], row_len)
            return y * _sigmoid(y) * cs_ref[...]
        if kind == "sigmoid":
            return _sigmoid(acc)
        return acc

    def body(modulate):
        for r in range(xm_ref.shape[0] // rb):
            rows = slice(r * rb, (r + 1) * rb)
            if modulate:
                _modulate_rows(xm_ref, x_ref, sh_ref, sc_ref, rows)
            xs = xm_ref[rows, :]
            if transposed:
                acc = lax.dot_general(w_ref[...], xs, (((1,), (1,)), ((), ())), preferred_element_type=F32)
                o_ref[0, :, rows] = epilogue(acc).astype(o_ref.dtype)
            else:
                acc = jnp.dot(xs, w_ref[...], preferred_element_type=F32)
                o_ref[0, rows, :] = epilogue(acc).astype(o_ref.dtype)

    if single_tile:
        body(True)
    else:
        first = pl.program_id(2) == 0
        pl.when(first)(lambda: body(True))
        pl.when(jnp.logical_not(first))(lambda: body(False))


def projection(x, shift, scale, w, cw, cb, cs, *, kind, tn, tm, row_len, transposed=False):
    bv, lv, d = x.shape
    n = w.shape[0] if transposed else w.shape[1]
    rb = min(tm, ML_CHUNK)
    assert n % tn == 0 and lv % tm == 0 and rb % row_len == 0
    bm = shift.shape[0]
    mod_map = (lambda b, i, j: (b, 0, 0)) if bm == bv else (lambda b, i, j: (0, 0, 0))
    if transposed:
        assert kind in ("plain", "sigmoid")
        w_spec = pl.BlockSpec((tn, d), lambda b, i, j: (j, 0))
        out_shape = jax.ShapeDtypeStruct((bv, n, lv), BF16)
        out_spec = pl.BlockSpec((1, tn, tm), lambda b, i, j: (b, j, i))
    else:
        w_spec = pl.BlockSpec((d, tn), lambda b, i, j: (0, j))
        out_shape = jax.ShapeDtypeStruct((bv, lv, n), BF16)
        out_spec = pl.BlockSpec((1, tm, tn), lambda b, i, j: (b, i, j))
    col = lambda rows: pl.BlockSpec((rows, tn), lambda b, i, j: (0, j))
    return pl.pallas_call(
        functools.partial(_proj_kernel, kind=kind, row_len=row_len, rb=rb, transposed=transposed,
                          single_tile=(n == tn)),
        out_shape=out_shape,
        grid=(bv, lv // tm, n // tn),
        in_specs=[pl.BlockSpec((1, tm, d), lambda b, i, j: (b, i, 0)),
                  pl.BlockSpec((1, 1, d), mod_map), pl.BlockSpec((1, 1, d), mod_map),
                  w_spec, col(3), col(1), col(1)],
        out_specs=out_spec,
        scratch_shapes=[pltpu.VMEM((tm, d), BF16)],
        compiler_params=_cparams(("parallel", "parallel", "arbitrary")),
        name="projection_" + kind + ("_t" if transposed else ""),
    )(x, shift, scale, w, cw, cb, cs)


def _proj_eo_kernel(x_ref, sh_ref, sc_ref, w_ref, cw_ref, cb_ref, oe_ref, oo_ref, xm_ref, tmp_ref, *, row_len, rb):
    nlc = w_ref.shape[1] // LANES

    def body(modulate):
        for r in range(xm_ref.shape[0] // rb):
            rows = slice(r * rb, (r + 1) * rb)
            if modulate:
                _modulate_rows(xm_ref, x_ref, sh_ref, sc_ref, rows)
            acc = jnp.dot(xm_ref[rows, :], w_ref[...], preferred_element_type=F32)
            val = _short_conv(acc, cw_ref[...], cb_ref[...], row_len)
            half = slice(r * rb // 2, (r + 1) * rb // 2)
            for c in range(nlc):
                lanes = slice(c * LANES, (c + 1) * LANES)
                tmp_ref[r * nlc + c] = val[:, lanes]
                oe_ref[0, half, lanes] = tmp_ref[r * nlc + c, pl.ds(0, rb // 2, stride=2), :].astype(oe_ref.dtype)
                oo_ref[0, half, lanes] = tmp_ref[r * nlc + c, pl.ds(1, rb // 2, stride=2), :].astype(oo_ref.dtype)

    first = pl.program_id(2) == 0
    pl.when(first)(lambda: body(True))
    pl.when(jnp.logical_not(first))(lambda: body(False))


def projection_even_odd(x, shift, scale, w, cw, cb, *, tn, tm, row_len):
    bv, lv, d = x.shape
    n = w.shape[1]
    rb = min(tm, ML_CHUNK)
    assert n % tn == 0 and n > tn and lv % tm == 0 and rb % row_len == 0
    bm = shift.shape[0]
    mod_map = (lambda b, i, j: (b, 0, 0)) if bm == bv else (lambda b, i, j: (0, 0, 0))
    col = lambda rows: pl.BlockSpec((rows, tn), lambda b, i, j: (0, j))
    out = jax.ShapeDtypeStruct((bv, lv // 2, n), BF16)
    half = lambda: pl.BlockSpec((1, tm // 2, tn), lambda b, i, j: (b, i, j))
    return pl.pallas_call(
        functools.partial(_proj_eo_kernel, row_len=row_len, rb=rb),
        out_shape=(out, out),
        grid=(bv, lv // tm, n // tn),
        in_specs=[pl.BlockSpec((1, tm, d), lambda b, i, j: (b, i, 0)),
                  pl.BlockSpec((1, 1, d), mod_map), pl.BlockSpec((1, 1, d), mod_map),
                  pl.BlockSpec((d, tn), lambda b, i, j: (0, j)), col(3), col(1)],
        out_specs=(half(), half()),
        scratch_shapes=[pltpu.VMEM((tm, d), BF16), pltpu.VMEM((tm // rb * (tn // LANES), rb, LANES), F32)],
        compiler_params=_cparams(("parallel", "parallel", "arbitrary")),
        name="projection_conv_even_odd",
    )(x, shift, scale, w, cw, cb)


def _log_sigmoid(x):
    return jnp.minimum(x, 0.0) - jnp.log(1.0 + jnp.exp(-jnp.abs(x)))


def _gate_kernel(x_ref, sh_ref, sc_ref, wg_ref, gb_ref, g_ref, xm_ref):
    _modulate_into(xm_ref, x_ref, sh_ref, sc_ref)
    T = ML_CHUNK
    hp = lax.Precision.HIGHEST
    r = lax.broadcasted_iota(jnp.int32, (T, T), 0)
    s = lax.broadcasted_iota(jnp.int32, (T, T), 1)
    tri_f = (s <= r).astype(F32)
    tri_b = (s >= r).astype(F32)
    kind = lax.broadcasted_iota(jnp.int32, (T, LANES), 1) // ML_HEADS
    for c in range(xm_ref.shape[0] // T):
        g = jnp.dot(xm_ref[c * T:(c + 1) * T, :], wg_ref[...], preferred_element_type=F32) + gb_ref[...]
        ls = _log_sigmoid(g)
        cum_f = jnp.dot(tri_f, ls, precision=hp, preferred_element_type=F32)
        cum_b = jnp.dot(tri_b, ls, precision=hp, preferred_element_type=F32)
        g_ref[0, c * T:(c + 1) * T, :] = jnp.where(kind == 1, cum_f, jnp.where(kind == 3, cum_b, g))


def gate_projection(x, shift, scale, wg, gb, *, tm):
    bv, lv, d = x.shape
    assert lv % tm == 0 and tm % ML_CHUNK == 0
    bm = shift.shape[0]
    mod_map = (lambda b, i: (b, 0, 0)) if bm == bv else (lambda b, i: (0, 0, 0))
    full = lambda a: pl.BlockSpec(a.shape, lambda b, i: (0, 0))
    return pl.pallas_call(
        _gate_kernel,
        out_shape=jax.ShapeDtypeStruct((bv, lv, LANES), F32),
        grid=(bv, lv // tm),
        in_specs=[pl.BlockSpec((1, tm, d), lambda b, i: (b, i, 0)),
                  pl.BlockSpec((1, 1, d), mod_map), pl.BlockSpec((1, 1, d), mod_map),
                  full(wg), full(gb)],
        out_specs=pl.BlockSpec((1, tm, LANES), lambda b, i: (b, i, 0)),
        scratch_shapes=[pltpu.VMEM((tm, d), BF16)],
        compiler_params=_cparams(("parallel", "parallel")),
        name="gate_projection",
    )(x, shift, scale, wg, gb)


def _hy_ffn_kernel(feat_ref, w1_ref, b1_ref, w2_ref, b2_ref, w3_ref, fr_ref, dec_ref, o_ref):
    hp = lax.Precision.HIGHEST
    feats = feat_ref[...]
    fr = fr_ref[...]
    h = jnp.sin(fr * (jnp.dot(feats, w1_ref[...], precision=hp, preferred_element_type=F32) + b1_ref[...]))
    h = jnp.sin(fr * (jnp.dot(h, w2_ref[...], precision=hp, preferred_element_type=F32) + b2_ref[...]))
    h = jnp.dot(h, w3_ref[...], precision=hp, preferred_element_type=F32)
    t = feats[:, 0:1]
    o_ref[...] = h * jnp.exp(-t * jnp.abs(dec_ref[...]))


def hyena_filter_taps(feats, w1p, b1, w2, b2, w3, freq, decay_cols):
    L = feats.shape[0]
    n = w3.shape[1]
    tl = min(L, 256)
    full = lambda a: pl.BlockSpec(a.shape, lambda i: (0, 0))
    return pl.pallas_call(
        _hy_ffn_kernel,
        out_shape=jax.ShapeDtypeStruct((L, n), F32),
        grid=(L // tl,),
        in_specs=[pl.BlockSpec((tl, feats.shape[1]), lambda i: (i, 0)),
                  full(w1p), full(b1), full(w2), full(b2), full(w3), full(freq), full(decay_cols)],
        out_specs=pl.BlockSpec((tl, n), lambda i: (i, 0)),
        compiler_params=_cparams(("parallel",)),
        name="hyena_filter_taps",
    )(feats, w1p, b1, w2, b2, w3, freq, decay_cols)


def _row0(shape, i):
    return (lax.broadcasted_iota(jnp.int32, shape, 0) + i * shape[0]) == 0


def _hy_spec_kernel(ce_ref, se_ref, co_ref, so_ref, hfe_ref, hbe_ref, hfo_ref, hbo_ref,
                    kpa_ref, kqa_ref, kpb_ref, kqb_ref, *, L):
    i = pl.program_id(1)
    hf_e, hf_o = hfe_ref[...], hfo_ref[...]
    hb_e, hb_o = hbe_ref[...], hbo_ref[...]
    m = lax.broadcasted_iota(jnp.int32, hb_e.shape, 0)
    hb_e = jnp.where(m == 0, 0.0, hb_e)
    s_e, s_o, d_e, d_o = hf_e + hb_e, hf_o + hb_o, hf_e - hb_e, hf_o - hb_o
    dot = lambda t, x: jnp.dot(t[...], x.astype(BF16), preferred_element_type=F32)
    pe, po, qe, qo = dot(ce_ref, s_e), dot(co_ref, s_o), dot(se_ref, d_e), dot(so_ref, d_o)
    r0 = _row0(pe.shape, i)
    p_half = jnp.sum(s_e * (1 - 2 * (m & 1)).astype(F32), axis=0, keepdims=True)
    scale = 1.0 / L
    kpa_ref[0] = (pe + po) * scale
    kpb_ref[0] = (pe - po) * scale
    kqa_ref[0] = jnp.where(r0, p_half, qe + qo) * scale
    kqb_ref[0] = jnp.where(r0, qo, qo - qe) * scale


def hyena_filter_spectrum(tables, taps, w):
    ce, se, co, so = tables[:4]
    H = ce.shape[0]
    tf = min(H, DFT_ROWS)
    out = jax.ShapeDtypeStruct((HY_ORDER, H, w), F32)
    tab = lambda: pl.BlockSpec((tf, H), lambda o, i: (i, 0))
    slot = lambda: pl.BlockSpec((1, tf, w), lambda o, i: (o, i, 0))
    tap = lambda parity, direction: pl.BlockSpec((H, w), lambda o, i: (parity, 2 * o + direction))
    return pl.pallas_call(
        functools.partial(_hy_spec_kernel, L=2 * H),
        out_shape=(out,) * 4,
        grid=(HY_ORDER, H // tf),
        in_specs=[tab(), tab(), tab(), tab(), tap(0, 0), tap(0, 1), tap(1, 0), tap(1, 1)],
        out_specs=(slot(),) * 4,
        compiler_params=_cparams(("parallel", "parallel")),
        name="hyena_filter_spectrum",
    )(ce, se, co, so, taps, taps, taps, taps)


def _hy_fwd_kernel(ce_ref, se_ref, co_ref, so_ref, ue_ref, uo_ref, kpa_ref, kqa_ref, kpb_ref, kqb_ref,
                   eep_ref, eeq_ref, eop_ref, eoq_ref):
    ue, uo = ue_ref[0], uo_ref[0]
    dot = lambda t, x: jnp.dot(t[...], x, preferred_element_type=F32)
    pe, qe, po, qo = dot(ce_ref, ue), dot(se_ref, ue), dot(co_ref, uo), dot(so_ref, uo)

    def emit(rows, pe, qe, po, qo, r0):
        sel = (lambda special, general: general) if r0 is None else (lambda special, general: jnp.where(r0, special, general))
        kpa, kqa, kpb, kqb = kpa_ref[0, rows, :], kqa_ref[0, rows, :], kpb_ref[0, rows, :], kqb_ref[0, rows, :]
        pa, pb = pe + po, pe - po
        qa, qb = sel(qe, qe + qo), sel(qo, qo - qe)
        ypa = sel(0.5 * pa * kpa, pa * kpa - qa * kqa)
        ypb = sel(0.5 * pb * kpb, pb * kpb - qb * kqb)
        yqa = sel(qa * kqa - qb * kqb, pa * kqa + qa * kpa)
        yqb = sel(qa * kqb + qb * kqa, pb * kqb + qb * kpb)
        eep_ref[0, rows, :] = (ypa + ypb).astype(eep_ref.dtype)
        eop_ref[0, rows, :] = (ypa - ypb).astype(eop_ref.dtype)
        eeq_ref[0, rows, :] = sel(yqa, yqa - yqb).astype(eeq_ref.dtype)
        eoq_ref[0, rows, :] = sel(yqb, yqa + yqb).astype(eoq_ref.dtype)

    emit(slice(None), pe, qe, po, qo, None)

    @pl.when(pl.program_id(1) == 0)
    def _():
        top = slice(0, BF16_TILE_ROWS)
        r0 = lax.broadcasted_iota(jnp.int32, (BF16_TILE_ROWS, pe.shape[1]), 0) == 0
        emit(top, pe[top], qe[top], po[top], qo[top], r0)


def hyena_forward(tables, u_e, u_o, u_col, ks, order):
    ce, se, co, so = tables[:4]
    bv, H, _ = u_e.shape
    w = ks[0].shape[2]
    tf = min(H, DFT_ROWS)
    out = jax.ShapeDtypeStruct((bv, H, w), BF16)
    tab = lambda: pl.BlockSpec((tf, H), lambda b, i: (i, 0))
    sig = lambda: pl.BlockSpec((1, H, w), lambda b, i: (b, 0, u_col))
    slot = lambda: pl.BlockSpec((1, tf, w), lambda b, i: (order, i, 0))
    res = lambda: pl.BlockSpec((1, tf, w), lambda b, i: (b, i, 0))
    return pl.pallas_call(
        _hy_fwd_kernel,
        out_shape=(out,) * 4,
        grid=(bv, H // tf),
        in_specs=[tab(), tab(), tab(), tab(), sig(), sig(), slot(), slot(), slot(), slot()],
        out_specs=(res(),) * 4,
        compiler_params=_cparams(("parallel", "parallel")),
        name="hyena_forward_dft",
    )(ce, se, co, so, u_e, u_o, *ks)


def _hy_inv_kernel(cet_ref, set_ref, cot_ref, sot_ref, eep_ref, eeq_ref, eop_ref, eoq_ref,
                   ue_ref, uo_ref, ge_ref, go_ref, bias_ref, *o_refs, interleave):
    dot = lambda t, x: jnp.dot(t[...], x[0], preferred_element_type=F32)
    ye = dot(cet_ref, eep_ref) + dot(set_ref, eeq_ref)
    yo = dot(cot_ref, eop_ref) + dot(sot_ref, eoq_ref)
    bias = bias_ref[0]
    oe = ge_ref[0].astype(F32) * (ye + ue_ref[0].astype(F32) * bias)
    oo = go_ref[0].astype(F32) * (yo + uo_ref[0].astype(F32) * bias)
    if interleave:
        o_ref, mix_ref = o_refs
        n = oe.shape[0]
        for c in range(oe.shape[1] // LANES):
            lanes = slice(c * LANES, (c + 1) * LANES)
            mix_ref[c, pl.ds(0, n, stride=2), :] = oe[:, lanes]
            mix_ref[c, pl.ds(1, n, stride=2), :] = oo[:, lanes]
            o_ref[0, :, lanes] = mix_ref[c].astype(o_ref.dtype)
    else:
        oe_ref, oo_ref = o_refs
        oe_ref[0] = oe.astype(oe_ref.dtype)
        oo_ref[0] = oo.astype(oo_ref.dtype)


def hyena_inverse(tables, es, u_e, u_o, u_col, g_e, g_o, g_col, bias, order, interleave):
    cet, set_, cot, sot = tables[0], tables[4], tables[5], tables[6]
    bv, H, w = es[0].shape
    tt = min(H, DFT_ROWS)
    tab = lambda: pl.BlockSpec((tt, H), lambda b, i: (i, 0))
    spec = lambda: pl.BlockSpec((1, H, w), lambda b, i: (b, 0, 0))
    tile = lambda col: pl.BlockSpec((1, tt, w), lambda b, i: (b, i, col))
    if interleave:
        out_shape = jax.ShapeDtypeStruct((bv, 2 * H, w), BF16)
        out_specs = pl.BlockSpec((1, 2 * tt, w), lambda b, i: (b, i, 0))
        scratch = [pltpu.VMEM((w // LANES, 2 * tt, LANES), F32)]
    else:
        out_shape = (jax.ShapeDtypeStruct((bv, H, w), BF16),) * 2
        out_specs = (tile(0), tile(0))
        scratch = []
    return pl.pallas_call(
        functools.partial(_hy_inv_kernel, interleave=interleave),
        out_shape=out_shape,
        grid=(bv, H // tt),
        in_specs=[tab(), tab(), tab(), tab(), spec(), spec(), spec(), spec(),
                  tile(u_col), tile(u_col), tile(g_col), tile(g_col),
                  pl.BlockSpec((1, 1, w), lambda b, i: (order, 0, 0))],
        out_specs=out_specs,
        scratch_shapes=scratch,
        compiler_params=_cparams(("parallel", "parallel")),
        name="hyena_inverse_dft",
    )(cet, set_, cot, sot, *es, u_e, u_o, g_e, g_o, bias)


def dft_tables(L):
    H = L // 2
    r = jnp.arange(H, dtype=jnp.int32)[:, None]
    c = jnp.arange(H, dtype=jnp.int32)[None, :]
    ang = lambda k: (k % (2 * L)).astype(F32) * (math.pi / L)
    alt_r, alt_c = (1 - 2 * (r % 2)).astype(F32), (1 - 2 * (c % 2)).astype(F32)
    a_e, a_o = ang(2 * r * c), ang(r * (2 * c + 1))
    bf = lambda a: a.astype(BF16)
    ce = bf(jnp.cos(a_e))
    se = bf(jnp.where(r == 0, alt_c, jnp.sin(a_e)))
    co = bf(jnp.cos(a_o))
    so = bf(jnp.where(r == 0, alt_c, jnp.sin(a_o)))
    set_ = bf(jnp.where(c == 0, alt_r, jnp.sin(a_e)))
    return ce, se, co, so, set_, co.T, so.T


def hyena_features(L):
    pos = jnp.arange(L, dtype=F32)
    t = pos / (L - 1)
    bands = jnp.linspace(1e-4, HY_BANDS - 1, HY_BANDS, dtype=F32)
    ang = (2.0 * math.pi / L) * pos[:, None] * bands[None, :]
    feats = jnp.concatenate([t[:, None], jnp.cos(ang), jnp.sin(ang)], axis=-1)
    feats = jnp.concatenate([feats[0::2], feats[1::2]], axis=0)
    return jnp.pad(feats, ((0, 0), (0, LANES - HY_FEAT)))


def hyena_branch(z_e, z_o, tables, feats, ffn, bias):
    w = bias.shape[-1]
    taps = hyena_filter_taps(feats, *ffn)
    ks = hyena_filter_spectrum(tables, taps, w)
    bias3 = bias.reshape(HY_ORDER, 1, w)
    es = hyena_forward(tables, z_e, z_o, 0, ks, 0)
    y1_e, y1_o = hyena_inverse(tables, es, z_e, z_o, 0, z_e, z_o, 1, bias3, 0, False)
    es = hyena_forward(tables, y1_e, y1_o, 0, ks, 1)
    return hyena_inverse(tables, es, y1_e, y1_o, 0, z_e, z_o, 2, bias3, 1, True)


def _mlstm_chunk(q, k, v1t, li_row, b_row, r_col, c, m, reverse):
    T = q.shape[0]
    s_idx = lax.broadcasted_iota(jnp.int32, (T, T), 0)
    t_idx = lax.broadcasted_iota(jnp.int32, (T, T), 1)
    mask = (s_idx >= t_idx) if reverse else (s_idx <= t_idx)
    nt = (((1,), (1,)), ((), ()))
    kq = lax.dot_general(k, q, nt, preferred_element_type=F32)
    d = jnp.where(mask, r_col + b_row, -1e30)
    m_loc = jnp.max(d, axis=0, keepdims=True)
    st = (kq * jnp.exp(d - m_loc)).astype(BF16)
    intra = jnp.dot(v1t, st, preferred_element_type=F32)
    g = b_row + m
    m_t = jnp.maximum(g, m_loc)
    inter = lax.dot_general(c.astype(BF16), q, nt, preferred_element_type=F32)
    num = jnp.exp(g - m_t) * inter + jnp.exp(m_loc - m_t) * intra
    den = num[HEAD_PAD - 1:HEAD_PAD, :]
    h = num * (1.0 / jnp.maximum(jnp.abs(den), jnp.exp(-m_t)))
    btot = b_row[:, 0:1] if reverse else b_row[:, T - 1:T]
    a = btot - b_row + li_row
    a_max = jnp.max(a, axis=1, keepdims=True)
    wv = (v1t.astype(F32) * jnp.exp(a - a_max)).astype(BF16)
    delta = jnp.dot(wv, k, preferred_element_type=F32)
    m_new = jnp.maximum(btot + m, a_max)
    c_new = jnp.exp(btot + m - m_new) * c + jnp.exp(a_max - m_new) * delta
    return h, c_new, m_new


def _mlstm_kernel(qc_ref, kc_ref, vc_ref, oc_ref, gcc_ref, grc_ref,
                  ql_ref, kl_ref, vl_ref, ol_ref, gcl_ref, grl_ref, ng_ref,
                  outc_ref, outl_ref, hf_ref, hb_ref, *, dh):
    T = ML_CHUNK
    L = ql_ref.shape[1]
    row = lax.broadcasted_iota(jnp.int32, (HEAD_PAD, T), 0)
    ones_row = row == HEAD_PAD - 1
    valid = row < dh
    ng = ng_ref[...]

    def gates(gc, gr, bwd):
        i = 2 if bwd else 0
        return gr[i:i + 1, :], gr[i + 1:i + 2, :], gc[:, i:i + 1] - gc[:, i + 1:i + 2]

    def finish(h, o):
        h = jnp.where(valid, h, 0.0)
        mu = jnp.sum(h, axis=0, keepdims=True) * (1.0 / dh)
        dlt = jnp.where(valid, h - mu, 0.0)
        var = jnp.sum(dlt * dlt, axis=0, keepdims=True) * (1.0 / dh)
        return (dlt * lax.rsqrt(var + LN_EPS) * ng * o.astype(F32)).astype(BF16)

    def aug(vt):
        return jnp.where(ones_row, jnp.ones_like(vt), vt)

    zero_c = jnp.zeros((HEAD_PAD, HEAD_PAD), F32)
    zero_m = jnp.zeros((1, 1), F32)

    q, k, v1t = qc_ref[0], kc_ref[0], aug(vc_ref[0])
    gc, gr = gcc_ref[0, 0], grc_ref[0, 0]
    h_f, c_f, m_f = _mlstm_chunk(q, k, v1t, *gates(gc, gr, False), zero_c, zero_m, False)
    h_b, c_b, m_b = _mlstm_chunk(q, k, v1t, *gates(gc, gr, True), zero_c, zero_m, True)
    outc_ref[0] = finish(h_f + h_b, oc_ref[0])

    nc = L // T
    for ci in range(nc):
        for bwd in (False, True):
            lo = (nc - 1 - ci) * T if bwd else ci * T
            gts = gates(gcl_ref[0, 0, lo:lo + T, :], grl_ref[0, 0, :, lo:lo + T], bwd)
            q, k, v1t = ql_ref[0, lo:lo + T, :], kl_ref[0, lo:lo + T, :], aug(vl_ref[0, :, lo:lo + T])
            if bwd:
                h_b, c_b, m_b = _mlstm_chunk(q, k, v1t, *gts, c_b, m_b, True)
                hb_ref[:, lo:lo + T] = h_b
            else:
                h_f, c_f, m_f = _mlstm_chunk(q, k, v1t, *gts, c_f, m_f, False)
                hf_ref[:, lo:lo + T] = h_f
    for ci in range(nc):
        lo = ci * T
        outl_ref[0, :, lo:lo + T] = finish(hf_ref[:, lo:lo + T] + hb_ref[:, lo:lo + T], ol_ref[0, :, lo:lo + T])


def mlstm_branch(qk_c, vt_c, ot_c, gate_c, qk_l, vt_l, ot_l, gate_l, norm_g_col, dh):
    bv, lc, _ = qk_c.shape
    L = qk_l.shape[1]
    H = ML_HEADS
    assert lc == ML_CHUNK and L % ML_CHUNK == 0

    def per_head(g):
        g = g[:, :, :4 * H].reshape(bv, -1, 4, H)
        return g.transpose(0, 3, 1, 2), g.transpose(0, 3, 2, 1)

    gcc, grc = per_head(gate_c)
    gcl, grl = per_head(gate_l)

    def qk_blk(length, part):
        return pl.BlockSpec((1, length, HEAD_PAD), lambda b, h: (b, 0, part * H + h))

    def t_blk(length):
        return pl.BlockSpec((1, HEAD_PAD, length), lambda b, h: (b, h, 0))

    def gspecs(length):
        return [pl.BlockSpec((1, 1, length, 4), lambda b, h: (b, h, 0, 0)),
                pl.BlockSpec((1, 1, 4, length), lambda b, h: (b, h, 0, 0))]

    def specs(length):
        return [qk_blk(length, 0), qk_blk(length, 1), t_blk(length), t_blk(length)] + gspecs(length)

    return pl.pallas_call(
        functools.partial(_mlstm_kernel, dh=dh),
        out_shape=(jax.ShapeDtypeStruct((bv, H * HEAD_PAD, lc), BF16),
                   jax.ShapeDtypeStruct((bv, H * HEAD_PAD, L), BF16)),
        grid=(bv, H),
        in_specs=specs(lc) + specs(L) + [pl.BlockSpec((HEAD_PAD, 1), lambda b, h: (h, 0))],
        out_specs=(t_blk(lc), t_blk(L)),
        scratch_shapes=[pltpu.VMEM((HEAD_PAD, L), F32), pltpu.VMEM((HEAD_PAD, L), F32)],
        compiler_params=_cparams(("parallel", "parallel")),
        name="mlstm_scan",
    )(qk_c, qk_c, vt_c, ot_c, gcc, grc, qk_l, qk_l, vt_l, ot_l, gcl, grl, norm_g_col)


def _s5_kernel(uc_ref, ul_ref, prow_ref, pcol_ref, bt_ref, cr_ref, cc_ref, y_ref,
               toep_ref, min_ref, mout_ref, kv_ref, s_ref, x_ref, u_ref, yv_ref, *, nb, nctx, nchunk):
    T, CG, P = S5_CHUNK, S5_GROUP, S5_STATE

    ch0 = 0
    for src in (uc_ref, ul_ref):
        xs = pltpu.einshape("bct->cbt", src[...].astype(F32))
        for ch in range(xs.shape[2] // T):
            for ci in range(CG):
                u_ref[(ch0 + ch) * nb:(ch0 + ch + 1) * nb, ci * T:(ci + 1) * T] = (
                    xs[ci, :, ch * T:(ch + 1) * T].astype(BF16))
        ch0 += xs.shape[2] // T

    hp = lax.Precision.HIGHEST
    lane = lax.broadcasted_iota(jnp.int32, (1, 2 * P), 1)
    re_lane = lane < P
    re_row = lax.broadcasted_iota(jnp.int32, (2 * P, 1), 0) < P

    def powers_rows(n, mag1, th1):
        m = jnp.exp(n * mag1)
        return m * jnp.cos(n * th1), m * jnp.sin(n * th1)

    def powers_cols(n, mag1, th1):
        m = jnp.exp(n * mag1)
        return m * jnp.cos(n * th1), m * jnp.sin(n * th1)

    at = []
    kv = jnp.zeros((CG * CG, 2 * T), F32)
    for d in range(2):
        a_re, a_im, ldt = prow_ref[0, d, 0:1, :], prow_ref[0, d, 1:2, :], prow_ref[0, d, 2:3, :]
        dt = jnp.exp(ldt)
        mag1, th1 = dt * a_re, dt * a_im
        ab_re, ab_im = jnp.exp(mag1) * jnp.cos(th1), jnp.exp(mag1) * jnp.sin(th1)
        den = a_re * a_re + a_im * a_im
        co_re = ((ab_re - 1.0) * a_re + ab_im * a_im) / den
        co_im = (ab_im * a_re - (ab_re - 1.0) * a_im) / den
        b_re, b_im = bt_ref[0, d, 0], bt_ref[0, d, 1]
        bb_re = co_re * b_re - co_im * b_im
        bb_im = co_re * b_im + co_im * b_re
        c_re, c_im = cr_ref[0, d, 0], cr_ref[0, d, 1]
        a_re_c, a_im_c = pcol_ref[0, d, :, 0:1], pcol_ref[0, d, :, 1:2]
        dt_c = jnp.exp(pcol_ref[0, d, :, 2:3])
        mag1_c, th1_c = dt_c * a_re_c, dt_c * a_im_c

        rep = lambda a: jnp.broadcast_to(a[:, None, :], (CG, CG, 2 * P)).reshape(CG * CG, 2 * P)
        til = lambda a: jnp.broadcast_to(a[None, :, :], (CG, CG, 2 * P)).reshape(CG * CG, 2 * P)
        w_re = rep(bb_re) * til(c_re) - rep(bb_im) * til(c_im)
        w_im = rep(bb_re) * til(c_im) + rep(bb_im) * til(c_re)
        w = jnp.where(re_lane, w_re, -w_im)
        lagp = lax.broadcasted_iota(jnp.int32, (1, 2 * T), 1)
        n = (T - 1 - lagp) if d else (lagp - (T - 1))
        ok = (n >= 0) & (lagp < 2 * T - 1)
        pc, ps = powers_cols(jnp.maximum(n, 0).astype(F32), mag1_c, th1_c)
        pw = jnp.where(ok, jnp.where(re_row, pc, ps), 0.0)
        kv = kv + jnp.dot(w, pw, precision=hp, preferred_element_type=F32)

        s_col = lax.broadcasted_iota(jnp.int32, (T, 1), 0)
        pc, ps = powers_rows((s_col if d else T - 1 - s_col).astype(F32), mag1, th1)
        for ci in range(CG):
            br, bi = bb_re[ci:ci + 1, :], bb_im[ci:ci + 1, :]
            blk = jnp.where(re_lane, br * pc - bi * ps, br * ps + bi * pc)
            min_ref[d, ci * T:(ci + 1) * T, :] = blk.astype(BF16)

        t_row = lax.broadcasted_iota(jnp.int32, (1, T), 1)
        pc, ps = powers_cols((T - t_row if d else t_row + 1).astype(F32), mag1_c, th1_c)
        for co in range(CG):
            cr, ci_ = cc_ref[0, d, 0, :, co:co + 1], cc_ref[0, d, 1, :, co:co + 1]
            blk = jnp.where(re_row, cr * pc - ci_ * ps, -(cr * ps + ci_ * pc))
            mout_ref[d, :, co * T:(co + 1) * T] = blk.astype(BF16)

        mt = jnp.exp(T * mag1)
        at.append(((mt * jnp.cos(T * th1))[:, :P], (mt * jnp.sin(T * th1))[:, :P]))

    kv_ref[...] = kv

    def build(ci, carry):
        r0 = pl.multiple_of(ci * T, T)
        for co in range(CG):
            vec = kv_ref[pl.ds(ci * CG + co, 1), :]
            rolled = pltpu.roll(jnp.broadcast_to(vec, (T, 2 * T)), T + 1, 1, stride=1, stride_axis=0)
            toep_ref[pl.ds(r0, T), co * T:(co + 1) * T] = rolled[:, :T].astype(BF16)
        return carry

    lax.fori_loop(0, CG, build, 0)

    u = u_ref[...]
    y = jnp.dot(u, toep_ref[...], preferred_element_type=F32)
    for d in range(2):
        s_ref[d] = jnp.dot(u, min_ref[d], preferred_element_type=F32)

    orders = (list(range(nchunk)), list(range(nctx - 1, -1, -1)) + list(range(nchunk - 1, nctx - 1, -1)))
    for d in range(2):
        at_re, at_im = at[d]
        x_re = jnp.zeros((nb, P), F32)
        x_im = jnp.zeros((nb, P), F32)
        for ch in orders[d]:
            x_ref[d, ch * nb:(ch + 1) * nb, :] = jnp.concatenate([x_re, x_im], axis=1)
            s = s_ref[d, ch * nb:(ch + 1) * nb, :]
            x_re, x_im = (at_re * x_re - at_im * x_im + s[:, :P], at_re * x_im + at_im * x_re + s[:, P:])
        y = y + jnp.dot(x_ref[d].astype(BF16), mout_ref[d], preferred_element_type=F32)

    for ch in range(nchunk):
        for co in range(CG):
            yv_ref[co, :, ch * T:(ch + 1) * T] = y[ch * nb:(ch + 1) * nb, co * T:(co + 1) * T]
    y_ref[...] = pltpu.einshape("cbt->bct", yv_ref[...]).astype(y_ref.dtype)


def s5_mix(ut_c, ut_l, a_re, a_im, log_dt, b_re, b_im, c_re, c_im):
    nb, w, lc = ut_c.shape
    L = ut_l.shape[2]
    T, CG, P = S5_CHUNK, S5_GROUP, S5_STATE
    G = w // CG
    nctx, nlat = lc // T, L // T
    nchunk = nctx + nlat

    dup = lambda a: jnp.concatenate([a, a], axis=-1)
    f32 = lambda a: a.astype(F32)
    ldt = jnp.broadcast_to(f32(log_dt)[:, :, None], (2, G, P))
    prow = jnp.stack([dup(f32(a_re)), dup(f32(a_im)), dup(ldt)], axis=2)
    prow = jnp.pad(prow, ((0, 0), (0, 0), (0, 5), (0, 0))).transpose(1, 0, 2, 3)
    pcol = prow.transpose(0, 1, 3, 2)
    bt = jnp.stack([dup(f32(b_re).transpose(0, 1, 3, 2)), dup(f32(b_im).transpose(0, 1, 3, 2))], axis=2)
    bt = bt.transpose(1, 0, 2, 3, 4)
    cr = jnp.stack([dup(f32(c_re)), dup(f32(c_im))], axis=2).transpose(1, 0, 2, 3, 4)
    cc = cr.transpose(0, 1, 2, 4, 3)

    blk = lambda a: pl.BlockSpec((1,) + a.shape[1:], lambda g: (g,) + (0,) * (a.ndim - 1))
    grp = lambda length: pl.BlockSpec((nb, CG, length), lambda g: (0, g, 0))
    r = nchunk * nb
    return pl.pallas_call(
        functools.partial(_s5_kernel, nb=nb, nctx=nctx, nchunk=nchunk),
        out_shape=jax.ShapeDtypeStruct((nb, w, lc + L), BF16),
        grid=(G,),
        in_specs=[grp(lc), grp(L), blk(prow), blk(pcol), blk(bt), blk(cr), blk(cc)],
        out_specs=grp(lc + L),
        scratch_shapes=[pltpu.VMEM((CG * T, CG * T), BF16), pltpu.VMEM((2, CG * T, 2 * P), BF16),
                        pltpu.VMEM((2, 2 * P, CG * T), BF16), pltpu.VMEM((CG * CG, 2 * T), F32),
                        pltpu.VMEM((2, r, 2 * P), F32), pltpu.VMEM((2, r, 2 * P), F32),
                        pltpu.VMEM((r, CG * T), BF16), pltpu.VMEM((CG, nb, lc + L), F32)],
        compiler_params=_cparams(("parallel",)),
        name="s5_mix",
    )(ut_c, ut_l, prow, pcol, bt, cr, cc)


def _gelu_tanh(x):
    return 0.5 * x * (1.0 + jnp.tanh(math.sqrt(2.0 / math.pi) * (x + 0.044715 * (x * x * x))))


def _merge_kernel(yhy_ref, yml_ref, ys_ref, u_ref, ghy_ref, gml_ref, gs5_ref,
                  d_ref, gw_ref, gb_ref, why_ref, wml_ref, ws5_ref, o_ref):
    tn = (((0,), (0,)), ((), ()))
    ys = ys_ref[0].astype(F32) + d_ref[...] * u_ref[0].astype(F32)
    z = _gelu_tanh(ys).astype(BF16)
    glu = jnp.dot(gw_ref[...], z, preferred_element_type=F32) + gb_ref[...]
    y_s5 = (z.astype(F32) * _sigmoid(glu)).astype(BF16)
    acc = ghy_ref[0].astype(F32) * jnp.dot(yhy_ref[0], why_ref[...], preferred_element_type=F32)
    acc = acc + gml_ref[0].astype(F32) * lax.dot_general(yml_ref[0], wml_ref[...], tn, preferred_element_type=F32)
    acc = acc + gs5_ref[0].astype(F32) * lax.dot_general(y_s5, ws5_ref[...], tn, preferred_element_type=F32)
    o_ref[0] = acc.astype(o_ref.dtype)


def merge_branches(y_hy, y_ml, ys_t, u_t, t_off, gate_arr, s5_d, glu_wt, glu_b, w_hy, w_ml, w_s5):
    bv, lv, w = y_hy.shape
    dm = w_hy.shape[1]
    tm = ML_CHUNK
    full = lambda a: pl.BlockSpec(a.shape, lambda b, i: (0,) * a.ndim)
    gate = lambda c: pl.BlockSpec((1, tm, dm), lambda b, i: (b, i, c))
    return pl.pallas_call(
        _merge_kernel,
        out_shape=jax.ShapeDtypeStruct((bv, lv, dm), BF16),
        grid=(bv, lv // tm),
        in_specs=[pl.BlockSpec((1, tm, w), lambda b, i: (b, i, 0)),
                  pl.BlockSpec((1, y_ml.shape[1], tm), lambda b, i: (b, 0, i)),
                  pl.BlockSpec((1, w, tm), lambda b, i: (b, 0, i + t_off)),
                  pl.BlockSpec((1, w, tm), lambda b, i: (b, 0, i)),
                  gate(0), gate(1), gate(2),
                  full(s5_d), full(glu_wt), full(glu_b), full(w_hy), full(w_ml), full(w_s5)],
        out_specs=pl.BlockSpec((1, tm, dm), lambda b, i: (b, i, 0)),
        compiler_params=_cparams(("parallel", "parallel")),
        name="merge_branches",
    )(y_hy, y_ml, ys_t, u_t, gate_arr, gate_arr, gate_arr, s5_d, glu_wt, glu_b, w_hy, w_ml, w_s5)


def _outproj_kernel(m_ref, w_ref, x_ref, gate_ref, g_ref, b_ref, o_ref, *, alpha):
    rb = min(m_ref.shape[1], LANES)
    for s in range(m_ref.shape[1] // rb):
        rows = slice(s * rb, (s + 1) * rb)
        y = jnp.dot(m_ref[0, rows, :], w_ref[...], preferred_element_type=F32)
        r = alpha * x_ref[0, rows, :] + gate_ref[0] * y
        o_ref[0, rows, :] = _layer_norm(r, g_ref[...], b_ref[...])


def out_projection(merged, w_out, x, gate, ln_g, ln_b, alpha, tm):
    bv, lv, d = x.shape
    bm = gate.shape[0]
    mod_map = (lambda b, i: (b, 0, 0)) if bm == bv else (lambda b, i: (0, 0, 0))
    vec = lambda: pl.BlockSpec((1, d), lambda b, i: (0, 0))
    return pl.pallas_call(
        functools.partial(_outproj_kernel, alpha=alpha),
        out_shape=jax.ShapeDtypeStruct((bv, lv, d), F32),
        grid=(bv, lv // tm),
        in_specs=[pl.BlockSpec((1, tm, d), lambda b, i: (b, i, 0)),
                  pl.BlockSpec((d, d), lambda b, i: (0, 0)),
                  pl.BlockSpec((1, tm, d), lambda b, i: (b, i, 0)),
                  pl.BlockSpec((1, 1, d), mod_map), vec(), vec()],
        out_specs=pl.BlockSpec((1, tm, d), lambda b, i: (b, i, 0)),
        compiler_params=_cparams(("parallel", "parallel")),
        name="out_projection_ln",
    )(merged, w_out, x, gate, ln_g, ln_b)


def _mlp_kernel(x_ref, sh_ref, sc_ref, gate_ref, w1_ref, w2_ref, g_ref, b_ref, o_ref, xm_ref, acc_ref, *, alpha):
    k = pl.program_id(2)

    @pl.when(k == 0)
    def _():
        xm_ref[...] = (x_ref[0] * (1.0 + sc_ref[0]) + sh_ref[0]).astype(BF16)
        acc_ref[...] = jnp.zeros_like(acc_ref)

    h = jnp.maximum(jnp.dot(xm_ref[...], w1_ref[...], preferred_element_type=F32), 0.0)
    acc_ref[...] += jnp.dot((h * h).astype(BF16), w2_ref[...], preferred_element_type=F32)

    @pl.when(k == pl.num_programs(2) - 1)
    def _():
        r = alpha * x_ref[0] + gate_ref[0] * acc_ref[...]
        o_ref[0] = _layer_norm(r, g_ref[...], b_ref[...])


def mlp_block(x, shift, scale, gate, w1, w2, ln_g, ln_b, alpha, tm, kf):
    bv, lv, d = x.shape
    dff = w1.shape[1]
    bm = gate.shape[0]
    mod_map = (lambda b, i, k: (b, 0, 0)) if bm == bv else (lambda b, i, k: (0, 0, 0))
    vec = lambda: pl.BlockSpec((1, d), lambda b, i, k: (0, 0))
    return pl.pallas_call(
        functools.partial(_mlp_kernel, alpha=alpha),
        out_shape=jax.ShapeDtypeStruct((bv, lv, d), F32),
        grid=(bv, lv // tm, dff // kf),
        in_specs=[pl.BlockSpec((1, tm, d), lambda b, i, k: (b, i, 0)),
                  pl.BlockSpec((1, 1, d), mod_map), pl.BlockSpec((1, 1, d), mod_map),
                  pl.BlockSpec((1, 1, d), mod_map),
                  pl.BlockSpec((d, kf), lambda b, i, k: (0, k)),
                  pl.BlockSpec((kf, d), lambda b, i, k: (k, 0)),
                  vec(), vec()],
        out_specs=pl.BlockSpec((1, tm, d), lambda b, i, k: (b, i, 0)),
        scratch_shapes=[pltpu.VMEM((tm, d), BF16), pltpu.VMEM((tm, d), F32)],
        compiler_params=_cparams(("parallel", "parallel", "arbitrary")),
        name="mlp_ln",
    )(x, shift, scale, gate, w1, w2, ln_g, ln_b)


def _pad_heads(a, axis, dh):
    shp = a.shape
    a = a.reshape(shp[:axis] + (ML_HEADS, dh) + shp[axis + 1:])
    pad = [(0, 0)] * a.ndim
    pad[axis + 1] = (0, HEAD_PAD - dh)
    a = jnp.pad(a, pad)
    return a.reshape(shp[:axis] + (ML_HEADS * HEAD_PAD,) + shp[axis + 1:])


def _layer_params(l, p, w_hy, w_ml, w_s5, d_model):
    dh = w_ml // ML_HEADS
    sizes = ((HY_ORDER + 1) * w_hy, 2 * w_ml, w_ml, w_ml, 4 * ML_HEADS, w_s5, N_BRANCH * d_model)
    pts = [0]
    for s in sizes:
        pts.append(pts[-1] + s)
    w_in = p["w_in"][l]
    hy, qk, v, o, gt, u, mg = (w_in[:, pts[i]:pts[i + 1]] for i in range(7))
    ph = lambda a: _pad_heads(a, a.ndim - 1, dh)
    none = lambda n: (jnp.zeros((3, n), F32), jnp.zeros((1, n), F32), jnp.ones((1, n), F32))
    half = ML_HEADS * HEAD_PAD
    cw, cb = p["ml_conv_w"][l], p["ml_conv_b"][l]
    out = {}
    out["mg"] = (mg.astype(BF16),) + none(mg.shape[1])
    out["hy"] = (hy.astype(BF16), p["hy_conv_w"][l], p["hy_conv_b"][l].reshape(1, -1), jnp.ones((1, hy.shape[1]), F32))
    out["qk"] = (jnp.concatenate([ph(qk[:, :w_ml]), ph(qk[:, w_ml:])], axis=1).astype(BF16),
                 jnp.concatenate([ph(cw[:, :w_ml]), ph(cw[:, w_ml:])], axis=1),
                 jnp.concatenate([ph(cb[:w_ml]), ph(cb[w_ml:])]).reshape(1, -1),
                 jnp.concatenate([jnp.ones((half,), F32), jnp.full((half,), dh ** -0.5, F32)]).reshape(1, -1))
    out["vt"] = (ph(v).T.astype(BF16),) + none(half)
    out["ot"] = (ph(o).T.astype(BF16),) + none(half)
    ngt = gt.shape[1]
    out["ut"] = (u.T.astype(BF16),) + none(u.shape[1])
    out["gt"] = (jnp.pad(gt, ((0, 0), (0, LANES - ngt))).astype(BF16),
                 jnp.pad(p["ml_gate_b"][l].reshape(-1), ((0, LANES - ngt),)).reshape(1, -1))
    out["norm_g"] = ph(p["ml_norm_g"][l]).reshape(-1, 1)
    out["w_ml_out"] = _pad_heads(p["w_ml_out"][l], 0, dh).astype(BF16)
    w3 = p["hy_ffn_w3"][l]
    dec = p["hy_decay"][l]
    dec_cols = jnp.broadcast_to(dec[:, None, :], (HY_ORDER, 2, w_hy)).reshape(1, -1)
    hpad = LANES - HY_HIDDEN
    row = lambda a: jnp.pad(a.reshape(1, -1), ((0, 0), (0, hpad)))
    out["hy_ffn"] = (jnp.pad(p["hy_ffn_w1"][l], ((0, LANES - HY_FEAT), (0, hpad))), row(p["hy_ffn_b1"][l]),
                     jnp.pad(p["hy_ffn_w2"][l], ((0, hpad), (0, hpad))), row(p["hy_ffn_b2"][l]),
                     jnp.pad(w3, ((0, hpad), (0, 0))), row(p["hy_sin_freq"][l]), dec_cols)
    return out


def kernel(x, c, ctx, c_ctx, w_mod, b_mod, w_in, hy_conv_w, hy_conv_b, hy_ffn_w1, hy_ffn_b1, hy_ffn_w2,
           hy_ffn_b2, hy_ffn_w3, hy_sin_freq, hy_decay, hy_bias, ml_conv_w, ml_conv_b, ml_gate_b, ml_norm_g,
           s5_a_re, s5_a_im, s5_log_dt, s5_b_re, s5_b_im, s5_c_re, s5_c_im, s5_d, s5_glu_w, s5_glu_b,
           w_hy_out, w_ml_out, w_s5_out, w_out, ln1_g, ln1_b, ln2_g, ln2_b, w_ff1, w_ff2):
    p = dict(w_in=w_in, hy_conv_w=hy_conv_w, hy_conv_b=hy_conv_b, hy_ffn_w1=hy_ffn_w1, hy_ffn_b1=hy_ffn_b1,
             hy_ffn_w2=hy_ffn_w2, hy_ffn_b2=hy_ffn_b2, hy_ffn_w3=hy_ffn_w3, hy_sin_freq=hy_sin_freq,
             hy_decay=hy_decay, ml_conv_w=ml_conv_w, ml_conv_b=ml_conv_b, ml_gate_b=ml_gate_b,
             ml_norm_g=ml_norm_g, w_ml_out=w_ml_out)
    B, L, D = x.shape
    LC = ctx.shape[1]
    depth = w_in.shape[0]
    w_hy = hy_bias.shape[-1]
    w_ml = ml_norm_g.shape[-1]
    w_s5 = s5_d.shape[-1]
    dh = w_ml // ML_HEADS
    alpha = (2 * depth) ** 0.25
    half = ML_HEADS * HEAD_PAD

    tab_l, tab_c = dft_tables(L), dft_tables(LC)
    feats_l, feats_c = hyena_features(L), hyena_features(LC)

    mrows = 8 * ((B + 1 + 7) // 8)
    c_rows = jnp.zeros((mrows, D), F32).at[:B].set(c).at[B].set(c_ctx)

    for l in range(depth):
        need_ctx = l < depth - 1
        lp = _layer_params(l, p, w_hy, w_ml, w_s5, D)
        mod = mod_vectors(c_rows, w_mod, b_mod.reshape(depth, 1, -1), l)
        mod_l = [mod[:B, k * D:(k + 1) * D].reshape(B, 1, D) for k in range(6)]
        mod_c = [mod[B:B + 1, k * D:(k + 1) * D].reshape(1, 1, D) for k in range(6)]

        def mixer_inputs(xv, md, row_len, full):
            bv, lv, _ = xv.shape
            tm = min(PROJ_ROWS, lv)
            proj = functools.partial(projection, xv, md[0], md[1], tm=tm, row_len=row_len)
            flat = md[0].shape[0] == 1
            xr = xv.reshape(1, bv * lv, D) if flat else xv
            tmr = min(PROJ_ROWS, xr.shape[1])
            rowp = functools.partial(projection, xr, md[0], md[1], tm=tmr, row_len=row_len)
            unflat = lambda a: a.reshape(bv, -1, a.shape[-1])
            r = {}
            r["qk"] = unflat(rowp(*lp["qk"], kind="convsilu", tn=half))
            r["vt"] = proj(*lp["vt"], kind="plain", tn=half, transposed=True)
            r["ot"] = proj(*lp["ot"], kind="sigmoid", tn=half, transposed=True)
            r["ut"] = proj(*lp["ut"], kind="plain", tn=w_s5, transposed=True)
            r["g"] = gate_projection(xv, md[0], md[1], *lp["gt"], tm=tm)
            if full:
                r["mg"] = unflat(rowp(*lp["mg"], kind="sigmoid", tn=2 * w_hy))
                z_e, z_o = projection_even_odd(xr, md[0], md[1], *lp["hy"][:3], tn=w_hy, tm=tmr, row_len=row_len)
                r["z"] = (unflat(z_e), unflat(z_o))
            return r

        r_l = mixer_inputs(x, mod_l, GRID_W, True)
        r_c = mixer_inputs(ctx, mod_c, LC, need_ctx)

        hm_c, hm_l = mlstm_branch(r_c["qk"], r_c["vt"], r_c["ot"], r_c["g"],
                                  r_l["qk"], r_l["vt"], r_l["ot"], r_l["g"], lp["norm_g"], dh)

        ys_t = s5_mix(r_c["ut"], r_l["ut"], s5_a_re[l], s5_a_im[l], s5_log_dt[l],
                      s5_b_re[l], s5_b_im[l], s5_c_re[l], s5_c_im[l])

        s5_dv = s5_d[l].reshape(-1, 1)
        glu_wt = s5_glu_w[l].T.astype(BF16)
        glu_b = s5_glu_b[l].reshape(-1, 1)
        w_hy_o = w_hy_out[l].astype(BF16)
        w_s5_o = w_s5_out[l].astype(BF16)
        w_o = w_out[l].astype(BF16)
        w1 = w_ff1[l].astype(BF16)
        w2 = w_ff2[l].astype(BF16)
        g1, b1 = ln1_g[l].reshape(1, -1), ln1_b[l].reshape(1, -1)
        g2, b2 = ln2_g[l].reshape(1, -1), ln2_b[l].reshape(1, -1)

        def finish(xv, r, hm, t_off, tables, feats, md):
            tm = TOKEN_ROWS
            y_hy = hyena_branch(*r["z"], tables, feats, lp["hy_ffn"], hy_bias[l])
            merged = merge_branches(y_hy, hm, ys_t, r["ut"], t_off, r["mg"], s5_dv, glu_wt, glu_b,
                                    w_hy_o, lp["w_ml_out"], w_s5_o)
            if md[2].shape[0] == 1:
                merged, xv = merged.reshape(1, -1, D), xv.reshape(1, -1, D)
            x1 = out_projection(merged, w_o, xv, md[2], g1, b1, alpha, tm)
            return mlp_block(x1, md[3], md[4], md[5], w1, w2, g2, b2, alpha, tm, MLP_FF_CHUNK)

        x = finish(x, r_l, hm_l, LC // ML_CHUNK, tab_l, feats_l, mod_l)
        if need_ctx:
            ctx = finish(ctx, r_c, hm_c, 0, tab_c, feats_c, mod_c).reshape(B, LC, D)
    return x
```

```python
import functools
import math

import jax
import jax.numpy as jnp
from jax import lax
from jax.experimental import pallas as pl
from jax.experimental.pallas import tpu as pltpu

F32 = jnp.float32
BF16 = jnp.bfloat16

GRID_W = 64
HY_ORDER = 2
HY_BANDS = 16
HY_FEAT = 1 + 2 * HY_BANDS
HY_HIDDEN = 64
ML_HEADS = 4
ML_CHUNK = 256
S5_GROUP = 16
S5_STATE = 64
N_BRANCH = 3
LN_EPS = 1e-5

LANES = 128
BF16_TILE_ROWS = 16
MXU_DIM = 256
VMEM_LIMIT = 56 * 1024 * 1024

PROJ_ROWS = 1024
TOKEN_ROWS = 512
MLP_FF_CHUNK = 1024
MOD_COLS = 1024
DFT_ROWS = 512

HEAD_PAD = MXU_DIM
S5_CHUNK = LANES


def _cparams(sem):
    return pltpu.CompilerParams(dimension_semantics=sem, vmem_limit_bytes=VMEM_LIMIT)


def _sigmoid(x):
    return 0.5 * jnp.tanh(0.5 * x) + 0.5


def _layer_norm(r, g, b):
    mu = jnp.mean(r, axis=-1, keepdims=True)
    d = r - mu
    var = jnp.mean(d * d, axis=-1, keepdims=True)
    return d * lax.rsqrt(var + LN_EPS) * g + b


def _mod_kernel(c_ref, w_ref, b_ref, o_ref):
    c = c_ref[...]
    a = (c * _sigmoid(c)).astype(BF16)
    o_ref[...] = jnp.dot(a, w_ref[0].astype(BF16), preferred_element_type=F32) + b_ref[0]


def mod_vectors(c_rows, w, b, layer):
    m, d = c_rows.shape
    n = w.shape[2]
    tn = MOD_COLS
    return pl.pallas_call(
        _mod_kernel,
        out_shape=jax.ShapeDtypeStruct((m, n), F32),
        grid=(n // tn,),
        in_specs=[pl.BlockSpec((m, d), lambda j: (0, 0)),
                  pl.BlockSpec((1, d, tn), lambda j: (layer, 0, j)),
                  pl.BlockSpec((1, 1, tn), lambda j: (layer, 0, j))],
        out_specs=pl.BlockSpec((m, tn), lambda j: (0, j)),
        compiler_params=_cparams(("parallel",)),
        name="mod_vectors",
    )(c_rows, w, b)


def _short_conv(acc, w, b, row_len):
    rows = acc.shape[0]
    t = lax.broadcasted_iota(jnp.int32, acc.shape, 0) & (row_len - 1)
    prev = jnp.where(t == 0, 0.0, pltpu.roll(acc, 1, 0))
    nxt = jnp.where(t == row_len - 1, 0.0, pltpu.roll(acc, rows - 1, 0))
    return prev * w[0:1] + acc * w[1:2] + nxt * w[2:3] + b


def _modulate_rows(xm_ref, x_ref, sh_ref, sc_ref, rows):
    xm_ref[rows, :] = (x_ref[0, rows, :] * (1.0 + sc_ref[0]) + sh_ref[0]).astype(BF16)


def _modulate_into(xm_ref, x_ref, sh_ref, sc_ref):
    _modulate_rows(xm_ref, x_ref, sh_ref, sc_ref, slice(None))


def _proj_kernel(x_ref, sh_ref, sc_ref, w_ref, cw_ref, cb_ref, cs_ref, o_ref, xm_ref, *, kind, row_len, rb,
                 transposed, single_tile):
    def epilogue(acc):
        if kind == "conv":
            return _short_conv(acc, cw_ref[...], cb_ref[...], row_len)
        if kind == "convsilu":
            y = _short_conv(acc, cw_ref[...], cb_ref[...], row_len)
            return y * _sigmoid(y) * cs_ref[...]
        if kind == "sigmoid":
            return _sigmoid(acc)
        return acc

    def body(modulate):
        for r in range(xm_ref.shape[0] // rb):
            rows = slice(r * rb, (r + 1) * rb)
            if modulate:
                _modulate_rows(xm_ref, x_ref, sh_ref, sc_ref, rows)
            xs = xm_ref[rows, :]
            if transposed:
                acc = lax.dot_general(w_ref[...], xs, (((1,), (1,)), ((), ())), preferred_element_type=F32)
                o_ref[0, :, rows] = epilogue(acc).astype(o_ref.dtype)
            else:
                acc = jnp.dot(xs, w_ref[...], preferred_element_type=F32)
                o_ref[0, rows, :] = epilogue(acc).astype(o_ref.dtype)

    if single_tile:
        body(True)
    else:
        first = pl.program_id(2) == 0
        pl.when(first)(lambda: body(True))
        pl.when(jnp.logical_not(first))(lambda: body(False))


def projection(x, shift, scale, w, cw, cb, cs, *, kind, tn, tm, row_len, transposed=False):
    bv, lv, d = x.shape
    n = w.shape[0] if transposed else w.shape[1]
    rb = min(tm, ML_CHUNK)
    assert n % tn == 0 and lv % tm == 0 and rb % row_len == 0
    bm = shift.shape[0]
    mod_map = (lambda b, i, j: (b, 0, 0)) if bm == bv else (lambda b, i, j: (0, 0, 0))
    if transposed:
        assert kind in ("plain", "sigmoid")
        w_spec = pl.BlockSpec((tn, d), lambda b, i, j: (j, 0))
        out_shape = jax.ShapeDtypeStruct((bv, n, lv), BF16)
        out_spec = pl.BlockSpec((1, tn, tm), lambda b, i, j: (b, j, i))
    else:
        w_spec = pl.BlockSpec((d, tn), lambda b, i, j: (0, j))
        out_shape = jax.ShapeDtypeStruct((bv, lv, n), BF16)
        out_spec = pl.BlockSpec((1, tm, tn), lambda b, i, j: (b, i, j))
    col = lambda rows: pl.BlockSpec((rows, tn), lambda b, i, j: (0, j))
    return pl.pallas_call(
        functools.partial(_proj_kernel, kind=kind, row_len=row_len, rb=rb, transposed=transposed,
                          single_tile=(n == tn)),
        out_shape=out_shape,
        grid=(bv, lv // tm, n // tn),
        in_specs=[pl.BlockSpec((1, tm, d), lambda b, i, j: (b, i, 0)),
                  pl.BlockSpec((1, 1, d), mod_map), pl.BlockSpec((1, 1, d), mod_map),
                  w_spec, col(3), col(1), col(1)],
        out_specs=out_spec,
        scratch_shapes=[pltpu.VMEM((tm, d), BF16)],
        compiler_params=_cparams(("parallel", "parallel", "arbitrary")),
        name="projection_" + kind + ("_t" if transposed else ""),
    )(x, shift, scale, w, cw, cb, cs)


def _proj_eo_kernel(x_ref, sh_ref, sc_ref, w_ref, cw_ref, cb_ref, oe_ref, oo_ref, xm_ref, tmp_ref, *, row_len, rb):
    nlc = w_ref.shape[1] // LANES

    def body(modulate):
        for r in range(xm_ref.shape[0] // rb):
            rows = slice(r * rb, (r + 1) * rb)
            if modulate:
                _modulate_rows(xm_ref, x_ref, sh_ref, sc_ref, rows)
            acc = jnp.dot(xm_ref[rows, :], w_ref[...], preferred_element_type=F32)
            val = _short_conv(acc, cw_ref[...], cb_ref[...], row_len)
            half = slice(r * rb // 2, (r + 1) * rb // 2)
            for c in range(nlc):
                lanes = slice(c * LANES, (c + 1) * LANES)
                tmp_ref[r * nlc + c] = val[:, lanes]
                oe_ref[0, half, lanes] = tmp_ref[r * nlc + c, pl.ds(0, rb // 2, stride=2), :].astype(oe_ref.dtype)
                oo_ref[0, half, lanes] = tmp_ref[r * nlc + c, pl.ds(1, rb // 2, stride=2), :].astype(oo_ref.dtype)

    first = pl.program_id(2) == 0
    pl.when(first)(lambda: body(True))
    pl.when(jnp.logical_not(first))(lambda: body(False))


def projection_even_odd(x, shift, scale, w, cw, cb, *, tn, tm, row_len):
    bv, lv, d = x.shape
    n = w.shape[1]
    rb = min(tm, ML_CHUNK)
    assert n % tn == 0 and n > tn and lv % tm == 0 and rb % row_len == 0
    bm = shift.shape[0]
    mod_map = (lambda b, i, j: (b, 0, 0)) if bm == bv else (lambda b, i, j: (0, 0, 0))
    col = lambda rows: pl.BlockSpec((rows, tn), lambda b, i, j: (0, j))
    out = jax.ShapeDtypeStruct((bv, lv // 2, n), BF16)
    half = lambda: pl.BlockSpec((1, tm // 2, tn), lambda b, i, j: (b, i, j))
    return pl.pallas_call(
        functools.partial(_proj_eo_kernel, row_len=row_len, rb=rb),
        out_shape=(out, out),
        grid=(bv, lv // tm, n // tn),
        in_specs=[pl.BlockSpec((1, tm, d), lambda b, i, j: (b, i, 0)),
                  pl.BlockSpec((1, 1, d), mod_map), pl.BlockSpec((1, 1, d), mod_map),
                  pl.BlockSpec((d, tn), lambda b, i, j: (0, j)), col(3), col(1)],
        out_specs=(half(), half()),
        scratch_shapes=[pltpu.VMEM((tm, d), BF16), pltpu.VMEM((tm // rb * (tn // LANES), rb, LANES), F32)],
        compiler_params=_cparams(("parallel", "parallel", "arbitrary")),
        name="projection_conv_even_odd",
    )(x, shift, scale, w, cw, cb)


def _log_sigmoid(x):
    return jnp.minimum(x, 0.0) - jnp.log(1.0 + jnp.exp(-jnp.abs(x)))


def _gate_kernel(x_ref, sh_ref, sc_ref, wg_ref, gb_ref, g_ref, xm_ref):
    _modulate_into(xm_ref, x_ref, sh_ref, sc_ref)
    T = ML_CHUNK
    hp = lax.Precision.HIGHEST
    r = lax.broadcasted_iota(jnp.int32, (T, T), 0)
    s = lax.broadcasted_iota(jnp.int32, (T, T), 1)
    tri_f = (s <= r).astype(F32)
    tri_b = (s >= r).astype(F32)
    kind = lax.broadcasted_iota(jnp.int32, (T, LANES), 1) // ML_HEADS
    for c in range(xm_ref.shape[0] // T):
        g = jnp.dot(xm_ref[c * T:(c + 1) * T, :], wg_ref[...], preferred_element_type=F32) + gb_ref[...]
        ls = _log_sigmoid(g)
        cum_f = jnp.dot(tri_f, ls, precision=hp, preferred_element_type=F32)
        cum_b = jnp.dot(tri_b, ls, precision=hp, preferred_element_type=F32)
        g_ref[0, c * T:(c + 1) * T, :] = jnp.where(kind == 1, cum_f, jnp.where(kind == 3, cum_b, g))


def gate_projection(x, shift, scale, wg, gb, *, tm):
    bv, lv, d = x.shape
    assert lv % tm == 0 and tm % ML_CHUNK == 0
    bm = shift.shape[0]
    mod_map = (lambda b, i: (b, 0, 0)) if bm == bv else (lambda b, i: (0, 0, 0))
    full = lambda a: pl.BlockSpec(a.shape, lambda b, i: (0, 0))
    return pl.pallas_call(
        _gate_kernel,
        out_shape=jax.ShapeDtypeStruct((bv, lv, LANES), F32),
        grid=(bv, lv // tm),
        in_specs=[pl.BlockSpec((1, tm, d), lambda b, i: (b, i, 0)),
                  pl.BlockSpec((1, 1, d), mod_map), pl.BlockSpec((1, 1, d), mod_map),
                  full(wg), full(gb)],
        out_specs=pl.BlockSpec((1, tm, LANES), lambda b, i: (b, i, 0)),
        scratch_shapes=[pltpu.VMEM((tm, d), BF16)],
        compiler_params=_cparams(("parallel", "parallel")),
        name="gate_projection",
    )(x, shift, scale, wg, gb)


def _hy_ffn_kernel(feat_ref, w1_ref, b1_ref, w2_ref, b2_ref, w3_ref, fr_ref, dec_ref, o_ref):
    hp = lax.Precision.HIGHEST
    feats = feat_ref[...]
    fr = fr_ref[...]
    h = jnp.sin(fr * (jnp.dot(feats, w1_ref[...], precision=hp, preferred_element_type=F32) + b1_ref[...]))
    h = jnp.sin(fr * (jnp.dot(h, w2_ref[...], precision=hp, preferred_element_type=F32) + b2_ref[...]))
    h = jnp.dot(h, w3_ref[...], precision=hp, preferred_element_type=F32)
    t = feats[:, 0:1]
    o_ref[...] = h * jnp.exp(-t * jnp.abs(dec_ref[...]))


def hyena_filter_taps(feats, w1p, b1, w2, b2, w3, freq, decay_cols):
    L = feats.shape[0]
    n = w3.shape[1]
    tl = min(L, 256)
    full = lambda a: pl.BlockSpec(a.shape, lambda i: (0, 0))
    return pl.pallas_call(
        _hy_ffn_kernel,
        out_shape=jax.ShapeDtypeStruct((L, n), F32),
        grid=(L // tl,),
        in_specs=[pl.BlockSpec((tl, feats.shape[1]), lambda i: (i, 0)),
                  full(w1p), full(b1), full(w2), full(b2), full(w3), full(freq), full(decay_cols)],
        out_specs=pl.BlockSpec((tl, n), lambda i: (i, 0)),
        compiler_params=_cparams(("parallel",)),
        name="hyena_filter_taps",
    )(feats, w1p, b1, w2, b2, w3, freq, decay_cols)


def _row0(shape, i):
    return (lax.broadcasted_iota(jnp.int32, shape, 0) + i * shape[0]) == 0


def _hy_spec_kernel(ce_ref, se_ref, co_ref, so_ref, hfe_ref, hbe_ref, hfo_ref, hbo_ref,
                    kpa_ref, kqa_ref, kpb_ref, kqb_ref, *, L):
    i = pl.program_id(1)
    hf_e, hf_o = hfe_ref[...], hfo_ref[...]
    hb_e, hb_o = hbe_ref[...], hbo_ref[...]
    m = lax.broadcasted_iota(jnp.int32, hb_e.shape, 0)
    hb_e = jnp.where(m == 0, 0.0, hb_e)
    s_e, s_o, d_e, d_o = hf_e + hb_e, hf_o + hb_o, hf_e - hb_e, hf_o - hb_o
    dot = lambda t, x: jnp.dot(t[...], x.astype(BF16), preferred_element_type=F32)
    pe, po, qe, qo = dot(ce_ref, s_e), dot(co_ref, s_o), dot(se_ref, d_e), dot(so_ref, d_o)
    r0 = _row0(pe.shape, i)
    p_half = jnp.sum(s_e * (1 - 2 * (m & 1)).astype(F32), axis=0, keepdims=True)
    scale = 1.0 / L
    kpa_ref[0] = (pe + po) * scale
    kpb_ref[0] = (pe - po) * scale
    kqa_ref[0] = jnp.where(r0, p_half, qe + qo) * scale
    kqb_ref[0] = jnp.where(r0, qo, qo - qe) * scale


def hyena_filter_spectrum(tables, taps, w):
    ce, se, co, so = tables[:4]
    H = ce.shape[0]
    tf = min(H, DFT_ROWS)
    out = jax.ShapeDtypeStruct((HY_ORDER, H, w), F32)
    tab = lambda: pl.BlockSpec((tf, H), lambda o, i: (i, 0))
    slot = lambda: pl.BlockSpec((1, tf, w), lambda o, i: (o, i, 0))
    tap = lambda parity, direction: pl.BlockSpec((H, w), lambda o, i: (parity, 2 * o + direction))
    return pl.pallas_call(
        functools.partial(_hy_spec_kernel, L=2 * H),
        out_shape=(out,) * 4,
        grid=(HY_ORDER, H // tf),
        in_specs=[tab(), tab(), tab(), tab(), tap(0, 0), tap(0, 1), tap(1, 0), tap(1, 1)],
        out_specs=(slot(),) * 4,
        compiler_params=_cparams(("parallel", "parallel")),
        name="hyena_filter_spectrum",
    )(ce, se, co, so, taps, taps, taps, taps)


def _hy_fwd_kernel(ce_ref, se_ref, co_ref, so_ref, ue_ref, uo_ref, kpa_ref, kqa_ref, kpb_ref, kqb_ref,
                   eep_ref, eeq_ref, eop_ref, eoq_ref):
    ue, uo = ue_ref[0], uo_ref[0]
    dot = lambda t, x: jnp.dot(t[...], x, preferred_element_type=F32)
    pe, qe, po, qo = dot(ce_ref, ue), dot(se_ref, ue), dot(co_ref, uo), dot(so_ref, uo)

    def emit(rows, pe, qe, po, qo, r0):
        sel = (lambda special, general: general) if r0 is None else (lambda special, general: jnp.where(r0, special, general))
        kpa, kqa, kpb, kqb = kpa_ref[0, rows, :], kqa_ref[0, rows, :], kpb_ref[0, rows, :], kqb_ref[0, rows, :]
        pa, pb = pe + po, pe - po
        qa, qb = sel(qe, qe + qo), sel(qo, qo - qe)
        ypa = sel(0.5 * pa * kpa, pa * kpa - qa * kqa)
        ypb = sel(0.5 * pb * kpb, pb * kpb - qb * kqb)
        yqa = sel(qa * kqa - qb * kqb, pa * kqa + qa * kpa)
        yqb = sel(qa * kqb + qb * kqa, pb * kqb + qb * kpb)
        eep_ref[0, rows, :] = (ypa + ypb).astype(eep_ref.dtype)
        eop_ref[0, rows, :] = (ypa - ypb).astype(eop_ref.dtype)
        eeq_ref[0, rows, :] = sel(yqa, yqa - yqb).astype(eeq_ref.dtype)
        eoq_ref[0, rows, :] = sel(yqb, yqa + yqb).astype(eoq_ref.dtype)

    emit(slice(None), pe, qe, po, qo, None)

    @pl.when(pl.program_id(1) == 0)
    def _():
        top = slice(0, BF16_TILE_ROWS)
        r0 = lax.broadcasted_iota(jnp.int32, (BF16_TILE_ROWS, pe.shape[1]), 0) == 0
        emit(top, pe[top], qe[top], po[top], qo[top], r0)


def hyena_forward(tables, u_e, u_o, u_col, ks, order):
    ce, se, co, so = tables[:4]
    bv, H, _ = u_e.shape
    w = ks[0].shape[2]
    tf = min(H, DFT_ROWS)
    out = jax.ShapeDtypeStruct((bv, H, w), BF16)
    tab = lambda: pl.BlockSpec((tf, H), lambda b, i: (i, 0))
    sig = lambda: pl.BlockSpec((1, H, w), lambda b, i: (b, 0, u_col))
    slot = lambda: pl.BlockSpec((1, tf, w), lambda b, i: (order, i, 0))
    res = lambda: pl.BlockSpec((1, tf, w), lambda b, i: (b, i, 0))
    return pl.pallas_call(
        _hy_fwd_kernel,
        out_shape=(out,) * 4,
        grid=(bv, H // tf),
        in_specs=[tab(), tab(), tab(), tab(), sig(), sig(), slot(), slot(), slot(), slot()],
        out_specs=(res(),) * 4,
        compiler_params=_cparams(("parallel", "parallel")),
        name="hyena_forward_dft",
    )(ce, se, co, so, u_e, u_o, *ks)


def _hy_inv_kernel(cet_ref, set_ref, cot_ref, sot_ref, eep_ref, eeq_ref, eop_ref, eoq_ref,
                   ue_ref, uo_ref, ge_ref, go_ref, bias_ref, *o_refs, interleave):
    dot = lambda t, x: jnp.dot(t[...], x[0], preferred_element_type=F32)
    ye = dot(cet_ref, eep_ref) + dot(set_ref, eeq_ref)
    yo = dot(cot_ref, eop_ref) + dot(sot_ref, eoq_ref)
    bias = bias_ref[0]
    oe = ge_ref[0].astype(F32) * (ye + ue_ref[0].astype(F32) * bias)
    oo = go_ref[0].astype(F32) * (yo + uo_ref[0].astype(F32) * bias)
    if interleave:
        o_ref, mix_ref = o_refs
        n = oe.shape[0]
        for c in range(oe.shape[1] // LANES):
            lanes = slice(c * LANES, (c + 1) * LANES)
            mix_ref[c, pl.ds(0, n, stride=2), :] = oe[:, lanes]
            mix_ref[c, pl.ds(1, n, stride=2), :] = oo[:, lanes]
            o_ref[0, :, lanes] = mix_ref[c].astype(o_ref.dtype)
    else:
        oe_ref, oo_ref = o_refs
        oe_ref[0] = oe.astype(oe_ref.dtype)
        oo_ref[0] = oo.astype(oo_ref.dtype)


def hyena_inverse(tables, es, u_e, u_o, u_col, g_e, g_o, g_col, bias, order, interleave):
    cet, set_, cot, sot = tables[0], tables[4], tables[5], tables[6]
    bv, H, w = es[0].shape
    tt = min(H, DFT_ROWS)
    tab = lambda: pl.BlockSpec((tt, H), lambda b, i: (i, 0))
    spec = lambda: pl.BlockSpec((1, H, w), lambda b, i: (b, 0, 0))
    tile = lambda col: pl.BlockSpec((1, tt, w), lambda b, i: (b, i, col))
    if interleave:
        out_shape = jax.ShapeDtypeStruct((bv, 2 * H, w), BF16)
        out_specs = pl.BlockSpec((1, 2 * tt, w), lambda b, i: (b, i, 0))
        scratch = [pltpu.VMEM((w // LANES, 2 * tt, LANES), F32)]
    else:
        out_shape = (jax.ShapeDtypeStruct((bv, H, w), BF16),) * 2
        out_specs = (tile(0), tile(0))
        scratch = []
    return pl.pallas_call(
        functools.partial(_hy_inv_kernel, interleave=interleave),
        out_shape=out_shape,
        grid=(bv, H // tt),
        in_specs=[tab(), tab(), tab(), tab(), spec(), spec(), spec(), spec(),
                  tile(u_col), tile(u_col), tile(g_col), tile(g_col),
                  pl.BlockSpec((1, 1, w), lambda b, i: (order, 0, 0))],
        out_specs=out_specs,
        scratch_shapes=scratch,
        compiler_params=_cparams(("parallel", "parallel")),
        name="hyena_inverse_dft",
    )(cet, set_, cot, sot, *es, u_e, u_o, g_e, g_o, bias)


def dft_tables(L):
    H = L // 2
    r = jnp.arange(H, dtype=jnp.int32)[:, None]
    c = jnp.arange(H, dtype=jnp.int32)[None, :]
    ang = lambda k: (k % (2 * L)).astype(F32) * (math.pi / L)
    alt_r, alt_c = (1 - 2 * (r % 2)).astype(F32), (1 - 2 * (c % 2)).astype(F32)
    a_e, a_o = ang(2 * r * c), ang(r * (2 * c + 1))
    bf = lambda a: a.astype(BF16)
    ce = bf(jnp.cos(a_e))
    se = bf(jnp.where(r == 0, alt_c, jnp.sin(a_e)))
    co = bf(jnp.cos(a_o))
    so = bf(jnp.where(r == 0, alt_c, jnp.sin(a_o)))
    set_ = bf(jnp.where(c == 0, alt_r, jnp.sin(a_e)))
    return ce, se, co, so, set_, co.T, so.T


def hyena_features(L):
    pos = jnp.arange(L, dtype=F32)
    t = pos / (L - 1)
    bands = jnp.linspace(1e-4, HY_BANDS - 1, HY_BANDS, dtype=F32)
    ang = (2.0 * math.pi / L) * pos[:, None] * bands[None, :]
    feats = jnp.concatenate([t[:, None], jnp.cos(ang), jnp.sin(ang)], axis=-1)
    feats = jnp.concatenate([feats[0::2], feats[1::2]], axis=0)
    return jnp.pad(feats, ((0, 0), (0, LANES - HY_FEAT)))


def hyena_branch(z_e, z_o, tables, feats, ffn, bias):
    w = bias.shape[-1]
    taps = hyena_filter_taps(feats, *ffn)
    ks = hyena_filter_spectrum(tables, taps, w)
    bias3 = bias.reshape(HY_ORDER, 1, w)
    es = hyena_forward(tables, z_e, z_o, 0, ks, 0)
    y1_e, y1_o = hyena_inverse(tables, es, z_e, z_o, 0, z_e, z_o, 1, bias3, 0, False)
    es = hyena_forward(tables, y1_e, y1_o, 0, ks, 1)
    return hyena_inverse(tables, es, y1_e, y1_o, 0, z_e, z_o, 2, bias3, 1, True)


def _mlstm_chunk(q, k, v1t, li_row, b_row, r_col, c, m, reverse):
    T = q.shape[0]
    s_idx = lax.broadcasted_iota(jnp.int32, (T, T), 0)
    t_idx = lax.broadcasted_iota(jnp.int32, (T, T), 1)
    mask = (s_idx >= t_idx) if reverse else (s_idx <= t_idx)
    nt = (((1,), (1,)), ((), ()))
    kq = lax.dot_general(k, q, nt, preferred_element_type=F32)
    d = jnp.where(mask, r_col + b_row, -1e30)
    m_loc = jnp.max(d, axis=0, keepdims=True)
    st = (kq * jnp.exp(d - m_loc)).astype(BF16)
    intra = jnp.dot(v1t, st, preferred_element_type=F32)
    g = b_row + m
    m_t = jnp.maximum(g, m_loc)
    inter = lax.dot_general(c.astype(BF16), q, nt, preferred_element_type=F32)
    num = jnp.exp(g - m_t) * inter + jnp.exp(m_loc - m_t) * intra
    den = num[HEAD_PAD - 1:HEAD_PAD, :]
    h = num * (1.0 / jnp.maximum(jnp.abs(den), jnp.exp(-m_t)))
    btot = b_row[:, 0:1] if reverse else b_row[:, T - 1:T]
    a = btot - b_row + li_row
    a_max = jnp.max(a, axis=1, keepdims=True)
    wv = (v1t.astype(F32) * jnp.exp(a - a_max)).astype(BF16)
    delta = jnp.dot(wv, k, preferred_element_type=F32)
    m_new = jnp.maximum(btot + m, a_max)
    c_new = jnp.exp(btot + m - m_new) * c + jnp.exp(a_max - m_new) * delta
    return h, c_new, m_new


def _mlstm_kernel(qc_ref, kc_ref, vc_ref, oc_ref, gcc_ref, grc_ref,
                  ql_ref, kl_ref, vl_ref, ol_ref, gcl_ref, grl_ref, ng_ref,
                  outc_ref, outl_ref, hf_ref, hb_ref, *, dh):
    T = ML_CHUNK
    L = ql_ref.shape[1]
    row = lax.broadcasted_iota(jnp.int32, (HEAD_PAD, T), 0)
    ones_row = row == HEAD_PAD - 1
    valid = row < dh
    ng = ng_ref[...]

    def gates(gc, gr, bwd):
        i = 2 if bwd else 0
        return gr[i:i + 1, :], gr[i + 1:i + 2, :], gc[:, i:i + 1] - gc[:, i + 1:i + 2]

    def finish(h, o):
        h = jnp.where(valid, h, 0.0)
        mu = jnp.sum(h, axis=0, keepdims=True) * (1.0 / dh)
        dlt = jnp.where(valid, h - mu, 0.0)
        var = jnp.sum(dlt * dlt, axis=0, keepdims=True) * (1.0 / dh)
        return (dlt * lax.rsqrt(var + LN_EPS) * ng * o.astype(F32)).astype(BF16)

    def aug(vt):
        return jnp.where(ones_row, jnp.ones_like(vt), vt)

    zero_c = jnp.zeros((HEAD_PAD, HEAD_PAD), F32)
    zero_m = jnp.zeros((1, 1), F32)

    q, k, v1t = qc_ref[0], kc_ref[0], aug(vc_ref[0])
    gc, gr = gcc_ref[0, 0], grc_ref[0, 0]
    h_f, c_f, m_f = _mlstm_chunk(q, k, v1t, *gates(gc, gr, False), zero_c, zero_m, False)
    h_b, c_b, m_b = _mlstm_chunk(q, k, v1t, *gates(gc, gr, True), zero_c, zero_m, True)
    outc_ref[0] = finish(h_f + h_b, oc_ref[0])

    nc = L // T
    for ci in range(nc):
        for bwd in (False, True):
            lo = (nc - 1 - ci) * T if bwd else ci * T
            gts = gates(gcl_ref[0, 0, lo:lo + T, :], grl_ref[0, 0, :, lo:lo + T], bwd)
            q, k, v1t = ql_ref[0, lo:lo + T, :], kl_ref[0, lo:lo + T, :], aug(vl_ref[0, :, lo:lo + T])
            if bwd:
                h_b, c_b, m_b = _mlstm_chunk(q, k, v1t, *gts, c_b, m_b, True)
                hb_ref[:, lo:lo + T] = h_b
            else:
                h_f, c_f, m_f = _mlstm_chunk(q, k, v1t, *gts, c_f, m_f, False)
                hf_ref[:, lo:lo + T] = h_f
    for ci in range(nc):
        lo = ci * T
        outl_ref[0, :, lo:lo + T] = finish(hf_ref[:, lo:lo + T] + hb_ref[:, lo:lo + T], ol_ref[0, :, lo:lo + T])


def mlstm_branch(qk_c, vt_c, ot_c, gate_c, qk_l, vt_l, ot_l, gate_l, norm_g_col, dh):
    bv, lc, _ = qk_c.shape
    L = qk_l.shape[1]
    H = ML_HEADS
    assert lc == ML_CHUNK and L % ML_CHUNK == 0

    def per_head(g):
        g = g[:, :, :4 * H].reshape(bv, -1, 4, H)
        return g.transpose(0, 3, 1, 2), g.transpose(0, 3, 2, 1)

    gcc, grc = per_head(gate_c)
    gcl, grl = per_head(gate_l)

    def qk_blk(length, part):
        return pl.BlockSpec((1, length, HEAD_PAD), lambda b, h: (b, 0, part * H + h))

    def t_blk(length):
        return pl.BlockSpec((1, HEAD_PAD, length), lambda b, h: (b, h, 0))

    def gspecs(length):
        return [pl.BlockSpec((1, 1, length, 4), lambda b, h: (b, h, 0, 0)),
                pl.BlockSpec((1, 1, 4, length), lambda b, h: (b, h, 0, 0))]

    def specs(length):
        return [qk_blk(length, 0), qk_blk(length, 1), t_blk(length), t_blk(length)] + gspecs(length)

    return pl.pallas_call(
        functools.partial(_mlstm_kernel, dh=dh),
        out_shape=(jax.ShapeDtypeStruct((bv, H * HEAD_PAD, lc), BF16),
                   jax.ShapeDtypeStruct((bv, H * HEAD_PAD, L), BF16)),
        grid=(bv, H),
        in_specs=specs(lc) + specs(L) + [pl.BlockSpec((HEAD_PAD, 1), lambda b, h: (h, 0))],
        out_specs=(t_blk(lc), t_blk(L)),
        scratch_shapes=[pltpu.VMEM((HEAD_PAD, L), F32), pltpu.VMEM((HEAD_PAD, L), F32)],
        compiler_params=_cparams(("parallel", "parallel")),
        name="mlstm_scan",
    )(qk_c, qk_c, vt_c, ot_c, gcc, grc, qk_l, qk_l, vt_l, ot_l, gcl, grl, norm_g_col)


def _s5_kernel(uc_ref, ul_ref, prow_ref, pcol_ref, bt_ref, cr_ref, cc_ref, y_ref,
               toep_ref, min_ref, mout_ref, kv_ref, s_ref, x_ref, u_ref, yv_ref, *, nb, nctx, nchunk):
    T, CG, P = S5_CHUNK, S5_GROUP, S5_STATE

    ch0 = 0
    for src in (uc_ref, ul_ref):
        xs = pltpu.einshape("bct->cbt", src[...].astype(F32))
        for ch in range(xs.shape[2] // T):
            for ci in range(CG):
                u_ref[(ch0 + ch) * nb:(ch0 + ch + 1) * nb, ci * T:(ci + 1) * T] = (
                    xs[ci, :, ch * T:(ch + 1) * T].astype(BF16))
        ch0 += xs.shape[2] // T

    hp = lax.Precision.HIGHEST
    lane = lax.broadcasted_iota(jnp.int32, (1, 2 * P), 1)
    re_lane = lane < P
    re_row = lax.broadcasted_iota(jnp.int32, (2 * P, 1), 0) < P

    def powers_rows(n, mag1, th1):
        m = jnp.exp(n * mag1)
        return m * jnp.cos(n * th1), m * jnp.sin(n * th1)

    def powers_cols(n, mag1, th1):
        m = jnp.exp(n * mag1)
        return m * jnp.cos(n * th1), m * jnp.sin(n * th1)

    at = []
    kv = jnp.zeros((CG * CG, 2 * T), F32)
    for d in range(2):
        a_re, a_im, ldt = prow_ref[0, d, 0:1, :], prow_ref[0, d, 1:2, :], prow_ref[0, d, 2:3, :]
        dt = jnp.exp(ldt)
        mag1, th1 = dt * a_re, dt * a_im
        ab_re, ab_im = jnp.exp(mag1) * jnp.cos(th1), jnp.exp(mag1) * jnp.sin(th1)
        den = a_re * a_re + a_im * a_im
        co_re = ((ab_re - 1.0) * a_re + ab_im * a_im) / den
        co_im = (ab_im * a_re - (ab_re - 1.0) * a_im) / den
        b_re, b_im = bt_ref[0, d, 0], bt_ref[0, d, 1]
        bb_re = co_re * b_re - co_im * b_im
        bb_im = co_re * b_im + co_im * b_re
        c_re, c_im = cr_ref[0, d, 0], cr_ref[0, d, 1]
        a_re_c, a_im_c = pcol_ref[0, d, :, 0:1], pcol_ref[0, d, :, 1:2]
        dt_c = jnp.exp(pcol_ref[0, d, :, 2:3])
        mag1_c, th1_c = dt_c * a_re_c, dt_c * a_im_c

        rep = lambda a: jnp.broadcast_to(a[:, None, :], (CG, CG, 2 * P)).reshape(CG * CG, 2 * P)
        til = lambda a: jnp.broadcast_to(a[None, :, :], (CG, CG, 2 * P)).reshape(CG * CG, 2 * P)
        w_re = rep(bb_re) * til(c_re) - rep(bb_im) * til(c_im)
        w_im = rep(bb_re) * til(c_im) + rep(bb_im) * til(c_re)
        w = jnp.where(re_lane, w_re, -w_im)
        lagp = lax.broadcasted_iota(jnp.int32, (1, 2 * T), 1)
        n = (T - 1 - lagp) if d else (lagp - (T - 1))
        ok = (n >= 0) & (lagp < 2 * T - 1)
        pc, ps = powers_cols(jnp.maximum(n, 0).astype(F32), mag1_c, th1_c)
        pw = jnp.where(ok, jnp.where(re_row, pc, ps), 0.0)
        kv = kv + jnp.dot(w, pw, precision=hp, preferred_element_type=F32)

        s_col = lax.broadcasted_iota(jnp.int32, (T, 1), 0)
        pc, ps = powers_rows((s_col if d else T - 1 - s_col).astype(F32), mag1, th1)
        for ci in range(CG):
            br, bi = bb_re[ci:ci + 1, :], bb_im[ci:ci + 1, :]
            blk = jnp.where(re_lane, br * pc - bi * ps, br * ps + bi * pc)
            min_ref[d, ci * T:(ci + 1) * T, :] = blk.astype(BF16)

        t_row = lax.broadcasted_iota(jnp.int32, (1, T), 1)
        pc, ps = powers_cols((T - t_row if d else t_row + 1).astype(F32), mag1_c, th1_c)
        for co in range(CG):
            cr, ci_ = cc_ref[0, d, 0, :, co:co + 1], cc_ref[0, d, 1, :, co:co + 1]
            blk = jnp.where(re_row, cr * pc - ci_ * ps, -(cr * ps + ci_ * pc))
            mout_ref[d, :, co * T:(co + 1) * T] = blk.astype(BF16)

        mt = jnp.exp(T * mag1)
        at.append(((mt * jnp.cos(T * th1))[:, :P], (mt * jnp.sin(T * th1))[:, :P]))

    kv_ref[...] = kv

    def build(ci, carry):
        r0 = pl.multiple_of(ci * T, T)
        for co in range(CG):
            vec = kv_ref[pl.ds(ci * CG + co, 1), :]
            rolled = pltpu.roll(jnp.broadcast_to(vec, (T, 2 * T)), T + 1, 1, stride=1, stride_axis=0)
            toep_ref[pl.ds(r0, T), co * T:(co + 1) * T] = rolled[:, :T].astype(BF16)
        return carry

    lax.fori_loop(0, CG, build, 0)

    u = u_ref[...]
    y = jnp.dot(u, toep_ref[...], preferred_element_type=F32)
    for d in range(2):
        s_ref[d] = jnp.dot(u, min_ref[d], preferred_element_type=F32)

    orders = (list(range(nchunk)), list(range(nctx - 1, -1, -1)) + list(range(nchunk - 1, nctx - 1, -1)))
    for d in range(2):
        at_re, at_im = at[d]
        x_re = jnp.zeros((nb, P), F32)
        x_im = jnp.zeros((nb, P), F32)
        for ch in orders[d]:
            x_ref[d, ch * nb:(ch + 1) * nb, :] = jnp.concatenate([x_re, x_im], axis=1)
            s = s_ref[d, ch * nb:(ch + 1) * nb, :]
            x_re, x_im = (at_re * x_re - at_im * x_im + s[:, :P], at_re * x_im + at_im * x_re + s[:, P:])
        y = y + jnp.dot(x_ref[d].astype(BF16), mout_ref[d], preferred_element_type=F32)

    for ch in range(nchunk):
        for co in range(CG):
            yv_ref[co, :, ch * T:(ch + 1) * T] = y[ch * nb:(ch + 1) * nb, co * T:(co + 1) * T]
    y_ref[...] = pltpu.einshape("cbt->bct", yv_ref[...]).astype(y_ref.dtype)


def s5_mix(ut_c, ut_l, a_re, a_im, log_dt, b_re, b_im, c_re, c_im):
    nb, w, lc = ut_c.shape
    L = ut_l.shape[2]
    T, CG, P = S5_CHUNK, S5_GROUP, S5_STATE
    G = w // CG
    nctx, nlat = lc // T, L // T
    nchunk = nctx + nlat

    dup = lambda a: jnp.concatenate([a, a], axis=-1)
    f32 = lambda a: a.astype(F32)
    ldt = jnp.broadcast_to(f32(log_dt)[:, :, None], (2, G, P))
    prow = jnp.stack([dup(f32(a_re)), dup(f32(a_im)), dup(ldt)], axis=2)
    prow = jnp.pad(prow, ((0, 0), (0, 0), (0, 5), (0, 0))).transpose(1, 0, 2, 3)
    pcol = prow.transpose(0, 1, 3, 2)
    bt = jnp.stack([dup(f32(b_re).transpose(0, 1, 3, 2)), dup(f32(b_im).transpose(0, 1, 3, 2))], axis=2)
    bt = bt.transpose(1, 0, 2, 3, 4)
    cr = jnp.stack([dup(f32(c_re)), dup(f32(c_im))], axis=2).transpose(1, 0, 2, 3, 4)
    cc = cr.transpose(0, 1, 2, 4, 3)

    blk = lambda a: pl.BlockSpec((1,) + a.shape[1:], lambda g: (g,) + (0,) * (a.ndim - 1))
    grp = lambda length: pl.BlockSpec((nb, CG, length), lambda g: (0, g, 0))
    r = nchunk * nb
    return pl.pallas_call(
        functools.partial(_s5_kernel, nb=nb, nctx=nctx, nchunk=nchunk),
        out_shape=jax.ShapeDtypeStruct((nb, w, lc + L), BF16),
        grid=(G,),
        in_specs=[grp(lc), grp(L), blk(prow), blk(pcol), blk(bt), blk(cr), blk(cc)],
        out_specs=grp(lc + L),
        scratch_shapes=[pltpu.VMEM((CG * T, CG * T), BF16), pltpu.VMEM((2, CG * T, 2 * P), BF16),
                        pltpu.VMEM((2, 2 * P, CG * T), BF16), pltpu.VMEM((CG * CG, 2 * T), F32),
                        pltpu.VMEM((2, r, 2 * P), F32), pltpu.VMEM((2, r, 2 * P), F32),
                        pltpu.VMEM((r, CG * T), BF16), pltpu.VMEM((CG, nb, lc + L), F32)],
        compiler_params=_cparams(("parallel",)),
        name="s5_mix",
    )(ut_c, ut_l, prow, pcol, bt, cr, cc)


def _gelu_tanh(x):
    return 0.5 * x * (1.0 + jnp.tanh(math.sqrt(2.0 / math.pi) * (x + 0.044715 * (x * x * x))))


def _merge_kernel(yhy_ref, yml_ref, ys_ref, u_ref, ghy_ref, gml_ref, gs5_ref,
                  d_ref, gw_ref, gb_ref, why_ref, wml_ref, ws5_ref, o_ref):
    tn = (((0,), (0,)), ((), ()))
    ys = ys_ref[0].astype(F32) + d_ref[...] * u_ref[0].astype(F32)
    z = _gelu_tanh(ys).astype(BF16)
    glu = jnp.dot(gw_ref[...], z, preferred_element_type=F32) + gb_ref[...]
    y_s5 = (z.astype(F32) * _sigmoid(glu)).astype(BF16)
    acc = ghy_ref[0].astype(F32) * jnp.dot(yhy_ref[0], why_ref[...], preferred_element_type=F32)
    acc = acc + gml_ref[0].astype(F32) * lax.dot_general(yml_ref[0], wml_ref[...], tn, preferred_element_type=F32)
    acc = acc + gs5_ref[0].astype(F32) * lax.dot_general(y_s5, ws5_ref[...], tn, preferred_element_type=F32)
    o_ref[0] = acc.astype(o_ref.dtype)


def merge_branches(y_hy, y_ml, ys_t, u_t, t_off, gate_arr, s5_d, glu_wt, glu_b, w_hy, w_ml, w_s5):
    bv, lv, w = y_hy.shape
    dm = w_hy.shape[1]
    tm = ML_CHUNK
    full = lambda a: pl.BlockSpec(a.shape, lambda b, i: (0,) * a.ndim)
    gate = lambda c: pl.BlockSpec((1, tm, dm), lambda b, i: (b, i, c))
    return pl.pallas_call(
        _merge_kernel,
        out_shape=jax.ShapeDtypeStruct((bv, lv, dm), BF16),
        grid=(bv, lv // tm),
        in_specs=[pl.BlockSpec((1, tm, w), lambda b, i: (b, i, 0)),
                  pl.BlockSpec((1, y_ml.shape[1], tm), lambda b, i: (b, 0, i)),
                  pl.BlockSpec((1, w, tm), lambda b, i: (b, 0, i + t_off)),
                  pl.BlockSpec((1, w, tm), lambda b, i: (b, 0, i)),
                  gate(0), gate(1), gate(2),
                  full(s5_d), full(glu_wt), full(glu_b), full(w_hy), full(w_ml), full(w_s5)],
        out_specs=pl.BlockSpec((1, tm, dm), lambda b, i: (b, i, 0)),
        compiler_params=_cparams(("parallel", "parallel")),
        name="merge_branches",
    )(y_hy, y_ml, ys_t, u_t, gate_arr, gate_arr, gate_arr, s5_d, glu_wt, glu_b, w_hy, w_ml, w_s5)


def _outproj_kernel(m_ref, w_ref, x_ref, gate_ref, g_ref, b_ref, o_ref, *, alpha):
    rb = min(m_ref.shape[1], LANES)
    for s in range(m_ref.shape[1] // rb):
        rows = slice(s * rb, (s + 1) * rb)
        y = jnp.dot(m_ref[0, rows, :], w_ref[...], preferred_element_type=F32)
        r = alpha * x_ref[0, rows, :] + gate_ref[0] * y
        o_ref[0, rows, :] = _layer_norm(r, g_ref[...], b_ref[...])


def out_projection(merged, w_out, x, gate, ln_g, ln_b, alpha, tm):
    bv, lv, d = x.shape
    bm = gate.shape[0]
    mod_map = (lambda b, i: (b, 0, 0)) if bm == bv else (lambda b, i: (0, 0, 0))
    vec = lambda: pl.BlockSpec((1, d), lambda b, i: (0, 0))
    return pl.pallas_call(
        functools.partial(_outproj_kernel, alpha=alpha),
        out_shape=jax.ShapeDtypeStruct((bv, lv, d), F32),
        grid=(bv, lv // tm),
        in_specs=[pl.BlockSpec((1, tm, d), lambda b, i: (b, i, 0)),
                  pl.BlockSpec((d, d), lambda b, i: (0, 0)),
                  pl.BlockSpec((1, tm, d), lambda b, i: (b, i, 0)),
                  pl.BlockSpec((1, 1, d), mod_map), vec(), vec()],
        out_specs=pl.BlockSpec((1, tm, d), lambda b, i: (b, i, 0)),
        compiler_params=_cparams(("parallel", "parallel")),
        name="out_projection_ln",
    )(merged, w_out, x, gate, ln_g, ln_b)


def _mlp_kernel(x_ref, sh_ref, sc_ref, gate_ref, w1_ref, w2_ref, g_ref, b_ref, o_ref, xm_ref, acc_ref, *, alpha):
    k = pl.program_id(2)
    last = pl.num_programs(2) - 1
    tm = xm_ref.shape[0]
    rb = min(tm, MXU_DIM)
    blocks = [slice(s * rb, (s + 1) * rb) for s in range(tm // rb)]

    def ffn(rows):
        h = jnp.maximum(jnp.dot(xm_ref[rows, :], w1_ref[...], preferred_element_type=F32), 0.0)
        return jnp.dot((h * h).astype(BF16), w2_ref[...], preferred_element_type=F32)

    @pl.when(k == 0)
    def _():
        for rows in blocks:
            _modulate_rows(xm_ref, x_ref, sh_ref, sc_ref, rows)
            acc_ref[rows, :] = ffn(rows)

    @pl.when((k > 0) & (k < last))
    def _():
        acc_ref[...] += ffn(slice(None))

    @pl.when(k == last)
    def _():
        for rows in blocks:
            r = alpha * x_ref[0, rows, :] + gate_ref[0] * (acc_ref[rows, :] + ffn(rows))
            o_ref[0, rows, :] = _layer_norm(r, g_ref[...], b_ref[...])


def mlp_block(x, shift, scale, gate, w1, w2, ln_g, ln_b, alpha, tm, kf):
    bv, lv, d = x.shape
    dff = w1.shape[1]
    assert dff // kf >= 2
    bm = gate.shape[0]
    mod_map = (lambda b, i, k: (b, 0, 0)) if bm == bv else (lambda b, i, k: (0, 0, 0))
    vec = lambda: pl.BlockSpec((1, d), lambda b, i, k: (0, 0))
    return pl.pallas_call(
        functools.partial(_mlp_kernel, alpha=alpha),
        out_shape=jax.ShapeDtypeStruct((bv, lv, d), F32),
        grid=(bv, lv // tm, dff // kf),
        in_specs=[pl.BlockSpec((1, tm, d), lambda b, i, k: (b, i, 0)),
                  pl.BlockSpec((1, 1, d), mod_map), pl.BlockSpec((1, 1, d), mod_map),
                  pl.BlockSpec((1, 1, d), mod_map),
                  pl.BlockSpec((d, kf), lambda b, i, k: (0, k)),
                  pl.BlockSpec((kf, d), lambda b, i, k: (k, 0)),
                  vec(), vec()],
        out_specs=pl.BlockSpec((1, tm, d), lambda b, i, k: (b, i, 0)),
        scratch_shapes=[pltpu.VMEM((tm, d), BF16), pltpu.VMEM((tm, d), F32)],
        compiler_params=_cparams(("parallel", "parallel", "arbitrary")),
        name="mlp_ln",
    )(x, shift, scale, gate, w1, w2, ln_g, ln_b)


def _pad_heads(a, axis, dh):
    shp = a.shape
    a = a.reshape(shp[:axis] + (ML_HEADS, dh) + shp[axis + 1:])
    pad = [(0, 0)] * a.ndim
    pad[axis + 1] = (0, HEAD_PAD - dh)
    a = jnp.pad(a, pad)
    return a.reshape(shp[:axis] + (ML_HEADS * HEAD_PAD,) + shp[axis + 1:])


def _layer_params(l, p, w_hy, w_ml, w_s5, d_model):
    dh = w_ml // ML_HEADS
    sizes = ((HY_ORDER + 1) * w_hy, 2 * w_ml, w_ml, w_ml, 4 * ML_HEADS, w_s5, N_BRANCH * d_model)
    pts = [0]
    for s in sizes:
        pts.append(pts[-1] + s)
    w_in = p["w_in"][l]
    hy, qk, v, o, gt, u, mg = (w_in[:, pts[i]:pts[i + 1]] for i in range(7))
    ph = lambda a: _pad_heads(a, a.ndim - 1, dh)
    none = lambda n: (jnp.zeros((3, n), F32), jnp.zeros((1, n), F32), jnp.ones((1, n), F32))
    half = ML_HEADS * HEAD_PAD
    cw, cb = p["ml_conv_w"][l], p["ml_conv_b"][l]
    out = {}
    out["mg"] = (mg.astype(BF16),) + none(mg.shape[1])
    out["hy"] = (hy.astype(BF16), p["hy_conv_w"][l], p["hy_conv_b"][l].reshape(1, -1), jnp.ones((1, hy.shape[1]), F32))
    out["qk"] = (jnp.concatenate([ph(qk[:, :w_ml]), ph(qk[:, w_ml:])], axis=1).astype(BF16),
                 jnp.concatenate([ph(cw[:, :w_ml]), ph(cw[:, w_ml:])], axis=1),
                 jnp.concatenate([ph(cb[:w_ml]), ph(cb[w_ml:])]).reshape(1, -1),
                 jnp.concatenate([jnp.ones((half,), F32), jnp.full((half,), dh ** -0.5, F32)]).reshape(1, -1))
    out["vt"] = (ph(v).T.astype(BF16),) + none(half)
    out["ot"] = (ph(o).T.astype(BF16),) + none(half)
    ngt = gt.shape[1]
    out["ut"] = (u.T.astype(BF16),) + none(u.shape[1])
    out["gt"] = (jnp.pad(gt, ((0, 0), (0, LANES - ngt))).astype(BF16),
                 jnp.pad(p["ml_gate_b"][l].reshape(-1), ((0, LANES - ngt),)).reshape(1, -1))
    out["norm_g"] = ph(p["ml_norm_g"][l]).reshape(-1, 1)
    out["w_ml_out"] = _pad_heads(p["w_ml_out"][l], 0, dh).astype(BF16)
    w3 = p["hy_ffn_w3"][l]
    dec = p["hy_decay"][l]
    dec_cols = jnp.broadcast_to(dec[:, None, :], (HY_ORDER, 2, w_hy)).reshape(1, -1)
    hpad = LANES - HY_HIDDEN
    row = lambda a: jnp.pad(a.reshape(1, -1), ((0, 0), (0, hpad)))
    out["hy_ffn"] = (jnp.pad(p["hy_ffn_w1"][l], ((0, LANES - HY_FEAT), (0, hpad))), row(p["hy_ffn_b1"][l]),
                     jnp.pad(p["hy_ffn_w2"][l], ((0, hpad), (0, hpad))), row(p["hy_ffn_b2"][l]),
                     jnp.pad(w3, ((0, hpad), (0, 0))), row(p["hy_sin_freq"][l]), dec_cols)
    return out


def kernel(x, c, ctx, c_ctx, w_mod, b_mod, w_in, hy_conv_w, hy_conv_b, hy_ffn_w1, hy_ffn_b1, hy_ffn_w2,
           hy_ffn_b2, hy_ffn_w3, hy_sin_freq, hy_decay, hy_bias, ml_conv_w, ml_conv_b, ml_gate_b, ml_norm_g,
           s5_a_re, s5_a_im, s5_log_dt, s5_b_re, s5_b_im, s5_c_re, s5_c_im, s5_d, s5_glu_w, s5_glu_b,
           w_hy_out, w_ml_out, w_s5_out, w_out, ln1_g, ln1_b, ln2_g, ln2_b, w_ff1, w_ff2):
    p = dict(w_in=w_in, hy_conv_w=hy_conv_w, hy_conv_b=hy_conv_b, hy_ffn_w1=hy_ffn_w1, hy_ffn_b1=hy_ffn_b1,
             hy_ffn_w2=hy_ffn_w2, hy_ffn_b2=hy_ffn_b2, hy_ffn_w3=hy_ffn_w3, hy_sin_freq=hy_sin_freq,
             hy_decay=hy_decay, ml_conv_w=ml_conv_w, ml_conv_b=ml_conv_b, ml_gate_b=ml_gate_b,
             ml_norm_g=ml_norm_g, w_ml_out=w_ml_out)
    B, L, D = x.shape
    LC = ctx.shape[1]
    depth = w_in.shape[0]
    w_hy = hy_bias.shape[-1]
    w_ml = ml_norm_g.shape[-1]
    w_s5 = s5_d.shape[-1]
    dh = w_ml // ML_HEADS
    alpha = (2 * depth) ** 0.25
    half = ML_HEADS * HEAD_PAD

    tab_l, tab_c = dft_tables(L), dft_tables(LC)
    feats_l, feats_c = hyena_features(L), hyena_features(LC)

    mrows = 8 * ((B + 1 + 7) // 8)
    c_rows = jnp.zeros((mrows, D), F32).at[:B].set(c).at[B].set(c_ctx)

    for l in range(depth):
        need_ctx = l < depth - 1
        lp = _layer_params(l, p, w_hy, w_ml, w_s5, D)
        mod = mod_vectors(c_rows, w_mod, b_mod.reshape(depth, 1, -1), l)
        mod_l = [mod[:B, k * D:(k + 1) * D].reshape(B, 1, D) for k in range(6)]
        mod_c = [mod[B:B + 1, k * D:(k + 1) * D].reshape(1, 1, D) for k in range(6)]

        def mixer_inputs(xv, md, row_len, full):
            bv, lv, _ = xv.shape
            tm = min(PROJ_ROWS, lv)
            proj = functools.partial(projection, xv, md[0], md[1], tm=tm, row_len=row_len)
            flat = md[0].shape[0] == 1
            xr = xv.reshape(1, bv * lv, D) if flat else xv
            tmr = min(PROJ_ROWS, xr.shape[1])
            rowp = functools.partial(projection, xr, md[0], md[1], tm=tmr, row_len=row_len)
            unflat = lambda a: a.reshape(bv, -1, a.shape[-1])
            r = {}
            r["qk"] = unflat(rowp(*lp["qk"], kind="convsilu", tn=half))
            r["vt"] = proj(*lp["vt"], kind="plain", tn=half, transposed=True)
            r["ot"] = proj(*lp["ot"], kind="sigmoid", tn=half, transposed=True)
            r["ut"] = proj(*lp["ut"], kind="plain", tn=w_s5, transposed=True)
            r["g"] = gate_projection(xv, md[0], md[1], *lp["gt"], tm=tm)
            if full:
                r["mg"] = unflat(rowp(*lp["mg"], kind="sigmoid", tn=2 * w_hy))
                z_e, z_o = projection_even_odd(xr, md[0], md[1], *lp["hy"][:3], tn=w_hy, tm=tmr, row_len=row_len)
                r["z"] = (unflat(z_e), unflat(z_o))
            return r

        r_l = mixer_inputs(x, mod_l, GRID_W, True)
        r_c = mixer_inputs(ctx, mod_c, LC, need_ctx)

        hm_c, hm_l = mlstm_branch(r_c["qk"], r_c["vt"], r_c["ot"], r_c["g"],
                                  r_l["qk"], r_l["vt"], r_l["ot"], r_l["g"], lp["norm_g"], dh)

        ys_t = s5_mix(r_c["ut"], r_l["ut"], s5_a_re[l], s5_a_im[l], s5_log_dt[l],
                      s5_b_re[l], s5_b_im[l], s5_c_re[l], s5_c_im[l])

        s5_dv = s5_d[l].reshape(-1, 1)
        glu_wt = s5_glu_w[l].T.astype(BF16)
        glu_b = s5_glu_b[l].reshape(-1, 1)
        w_hy_o = w_hy_out[l].astype(BF16)
        w_s5_o = w_s5_out[l].astype(BF16)
        w_o = w_out[l].astype(BF16)
        w1 = w_ff1[l].astype(BF16)
        w2 = w_ff2[l].astype(BF16)
        g1, b1 = ln1_g[l].reshape(1, -1), ln1_b[l].reshape(1, -1)
        g2, b2 = ln2_g[l].reshape(1, -1), ln2_b[l].reshape(1, -1)

        def finish(xv, r, hm, t_off, tables, feats, md):
            tm = TOKEN_ROWS
            y_hy = hyena_branch(*r["z"], tables, feats, lp["hy_ffn"], hy_bias[l])
            merged = merge_branches(y_hy, hm, ys_t, r["ut"], t_off, r["mg"], s5_dv, glu_wt, glu_b,
                                    w_hy_o, lp["w_ml_out"], w_s5_o)
            if md[2].shape[0] == 1:
                merged, xv = merged.reshape(1, -1, D), xv.reshape(1, -1, D)
            x1 = out_projection(merged, w_o, xv, md[2], g1, b1, alpha, tm)
            return mlp_block(x1, md[3], md[4], md[5], w1, w2, g2, b2, alpha, tm, MLP_FF_CHUNK)

        x = finish(x, r_l, hm_l, LC // ML_CHUNK, tab_l, feats_l, mod_l)
        if need_ctx:
            ctx = finish(ctx, r_c, hm_c, 0, tab_c, feats_c, mod_c).reshape(B, LC, D)
    return x
```

```python
import functools
import math

import jax
import jax.numpy as jnp
from jax import lax
from jax.experimental import pallas as pl
from jax.experimental.pallas import tpu as pltpu

F32 = jnp.float32
BF16 = jnp.bfloat16

GRID_W = 64
HY_ORDER = 2
HY_BANDS = 16
HY_FEAT = 1 + 2 * HY_BANDS
HY_HIDDEN = 64
ML_HEADS = 4
ML_CHUNK = 256
S5_GROUP = 16
S5_STATE = 64
N_BRANCH = 3
LN_EPS = 1e-5

LANES = 128
BF16_TILE_ROWS = 16
MXU_DIM = 256
VMEM_LIMIT = 56 * 1024 * 1024

PROJ_ROWS = 1024
TOKEN_ROWS = 512
MLP_FF_CHUNK = 1024
MOD_COLS = 1024
DFT_ROWS = 512

HEAD_PAD = MXU_DIM
S5_CHUNK = LANES


def _cparams(sem):
    return pltpu.CompilerParams(dimension_semantics=sem, vmem_limit_bytes=VMEM_LIMIT)


def _sigmoid(x):
    return 0.5 * jnp.tanh(0.5 * x) + 0.5


def _layer_norm(r, g, b):
    mu = jnp.mean(r, axis=-1, keepdims=True)
    d = r - mu
    var = jnp.mean(d * d, axis=-1, keepdims=True)
    return d * lax.rsqrt(var + LN_EPS) * g + b


def _mod_kernel(c_ref, w_ref, b_ref, o_ref):
    c = c_ref[...]
    a = (c * _sigmoid(c)).astype(BF16)
    o_ref[...] = jnp.dot(a, w_ref[0].astype(BF16), preferred_element_type=F32) + b_ref[0]


def mod_vectors(c_rows, w, b, layer):
    m, d = c_rows.shape
    n = w.shape[2]
    tn = MOD_COLS
    return pl.pallas_call(
        _mod_kernel,
        out_shape=jax.ShapeDtypeStruct((m, n), F32),
        grid=(n // tn,),
        in_specs=[pl.BlockSpec((m, d), lambda j: (0, 0)),
                  pl.BlockSpec((1, d, tn), lambda j: (layer, 0, j)),
                  pl.BlockSpec((1, 1, tn), lambda j: (layer, 0, j))],
        out_specs=pl.BlockSpec((m, tn), lambda j: (0, j)),
        compiler_params=_cparams(("parallel",)),
        name="mod_vectors",
    )(c_rows, w, b)


def _short_conv(acc, w, b, row_len):
    rows = acc.shape[0]
    t = lax.broadcasted_iota(jnp.int32, acc.shape, 0) & (row_len - 1)
    prev = jnp.where(t == 0, 0.0, pltpu.roll(acc, 1, 0))
    nxt = jnp.where(t == row_len - 1, 0.0, pltpu.roll(acc, rows - 1, 0))
    return prev * w[0:1] + acc * w[1:2] + nxt * w[2:3] + b


def _modulate_rows(xm_ref, x_ref, sh_ref, sc_ref, rows):
    xm_ref[rows, :] = (x_ref[0, rows, :] * (1.0 + sc_ref[0]) + sh_ref[0]).astype(BF16)


def _modulate_into(xm_ref, x_ref, sh_ref, sc_ref):
    _modulate_rows(xm_ref, x_ref, sh_ref, sc_ref, slice(None))


def _proj_kernel(x_ref, sh_ref, sc_ref, w_ref, cw_ref, cb_ref, cs_ref, o_ref, xm_ref, *, kind, row_len, rb,
                 transposed, single_tile):
    def epilogue(acc):
        if kind == "conv":
            return _short_conv(acc, cw_ref[...], cb_ref[...], row_len)
        if kind == "convsilu":
            y = _short_conv(acc, cw_ref[...], cb_ref[...], row_len)
            return y * _sigmoid(y) * cs_ref[...]
        if kind == "sigmoid":
            return _sigmoid(acc)
        return acc

    def body(modulate):
        for r in range(xm_ref.shape[0] // rb):
            rows = slice(r * rb, (r + 1) * rb)
            if modulate:
                _modulate_rows(xm_ref, x_ref, sh_ref, sc_ref, rows)
            xs = xm_ref[rows, :]
            if transposed:
                acc = lax.dot_general(w_ref[...], xs, (((1,), (1,)), ((), ())), preferred_element_type=F32)
                o_ref[0, :, rows] = epilogue(acc).astype(o_ref.dtype)
            else:
                acc = jnp.dot(xs, w_ref[...], preferred_element_type=F32)
                o_ref[0, rows, :] = epilogue(acc).astype(o_ref.dtype)

    if single_tile:
        body(True)
    else:
        first = pl.program_id(2) == 0
        pl.when(first)(lambda: body(True))
        pl.when(jnp.logical_not(first))(lambda: body(False))


def projection(x, shift, scale, w, cw, cb, cs, *, kind, tn, tm, row_len, transposed=False):
    bv, lv, d = x.shape
    n = w.shape[0] if transposed else w.shape[1]
    rb = min(tm, ML_CHUNK)
    assert n % tn == 0 and lv % tm == 0 and rb % row_len == 0
    bm = shift.shape[0]
    mod_map = (lambda b, i, j: (b, 0, 0)) if bm == bv else (lambda b, i, j: (0, 0, 0))
    if transposed:
        assert kind in ("plain", "sigmoid")
        w_spec = pl.BlockSpec((tn, d), lambda b, i, j: (j, 0))
        out_shape = jax.ShapeDtypeStruct((bv, n, lv), BF16)
        out_spec = pl.BlockSpec((1, tn, tm), lambda b, i, j: (b, j, i))
    else:
        w_spec = pl.BlockSpec((d, tn), lambda b, i, j: (0, j))
        out_shape = jax.ShapeDtypeStruct((bv, lv, n), BF16)
        out_spec = pl.BlockSpec((1, tm, tn), lambda b, i, j: (b, i, j))
    col = lambda rows: pl.BlockSpec((rows, tn), lambda b, i, j: (0, j))
    return pl.pallas_call(
        functools.partial(_proj_kernel, kind=kind, row_len=row_len, rb=rb, transposed=transposed,
                          single_tile=(n == tn)),
        out_shape=out_shape,
        grid=(bv, lv // tm, n // tn),
        in_specs=[pl.BlockSpec((1, tm, d), lambda b, i, j: (b, i, 0)),
                  pl.BlockSpec((1, 1, d), mod_map), pl.BlockSpec((1, 1, d), mod_map),
                  w_spec, col(3), col(1), col(1)],
        out_specs=out_spec,
        scratch_shapes=[pltpu.VMEM((tm, d), BF16)],
        compiler_params=_cparams(("parallel", "parallel", "arbitrary")),
        name="projection_" + kind + ("_t" if transposed else ""),
    )(x, shift, scale, w, cw, cb, cs)


def _proj_eo_kernel(x_ref, sh_ref, sc_ref, w_ref, cw_ref, cb_ref, oe_ref, oo_ref, xm_ref, tmp_ref, *, row_len, rb):
    nlc = w_ref.shape[1] // LANES

    def body(modulate):
        for r in range(xm_ref.shape[0] // rb):
            rows = slice(r * rb, (r + 1) * rb)
            if modulate:
                _modulate_rows(xm_ref, x_ref, sh_ref, sc_ref, rows)
            acc = jnp.dot(xm_ref[rows, :], w_ref[...], preferred_element_type=F32)
            val = _short_conv(acc, cw_ref[...], cb_ref[...], row_len)
            half = slice(r * rb // 2, (r + 1) * rb // 2)
            for c in range(nlc):
                lanes = slice(c * LANES, (c + 1) * LANES)
                tmp_ref[r * nlc + c] = val[:, lanes]
                oe_ref[0, half, lanes] = tmp_ref[r * nlc + c, pl.ds(0, rb // 2, stride=2), :].astype(oe_ref.dtype)
                oo_ref[0, half, lanes] = tmp_ref[r * nlc + c, pl.ds(1, rb // 2, stride=2), :].astype(oo_ref.dtype)

    first = pl.program_id(2) == 0
    pl.when(first)(lambda: body(True))
    pl.when(jnp.logical_not(first))(lambda: body(False))


def projection_even_odd(x, shift, scale, w, cw, cb, *, tn, tm, row_len):
    bv, lv, d = x.shape
    n = w.shape[1]
    rb = min(tm, ML_CHUNK)
    assert n % tn == 0 and n > tn and lv % tm == 0 and rb % row_len == 0
    bm = shift.shape[0]
    mod_map = (lambda b, i, j: (b, 0, 0)) if bm == bv else (lambda b, i, j: (0, 0, 0))
    col = lambda rows: pl.BlockSpec((rows, tn), lambda b, i, j: (0, j))
    out = jax.ShapeDtypeStruct((bv, lv // 2, n), BF16)
    half = lambda: pl.BlockSpec((1, tm // 2, tn), lambda b, i, j: (b, i, j))
    return pl.pallas_call(
        functools.partial(_proj_eo_kernel, row_len=row_len, rb=rb),
        out_shape=(out, out),
        grid=(bv, lv // tm, n // tn),
        in_specs=[pl.BlockSpec((1, tm, d), lambda b, i, j: (b, i, 0)),
                  pl.BlockSpec((1, 1, d), mod_map), pl.BlockSpec((1, 1, d), mod_map),
                  pl.BlockSpec((d, tn), lambda b, i, j: (0, j)), col(3), col(1)],
        out_specs=(half(), half()),
        scratch_shapes=[pltpu.VMEM((tm, d), BF16), pltpu.VMEM((tm // rb * (tn // LANES), rb, LANES), F32)],
        compiler_params=_cparams(("parallel", "parallel", "arbitrary")),
        name="projection_conv_even_odd",
    )(x, shift, scale, w, cw, cb)


def _log_sigmoid(x):
    return jnp.minimum(x, 0.0) - jnp.log(1.0 + jnp.exp(-jnp.abs(x)))


def _gate_kernel(x_ref, sh_ref, sc_ref, wg_ref, gb_ref, g_ref, xm_ref):
    _modulate_into(xm_ref, x_ref, sh_ref, sc_ref)
    T = ML_CHUNK
    hp = lax.Precision.HIGHEST
    r = lax.broadcasted_iota(jnp.int32, (T, T), 0)
    s = lax.broadcasted_iota(jnp.int32, (T, T), 1)
    tri_f = (s <= r).astype(F32)
    tri_b = (s >= r).astype(F32)
    kind = lax.broadcasted_iota(jnp.int32, (T, LANES), 1) // ML_HEADS
    for c in range(xm_ref.shape[0] // T):
        g = jnp.dot(xm_ref[c * T:(c + 1) * T, :], wg_ref[...], preferred_element_type=F32) + gb_ref[...]
        ls = _log_sigmoid(g)
        cum_f = jnp.dot(tri_f, ls, precision=hp, preferred_element_type=F32)
        cum_b = jnp.dot(tri_b, ls, precision=hp, preferred_element_type=F32)
        g_ref[0, c * T:(c + 1) * T, :] = jnp.where(kind == 1, cum_f, jnp.where(kind == 3, cum_b, g))


def gate_projection(x, shift, scale, wg, gb, *, tm):
    bv, lv, d = x.shape
    assert lv % tm == 0 and tm % ML_CHUNK == 0
    bm = shift.shape[0]
    mod_map = (lambda b, i: (b, 0, 0)) if bm == bv else (lambda b, i: (0, 0, 0))
    full = lambda a: pl.BlockSpec(a.shape, lambda b, i: (0, 0))
    return pl.pallas_call(
        _gate_kernel,
        out_shape=jax.ShapeDtypeStruct((bv, lv, LANES), F32),
        grid=(bv, lv // tm),
        in_specs=[pl.BlockSpec((1, tm, d), lambda b, i: (b, i, 0)),
                  pl.BlockSpec((1, 1, d), mod_map), pl.BlockSpec((1, 1, d), mod_map),
                  full(wg), full(gb)],
        out_specs=pl.BlockSpec((1, tm, LANES), lambda b, i: (b, i, 0)),
        scratch_shapes=[pltpu.VMEM((tm, d), BF16)],
        compiler_params=_cparams(("parallel", "parallel")),
        name="gate_projection",
    )(x, shift, scale, wg, gb)


def _hy_ffn_kernel(feat_ref, w1_ref, b1_ref, w2_ref, b2_ref, w3_ref, fr_ref, dec_ref, o_ref):
    hp = lax.Precision.HIGHEST
    feats = feat_ref[...]
    fr = fr_ref[...]
    h = jnp.sin(fr * (jnp.dot(feats, w1_ref[...], precision=hp, preferred_element_type=F32) + b1_ref[...]))
    h = jnp.sin(fr * (jnp.dot(h, w2_ref[...], precision=hp, preferred_element_type=F32) + b2_ref[...]))
    h = jnp.dot(h, w3_ref[...], precision=hp, preferred_element_type=F32)
    t = feats[:, 0:1]
    o_ref[...] = h * jnp.exp(-t * jnp.abs(dec_ref[...]))


def hyena_filter_taps(feats, w1p, b1, w2, b2, w3, freq, decay_cols):
    L = feats.shape[0]
    n = w3.shape[1]
    tl = min(L, 256)
    full = lambda a: pl.BlockSpec(a.shape, lambda i: (0, 0))
    return pl.pallas_call(
        _hy_ffn_kernel,
        out_shape=jax.ShapeDtypeStruct((L, n), F32),
        grid=(L // tl,),
        in_specs=[pl.BlockSpec((tl, feats.shape[1]), lambda i: (i, 0)),
                  full(w1p), full(b1), full(w2), full(b2), full(w3), full(freq), full(decay_cols)],
        out_specs=pl.BlockSpec((tl, n), lambda i: (i, 0)),
        compiler_params=_cparams(("parallel",)),
        name="hyena_filter_taps",
    )(feats, w1p, b1, w2, b2, w3, freq, decay_cols)


def _row0(shape, i):
    return (lax.broadcasted_iota(jnp.int32, shape, 0) + i * shape[0]) == 0


def _hy_spec_kernel(ce_ref, se_ref, co_ref, so_ref, hfe_ref, hbe_ref, hfo_ref, hbo_ref,
                    kpa_ref, kqa_ref, kpb_ref, kqb_ref, *, L):
    i = pl.program_id(1)
    hf_e, hf_o = hfe_ref[...], hfo_ref[...]
    hb_e, hb_o = hbe_ref[...], hbo_ref[...]
    m = lax.broadcasted_iota(jnp.int32, hb_e.shape, 0)
    hb_e = jnp.where(m == 0, 0.0, hb_e)
    s_e, s_o, d_e, d_o = hf_e + hb_e, hf_o + hb_o, hf_e - hb_e, hf_o - hb_o
    dot = lambda t, x: jnp.dot(t[...], x.astype(BF16), preferred_element_type=F32)
    pe, po, qe, qo = dot(ce_ref, s_e), dot(co_ref, s_o), dot(se_ref, d_e), dot(so_ref, d_o)
    r0 = _row0(pe.shape, i)
    p_half = jnp.sum(s_e * (1 - 2 * (m & 1)).astype(F32), axis=0, keepdims=True)
    scale = 1.0 / L
    kpa_ref[0] = (pe + po) * scale
    kpb_ref[0] = (pe - po) * scale
    kqa_ref[0] = jnp.where(r0, p_half, qe + qo) * scale
    kqb_ref[0] = jnp.where(r0, qo, qo - qe) * scale


def hyena_filter_spectrum(tables, taps, w):
    ce, se, co, so = tables[:4]
    H = ce.shape[0]
    tf = min(H, DFT_ROWS)
    out = jax.ShapeDtypeStruct((HY_ORDER, H, w), F32)
    tab = lambda: pl.BlockSpec((tf, H), lambda o, i: (i, 0))
    slot = lambda: pl.BlockSpec((1, tf, w), lambda o, i: (o, i, 0))
    tap = lambda parity, direction: pl.BlockSpec((H, w), lambda o, i: (parity, 2 * o + direction))
    return pl.pallas_call(
        functools.partial(_hy_spec_kernel, L=2 * H),
        out_shape=(out,) * 4,
        grid=(HY_ORDER, H // tf),
        in_specs=[tab(), tab(), tab(), tab(), tap(0, 0), tap(0, 1), tap(1, 0), tap(1, 1)],
        out_specs=(slot(),) * 4,
        compiler_params=_cparams(("parallel", "parallel")),
        name="hyena_filter_spectrum",
    )(ce, se, co, so, taps, taps, taps, taps)


def _hy_fwd_kernel(ce_ref, se_ref, co_ref, so_ref, ue_ref, uo_ref, kpa_ref, kqa_ref, kpb_ref, kqb_ref,
                   eep_ref, eeq_ref, eop_ref, eoq_ref):
    ue, uo = ue_ref[0], uo_ref[0]
    dot = lambda t, x: jnp.dot(t[...], x, preferred_element_type=F32)
    pe, qe, po, qo = dot(ce_ref, ue), dot(se_ref, ue), dot(co_ref, uo), dot(so_ref, uo)

    def emit(rows, pe, qe, po, qo, r0):
        sel = (lambda special, general: general) if r0 is None else (lambda special, general: jnp.where(r0, special, general))
        kpa, kqa, kpb, kqb = kpa_ref[0, rows, :], kqa_ref[0, rows, :], kpb_ref[0, rows, :], kqb_ref[0, rows, :]
        pa, pb = pe + po, pe - po
        qa, qb = sel(qe, qe + qo), sel(qo, qo - qe)
        ypa = sel(0.5 * pa * kpa, pa * kpa - qa * kqa)
        ypb = sel(0.5 * pb * kpb, pb * kpb - qb * kqb)
        yqa = sel(qa * kqa - qb * kqb, pa * kqa + qa * kpa)
        yqb = sel(qa * kqb + qb * kqa, pb * kqb + qb * kpb)
        eep_ref[0, rows, :] = (ypa + ypb).astype(eep_ref.dtype)
        eop_ref[0, rows, :] = (ypa - ypb).astype(eop_ref.dtype)
        eeq_ref[0, rows, :] = sel(yqa, yqa - yqb).astype(eeq_ref.dtype)
        eoq_ref[0, rows, :] = sel(yqb, yqa + yqb).astype(eoq_ref.dtype)

    emit(slice(None), pe, qe, po, qo, None)

    @pl.when(pl.program_id(1) == 0)
    def _():
        top = slice(0, BF16_TILE_ROWS)
        r0 = lax.broadcasted_iota(jnp.int32, (BF16_TILE_ROWS, pe.shape[1]), 0) == 0
        emit(top, pe[top], qe[top], po[top], qo[top], r0)


def hyena_forward(tables, u_e, u_o, u_col, ks, order):
    ce, se, co, so = tables[:4]
    bv, H, _ = u_e.shape
    w = ks[0].shape[2]
    tf = min(H, DFT_ROWS)
    out = jax.ShapeDtypeStruct((bv, H, w), BF16)
    tab = lambda: pl.BlockSpec((tf, H), lambda b, i: (i, 0))
    sig = lambda: pl.BlockSpec((1, H, w), lambda b, i: (b, 0, u_col))
    slot = lambda: pl.BlockSpec((1, tf, w), lambda b, i: (order, i, 0))
    res = lambda: pl.BlockSpec((1, tf, w), lambda b, i: (b, i, 0))
    return pl.pallas_call(
        _hy_fwd_kernel,
        out_shape=(out,) * 4,
        grid=(bv, H // tf),
        in_specs=[tab(), tab(), tab(), tab(), sig(), sig(), slot(), slot(), slot(), slot()],
        out_specs=(res(),) * 4,
        compiler_params=_cparams(("parallel", "parallel")),
        name="hyena_forward_dft",
    )(ce, se, co, so, u_e, u_o, *ks)


def _hy_inv_kernel(cet_ref, set_ref, cot_ref, sot_ref, eep_ref, eeq_ref, eop_ref, eoq_ref,
                   ue_ref, uo_ref, ge_ref, go_ref, bias_ref, *o_refs, interleave):
    dot = lambda t, x: jnp.dot(t[...], x[0], preferred_element_type=F32)
    ye = dot(cet_ref, eep_ref) + dot(set_ref, eeq_ref)
    yo = dot(cot_ref, eop_ref) + dot(sot_ref, eoq_ref)
    bias = bias_ref[0]
    oe = ge_ref[0].astype(F32) * (ye + ue_ref[0].astype(F32) * bias)
    oo = go_ref[0].astype(F32) * (yo + uo_ref[0].astype(F32) * bias)
    if interleave:
        o_ref, mix_ref = o_refs
        n = oe.shape[0]
        for c in range(oe.shape[1] // LANES):
            lanes = slice(c * LANES, (c + 1) * LANES)
            mix_ref[c, pl.ds(0, n, stride=2), :] = oe[:, lanes]
            mix_ref[c, pl.ds(1, n, stride=2), :] = oo[:, lanes]
            o_ref[0, :, lanes] = mix_ref[c].astype(o_ref.dtype)
    else:
        oe_ref, oo_ref = o_refs
        oe_ref[0] = oe.astype(oe_ref.dtype)
        oo_ref[0] = oo.astype(oo_ref.dtype)


def hyena_inverse(tables, es, u_e, u_o, u_col, g_e, g_o, g_col, bias, order, interleave):
    cet, set_, cot, sot = tables[0], tables[4], tables[5], tables[6]
    bv, H, w = es[0].shape
    tt = min(H, DFT_ROWS)
    tab = lambda: pl.BlockSpec((tt, H), lambda b, i: (i, 0))
    spec = lambda: pl.BlockSpec((1, H, w), lambda b, i: (b, 0, 0))
    tile = lambda col: pl.BlockSpec((1, tt, w), lambda b, i: (b, i, col))
    if interleave:
        out_shape = jax.ShapeDtypeStruct((bv, 2 * H, w), BF16)
        out_specs = pl.BlockSpec((1, 2 * tt, w), lambda b, i: (b, i, 0))
        scratch = [pltpu.VMEM((w // LANES, 2 * tt, LANES), F32)]
    else:
        out_shape = (jax.ShapeDtypeStruct((bv, H, w), BF16),) * 2
        out_specs = (tile(0), tile(0))
        scratch = []
    return pl.pallas_call(
        functools.partial(_hy_inv_kernel, interleave=interleave),
        out_shape=out_shape,
        grid=(bv, H // tt),
        in_specs=[tab(), tab(), tab(), tab(), spec(), spec(), spec(), spec(),
                  tile(u_col), tile(u_col), tile(g_col), tile(g_col),
                  pl.BlockSpec((1, 1, w), lambda b, i: (order, 0, 0))],
        out_specs=out_specs,
        scratch_shapes=scratch,
        compiler_params=_cparams(("parallel", "parallel")),
        name="hyena_inverse_dft",
    )(cet, set_, cot, sot, *es, u_e, u_o, g_e, g_o, bias)


def dft_tables(L):
    H = L // 2
    r = jnp.arange(H, dtype=jnp.int32)[:, None]
    c = jnp.arange(H, dtype=jnp.int32)[None, :]
    ang = lambda k: (k % (2 * L)).astype(F32) * (math.pi / L)
    alt_r, alt_c = (1 - 2 * (r % 2)).astype(F32), (1 - 2 * (c % 2)).astype(F32)
    a_e, a_o = ang(2 * r * c), ang(r * (2 * c + 1))
    bf = lambda a: a.astype(BF16)
    ce = bf(jnp.cos(a_e))
    se = bf(jnp.where(r == 0, alt_c, jnp.sin(a_e)))
    co = bf(jnp.cos(a_o))
    so = bf(jnp.where(r == 0, alt_c, jnp.sin(a_o)))
    set_ = bf(jnp.where(c == 0, alt_r, jnp.sin(a_e)))
    return ce, se, co, so, set_, co.T, so.T


def hyena_features(L):
    pos = jnp.arange(L, dtype=F32)
    t = pos / (L - 1)
    bands = jnp.linspace(1e-4, HY_BANDS - 1, HY_BANDS, dtype=F32)
    ang = (2.0 * math.pi / L) * pos[:, None] * bands[None, :]
    feats = jnp.concatenate([t[:, None], jnp.cos(ang), jnp.sin(ang)], axis=-1)
    feats = jnp.concatenate([feats[0::2], feats[1::2]], axis=0)
    return jnp.pad(feats, ((0, 0), (0, LANES - HY_FEAT)))


def hyena_branch(z_e, z_o, tables, feats, ffn, bias):
    w = bias.shape[-1]
    taps = hyena_filter_taps(feats, *ffn)
    ks = hyena_filter_spectrum(tables, taps, w)
    bias3 = bias.reshape(HY_ORDER, 1, w)
    es = hyena_forward(tables, z_e, z_o, 0, ks, 0)
    y1_e, y1_o = hyena_inverse(tables, es, z_e, z_o, 0, z_e, z_o, 1, bias3, 0, False)
    es = hyena_forward(tables, y1_e, y1_o, 0, ks, 1)
    return hyena_inverse(tables, es, y1_e, y1_o, 0, z_e, z_o, 2, bias3, 1, True)


def _mlstm_chunk(q, k, v1t, li_row, b_row, r_col, c, m, reverse):
    T = q.shape[0]
    s_idx = lax.broadcasted_iota(jnp.int32, (T, T), 0)
    t_idx = lax.broadcasted_iota(jnp.int32, (T, T), 1)
    mask = (s_idx >= t_idx) if reverse else (s_idx <= t_idx)
    nt = (((1,), (1,)), ((), ()))
    kq = lax.dot_general(k, q, nt, preferred_element_type=F32)
    d = jnp.where(mask, r_col + b_row, -1e30)
    m_loc = jnp.max(d, axis=0, keepdims=True)
    st = (kq * jnp.exp(d - m_loc)).astype(BF16)
    intra = jnp.dot(v1t, st, preferred_element_type=F32)
    g = b_row + m
    m_t = jnp.maximum(g, m_loc)
    inter = lax.dot_general(c.astype(BF16), q, nt, preferred_element_type=F32)
    num = jnp.exp(g - m_t) * inter + jnp.exp(m_loc - m_t) * intra
    den = num[HEAD_PAD - 1:HEAD_PAD, :]
    h = num * (1.0 / jnp.maximum(jnp.abs(den), jnp.exp(-m_t)))
    btot = b_row[:, 0:1] if reverse else b_row[:, T - 1:T]
    a = btot - b_row + li_row
    a_max = jnp.max(a, axis=1, keepdims=True)
    wv = (v1t.astype(F32) * jnp.exp(a - a_max)).astype(BF16)
    delta = jnp.dot(wv, k, preferred_element_type=F32)
    m_new = jnp.maximum(btot + m, a_max)
    c_new = jnp.exp(btot + m - m_new) * c + jnp.exp(a_max - m_new) * delta
    return h, c_new, m_new


def _mlstm_kernel(qc_ref, kc_ref, vc_ref, oc_ref, gcc_ref, grc_ref,
                  ql_ref, kl_ref, vl_ref, ol_ref, gcl_ref, grl_ref, ng_ref,
                  outc_ref, outl_ref, hf_ref, hb_ref, *, dh):
    T = ML_CHUNK
    L = ql_ref.shape[1]
    row = lax.broadcasted_iota(jnp.int32, (HEAD_PAD, T), 0)
    ones_row = row == HEAD_PAD - 1
    valid = row < dh
    ng = ng_ref[...]

    head = pl.program_id(1)
    gate_lane = lax.broadcasted_iota(jnp.int32, (1, LANES), 1)

    def gates(gc, gr, bwd):
        i = 2 if bwd else 0
        pick = ((gate_lane == i * ML_HEADS + head).astype(F32)
                - (gate_lane == (i + 1) * ML_HEADS + head).astype(F32))
        return gr[i:i + 1, :], gr[i + 1:i + 2, :], jnp.sum(gc * pick, axis=1, keepdims=True)

    def finish(h, o):
        h = jnp.where(valid, h, 0.0)
        mu = jnp.sum(h, axis=0, keepdims=True) * (1.0 / dh)
        dlt = jnp.where(valid, h - mu, 0.0)
        var = jnp.sum(dlt * dlt, axis=0, keepdims=True) * (1.0 / dh)
        return (dlt * lax.rsqrt(var + LN_EPS) * ng * o.astype(F32)).astype(BF16)

    def aug(vt):
        return jnp.where(ones_row, jnp.ones_like(vt), vt)

    zero_c = jnp.zeros((HEAD_PAD, HEAD_PAD), F32)
    zero_m = jnp.zeros((1, 1), F32)

    q, k, v1t = qc_ref[0], kc_ref[0], aug(vc_ref[0])
    gc, gr = gcc_ref[0], grc_ref[0, 0]
    h_f, c_f, m_f = _mlstm_chunk(q, k, v1t, *gates(gc, gr, False), zero_c, zero_m, False)
    h_b, c_b, m_b = _mlstm_chunk(q, k, v1t, *gates(gc, gr, True), zero_c, zero_m, True)
    outc_ref[0] = finish(h_f + h_b, oc_ref[0])

    nc = L // T
    for ci in range(nc):
        for bwd in (False, True):
            lo = (nc - 1 - ci) * T if bwd else ci * T
            gts = gates(gcl_ref[0, lo:lo + T, :], grl_ref[0, 0, :, lo:lo + T], bwd)
            q, k, v1t = ql_ref[0, lo:lo + T, :], kl_ref[0, lo:lo + T, :], aug(vl_ref[0, :, lo:lo + T])
            if bwd:
                h_b, c_b, m_b = _mlstm_chunk(q, k, v1t, *gts, c_b, m_b, True)
                hb_ref[:, lo:lo + T] = h_b
            else:
                h_f, c_f, m_f = _mlstm_chunk(q, k, v1t, *gts, c_f, m_f, False)
                hf_ref[:, lo:lo + T] = h_f
    for ci in range(nc):
        lo = ci * T
        outl_ref[0, :, lo:lo + T] = finish(hf_ref[:, lo:lo + T] + hb_ref[:, lo:lo + T], ol_ref[0, :, lo:lo + T])


def mlstm_branch(qk_c, vt_c, ot_c, gate_c, qk_l, vt_l, ot_l, gate_l, norm_g_col, dh):
    bv, lc, _ = qk_c.shape
    L = qk_l.shape[1]
    H = ML_HEADS
    assert lc == ML_CHUNK and L % ML_CHUNK == 0

    def per_head(g):
        return g[:, :, :4 * H].reshape(bv, -1, 4, H).transpose(0, 3, 2, 1)

    grc, grl = per_head(gate_c), per_head(gate_l)

    def qk_blk(length, part):
        return pl.BlockSpec((1, length, HEAD_PAD), lambda b, h: (b, 0, part * H + h))

    def t_blk(length):
        return pl.BlockSpec((1, HEAD_PAD, length), lambda b, h: (b, h, 0))

    def gspecs(length):
        return [pl.BlockSpec((1, length, LANES), lambda b, h: (b, 0, 0)),
                pl.BlockSpec((1, 1, 4, length), lambda b, h: (b, h, 0, 0))]

    def specs(length):
        return [qk_blk(length, 0), qk_blk(length, 1), t_blk(length), t_blk(length)] + gspecs(length)

    return pl.pallas_call(
        functools.partial(_mlstm_kernel, dh=dh),
        out_shape=(jax.ShapeDtypeStruct((bv, H * HEAD_PAD, lc), BF16),
                   jax.ShapeDtypeStruct((bv, H * HEAD_PAD, L), BF16)),
        grid=(bv, H),
        in_specs=specs(lc) + specs(L) + [pl.BlockSpec((HEAD_PAD, 1), lambda b, h: (h, 0))],
        out_specs=(t_blk(lc), t_blk(L)),
        scratch_shapes=[pltpu.VMEM((HEAD_PAD, L), F32), pltpu.VMEM((HEAD_PAD, L), F32)],
        compiler_params=_cparams(("parallel", "parallel")),
        name="mlstm_scan",
    )(qk_c, qk_c, vt_c, ot_c, gate_c, grc, qk_l, qk_l, vt_l, ot_l, gate_l, grl, norm_g_col)


def _s5_kernel(uc_ref, ul_ref, prow_ref, pcol_ref, bt_ref, cr_ref, cc_ref, y_ref,
               toep_ref, min_ref, mout_ref, kv_ref, s_ref, x_ref, u_ref, yv_ref, *, nb, nctx, nchunk):
    T, CG, P = S5_CHUNK, S5_GROUP, S5_STATE

    ch0 = 0
    for src in (uc_ref, ul_ref):
        xs = pltpu.einshape("bct->cbt", src[...].astype(F32))
        for ch in range(xs.shape[2] // T):
            for ci in range(CG):
                u_ref[(ch0 + ch) * nb:(ch0 + ch + 1) * nb, ci * T:(ci + 1) * T] = (
                    xs[ci, :, ch * T:(ch + 1) * T].astype(BF16))
        ch0 += xs.shape[2] // T

    hp = lax.Precision.HIGHEST
    lane = lax.broadcasted_iota(jnp.int32, (1, 2 * P), 1)
    re_lane = lane < P
    re_row = lax.broadcasted_iota(jnp.int32, (2 * P, 1), 0) < P

    def powers_rows(n, mag1, th1):
        m = jnp.exp(n * mag1)
        return m * jnp.cos(n * th1), m * jnp.sin(n * th1)

    def powers_cols(n, mag1, th1):
        m = jnp.exp(n * mag1)
        return m * jnp.cos(n * th1), m * jnp.sin(n * th1)

    at = []
    kv = jnp.zeros((CG * CG, 2 * T), F32)
    for d in range(2):
        a_re, a_im, ldt = prow_ref[0, d, 0:1, :], prow_ref[0, d, 1:2, :], prow_ref[0, d, 2:3, :]
        dt = jnp.exp(ldt)
        mag1, th1 = dt * a_re, dt * a_im
        ab_re, ab_im = jnp.exp(mag1) * jnp.cos(th1), jnp.exp(mag1) * jnp.sin(th1)
        den = a_re * a_re + a_im * a_im
        co_re = ((ab_re - 1.0) * a_re + ab_im * a_im) / den
        co_im = (ab_im * a_re - (ab_re - 1.0) * a_im) / den
        b_re, b_im = bt_ref[0, d, 0], bt_ref[0, d, 1]
        bb_re = co_re * b_re - co_im * b_im
        bb_im = co_re * b_im + co_im * b_re
        c_re, c_im = cr_ref[0, d, 0], cr_ref[0, d, 1]
        a_re_c, a_im_c = pcol_ref[0, d, :, 0:1], pcol_ref[0, d, :, 1:2]
        dt_c = jnp.exp(pcol_ref[0, d, :, 2:3])
        mag1_c, th1_c = dt_c * a_re_c, dt_c * a_im_c

        rep = lambda a: jnp.broadcast_to(a[:, None, :], (CG, CG, 2 * P)).reshape(CG * CG, 2 * P)
        til = lambda a: jnp.broadcast_to(a[None, :, :], (CG, CG, 2 * P)).reshape(CG * CG, 2 * P)
        w_re = rep(bb_re) * til(c_re) - rep(bb_im) * til(c_im)
        w_im = rep(bb_re) * til(c_im) + rep(bb_im) * til(c_re)
        w = jnp.where(re_lane, w_re, -w_im)
        lagp = lax.broadcasted_iota(jnp.int32, (1, 2 * T), 1)
        n = (T - 1 - lagp) if d else (lagp - (T - 1))
        ok = (n >= 0) & (lagp < 2 * T - 1)
        pc, ps = powers_cols(jnp.maximum(n, 0).astype(F32), mag1_c, th1_c)
        pw = jnp.where(ok, jnp.where(re_row, pc, ps), 0.0)
        kv = kv + jnp.dot(w, pw, precision=hp, preferred_element_type=F32)

        s_col = lax.broadcasted_iota(jnp.int32, (T, 1), 0)
        pc, ps = powers_rows((s_col if d else T - 1 - s_col).astype(F32), mag1, th1)
        for ci in range(CG):
            br, bi = bb_re[ci:ci + 1, :], bb_im[ci:ci + 1, :]
            blk = jnp.where(re_lane, br * pc - bi * ps, br * ps + bi * pc)
            min_ref[d, ci * T:(ci + 1) * T, :] = blk.astype(BF16)

        t_row = lax.broadcasted_iota(jnp.int32, (1, T), 1)
        pc, ps = powers_cols((T - t_row if d else t_row + 1).astype(F32), mag1_c, th1_c)
        for co in range(CG):
            cr, ci_ = cc_ref[0, d, 0, :, co:co + 1], cc_ref[0, d, 1, :, co:co + 1]
            blk = jnp.where(re_row, cr * pc - ci_ * ps, -(cr * ps + ci_ * pc))
            mout_ref[d, :, co * T:(co + 1) * T] = blk.astype(BF16)

        mt = jnp.exp(T * mag1)
        at.append(((mt * jnp.cos(T * th1))[:, :P], (mt * jnp.sin(T * th1))[:, :P]))

    kv_ref[...] = kv

    def build(ci, carry):
        r0 = pl.multiple_of(ci * T, T)
        for co in range(CG):
            vec = kv_ref[pl.ds(ci * CG + co, 1), :]
            rolled = pltpu.roll(jnp.broadcast_to(vec, (T, 2 * T)), T + 1, 1, stride=1, stride_axis=0)
            toep_ref[pl.ds(r0, T), co * T:(co + 1) * T] = rolled[:, :T].astype(BF16)
        return carry

    lax.fori_loop(0, CG, build, 0)

    u = u_ref[...]
    y = jnp.dot(u, toep_ref[...], preferred_element_type=F32)
    for d in range(2):
        s_ref[d] = jnp.dot(u, min_ref[d], preferred_element_type=F32)

    orders = (list(range(nchunk)), list(range(nctx - 1, -1, -1)) + list(range(nchunk - 1, nctx - 1, -1)))
    for d in range(2):
        at_re, at_im = at[d]
        x_re = jnp.zeros((nb, P), F32)
        x_im = jnp.zeros((nb, P), F32)
        for ch in orders[d]:
            x_ref[d, ch * nb:(ch + 1) * nb, :] = jnp.concatenate([x_re, x_im], axis=1)
            s = s_ref[d, ch * nb:(ch + 1) * nb, :]
            x_re, x_im = (at_re * x_re - at_im * x_im + s[:, :P], at_re * x_im + at_im * x_re + s[:, P:])
        y = y + jnp.dot(x_ref[d].astype(BF16), mout_ref[d], preferred_element_type=F32)

    for ch in range(nchunk):
        for co in range(CG):
            yv_ref[co, :, ch * T:(ch + 1) * T] = y[ch * nb:(ch + 1) * nb, co * T:(co + 1) * T]
    y_ref[...] = pltpu.einshape("cbt->bct", yv_ref[...]).astype(y_ref.dtype)


def s5_mix(ut_c, ut_l, a_re, a_im, log_dt, b_re, b_im, c_re, c_im):
    nb, w, lc = ut_c.shape
    L = ut_l.shape[2]
    T, CG, P = S5_CHUNK, S5_GROUP, S5_STATE
    G = w // CG
    nctx, nlat = lc // T, L // T
    nchunk = nctx + nlat

    dup = lambda a: jnp.concatenate([a, a], axis=-1)
    f32 = lambda a: a.astype(F32)
    ldt = jnp.broadcast_to(f32(log_dt)[:, :, None], (2, G, P))
    prow = jnp.stack([dup(f32(a_re)), dup(f32(a_im)), dup(ldt)], axis=2)
    prow = jnp.pad(prow, ((0, 0), (0, 0), (0, 5), (0, 0))).transpose(1, 0, 2, 3)
    pcol = prow.transpose(0, 1, 3, 2)
    bt = jnp.stack([dup(f32(b_re).transpose(0, 1, 3, 2)), dup(f32(b_im).transpose(0, 1, 3, 2))], axis=2)
    bt = bt.transpose(1, 0, 2, 3, 4)
    cr = jnp.stack([dup(f32(c_re)), dup(f32(c_im))], axis=2).transpose(1, 0, 2, 3, 4)
    cc = cr.transpose(0, 1, 2, 4, 3)

    blk = lambda a: pl.BlockSpec((1,) + a.shape[1:], lambda g: (g,) + (0,) * (a.ndim - 1))
    grp = lambda length: pl.BlockSpec((nb, CG, length), lambda g: (0, g, 0))
    r = nchunk * nb
    return pl.pallas_call(
        functools.partial(_s5_kernel, nb=nb, nctx=nctx, nchunk=nchunk),
        out_shape=jax.ShapeDtypeStruct((nb, w, lc + L), BF16),
        grid=(G,),
        in_specs=[grp(lc), grp(L), blk(prow), blk(pcol), blk(bt), blk(cr), blk(cc)],
        out_specs=grp(lc + L),
        scratch_shapes=[pltpu.VMEM((CG * T, CG * T), BF16), pltpu.VMEM((2, CG * T, 2 * P), BF16),
                        pltpu.VMEM((2, 2 * P, CG * T), BF16), pltpu.VMEM((CG * CG, 2 * T), F32),
                        pltpu.VMEM((2, r, 2 * P), F32), pltpu.VMEM((2, r, 2 * P), F32),
                        pltpu.VMEM((r, CG * T), BF16), pltpu.VMEM((CG, nb, lc + L), F32)],
        compiler_params=_cparams(("parallel",)),
        name="s5_mix",
    )(ut_c, ut_l, prow, pcol, bt, cr, cc)


def _gelu_tanh(x):
    return 0.5 * x * (1.0 + jnp.tanh(math.sqrt(2.0 / math.pi) * (x + 0.044715 * (x * x * x))))


def _merge_kernel(yhy_ref, yml_ref, ys_ref, u_ref, ghy_ref, gml_ref, gs5_ref,
                  d_ref, gw_ref, gb_ref, why_ref, wml_ref, ws5_ref, o_ref):
    tn = (((0,), (0,)), ((), ()))
    ys = ys_ref[0].astype(F32) + d_ref[...] * u_ref[0].astype(F32)
    z = _gelu_tanh(ys).astype(BF16)
    glu = jnp.dot(gw_ref[...], z, preferred_element_type=F32) + gb_ref[...]
    y_s5 = (z.astype(F32) * _sigmoid(glu)).astype(BF16)
    acc = ghy_ref[0].astype(F32) * jnp.dot(yhy_ref[0], why_ref[...], preferred_element_type=F32)
    acc = acc + gml_ref[0].astype(F32) * lax.dot_general(yml_ref[0], wml_ref[...], tn, preferred_element_type=F32)
    acc = acc + gs5_ref[0].astype(F32) * lax.dot_general(y_s5, ws5_ref[...], tn, preferred_element_type=F32)
    o_ref[0] = acc.astype(o_ref.dtype)


def merge_branches(y_hy, y_ml, ys_t, u_t, t_off, gate_arr, s5_d, glu_wt, glu_b, w_hy, w_ml, w_s5):
    bv, lv, w = y_hy.shape
    dm = w_hy.shape[1]
    tm = ML_CHUNK
    full = lambda a: pl.BlockSpec(a.shape, lambda b, i: (0,) * a.ndim)
    gate = lambda c: pl.BlockSpec((1, tm, dm), lambda b, i: (b, i, c))
    return pl.pallas_call(
        _merge_kernel,
        out_shape=jax.ShapeDtypeStruct((bv, lv, dm), BF16),
        grid=(bv, lv // tm),
        in_specs=[pl.BlockSpec((1, tm, w), lambda b, i: (b, i, 0)),
                  pl.BlockSpec((1, y_ml.shape[1], tm), lambda b, i: (b, 0, i)),
                  pl.BlockSpec((1, w, tm), lambda b, i: (b, 0, i + t_off)),
                  pl.BlockSpec((1, w, tm), lambda b, i: (b, 0, i)),
                  gate(0), gate(1), gate(2),
                  full(s5_d), full(glu_wt), full(glu_b), full(w_hy), full(w_ml), full(w_s5)],
        out_specs=pl.BlockSpec((1, tm, dm), lambda b, i: (b, i, 0)),
        compiler_params=_cparams(("parallel", "parallel")),
        name="merge_branches",
    )(y_hy, y_ml, ys_t, u_t, gate_arr, gate_arr, gate_arr, s5_d, glu_wt, glu_b, w_hy, w_ml, w_s5)


def _outproj_kernel(m_ref, w_ref, x_ref, gate_ref, g_ref, b_ref, o_ref, *, alpha):
    rb = min(m_ref.shape[1], LANES)
    for s in range(m_ref.shape[1] // rb):
        rows = slice(s * rb, (s + 1) * rb)
        y = jnp.dot(m_ref[0, rows, :], w_ref[...], preferred_element_type=F32)
        r = alpha * x_ref[0, rows, :] + gate_ref[0] * y
        o_ref[0, rows, :] = _layer_norm(r, g_ref[...], b_ref[...])


def out_projection(merged, w_out, x, gate, ln_g, ln_b, alpha, tm):
    bv, lv, d = x.shape
    bm = gate.shape[0]
    mod_map = (lambda b, i: (b, 0, 0)) if bm == bv else (lambda b, i: (0, 0, 0))
    vec = lambda: pl.BlockSpec((1, d), lambda b, i: (0, 0))
    return pl.pallas_call(
        functools.partial(_outproj_kernel, alpha=alpha),
        out_shape=jax.ShapeDtypeStruct((bv, lv, d), F32),
        grid=(bv, lv // tm),
        in_specs=[pl.BlockSpec((1, tm, d), lambda b, i: (b, i, 0)),
                  pl.BlockSpec((d, d), lambda b, i: (0, 0)),
                  pl.BlockSpec((1, tm, d), lambda b, i: (b, i, 0)),
                  pl.BlockSpec((1, 1, d), mod_map), vec(), vec()],
        out_specs=pl.BlockSpec((1, tm, d), lambda b, i: (b, i, 0)),
        compiler_params=_cparams(("parallel", "parallel")),
        name="out_projection_ln",
    )(merged, w_out, x, gate, ln_g, ln_b)


def _mlp_kernel(x_ref, sh_ref, sc_ref, gate_ref, w1_ref, w2_ref, g_ref, b_ref, o_ref, xm_ref, acc_ref, *, alpha):
    k = pl.program_id(2)
    last = pl.num_programs(2) - 1
    tm = xm_ref.shape[0]
    rb = min(tm, MXU_DIM)
    blocks = [slice(s * rb, (s + 1) * rb) for s in range(tm // rb)]

    def ffn(rows):
        h = jnp.maximum(jnp.dot(xm_ref[rows, :], w1_ref[...], preferred_element_type=F32), 0.0)
        return jnp.dot((h * h).astype(BF16), w2_ref[...], preferred_element_type=F32)

    @pl.when(k == 0)
    def _():
        for rows in blocks:
            _modulate_rows(xm_ref, x_ref, sh_ref, sc_ref, rows)
            acc_ref[rows, :] = ffn(rows)

    @pl.when((k > 0) & (k < last))
    def _():
        acc_ref[...] += ffn(slice(None))

    @pl.when(k == last)
    def _():
        for rows in blocks:
            r = alpha * x_ref[0, rows, :] + gate_ref[0] * (acc_ref[rows, :] + ffn(rows))
            o_ref[0, rows, :] = _layer_norm(r, g_ref[...], b_ref[...])


def mlp_block(x, shift, scale, gate, w1, w2, ln_g, ln_b, alpha, tm, kf):
    bv, lv, d = x.shape
    dff = w1.shape[1]
    assert dff // kf >= 2
    bm = gate.shape[0]
    mod_map = (lambda b, i, k: (b, 0, 0)) if bm == bv else (lambda b, i, k: (0, 0, 0))
    vec = lambda: pl.BlockSpec((1, d), lambda b, i, k: (0, 0))
    return pl.pallas_call(
        functools.partial(_mlp_kernel, alpha=alpha),
        out_shape=jax.ShapeDtypeStruct((bv, lv, d), F32),
        grid=(bv, lv // tm, dff // kf),
        in_specs=[pl.BlockSpec((1, tm, d), lambda b, i, k: (b, i, 0)),
                  pl.BlockSpec((1, 1, d), mod_map), pl.BlockSpec((1, 1, d), mod_map),
                  pl.BlockSpec((1, 1, d), mod_map),
                  pl.BlockSpec((d, kf), lambda b, i, k: (0, k)),
                  pl.BlockSpec((kf, d), lambda b, i, k: (k, 0)),
                  vec(), vec()],
        out_specs=pl.BlockSpec((1, tm, d), lambda b, i, k: (b, i, 0)),
        scratch_shapes=[pltpu.VMEM((tm, d), BF16), pltpu.VMEM((tm, d), F32)],
        compiler_params=_cparams(("parallel", "parallel", "arbitrary")),
        name="mlp_ln",
    )(x, shift, scale, gate, w1, w2, ln_g, ln_b)


def _pad_heads(a, axis, dh):
    shp = a.shape
    a = a.reshape(shp[:axis] + (ML_HEADS, dh) + shp[axis + 1:])
    pad = [(0, 0)] * a.ndim
    pad[axis + 1] = (0, HEAD_PAD - dh)
    a = jnp.pad(a, pad)
    return a.reshape(shp[:axis] + (ML_HEADS * HEAD_PAD,) + shp[axis + 1:])


def _layer_params(l, p, w_hy, w_ml, w_s5, d_model):
    dh = w_ml // ML_HEADS
    sizes = ((HY_ORDER + 1) * w_hy, 2 * w_ml, w_ml, w_ml, 4 * ML_HEADS, w_s5, N_BRANCH * d_model)
    pts = [0]
    for s in sizes:
        pts.append(pts[-1] + s)
    w_in = p["w_in"][l]
    hy, qk, v, o, gt, u, mg = (w_in[:, pts[i]:pts[i + 1]] for i in range(7))
    ph = lambda a: _pad_heads(a, a.ndim - 1, dh)
    none = lambda n: (jnp.zeros((3, n), F32), jnp.zeros((1, n), F32), jnp.ones((1, n), F32))
    half = ML_HEADS * HEAD_PAD
    cw, cb = p["ml_conv_w"][l], p["ml_conv_b"][l]
    out = {}
    out["mg"] = (mg.astype(BF16),) + none(mg.shape[1])
    out["hy"] = (hy.astype(BF16), p["hy_conv_w"][l], p["hy_conv_b"][l].reshape(1, -1), jnp.ones((1, hy.shape[1]), F32))
    out["qk"] = (jnp.concatenate([ph(qk[:, :w_ml]), ph(qk[:, w_ml:])], axis=1).astype(BF16),
                 jnp.concatenate([ph(cw[:, :w_ml]), ph(cw[:, w_ml:])], axis=1),
                 jnp.concatenate([ph(cb[:w_ml]), ph(cb[w_ml:])]).reshape(1, -1),
                 jnp.concatenate([jnp.ones((half,), F32), jnp.full((half,), dh ** -0.5, F32)]).reshape(1, -1))
    out["vt"] = (ph(v).T.astype(BF16),) + none(half)
    out["ot"] = (ph(o).T.astype(BF16),) + none(half)
    ngt = gt.shape[1]
    out["ut"] = (u.T.astype(BF16),) + none(u.shape[1])
    out["gt"] = (jnp.pad(gt, ((0, 0), (0, LANES - ngt))).astype(BF16),
                 jnp.pad(p["ml_gate_b"][l].reshape(-1), ((0, LANES - ngt),)).reshape(1, -1))
    out["norm_g"] = ph(p["ml_norm_g"][l]).reshape(-1, 1)
    out["w_ml_out"] = _pad_heads(p["w_ml_out"][l], 0, dh).astype(BF16)
    w3 = p["hy_ffn_w3"][l]
    dec = p["hy_decay"][l]
    dec_cols = jnp.broadcast_to(dec[:, None, :], (HY_ORDER, 2, w_hy)).reshape(1, -1)
    hpad = LANES - HY_HIDDEN
    row = lambda a: jnp.pad(a.reshape(1, -1), ((0, 0), (0, hpad)))
    out["hy_ffn"] = (jnp.pad(p["hy_ffn_w1"][l], ((0, LANES - HY_FEAT), (0, hpad))), row(p["hy_ffn_b1"][l]),
                     jnp.pad(p["hy_ffn_w2"][l], ((0, hpad), (0, hpad))), row(p["hy_ffn_b2"][l]),
                     jnp.pad(w3, ((0, hpad), (0, 0))), row(p["hy_sin_freq"][l]), dec_cols)
    return out


def kernel(x, c, ctx, c_ctx, w_mod, b_mod, w_in, hy_conv_w, hy_conv_b, hy_ffn_w1, hy_ffn_b1, hy_ffn_w2,
           hy_ffn_b2, hy_ffn_w3, hy_sin_freq, hy_decay, hy_bias, ml_conv_w, ml_conv_b, ml_gate_b, ml_norm_g,
           s5_a_re, s5_a_im, s5_log_dt, s5_b_re, s5_b_im, s5_c_re, s5_c_im, s5_d, s5_glu_w, s5_glu_b,
           w_hy_out, w_ml_out, w_s5_out, w_out, ln1_g, ln1_b, ln2_g, ln2_b, w_ff1, w_ff2):
    p = dict(w_in=w_in, hy_conv_w=hy_conv_w, hy_conv_b=hy_conv_b, hy_ffn_w1=hy_ffn_w1, hy_ffn_b1=hy_ffn_b1,
             hy_ffn_w2=hy_ffn_w2, hy_ffn_b2=hy_ffn_b2, hy_ffn_w3=hy_ffn_w3, hy_sin_freq=hy_sin_freq,
             hy_decay=hy_decay, ml_conv_w=ml_conv_w, ml_conv_b=ml_conv_b, ml_gate_b=ml_gate_b,
             ml_norm_g=ml_norm_g, w_ml_out=w_ml_out)
    B, L, D = x.shape
    LC = ctx.shape[1]
    depth = w_in.shape[0]
    w_hy = hy_bias.shape[-1]
    w_ml = ml_norm_g.shape[-1]
    w_s5 = s5_d.shape[-1]
    dh = w_ml // ML_HEADS
    alpha = (2 * depth) ** 0.25
    half = ML_HEADS * HEAD_PAD

    tab_l, tab_c = dft_tables(L), dft_tables(LC)
    feats_l, feats_c = hyena_features(L), hyena_features(LC)

    mrows = 8 * ((B + 1 + 7) // 8)
    c_rows = jnp.zeros((mrows, D), F32).at[:B].set(c).at[B].set(c_ctx)

    for l in range(depth):
        need_ctx = l < depth - 1
        lp = _layer_params(l, p, w_hy, w_ml, w_s5, D)
        mod = mod_vectors(c_rows, w_mod, b_mod.reshape(depth, 1, -1), l)
        mod_l = [mod[:B, k * D:(k + 1) * D].reshape(B, 1, D) for k in range(6)]
        mod_c = [mod[B:B + 1, k * D:(k + 1) * D].reshape(1, 1, D) for k in range(6)]

        def mixer_inputs(xv, md, row_len, full):
            bv, lv, _ = xv.shape
            tm = min(PROJ_ROWS, lv)
            proj = functools.partial(projection, xv, md[0], md[1], tm=tm, row_len=row_len)
            flat = md[0].shape[0] == 1
            xr = xv.reshape(1, bv * lv, D) if flat else xv
            tmr = min(PROJ_ROWS, xr.shape[1])
            rowp = functools.partial(projection, xr, md[0], md[1], tm=tmr, row_len=row_len)
            unflat = lambda a: a.reshape(bv, -1, a.shape[-1])
            r = {}
            r["qk"] = unflat(rowp(*lp["qk"], kind="convsilu", tn=half))
            r["vt"] = proj(*lp["vt"], kind="plain", tn=half, transposed=True)
            r["ot"] = proj(*lp["ot"], kind="sigmoid", tn=half, transposed=True)
            r["ut"] = proj(*lp["ut"], kind="plain", tn=w_s5, transposed=True)
            r["g"] = gate_projection(xv, md[0], md[1], *lp["gt"], tm=tm)
            if full:
                r["mg"] = unflat(rowp(*lp["mg"], kind="sigmoid", tn=2 * w_hy))
                z_e, z_o = projection_even_odd(xr, md[0], md[1], *lp["hy"][:3], tn=w_hy, tm=tmr, row_len=row_len)
                r["z"] = (unflat(z_e), unflat(z_o))
            return r

        r_l = mixer_inputs(x, mod_l, GRID_W, True)
        r_c = mixer_inputs(ctx, mod_c, LC, need_ctx)

        hm_c, hm_l = mlstm_branch(r_c["qk"], r_c["vt"], r_c["ot"], r_c["g"],
                                  r_l["qk"], r_l["vt"], r_l["ot"], r_l["g"], lp["norm_g"], dh)

        ys_t = s5_mix(r_c["ut"], r_l["ut"], s5_a_re[l], s5_a_im[l], s5_log_dt[l],
                      s5_b_re[l], s5_b_im[l], s5_c_re[l], s5_c_im[l])

        s5_dv = s5_d[l].reshape(-1, 1)
        glu_wt = s5_glu_w[l].T.astype(BF16)
        glu_b = s5_glu_b[l].reshape(-1, 1)
        w_hy_o = w_hy_out[l].astype(BF16)
        w_s5_o = w_s5_out[l].astype(BF16)
        w_o = w_out[l].astype(BF16)
        w1 = w_ff1[l].astype(BF16)
        w2 = w_ff2[l].astype(BF16)
        g1, b1 = ln1_g[l].reshape(1, -1), ln1_b[l].reshape(1, -1)
        g2, b2 = ln2_g[l].reshape(1, -1), ln2_b[l].reshape(1, -1)

        def finish(xv, r, hm, t_off, tables, feats, md):
            tm = TOKEN_ROWS
            y_hy = hyena_branch(*r["z"], tables, feats, lp["hy_ffn"], hy_bias[l])
            merged = merge_branches(y_hy, hm, ys_t, r["ut"], t_off, r["mg"], s5_dv, glu_wt, glu_b,
                                    w_hy_o, lp["w_ml_out"], w_s5_o)
            if md[2].shape[0] == 1:
                merged, xv = merged.reshape(1, -1, D), xv.reshape(1, -1, D)
            x1 = out_projection(merged, w_o, xv, md[2], g1, b1, alpha, tm)
            return mlp_block(x1, md[3], md[4], md[5], w1, w2, g2, b2, alpha, tm, MLP_FF_CHUNK)

        x = finish(x, r_l, hm_l, LC // ML_CHUNK, tab_l, feats_l, mod_l)
        if need_ctx:
            ctx = finish(ctx, r_c, hm_c, 0, tab_c, feats_c, mod_c).reshape(B, LC, D)
    return x
```
